```python
import jax, jax.numpy as jnp
from jax import lax
import numpy as np

D_MODEL = 1024
BATCH = 8
SEQ = 4096
DEPTH = 4

CHUNK = 64
N_MEM = 256
EPS = 1e-6
HG_HEADS = 4
HG_DK = 128
HG_DV = 128
HG_W = HG_HEADS * HG_DK
CV_W = 512
CV_KERNEL = 31
POOL_WINDOWS = (2, 4, 8, 16)
POOL_GROUPS = 4
POOL_GW = 128
POOL_W = POOL_GROUPS * POOL_GW
LRU_W = 512
LRU_HEADS = 8
LRU_HD = LRU_W // LRU_HEADS
LRU_CONV = 4
LRU_C = 8.0
N_BRANCH = 4
BRANCH_W = 512
IN_SIZES = (HG_W, HG_W, HG_W, HG_W, 2 * CV_W, POOL_W, LRU_W, LRU_W, N_BRANCH * D_MODEL)
IN_W = HG_W * 4 + 2 * CV_W + POOL_W + 2 * LRU_W + N_BRANCH * D_MODEL
XA_HEADS = 4
XA_HD = D_MODEL // XA_HEADS
D_FF = 4 * D_MODEL

kernel_name = 'hybrid_chunk_causal_streaming_encoder'


def rmsnorm(x, w):
    xf = x.astype(jnp.float32)
    y = xf * lax.rsqrt(jnp.mean(xf * xf, axis=-1, keepdims=True) + EPS) * w.astype(jnp.float32)
    return y.astype(x.dtype)


def layernorm(x, w, b):
    xf = x.astype(jnp.float32)
    mu = jnp.mean(xf, axis=-1, keepdims=True)
    var = jnp.mean(jnp.square(xf - mu), axis=-1, keepdims=True)
    return ((xf - mu) * lax.rsqrt(var + EPS) * w.astype(jnp.float32) + b.astype(jnp.float32)).astype(x.dtype)


def causal_depthwise_conv(z, w, b):
    k = w.shape[0]
    out = lax.conv_general_dilated(z, w[:, None, :].astype(z.dtype), window_strides=(1,),
                                   padding=[(k - 1, 0)], dimension_numbers=('NWC', 'WIO', 'NWC'),
                                   feature_group_count=z.shape[-1])
    return out + b.astype(z.dtype)


def split_in(proj):
    parts = []
    off = 0
    for n in IN_SIZES:
        parts.append(proj[..., off:off + n])
        off += n
    return parts


def hgrn2_mixer(q, f_pre, v, g, lb, norm_w):
    bsz, seq, _ = q.shape
    n = seq // CHUNK
    f32 = jnp.float32
    fg = lb + (1.0 - lb) * jax.nn.sigmoid(f_pre.astype(f32))
    kk = 1.0 - fg
    logf = jnp.log(fg)
    qf = jax.nn.silu(q.astype(f32))

    def heads(t):
        return t.reshape(bsz, n, CHUNK, HG_HEADS, -1).transpose(1, 0, 3, 2, 4)

    qc, kc, vc, lc = heads(qf), heads(kk), heads(v.astype(f32)), heads(logf)
    bc = jnp.cumsum(lc, axis=3)
    causal = jnp.tril(jnp.ones((CHUNK, CHUNK), dtype=bool))

    def step(state, inp):
        q_, k_, v_, b_ = inp
        diff = b_[:, :, :, None, :] - b_[:, :, None, :, :]
        decay = jnp.exp(jnp.where(causal[:, :, None], diff, -jnp.inf))
        att = jnp.einsum('bhtd,bhsd,bhtsd->bhts', q_, k_, decay)
        o = (jnp.einsum('bhts,bhsv->bhtv', att, v_)
             + jnp.einsum('bhtd,bhdv->bhtv', q_ * jnp.exp(b_), state))
        b_last = b_[:, :, -1:, :]
        state = (jnp.exp(b_last[:, :, 0, :])[..., None] * state
                 + jnp.einsum('bhsd,bhsv->bhdv', k_ * jnp.exp(b_last - b_), v_))
        return state, o

    state0 = jnp.zeros((bsz, HG_HEADS, HG_DK, HG_DV), f32)
    _, o = lax.scan(step, state0, (qc, kc, vc, bc))
    o = o.transpose(1, 0, 3, 2, 4).reshape(bsz, seq, HG_HEADS, HG_DV)
    gh = g.astype(f32).reshape(bsz, seq, HG_HEADS, HG_DV)
    o = rmsnorm(o, norm_w) * jax.nn.silu(gh)
    return o.reshape(bsz, seq, HG_HEADS * HG_DV).astype(q.dtype)


def conformer_conv(u, dw_w, dw_b, ln_w, ln_b):
    a, b = u[..., :CV_W], u[..., CV_W:]
    z = a * jax.nn.sigmoid(b)
    z = causal_depthwise_conv(z, dw_w, dw_b)
    return jax.nn.silu(layernorm(z, ln_w, ln_b))


def pool_mixer(u, w_grp, scale):
    bsz, seq, _ = u.shape
    uf = u.astype(jnp.float32).reshape(bsz, seq, POOL_GROUPS, POOL_GW)
    cs = jnp.pad(jnp.cumsum(uf, axis=1), ((0, 0), (1, 0), (0, 0), (0, 0)))
    t = jnp.arange(seq)
    pooled = []
    for gi, w in enumerate(POOL_WINDOWS):
        lo = jnp.maximum(t + 1 - w, 0)
        csg = cs[:, :, gi]
        cnt = (t + 1 - lo).astype(jnp.float32)
        pooled.append((csg[:, 1:] - csg[:, lo]) / cnt[None, :, None])
    pooled = jnp.stack(pooled, axis=2) - uf
    y = jnp.einsum('bsgc,gcd->bsgd', pooled, w_grp.astype(jnp.float32))
    y = y.reshape(bsz, seq, POOL_W) * scale.astype(jnp.float32)
    return y.astype(u.dtype)


def rglru_mixer(xb, yb, conv_w, conv_b, w_a, b_a, w_x, b_x, lam):
    bsz, seq, _ = xb.shape
    f32 = jnp.float32
    xc = causal_depthwise_conv(xb, conv_w, conv_b)
    xh = xc.astype(f32).reshape(bsz, seq, LRU_HEADS, LRU_HD)
    r = jax.nn.sigmoid(jnp.einsum('bshi,hij->bshj', xh, w_a.astype(f32)) + b_a.astype(f32))
    ig = jax.nn.sigmoid(jnp.einsum('bshi,hij->bshj', xh, w_x.astype(f32)) + b_x.astype(f32))
    log_a = -LRU_C * r * jax.nn.softplus(-lam.astype(f32))
    a = jnp.exp(log_a)
    bterm = jnp.sqrt(-jnp.expm1(2.0 * log_a)) * (ig * xh)

    def combine(c1, c2):
        a1, b1 = c1
        a2, b2 = c2
        return a1 * a2, a2 * b1 + b2

    _, hstate = lax.associative_scan(combine, (a, bterm), axis=1)
    hstate = hstate.reshape(bsz, seq, LRU_W)
    return (hstate * jax.nn.gelu(yb.astype(f32))).astype(xb.dtype)


def mem_attention(h, mem_n, w_q, w_kv, w_o):
    bsz, seq, _ = h.shape
    q = (h @ w_q).reshape(bsz, seq, XA_HEADS, XA_HD)
    kv = mem_n @ w_kv
    k = kv[..., :D_MODEL].reshape(bsz, -1, XA_HEADS, XA_HD)
    v = kv[..., D_MODEL:].reshape(bsz, -1, XA_HEADS, XA_HD)
    s = jnp.einsum('bshd,bmhd->bhsm', q, k).astype(jnp.float32) * (XA_HD ** -0.5)
    p = jax.nn.softmax(s, axis=-1).astype(v.dtype)
    o = jnp.einsum('bhsm,bmhd->bshd', p, v).reshape(bsz, seq, D_MODEL)
    return o @ w_o


def _fwd_setup_inputs(seed: int = 0) -> dict:
    key = jax.random.key(seed)
    ks = list(jax.random.split(key, 40))
    f32 = jnp.float32

    def nrm(shape, fan_in, scale=1.0):
        return jax.random.normal(ks.pop(), shape, f32) * (scale * fan_in ** -0.5)

    def gain(shape):
        return 1.0 + 0.02 * jax.random.normal(ks.pop(), shape, f32)

    def small(shape, s=0.02):
        return s * jax.random.normal(ks.pop(), shape, f32)

    u = jax.random.uniform(ks.pop(), (DEPTH, LRU_HEADS, LRU_HD), f32,
                           minval=-np.log(0.999) / LRU_C, maxval=-np.log(0.9) / LRU_C)
    lru_lambda = -jnp.log(jnp.expm1(u))
    return {
        'x': jax.random.normal(ks.pop(), (BATCH, SEQ, D_MODEL), f32),
        'mem': jax.random.normal(ks.pop(), (BATCH, N_MEM, D_MODEL), f32),
        'norm_mix_w': gain((DEPTH, D_MODEL)),
        'w_in': nrm((DEPTH, D_MODEL, IN_W), D_MODEL),
        'hg_lb_raw': small((DEPTH, HG_W), 0.5),
        'hg_norm_w': gain((DEPTH, HG_DV)),
        'cv_dw_w': nrm((DEPTH, CV_KERNEL, CV_W), CV_KERNEL),
        'cv_dw_b': small((DEPTH, CV_W)),
        'cv_ln_w': gain((DEPTH, CV_W)),
        'cv_ln_b': small((DEPTH, CV_W)),
        'pl_w': nrm((DEPTH, POOL_GROUPS, POOL_GW, POOL_GW), POOL_GW),
        'pl_scale': gain((DEPTH, POOL_W)),
        'lru_conv_w': nrm((DEPTH, LRU_CONV, LRU_W), LRU_CONV),
        'lru_conv_b': small((DEPTH, LRU_W)),
        'lru_wa': nrm((DEPTH, LRU_HEADS, LRU_HD, LRU_HD), LRU_HD),
        'lru_ba': small((DEPTH, LRU_HEADS, LRU_HD)),
        'lru_wx': nrm((DEPTH, LRU_HEADS, LRU_HD, LRU_HD), LRU_HD),
        'lru_bx': small((DEPTH, LRU_HEADS, LRU_HD)),
        'lru_lambda': lru_lambda,
        'gate_b': small((DEPTH, N_BRANCH, D_MODEL)),
        'w_branch': nrm((DEPTH, N_BRANCH, BRANCH_W, D_MODEL), BRANCH_W),
        'w_out': nrm((DEPTH, D_MODEL, D_MODEL), D_MODEL, 0.5),
        'norm_mem_w': gain((DEPTH, D_MODEL)),
        'mem_norm_w': gain((DEPTH, D_MODEL)),
        'xa_wq': nrm((DEPTH, D_MODEL, D_MODEL), D_MODEL),
        'xa_wkv': nrm((DEPTH, D_MODEL, 2 * D_MODEL), D_MODEL),
        'xa_wo': nrm((DEPTH, D_MODEL, D_MODEL), D_MODEL, 0.5),
        'norm_ffn_w': gain((DEPTH, D_MODEL)),
        'ffn_w1': nrm((DEPTH, D_MODEL, D_FF), D_MODEL),
        'ffn_w2': nrm((DEPTH, D_FF, D_MODEL), D_FF, 0.5),
        'final_norm_w': gain((D_MODEL,)),
    }


def _fwd_reference(x, mem, norm_mix_w, w_in, hg_lb_raw, hg_norm_w, cv_dw_w, cv_dw_b, cv_ln_w, cv_ln_b,
              pl_w, pl_scale, lru_conv_w, lru_conv_b, lru_wa, lru_ba, lru_wx, lru_bx, lru_lambda,
              gate_b, w_branch, w_out, norm_mem_w, mem_norm_w, xa_wq, xa_wkv, xa_wo,
              norm_ffn_w, ffn_w1, ffn_w2, final_norm_w):
    bsz, seq, _ = x.shape
    lb = jnp.cumsum(jax.nn.softmax(hg_lb_raw.astype(jnp.float32), axis=0), axis=0)
    lb = lb - lb[0:1]
    for l in range(DEPTH):
        h = rmsnorm(x, norm_mix_w[l])
        proj = h @ w_in[l]
        hq, hf, hv, hg, cv_u, pl_u, lru_x, lru_y, gate_pre = split_in(proj)
        b_hg = hgrn2_mixer(hq, hf, hv, hg, lb[l], hg_norm_w[l])
        b_cv = conformer_conv(cv_u, cv_dw_w[l], cv_dw_b[l], cv_ln_w[l], cv_ln_b[l])
        b_pl = pool_mixer(pl_u, pl_w[l], pl_scale[l])
        b_lru = rglru_mixer(lru_x, lru_y, lru_conv_w[l], lru_conv_b[l], lru_wa[l], lru_ba[l],
                            lru_wx[l], lru_bx[l], lru_lambda[l])
        branches = jnp.stack([b_hg, b_cv, b_pl, b_lru], axis=2)
        up = jnp.einsum('bskm,kmd->bskd', branches, w_branch[l])
        gates = jax.nn.sigmoid(gate_pre.reshape(bsz, seq, N_BRANCH, D_MODEL) + gate_b[l])
        merged = jnp.sum(gates * up, axis=2)
        x = x + (merged @ w_out[l]).astype(x.dtype)
        h = rmsnorm(x, norm_mem_w[l])
        x = x + mem_attention(h, rmsnorm(mem, mem_norm_w[l]), xa_wq[l], xa_wkv[l], xa_wo[l]).astype(x.dtype)
        h = rmsnorm(x, norm_ffn_w[l])
        x = x + (jnp.square(jax.nn.relu(h @ ffn_w1[l])) @ ffn_w2[l]).astype(x.dtype)
    return rmsnorm(x, final_norm_w)


import jax as _jax
import jax.numpy as _jnp

TWIN_FORMAT = 'train_step'
FWD_PARAMS = ['x', 'mem', 'norm_mix_w', 'w_in', 'hg_lb_raw', 'hg_norm_w', 'cv_dw_w', 'cv_dw_b', 'cv_ln_w', 'cv_ln_b', 'pl_w', 'pl_scale', 'lru_conv_w', 'lru_conv_b', 'lru_wa', 'lru_ba', 'lru_wx', 'lru_bx', 'lru_lambda', 'gate_b', 'w_branch', 'w_out', 'norm_mem_w', 'mem_norm_w', 'xa_wq', 'xa_wkv', 'xa_wo', 'norm_ffn_w', 'ffn_w1', 'ffn_w2', 'final_norm_w']
TWIN_WEIGHTS = ['norm_mix_w', 'w_in', 'hg_lb_raw', 'hg_norm_w', 'cv_dw_w', 'cv_dw_b', 'cv_ln_w', 'cv_ln_b', 'pl_w', 'pl_scale', 'lru_conv_w', 'lru_conv_b', 'lru_wa', 'lru_ba', 'lru_wx', 'lru_bx', 'lru_lambda', 'gate_b', 'w_branch', 'w_out', 'norm_mem_w', 'mem_norm_w', 'xa_wq', 'xa_wkv', 'xa_wo', 'norm_ffn_w', 'ffn_w1', 'ffn_w2', 'final_norm_w']
TWIN_DIFF_INPUT = 'x'
TWIN_INPUTS = ['x', 'mem', 'norm_mix_w', 'w_in', 'hg_lb_raw', 'hg_norm_w', 'cv_dw_w', 'cv_dw_b', 'cv_ln_w', 'cv_ln_b', 'pl_w', 'pl_scale', 'lru_conv_w', 'lru_conv_b', 'lru_wa', 'lru_ba', 'lru_wx', 'lru_bx', 'lru_lambda', 'gate_b', 'w_branch', 'w_out', 'norm_mem_w', 'mem_norm_w', 'xa_wq', 'xa_wkv', 'xa_wo', 'norm_ffn_w', 'ffn_w1', 'ffn_w2', 'final_norm_w', 'loss_target', 'm_norm_mix_w', 'm_w_in', 'm_hg_lb_raw', 'm_hg_norm_w', 'm_cv_dw_w', 'm_cv_dw_b', 'm_cv_ln_w', 'm_cv_ln_b', 'm_pl_w', 'm_pl_scale', 'm_lru_conv_w', 'm_lru_conv_b', 'm_lru_wa', 'm_lru_ba', 'm_lru_wx', 'm_lru_bx', 'm_lru_lambda', 'm_gate_b', 'm_w_branch', 'm_w_out', 'm_norm_mem_w', 'm_mem_norm_w', 'm_xa_wq', 'm_xa_wkv', 'm_xa_wo', 'm_norm_ffn_w', 'm_ffn_w1', 'm_ffn_w2', 'm_final_norm_w', 'v_norm_mix_w', 'v_w_in', 'v_hg_lb_raw', 'v_hg_norm_w', 'v_cv_dw_w', 'v_cv_dw_b', 'v_cv_ln_w', 'v_cv_ln_b', 'v_pl_w', 'v_pl_scale', 'v_lru_conv_w', 'v_lru_conv_b', 'v_lru_wa', 'v_lru_ba', 'v_lru_wx', 'v_lru_bx', 'v_lru_lambda', 'v_gate_b', 'v_w_branch', 'v_w_out', 'v_norm_mem_w', 'v_mem_norm_w', 'v_xa_wq', 'v_xa_wkv', 'v_xa_wo', 'v_norm_ffn_w', 'v_ffn_w1', 'v_ffn_w2', 'v_final_norm_w']
TWIN_OUTPUTS = ['loss', 'grad_x', 'grad_norm_mix_w', 'grad_w_in', 'grad_hg_lb_raw', 'grad_hg_norm_w', 'grad_cv_dw_w', 'grad_cv_dw_b', 'grad_cv_ln_w', 'grad_cv_ln_b', 'grad_pl_w', 'grad_pl_scale', 'grad_lru_conv_w', 'grad_lru_conv_b', 'grad_lru_wa', 'grad_lru_ba', 'grad_lru_wx', 'grad_lru_bx', 'grad_lru_lambda', 'grad_gate_b', 'grad_w_branch', 'grad_w_out', 'grad_norm_mem_w', 'grad_mem_norm_w', 'grad_xa_wq', 'grad_xa_wkv', 'grad_xa_wo', 'grad_norm_ffn_w', 'grad_ffn_w1', 'grad_ffn_w2', 'grad_final_norm_w', 'delta_norm_mix_w', 'delta_w_in', 'delta_hg_lb_raw', 'delta_hg_norm_w', 'delta_cv_dw_w', 'delta_cv_dw_b', 'delta_cv_ln_w', 'delta_cv_ln_b', 'delta_pl_w', 'delta_pl_scale', 'delta_lru_conv_w', 'delta_lru_conv_b', 'delta_lru_wa', 'delta_lru_ba', 'delta_lru_wx', 'delta_lru_bx', 'delta_lru_lambda', 'delta_gate_b', 'delta_w_branch', 'delta_w_out', 'delta_norm_mem_w', 'delta_mem_norm_w', 'delta_xa_wq', 'delta_xa_wkv', 'delta_xa_wo', 'delta_norm_ffn_w', 'delta_ffn_w1', 'delta_ffn_w2', 'delta_final_norm_w', 'new_m_norm_mix_w', 'new_m_w_in', 'new_m_hg_lb_raw', 'new_m_hg_norm_w', 'new_m_cv_dw_w', 'new_m_cv_dw_b', 'new_m_cv_ln_w', 'new_m_cv_ln_b', 'new_m_pl_w', 'new_m_pl_scale', 'new_m_lru_conv_w', 'new_m_lru_conv_b', 'new_m_lru_wa', 'new_m_lru_ba', 'new_m_lru_wx', 'new_m_lru_bx', 'new_m_lru_lambda', 'new_m_gate_b', 'new_m_w_branch', 'new_m_w_out', 'new_m_norm_mem_w', 'new_m_mem_norm_w', 'new_m_xa_wq', 'new_m_xa_wkv', 'new_m_xa_wo', 'new_m_norm_ffn_w', 'new_m_ffn_w1', 'new_m_ffn_w2', 'new_m_final_norm_w', 'new_v_norm_mix_w', 'new_v_w_in', 'new_v_hg_lb_raw', 'new_v_hg_norm_w', 'new_v_cv_dw_w', 'new_v_cv_dw_b', 'new_v_cv_ln_w', 'new_v_cv_ln_b', 'new_v_pl_w', 'new_v_pl_scale', 'new_v_lru_conv_w', 'new_v_lru_conv_b', 'new_v_lru_wa', 'new_v_lru_ba', 'new_v_lru_wx', 'new_v_lru_bx', 'new_v_lru_lambda', 'new_v_gate_b', 'new_v_w_branch', 'new_v_w_out', 'new_v_norm_mem_w', 'new_v_mem_norm_w', 'new_v_xa_wq', 'new_v_xa_wkv', 'new_v_xa_wo', 'new_v_norm_ffn_w', 'new_v_ffn_w1', 'new_v_ffn_w2', 'new_v_final_norm_w']
TWIN_LEAF_KINDS = {'loss': 'loss', 'grad_x': 'grad_x', 'grad_norm_mix_w': 'grad_w', 'grad_w_in': 'grad_w', 'grad_hg_lb_raw': 'grad_w', 'grad_hg_norm_w': 'grad_w', 'grad_cv_dw_w': 'grad_w', 'grad_cv_dw_b': 'grad_w', 'grad_cv_ln_w': 'grad_w', 'grad_cv_ln_b': 'grad_w', 'grad_pl_w': 'grad_w', 'grad_pl_scale': 'grad_w', 'grad_lru_conv_w': 'grad_w', 'grad_lru_conv_b': 'grad_w', 'grad_lru_wa': 'grad_w', 'grad_lru_ba': 'grad_w', 'grad_lru_wx': 'grad_w', 'grad_lru_bx': 'grad_w', 'grad_lru_lambda': 'grad_w', 'grad_gate_b': 'grad_w', 'grad_w_branch': 'grad_w', 'grad_w_out': 'grad_w', 'grad_norm_mem_w': 'grad_w', 'grad_mem_norm_w': 'grad_w', 'grad_xa_wq': 'grad_w', 'grad_xa_wkv': 'grad_w', 'grad_xa_wo': 'grad_w', 'grad_norm_ffn_w': 'grad_w', 'grad_ffn_w1': 'grad_w', 'grad_ffn_w2': 'grad_w', 'grad_final_norm_w': 'grad_w', 'delta_norm_mix_w': 'delta_w', 'delta_w_in': 'delta_w', 'delta_hg_lb_raw': 'delta_w', 'delta_hg_norm_w': 'delta_w', 'delta_cv_dw_w': 'delta_w', 'delta_cv_dw_b': 'delta_w', 'delta_cv_ln_w': 'delta_w', 'delta_cv_ln_b': 'delta_w', 'delta_pl_w': 'delta_w', 'delta_pl_scale': 'delta_w', 'delta_lru_conv_w': 'delta_w', 'delta_lru_conv_b': 'delta_w', 'delta_lru_wa': 'delta_w', 'delta_lru_ba': 'delta_w', 'delta_lru_wx': 'delta_w', 'delta_lru_bx': 'delta_w', 'delta_lru_lambda': 'delta_w', 'delta_gate_b': 'delta_w', 'delta_w_branch': 'delta_w', 'delta_w_out': 'delta_w', 'delta_norm_mem_w': 'delta_w', 'delta_mem_norm_w': 'delta_w', 'delta_xa_wq': 'delta_w', 'delta_xa_wkv': 'delta_w', 'delta_xa_wo': 'delta_w', 'delta_norm_ffn_w': 'delta_w', 'delta_ffn_w1': 'delta_w', 'delta_ffn_w2': 'delta_w', 'delta_final_norm_w': 'delta_w', 'new_m_norm_mix_w': 'new_m', 'new_m_w_in': 'new_m', 'new_m_hg_lb_raw': 'new_m', 'new_m_hg_norm_w': 'new_m', 'new_m_cv_dw_w': 'new_m', 'new_m_cv_dw_b': 'new_m', 'new_m_cv_ln_w': 'new_m', 'new_m_cv_ln_b': 'new_m', 'new_m_pl_w': 'new_m', 'new_m_pl_scale': 'new_m', 'new_m_lru_conv_w': 'new_m', 'new_m_lru_conv_b': 'new_m', 'new_m_lru_wa': 'new_m', 'new_m_lru_ba': 'new_m', 'new_m_lru_wx': 'new_m', 'new_m_lru_bx': 'new_m', 'new_m_lru_lambda': 'new_m', 'new_m_gate_b': 'new_m', 'new_m_w_branch': 'new_m', 'new_m_w_out': 'new_m', 'new_m_norm_mem_w': 'new_m', 'new_m_mem_norm_w': 'new_m', 'new_m_xa_wq': 'new_m', 'new_m_xa_wkv': 'new_m', 'new_m_xa_wo': 'new_m', 'new_m_norm_ffn_w': 'new_m', 'new_m_ffn_w1': 'new_m', 'new_m_ffn_w2': 'new_m', 'new_m_final_norm_w': 'new_m', 'new_v_norm_mix_w': 'new_v', 'new_v_w_in': 'new_v', 'new_v_hg_lb_raw': 'new_v', 'new_v_hg_norm_w': 'new_v', 'new_v_cv_dw_w': 'new_v', 'new_v_cv_dw_b': 'new_v', 'new_v_cv_ln_w': 'new_v', 'new_v_cv_ln_b': 'new_v', 'new_v_pl_w': 'new_v', 'new_v_pl_scale': 'new_v', 'new_v_lru_conv_w': 'new_v', 'new_v_lru_conv_b': 'new_v', 'new_v_lru_wa': 'new_v', 'new_v_lru_ba': 'new_v', 'new_v_lru_wx': 'new_v', 'new_v_lru_bx': 'new_v', 'new_v_lru_lambda': 'new_v', 'new_v_gate_b': 'new_v', 'new_v_w_branch': 'new_v', 'new_v_w_out': 'new_v', 'new_v_norm_mem_w': 'new_v', 'new_v_mem_norm_w': 'new_v', 'new_v_xa_wq': 'new_v', 'new_v_xa_wkv': 'new_v', 'new_v_xa_wo': 'new_v', 'new_v_norm_ffn_w': 'new_v', 'new_v_ffn_w1': 'new_v', 'new_v_ffn_w2': 'new_v', 'new_v_final_norm_w': 'new_v'}


def _forward(args):
    return _fwd_reference(*[args[k] for k in FWD_PARAMS])


def _output_shape():
    def fwd():
        inp = _fwd_setup_inputs(0)
        return _fwd_reference(*[inp[k] for k in FWD_PARAMS])
    out = _jax.eval_shape(fwd)
    return out.shape, out.dtype

N_MICROBATCH = 1
ADAM_LR = 0.001
ADAM_B1 = 0.9
ADAM_B2 = 0.999
ADAM_EPS = 1e-08
ADAM_WD = 0.01
ADAM_STEP = 10
PER_EXAMPLE_BATCH_AXIS = {'x': 0, 'mem': 0, 'loss_target': 0}
SHARED_INPUTS = []
_WEIGHT_DTYPES = {'norm_mix_w': _jnp.float32, 'w_in': _jnp.float32, 'hg_lb_raw': _jnp.float32, 'hg_norm_w': _jnp.float32, 'cv_dw_w': _jnp.float32, 'cv_dw_b': _jnp.float32, 'cv_ln_w': _jnp.float32, 'cv_ln_b': _jnp.float32, 'pl_w': _jnp.float32, 'pl_scale': _jnp.float32, 'lru_conv_w': _jnp.float32, 'lru_conv_b': _jnp.float32, 'lru_wa': _jnp.float32, 'lru_ba': _jnp.float32, 'lru_wx': _jnp.float32, 'lru_bx': _jnp.float32, 'lru_lambda': _jnp.float32, 'gate_b': _jnp.float32, 'w_branch': _jnp.float32, 'w_out': _jnp.float32, 'norm_mem_w': _jnp.float32, 'mem_norm_w': _jnp.float32, 'xa_wq': _jnp.float32, 'xa_wkv': _jnp.float32, 'xa_wo': _jnp.float32, 'norm_ffn_w': _jnp.float32, 'ffn_w1': _jnp.float32, 'ffn_w2': _jnp.float32, 'final_norm_w': _jnp.float32}
MOMENT_SCALE = {'norm_mix_w': 9.014076e-02, 'w_in': 2.956072e-02, 'hg_lb_raw': 2.223616e-03, 'hg_norm_w': 8.237598e-02, 'cv_dw_w': 3.629097e-02, 'cv_dw_b': 9.667111e-02, 'cv_ln_w': 5.111176e-02, 'cv_ln_b': 5.613069e-02, 'pl_w': 4.940857e-02, 'pl_scale': 4.937355e-02, 'lru_conv_w': 5.976869e-02, 'lru_conv_b': 3.102597e-01, 'lru_wa': 1.233927e-02, 'lru_ba': 1.295601e-02, 'lru_wx': 2.414616e-02, 'lru_bx': 2.610417e-02, 'lru_lambda': 3.021233e-02, 'gate_b': 1.335700e-02, 'w_branch': 3.277441e-02, 'w_out': 1.322128e-01, 'norm_mem_w': 7.851606e-03, 'mem_norm_w': 1.337102e-02, 'xa_wq': 7.889108e-03, 'xa_wkv': 8.742960e-03, 'xa_wo': 1.899242e-02, 'norm_ffn_w': 9.323638e-02, 'ffn_w1': 4.687718e-02, 'ffn_w2': 2.047297e-01, 'final_norm_w': 3.238231e+01}


def _to_microbatches(a, axis):
    t = _jnp.moveaxis(a, axis, 0)
    t = t.reshape((N_MICROBATCH, t.shape[0] // N_MICROBATCH) + t.shape[1:])
    return _jnp.moveaxis(t, 1, axis + 1)


def setup_inputs(seed: int = 0) -> dict:
    inp = _fwd_setup_inputs(seed)
    key = _jax.random.fold_in(_jax.random.key(seed), 7919)
    shape, _ = _output_shape()
    out = dict(inp)
    out["loss_target"] = _jax.random.normal(_jax.random.fold_in(key, 0), shape, _jnp.float32)
    for i, name in enumerate(TWIN_WEIGHTS):
        w = inp[name].astype(_jnp.float32)
        if MOMENT_SCALE is None:
            s = _jnp.sqrt(_jnp.mean(_jnp.square(w)) + 1e-30)
        else:
            s = MOMENT_SCALE[name]
        km, kv = _jax.random.split(_jax.random.fold_in(key, i + 1))
        out[name] = w
        out["m_" + name] = s * _jax.random.normal(km, w.shape, _jnp.float32)
        out["v_" + name] = (s * s) * _jax.random.uniform(kv, w.shape, _jnp.float32, 0.5, 1.5)
    if N_MICROBATCH > 1:
        for name, axis in PER_EXAMPLE_BATCH_AXIS.items():
            out[name] = _to_microbatches(out[name], axis)
    return {'x': out['x'], 'mem': out['mem'], 'norm_mix_w': out['norm_mix_w'], 'w_in': out['w_in'], 'hg_lb_raw': out['hg_lb_raw'], 'hg_norm_w': out['hg_norm_w'], 'cv_dw_w': out['cv_dw_w'], 'cv_dw_b': out['cv_dw_b'], 'cv_ln_w': out['cv_ln_w'], 'cv_ln_b': out['cv_ln_b'], 'pl_w': out['pl_w'], 'pl_scale': out['pl_scale'], 'lru_conv_w': out['lru_conv_w'], 'lru_conv_b': out['lru_conv_b'], 'lru_wa': out['lru_wa'], 'lru_ba': out['lru_ba'], 'lru_wx': out['lru_wx'], 'lru_bx': out['lru_bx'], 'lru_lambda': out['lru_lambda'], 'gate_b': out['gate_b'], 'w_branch': out['w_branch'], 'w_out': out['w_out'], 'norm_mem_w': out['norm_mem_w'], 'mem_norm_w': out['mem_norm_w'], 'xa_wq': out['xa_wq'], 'xa_wkv': out['xa_wkv'], 'xa_wo': out['xa_wo'], 'norm_ffn_w': out['norm_ffn_w'], 'ffn_w1': out['ffn_w1'], 'ffn_w2': out['ffn_w2'], 'final_norm_w': out['final_norm_w'], 'loss_target': out['loss_target'], 'm_norm_mix_w': out['m_norm_mix_w'], 'm_w_in': out['m_w_in'], 'm_hg_lb_raw': out['m_hg_lb_raw'], 'm_hg_norm_w': out['m_hg_norm_w'], 'm_cv_dw_w': out['m_cv_dw_w'], 'm_cv_dw_b': out['m_cv_dw_b'], 'm_cv_ln_w': out['m_cv_ln_w'], 'm_cv_ln_b': out['m_cv_ln_b'], 'm_pl_w': out['m_pl_w'], 'm_pl_scale': out['m_pl_scale'], 'm_lru_conv_w': out['m_lru_conv_w'], 'm_lru_conv_b': out['m_lru_conv_b'], 'm_lru_wa': out['m_lru_wa'], 'm_lru_ba': out['m_lru_ba'], 'm_lru_wx': out['m_lru_wx'], 'm_lru_bx': out['m_lru_bx'], 'm_lru_lambda': out['m_lru_lambda'], 'm_gate_b': out['m_gate_b'], 'm_w_branch': out['m_w_branch'], 'm_w_out': out['m_w_out'], 'm_norm_mem_w': out['m_norm_mem_w'], 'm_mem_norm_w': out['m_mem_norm_w'], 'm_xa_wq': out['m_xa_wq'], 'm_xa_wkv': out['m_xa_wkv'], 'm_xa_wo': out['m_xa_wo'], 'm_norm_ffn_w': out['m_norm_ffn_w'], 'm_ffn_w1': out['m_ffn_w1'], 'm_ffn_w2': out['m_ffn_w2'], 'm_final_norm_w': out['m_final_norm_w'], 'v_norm_mix_w': out['v_norm_mix_w'], 'v_w_in': out['v_w_in'], 'v_hg_lb_raw': out['v_hg_lb_raw'], 'v_hg_norm_w': out['v_hg_norm_w'], 'v_cv_dw_w': out['v_cv_dw_w'], 'v_cv_dw_b': out['v_cv_dw_b'], 'v_cv_ln_w': out['v_cv_ln_w'], 'v_cv_ln_b': out['v_cv_ln_b'], 'v_pl_w': out['v_pl_w'], 'v_pl_scale': out['v_pl_scale'], 'v_lru_conv_w': out['v_lru_conv_w'], 'v_lru_conv_b': out['v_lru_conv_b'], 'v_lru_wa': out['v_lru_wa'], 'v_lru_ba': out['v_lru_ba'], 'v_lru_wx': out['v_lru_wx'], 'v_lru_bx': out['v_lru_bx'], 'v_lru_lambda': out['v_lru_lambda'], 'v_gate_b': out['v_gate_b'], 'v_w_branch': out['v_w_branch'], 'v_w_out': out['v_w_out'], 'v_norm_mem_w': out['v_norm_mem_w'], 'v_mem_norm_w': out['v_mem_norm_w'], 'v_xa_wq': out['v_xa_wq'], 'v_xa_wkv': out['v_xa_wkv'], 'v_xa_wo': out['v_xa_wo'], 'v_norm_ffn_w': out['v_norm_ffn_w'], 'v_ffn_w1': out['v_ffn_w1'], 'v_ffn_w2': out['v_ffn_w2'], 'v_final_norm_w': out['v_final_norm_w']}


def _loss(weights, diff, rest, loss_target):
    with _jax.named_scope("forward"):
        args = {**rest, TWIN_DIFF_INPUT: diff, **{k: w.astype(_WEIGHT_DTYPES[k]) for k, w in weights.items()}}
        y = _forward(args)
    with _jax.named_scope("loss_head"):
        err = _jnp.square(y.astype(_jnp.float32) - loss_target)
        return 0.5 * _jnp.sum(_jnp.mean(err, axis=-1)) if err.ndim else 0.5 * err


def _adamw(w, g, m, v):
    m = ADAM_B1 * m + (1.0 - ADAM_B1) * g
    v = ADAM_B2 * v + (1.0 - ADAM_B2) * _jnp.square(g)
    m_hat = m / (1.0 - ADAM_B1 ** ADAM_STEP)
    v_hat = v / (1.0 - ADAM_B2 ** ADAM_STEP)
    delta = -ADAM_LR * (m_hat / (_jnp.sqrt(v_hat) + ADAM_EPS) + ADAM_WD * w)
    return delta, m, v


def reference(x, mem, norm_mix_w, w_in, hg_lb_raw, hg_norm_w, cv_dw_w, cv_dw_b, cv_ln_w, cv_ln_b, pl_w, pl_scale, lru_conv_w, lru_conv_b, lru_wa, lru_ba, lru_wx, lru_bx, lru_lambda, gate_b, w_branch, w_out, norm_mem_w, mem_norm_w, xa_wq, xa_wkv, xa_wo, norm_ffn_w, ffn_w1, ffn_w2, final_norm_w, loss_target, m_norm_mix_w, m_w_in, m_hg_lb_raw, m_hg_norm_w, m_cv_dw_w, m_cv_dw_b, m_cv_ln_w, m_cv_ln_b, m_pl_w, m_pl_scale, m_lru_conv_w, m_lru_conv_b, m_lru_wa, m_lru_ba, m_lru_wx, m_lru_bx, m_lru_lambda, m_gate_b, m_w_branch, m_w_out, m_norm_mem_w, m_mem_norm_w, m_xa_wq, m_xa_wkv, m_xa_wo, m_norm_ffn_w, m_ffn_w1, m_ffn_w2, m_final_norm_w, v_norm_mix_w, v_w_in, v_hg_lb_raw, v_hg_norm_w, v_cv_dw_w, v_cv_dw_b, v_cv_ln_w, v_cv_ln_b, v_pl_w, v_pl_scale, v_lru_conv_w, v_lru_conv_b, v_lru_wa, v_lru_ba, v_lru_wx, v_lru_bx, v_lru_lambda, v_gate_b, v_w_branch, v_w_out, v_norm_mem_w, v_mem_norm_w, v_xa_wq, v_xa_wkv, v_xa_wo, v_norm_ffn_w, v_ffn_w1, v_ffn_w2, v_final_norm_w):
    given = dict(x=x, mem=mem, norm_mix_w=norm_mix_w, w_in=w_in, hg_lb_raw=hg_lb_raw, hg_norm_w=hg_norm_w, cv_dw_w=cv_dw_w, cv_dw_b=cv_dw_b, cv_ln_w=cv_ln_w, cv_ln_b=cv_ln_b, pl_w=pl_w, pl_scale=pl_scale, lru_conv_w=lru_conv_w, lru_conv_b=lru_conv_b, lru_wa=lru_wa, lru_ba=lru_ba, lru_wx=lru_wx, lru_bx=lru_bx, lru_lambda=lru_lambda, gate_b=gate_b, w_branch=w_branch, w_out=w_out, norm_mem_w=norm_mem_w, mem_norm_w=mem_norm_w, xa_wq=xa_wq, xa_wkv=xa_wkv, xa_wo=xa_wo, norm_ffn_w=norm_ffn_w, ffn_w1=ffn_w1, ffn_w2=ffn_w2, final_norm_w=final_norm_w, loss_target=loss_target, m_norm_mix_w=m_norm_mix_w, m_w_in=m_w_in, m_hg_lb_raw=m_hg_lb_raw, m_hg_norm_w=m_hg_norm_w, m_cv_dw_w=m_cv_dw_w, m_cv_dw_b=m_cv_dw_b, m_cv_ln_w=m_cv_ln_w, m_cv_ln_b=m_cv_ln_b, m_pl_w=m_pl_w, m_pl_scale=m_pl_scale, m_lru_conv_w=m_lru_conv_w, m_lru_conv_b=m_lru_conv_b, m_lru_wa=m_lru_wa, m_lru_ba=m_lru_ba, m_lru_wx=m_lru_wx, m_lru_bx=m_lru_bx, m_lru_lambda=m_lru_lambda, m_gate_b=m_gate_b, m_w_branch=m_w_branch, m_w_out=m_w_out, m_norm_mem_w=m_norm_mem_w, m_mem_norm_w=m_mem_norm_w, m_xa_wq=m_xa_wq, m_xa_wkv=m_xa_wkv, m_xa_wo=m_xa_wo, m_norm_ffn_w=m_norm_ffn_w, m_ffn_w1=m_ffn_w1, m_ffn_w2=m_ffn_w2, m_final_norm_w=m_final_norm_w, v_norm_mix_w=v_norm_mix_w, v_w_in=v_w_in, v_hg_lb_raw=v_hg_lb_raw, v_hg_norm_w=v_hg_norm_w, v_cv_dw_w=v_cv_dw_w, v_cv_dw_b=v_cv_dw_b, v_cv_ln_w=v_cv_ln_w, v_cv_ln_b=v_cv_ln_b, v_pl_w=v_pl_w, v_pl_scale=v_pl_scale, v_lru_conv_w=v_lru_conv_w, v_lru_conv_b=v_lru_conv_b, v_lru_wa=v_lru_wa, v_lru_ba=v_lru_ba, v_lru_wx=v_lru_wx, v_lru_bx=v_lru_bx, v_lru_lambda=v_lru_lambda, v_gate_b=v_gate_b, v_w_branch=v_w_branch, v_w_out=v_w_out, v_norm_mem_w=v_norm_mem_w, v_mem_norm_w=v_mem_norm_w, v_xa_wq=v_xa_wq, v_xa_wkv=v_xa_wkv, v_xa_wo=v_xa_wo, v_norm_ffn_w=v_norm_ffn_w, v_ffn_w1=v_ffn_w1, v_ffn_w2=v_ffn_w2, v_final_norm_w=v_final_norm_w)
    weights = {n: given[n] for n in TWIN_WEIGHTS}
    shared = {n: given[n] for n in SHARED_INPUTS}
    per_example = {n: given[n] for n in ['x', 'mem']}
    grad_fn = _jax.value_and_grad(_loss, argnums=(0, 1))

    def one_microbatch(ex, loss_target):
        ex = dict(ex)
        diff = ex.pop(TWIN_DIFF_INPUT)
        return grad_fn(weights, diff, {**shared, **ex}, loss_target)

    if N_MICROBATCH == 1:
        loss, (grad_w, grad_x) = one_microbatch(per_example, given["loss_target"])
    else:
        def body(carry, xs):
            loss_sum, grad_sum = carry
            l_k, (gw_k, gx_k) = one_microbatch(xs[0], xs[1])
            with _jax.named_scope("update"):
                return (loss_sum + l_k, _jax.tree.map(_jnp.add, grad_sum, gw_k)), gx_k

        init = (_jnp.zeros((), _jnp.float32), _jax.tree.map(_jnp.zeros_like, weights))
        (loss, grad_w), grad_x = _jax.lax.scan(body, init, (per_example, given["loss_target"]))
    with _jax.named_scope("update"):
        delta_w, new_m, new_v = {}, {}, {}
        for n in TWIN_WEIGHTS:
            delta_w[n], new_m[n], new_v[n] = _adamw(weights[n], grad_w[n], given["m_" + n], given["v_" + n])
    return (loss, grad_x, *[grad_w[n] for n in TWIN_WEIGHTS], *[delta_w[n] for n in TWIN_WEIGHTS],
            *[new_m[n] for n in TWIN_WEIGHTS], *[new_v[n] for n in TWIN_WEIGHTS])
```

```python
import functools
import math

import jax
import jax.numpy as jnp
from jax import lax
from jax.experimental import pallas as pl
from jax.experimental.pallas import tpu as pltpu

F32 = jnp.float32
BF16 = jnp.bfloat16
MESH = pl.DeviceIdType.MESH
ANY = pl.BlockSpec(memory_space=pl.ANY)

D_MODEL = 1024
DEPTH = 4
CHUNK = 64
SUB = 16
EPS = 1e-6
HG_HEADS, HG_D = 4, 128
CV_W, CV_K = 512, 31
CV_HALO = 32
POOL_WINDOWS = (2, 4, 8, 16)
POOL_HALO = 16
LRU_W, LRU_HEADS, LRU_HD, LRU_CONV = 512, 8, 64, 4
LRU_HALO = 8
LRU_C = 8.0
MIX_W = 4608
IN_W = 8704
XA_HEADS, XA_HD = 4, 256
D_FF = 4096
FF_CHUNK = 1024
ADAM_LR, ADAM_B1, ADAM_B2, ADAM_EPS, ADAM_WD, ADAM_STEP = 0.001, 0.9, 0.999, 1e-08, 0.01, 10
VMEM_LIMIT = 56 * 1024 * 1024
EXP_CLAMP = 80.0
HI = lax.Precision.HIGHEST


def _params(sem=None):
    return pltpu.CompilerParams(dimension_semantics=sem, vmem_limit_bytes=VMEM_LIMIT)


def _sigmoid(x):
    return 1.0 / (1.0 + jnp.exp(-x))


def _dsilu(x, s):
    return s * (1.0 + x * (1.0 - s))


_GELU_C = math.sqrt(2.0 / math.pi)


def _gelu_parts(x):
    t = jnp.tanh(_GELU_C * (x + 0.044715 * x * x * x))
    g = 0.5 * x * (1.0 + t)
    dg = 0.5 * (1.0 + t) + 0.5 * x * (1.0 - t * t) * _GELU_C * (1.0 + 3 * 0.044715 * x * x)
    return g, dg


def _dot(a, b, dims, precision=None):
    return lax.dot_general(a, b, (dims, ((), ())), precision=precision, preferred_element_type=F32)


def _nn(a, b, **k):
    return _dot(a, b, ((1,), (0,)), **k)


def _nt(a, b, **k):
    return _dot(a, b, ((1,), (1,)), **k)


def _tn(a, b, **k):
    return _dot(a, b, ((0,), (0,)), **k)


def _rms_fwd(x, w):
    r = lax.rsqrt(jnp.mean(x * x, axis=-1, keepdims=True) + EPS)
    return x * r * w, r


def _rms_bwd(x, r, w, dy):
    xr = x * r
    g = dy * w
    dx = r * (g - xr * jnp.mean(g * xr, axis=-1, keepdims=True))
    return dx, jnp.sum(dy * xr, axis=0, keepdims=True)


def _lw(shape, index, layer):
    return pl.BlockSpec((None,) + tuple(shape), lambda *g: (layer,) + tuple(index(*g)))


def _mm(a, b, mode, out_dtype, name, tm=512, tn=512, b_col0=0, n=None, layer=None):
    bs = b.shape if layer is None else b.shape[1:]
    if mode == "nn":
        m, k = a.shape
        n = bs[1] if n is None else n
    elif mode == "nt":
        m, k = a.shape
        n = bs[0] if n is None else n
    else:
        k, m = a.shape
        n = bs[1] if n is None else n
    tm, tn = min(tm, m), min(tn, n)
    assert m % tm == 0 and n % tn == 0 and b_col0 % tn == 0
    off = b_col0 // tn

    def body(a_ref, b_ref, o_ref):
        av, bv = a_ref[...].astype(BF16), b_ref[...].astype(BF16)
        o_ref[...] = (_nn if mode == "nn" else _nt if mode == "nt" else _tn)(av, bv).astype(out_dtype)

    def bspec(shape, index):
        return pl.BlockSpec(shape, index) if layer is None else _lw(shape, index, layer)

    if mode == "tn":
        grid = (m // tm, n // tn)
        a_spec = pl.BlockSpec((k, tm), lambda i, j: (0, i))
        b_spec = bspec((k, tn), lambda i, j: (0, j + off))
        o_spec = pl.BlockSpec((tm, tn), lambda i, j: (i, j))
    else:
        grid = (n // tn, m // tm)
        a_spec = pl.BlockSpec((tm, k), lambda j, i: (i, 0))
        if mode == "nn":
            b_spec = bspec((k, tn), lambda j, i: (0, j + off))
        else:
            b_spec = bspec((tn, k), lambda j, i: (j + off, 0))
        o_spec = pl.BlockSpec((tm, tn), lambda j, i: (i, j))
    return pl.pallas_call(
        body, out_shape=jax.ShapeDtypeStruct((m, n), out_dtype), grid=grid,
        in_specs=[a_spec, b_spec], out_specs=o_spec, name=name,
        compiler_params=_params(("parallel", "parallel")),
    )(a, b)


def _norm_fwd(x, w, name, ts=512):
    s, d = x.shape
    ts = min(ts, s)

    def body(x_ref, w_ref, o_ref):
        o_ref[...] = _rms_fwd(x_ref[...], w_ref[...])[0].astype(BF16)

    return pl.pallas_call(
        body, out_shape=jax.ShapeDtypeStruct((s, d), BF16), grid=(s // ts,),
        in_specs=[pl.BlockSpec((ts, d), lambda i: (i, 0)), pl.BlockSpec((1, d), lambda i: (0, 0))],
        out_specs=pl.BlockSpec((ts, d), lambda i: (i, 0)), name=name, compiler_params=_params(("parallel",)),
    )(x, w)


def _norm_bwd(x, w, dy, dres, name, ts=512):
    s, d = x.shape
    ts = min(ts, s)
    with_res = dres is not None

    def body(*refs):
        if with_res:
            x_ref, w_ref, dy_ref, dres_ref, dx_ref, dw_ref = refs
        else:
            x_ref, w_ref, dy_ref, dx_ref, dw_ref = refs
        xv = x_ref[...]
        r = lax.rsqrt(jnp.mean(xv * xv, axis=-1, keepdims=True) + EPS)
        dx, dw = _rms_bwd(xv, r, w_ref[...], dy_ref[...])
        dx_ref[...] = dx + dres_ref[...] if with_res else dx

        @pl.when(pl.program_id(0) == 0)
        def _():
            dw_ref[...] = jnp.zeros_like(dw_ref)

        dw_ref[...] += dw

    row = pl.BlockSpec((ts, d), lambda i: (i, 0))
    vec = pl.BlockSpec((1, d), lambda i: (0, 0))
    return pl.pallas_call(
        body, out_shape=(jax.ShapeDtypeStruct((s, d), F32), jax.ShapeDtypeStruct((1, d), F32)), grid=(s // ts,),
        in_specs=[row, vec, row] + ([row] if with_res else []), out_specs=(row, vec), name=name,
        compiler_params=_params(("arbitrary",)),
    )(*([x, w, dy] + ([dres] if with_res else [])))


def _ffn_fwd(x, nw, w1, w2, layer, ts=256):
    s, d = x.shape
    ts = min(ts, s)
    nj = D_FF // FF_CHUNK

    def body(x_ref, nw_ref, w1_ref, w2_ref, o_ref, h_scr, acc):
        j = pl.program_id(1)

        @pl.when(j == 0)
        def _():
            h_scr[...] = _rms_fwd(x_ref[...], nw_ref[...])[0].astype(BF16)
            acc[...] = jnp.zeros_like(acc)

        a = _nn(h_scr[...], w1_ref[...])
        rl = jnp.maximum(a, 0.0)
        acc[...] += _nn((rl * rl).astype(BF16), w2_ref[...])

        @pl.when(j == nj - 1)
        def _():
            o_ref[...] = x_ref[...] + acc[...]

    row = pl.BlockSpec((ts, d), lambda i, j: (i, 0))
    return pl.pallas_call(
        body, out_shape=jax.ShapeDtypeStruct((s, d), F32), grid=(s // ts, nj),
        in_specs=[row, pl.BlockSpec((1, d), lambda i, j: (0, 0)),
                  _lw((d, FF_CHUNK), lambda i, j: (0, j), layer), _lw((FF_CHUNK, d), lambda i, j: (j, 0), layer)],
        out_specs=row, scratch_shapes=[pltpu.VMEM((ts, d), BF16), pltpu.VMEM((ts, d), F32)], name="ffn_fwd",
        compiler_params=_params(("parallel", "arbitrary")),
    )(x, nw, w1, w2)


def _ffn_bwd(x, dxo, nw, w1, w2, layer, ts=256):
    s, d = x.shape
    ts = min(ts, s)
    nj = D_FF // FF_CHUNK

    def body(x_ref, dxo_ref, nw_ref, w1_ref, w2_ref, dx_ref, dnw_ref, h_ref, da_ref, r_ref, dxb_ref, dh):
        i, j = pl.program_id(0), pl.program_id(1)

        @pl.when(j == 0)
        def _():
            h_ref[...] = _rms_fwd(x_ref[...], nw_ref[...])[0].astype(BF16)
            dxb_ref[...] = dxo_ref[...].astype(BF16)
            dh[...] = jnp.zeros_like(dh)

        a = _nn(h_ref[...], w1_ref[...])
        rl = jnp.maximum(a, 0.0)
        r_ref[...] = (rl * rl).astype(BF16)
        da = (_nt(dxb_ref[...], w2_ref[...]) * (2.0 * rl)).astype(BF16)
        da_ref[...] = da
        dh[...] += _nt(da, w1_ref[...])

        @pl.when(jnp.logical_and(i == 0, j == 0))
        def _():
            dnw_ref[...] = jnp.zeros_like(dnw_ref)

        @pl.when(j == nj - 1)
        def _():
            xv = x_ref[...]
            r = lax.rsqrt(jnp.mean(xv * xv, axis=-1, keepdims=True) + EPS)
            dx, dw = _rms_bwd(xv, r, nw_ref[...], dh[...])
            dx_ref[...] = dxo_ref[...] + dx
            dnw_ref[...] += dw

    row = pl.BlockSpec((ts, d), lambda i, j: (i, 0))
    vec = pl.BlockSpec((1, d), lambda i, j: (0, 0))
    ffc = pl.BlockSpec((ts, FF_CHUNK), lambda i, j: (i, j))
    return pl.pallas_call(
        body,
        out_shape=(jax.ShapeDtypeStruct((s, d), F32), jax.ShapeDtypeStruct((1, d), F32), jax.ShapeDtypeStruct((s, d), BF16),
                   jax.ShapeDtypeStruct((s, D_FF), BF16), jax.ShapeDtypeStruct((s, D_FF), BF16), jax.ShapeDtypeStruct((s, d), BF16)),
        grid=(s // ts, nj),
        in_specs=[row, row, vec, _lw((d, FF_CHUNK), lambda i, j: (0, j), layer), _lw((FF_CHUNK, d), lambda i, j: (j, 0), layer)],
        out_specs=(row, vec, row, ffc, ffc, row), scratch_shapes=[pltpu.VMEM((ts, d), F32)], name="ffn_bwd",
        compiler_params=_params(("arbitrary", "arbitrary")),
    )(x, dxo, nw, w1, w2)


def _attn_probs(q, k_ref):
    ps = []
    for hd in range(XA_HEADS):
        c = slice(hd * XA_HD, (hd + 1) * XA_HD)
        sc = _nt(q[:, c].astype(BF16), k_ref[:, c]) * (XA_HD ** -0.5)
        e = jnp.exp(sc - jnp.max(sc, axis=-1, keepdims=True))
        ps.append(e / jnp.sum(e, axis=-1, keepdims=True))
    return ps


def _attn_fwd(x, nw, wq, kv, wo, layer, ts=256):
    s, d = x.shape
    ts = min(ts, s)
    nm = kv.shape[0]

    def body(x_ref, nw_ref, wq_ref, k_ref, v_ref, wo_ref, o_ref):
        xv = x_ref[...]
        h = _rms_fwd(xv, nw_ref[...])[0].astype(BF16)
        q = _nn(h, wq_ref[...])
        ps = _attn_probs(q, k_ref)
        o = jnp.concatenate([_nn(ps[hd].astype(BF16), v_ref[:, hd * XA_HD:(hd + 1) * XA_HD]) for hd in range(XA_HEADS)], axis=1)
        o_ref[...] = xv + _nn(o.astype(BF16), wo_ref[...])

    row = pl.BlockSpec((ts, d), lambda i: (i, 0))
    full = lambda r, c: pl.BlockSpec((r, c), lambda i: (0, 0))
    wsp = _lw((d, d), lambda i: (0, 0), layer)
    return pl.pallas_call(
        body, out_shape=jax.ShapeDtypeStruct((s, d), F32), grid=(s // ts,),
        in_specs=[row, full(1, d), wsp, full(nm, d), pl.BlockSpec((nm, d), lambda i: (0, 1)), wsp], out_specs=row, name="attn_fwd",
        compiler_params=_params(("parallel",)),
    )(x, nw, wq, kv, kv, wo)


def _attn_bwd(x, dxo, nw, wq, kv, wo, layer, ts=256):
    s, d = x.shape
    ts = min(ts, s)
    nm = kv.shape[0]

    def body(x_ref, dxo_ref, nw_ref, wq_ref, k_ref, v_ref, wo_ref,
             dx_ref, dnw_ref, h_ref, o_ref, dq_ref, dxb_ref, dk_ref, dv_ref):
        xv = x_ref[...]
        hf, r = _rms_fwd(xv, nw_ref[...])
        h = hf.astype(BF16)
        h_ref[...] = h
        q = _nn(h, wq_ref[...])
        qb = q.astype(BF16)
        ps = _attn_probs(q, k_ref)
        dxb = dxo_ref[...].astype(BF16)
        dxb_ref[...] = dxb
        do = _nt(dxb, wo_ref[...])

        @pl.when(pl.program_id(0) == 0)
        def _():
            dnw_ref[...] = jnp.zeros_like(dnw_ref)
            dk_ref[...] = jnp.zeros_like(dk_ref)
            dv_ref[...] = jnp.zeros_like(dv_ref)

        dqs = []
        for hd in range(XA_HEADS):
            c = slice(hd * XA_HD, (hd + 1) * XA_HD)
            p = ps[hd]
            pb = p.astype(BF16)
            dob = do[:, c].astype(BF16)
            o_ref[:, c] = _nn(pb, v_ref[:, c]).astype(BF16)
            dp = _nt(dob, v_ref[:, c])
            ds = (p * (dp - jnp.sum(p * dp, axis=-1, keepdims=True)) * (XA_HD ** -0.5)).astype(BF16)
            dqs.append(_nn(ds, k_ref[:, c]))
            dk_ref[:, c] += _tn(ds, qb[:, c])
            dv_ref[:, c] += _tn(pb, dob)
        dq = jnp.concatenate(dqs, axis=1).astype(BF16)
        dq_ref[...] = dq
        dx, dw = _rms_bwd(xv, r, nw_ref[...], _nt(dq, wq_ref[...]))
        dx_ref[...] = dxo_ref[...] + dx
        dnw_ref[...] += dw

    row = pl.BlockSpec((ts, d), lambda i: (i, 0))
    full = lambda r, c: pl.BlockSpec((r, c), lambda i: (0, 0))
    sd = lambda dt: jax.ShapeDtypeStruct((s, d), dt)
    return pl.pallas_call(
        body,
        out_shape=(sd(F32), jax.ShapeDtypeStruct((1, d), F32), sd(BF16), sd(BF16), sd(BF16), sd(BF16),
                   jax.ShapeDtypeStruct((nm, d), F32), jax.ShapeDtypeStruct((nm, d), F32)),
        grid=(s // ts,),
        in_specs=[row, row, full(1, d), _lw((d, d), lambda i: (0, 0), layer), full(nm, d), pl.BlockSpec((nm, d), lambda i: (0, 1)),
                  _lw((d, d), lambda i: (0, 0), layer)],
        out_specs=(row, full(1, d), row, row, row, row, full(nm, d), full(nm, d)), name="attn_bwd",
        compiler_params=_params(("arbitrary",)),
    )(x, dxo, nw, wq, kv, kv, wo)


GATE_BLK0 = MIX_W // 512


def _merge_specs(ts, layer):
    row = pl.BlockSpec((ts, D_MODEL), lambda i: (i, 0))
    br = pl.BlockSpec((ts, 512), lambda i: (i, 0))
    gates = [pl.BlockSpec((ts, 512), functools.partial(lambda n, i: (i, GATE_BLK0 + n), n)) for n in range(8)]
    full = lambda *shape: pl.BlockSpec(shape, lambda i: (0,) * len(shape))
    weights = [full(4, D_MODEL), _lw((4, 512, D_MODEL), lambda i: (0, 0, 0), layer), _lw((D_MODEL, D_MODEL), lambda i: (0, 0), layer)]
    return row, br, gates, full, weights


def _merge_gates(gp_refs, gb_ref, kb):
    gp = jnp.concatenate([gp_refs[2 * kb][...], gp_refs[2 * kb + 1][...]], axis=1)
    return _sigmoid(gp + gb_ref[kb:kb + 1, :])


def _merge_fwd(x, branches, proj, gate_b, wb, wout, layer, ts=256):
    s, d = x.shape
    ts = min(ts, s)

    def body(x_ref, b0, b1, b2, b3, g0, g1, g2, g3, g4, g5, g6, g7, gb_ref, wb_ref, wo_ref, o_ref):
        brs, gps = (b0, b1, b2, b3), (g0, g1, g2, g3, g4, g5, g6, g7)
        merged = jnp.zeros((ts, d), F32)
        for kb in range(4):
            merged += _merge_gates(gps, gb_ref, kb) * _nn(brs[kb][...], wb_ref[kb])
        o_ref[...] = x_ref[...] + _nn(merged.astype(BF16), wo_ref[...])

    row, br, gates, full, weights = _merge_specs(ts, layer)
    return pl.pallas_call(
        body, out_shape=jax.ShapeDtypeStruct((s, d), F32), grid=(s // ts,),
        in_specs=[row, br, br, br, br] + gates + weights, out_specs=row, name="merge_fwd",
        compiler_params=_params(("parallel",)),
    )(x, *branches, *([proj] * 8), gate_b, wb, wout)


def _merge_bwd(dxo, branches, proj, gate_b, wb, wout, layer, ts=256):
    s, d = dxo.shape
    ts = min(ts, s)

    def body(dxo_ref, b0, b1, b2, b3, g0, g1, g2, g3, g4, g5, g6, g7, gb_ref, wb_ref, wo_ref,
             db0, db1, db2, db3, dgp_ref, dup_ref, mg_ref, dxb_ref, dgb_ref):
        brs, gps, dbs = (b0, b1, b2, b3), (g0, g1, g2, g3, g4, g5, g6, g7), (db0, db1, db2, db3)
        dxb = dxo_ref[...].astype(BF16)
        dxb_ref[...] = dxb
        dm = _nt(dxb, wo_ref[...])

        @pl.when(pl.program_id(0) == 0)
        def _():
            dgb_ref[...] = jnp.zeros_like(dgb_ref)

        merged = jnp.zeros((ts, d), F32)
        for kb in range(4):
            c = slice(kb * d, (kb + 1) * d)
            g = _merge_gates(gps, gb_ref, kb)
            up = _nn(brs[kb][...], wb_ref[kb])
            merged += g * up
            dup = (dm * g).astype(BF16)
            dup_ref[:, c] = dup
            dgp = dm * up * g * (1.0 - g)
            dgp_ref[:, c] = dgp.astype(BF16)
            dgb_ref[kb:kb + 1, :] += jnp.sum(dgp, axis=0, keepdims=True)
            dbs[kb][...] = _nt(dup, wb_ref[kb])
        mg_ref[...] = merged.astype(BF16)

    row, br, gates, full, weights = _merge_specs(ts, layer)
    wide = pl.BlockSpec((ts, 4 * d), lambda i: (i, 0))
    sb = jax.ShapeDtypeStruct((s, 512), F32)
    return pl.pallas_call(
        body,
        out_shape=(sb, sb, sb, sb, jax.ShapeDtypeStruct((s, 4 * d), BF16), jax.ShapeDtypeStruct((s, 4 * d), BF16),
                   jax.ShapeDtypeStruct((s, d), BF16), jax.ShapeDtypeStruct((s, d), BF16), jax.ShapeDtypeStruct((4, d), F32)),
        grid=(s // ts,),
        in_specs=[row, br, br, br, br] + gates + weights,
        out_specs=(br, br, br, br, wide, wide, row, row, full(4, d)), name="merge_bwd",
        compiler_params=_params(("arbitrary",)),
    )(dxo, *branches, *([proj] * 8), gate_b, wb, wout)


def _tri(n, upper=False):
    r = lax.broadcasted_iota(jnp.int32, (n, n), 0)
    c = lax.broadcasted_iota(jnp.int32, (n, n), 1)
    return jnp.where((c >= r) if upper else (c <= r), 1.0, 0.0).astype(F32)


def _hg_gates(hq, hf, lb):
    sg = _sigmoid(hf)
    fg = lb + (1.0 - lb) * sg
    sq = _sigmoid(hq)
    return sg, fg, 1.0 - fg, jnp.log(fg), hq * sq, sq


def _hg_intra(qf, kk, b):
    out = []
    col = lax.broadcasted_iota(jnp.int32, (SUB, CHUNK), 1)
    row = lax.broadcasted_iota(jnp.int32, (SUB, CHUNK), 0)
    for i in range(CHUNK // SUB):
        rs = slice(i * SUB, (i + 1) * SUB)
        ref = b[i * SUB - 1:i * SUB, :] if i else jnp.zeros((1, b.shape[1]), F32)
        eq = jnp.exp(b[rs] - ref)
        ek = jnp.exp(jnp.minimum(ref - b, EXP_CLAMP))
        out.append((qf[rs] * eq, kk * ek, col <= row + i * SUB, eq, ek))
    return out


def _hg_chunk_fwd(qf, kk, b, v, st):
    parts = _hg_intra(qf, kk, b)
    vb = v.astype(BF16)
    att = [jnp.where(m, _nt(qt.astype(BF16), kt.astype(BF16)), 0.0) for qt, kt, m, _, _ in parts]
    o = jnp.concatenate([_nn(a.astype(BF16), vb) for a in att], axis=0)
    qh = qf * jnp.exp(b)
    o = o + _nt(qh.astype(BF16), st.astype(BF16))
    bl = b[CHUNK - 1:CHUNK, :]
    kh = kk * jnp.exp(bl - b)
    return o, parts, att, qh, kh, jnp.exp(bl)


def _hgrn_fwd(proj, lb, nw, ts=256):
    s = proj.shape[0]
    ts = min(ts, s)
    nch = ts // CHUNK

    def body(q_ref, f_ref, v_ref, g_ref, lb_ref, nw_ref, o_ref, st_ref, st):
        @pl.when(pl.program_id(0) == 0)
        def _():
            st[...] = jnp.zeros_like(st)

        tri = _tri(CHUNK)

        def chunk(c, carry):
            rows = pl.ds(pl.multiple_of(c * CHUNK, CHUNK), CHUNK)
            _, _, kk, lf, qf, _ = _hg_gates(q_ref[rows, :], f_ref[rows, :], lb_ref[...])
            b = _nn(tri, lf, precision=lax.Precision.HIGHEST)
            hv, hg = v_ref[rows, :], g_ref[rows, :]
            st_ref[c] = st[...]
            for h in range(HG_HEADS):
                cs = slice(h * HG_D, (h + 1) * HG_D)
                o, _, _, _, kh, ebl = _hg_chunk_fwd(qf[:, cs], kk[:, cs], b[:, cs], hv[:, cs], st[h])
                st[h] = st[h] * ebl + _tn(hv[:, cs].astype(BF16), kh.astype(BF16))
                on = _rms_fwd(o, nw_ref[...])[0]
                gh = hg[:, cs]
                o_ref[rows, cs] = (on * gh * _sigmoid(gh)).astype(BF16)
            return carry

        lax.fori_loop(0, nch, chunk, 0)

    col = lambda n: pl.BlockSpec((ts, 512), functools.partial(lambda n, i: (i, n), n))
    return pl.pallas_call(
        body,
        out_shape=(jax.ShapeDtypeStruct((s, 512), BF16), jax.ShapeDtypeStruct((s // CHUNK, HG_HEADS, HG_D, HG_D), F32)),
        grid=(s // ts,),
        in_specs=[col(0), col(1), col(2), col(3), pl.BlockSpec((1, 512), lambda i: (0, 0)), pl.BlockSpec((1, HG_D), lambda i: (0, 0))],
        out_specs=(pl.BlockSpec((ts, 512), lambda i: (i, 0)), pl.BlockSpec((nch, HG_HEADS, HG_D, HG_D), lambda i: (i, 0, 0, 0))),
        scratch_shapes=[pltpu.VMEM((HG_HEADS, HG_D, HG_D), F32)], name="hgrn_fwd",
        compiler_params=_params(("arbitrary",)),
    )(proj, proj, proj, proj, lb, nw)


def _hgrn_bwd(proj, dout, states, lb, nw, ts=256):
    s = proj.shape[0]
    ts = min(ts, s)
    nch = ts // CHUNK
    nt = s // ts

    def body(q_ref, f_ref, v_ref, g_ref, do_ref, st_ref, lb_ref, nw_ref, dp_ref, dlb_ref, dnw_ref, dst):
        @pl.when(pl.program_id(0) == 0)
        def _():
            dst[...] = jnp.zeros_like(dst)
            dlb_ref[...] = jnp.zeros_like(dlb_ref)
            dnw_ref[...] = jnp.zeros_like(dnw_ref)

        tri, triu = _tri(CHUNK), _tri(CHUNK, upper=True)
        last = lax.broadcasted_iota(jnp.int32, (CHUNK, HG_D), 0) == CHUNK - 1
        nwv = nw_ref[...]

        def chunk(cc, carry):
            c = nch - 1 - cc
            rows = pl.ds(pl.multiple_of(c * CHUNK, CHUNK), CHUNK)
            hq, hf, hv, hg = q_ref[rows, :], f_ref[rows, :], v_ref[rows, :], g_ref[rows, :]
            lbv = lb_ref[...]
            sg, fg, kk, lf, qf, sq = _hg_gates(hq, hf, lbv)
            b = _nn(tri, lf, precision=lax.Precision.HIGHEST)
            dov = do_ref[rows, :]
            dqf_l, dkk_l, db_l, dv_l, dg_l = [], [], [], [], []
            for h in range(HG_HEADS):
                cs = slice(h * HG_D, (h + 1) * HG_D)
                stp = st_ref[c, h]
                bh, vh, gh = b[:, cs], hv[:, cs], hg[:, cs]
                o, parts, att, qh, kh, ebl = _hg_chunk_fwd(qf[:, cs], kk[:, cs], bh, vh, stp)
                sgg = _sigmoid(gh)
                on, r = _rms_fwd(o, nwv)
                d_on = dov[:, cs] * (gh * sgg)
                dg_l.append(dov[:, cs] * on * _dsilu(gh, sgg))
                do, dnw = _rms_bwd(o, r, nwv, d_on)
                dnw_ref[...] += dnw
                dob, vb = do.astype(BF16), vh.astype(BF16)
                dsth = dst[h]
                dstb = dsth.astype(BF16)
                dqh = _nn(do, stp, precision=HI)
                dkh = _nn(vh, dsth, precision=HI)
                dv = _nt(kh.astype(BF16), dstb)
                eb = jnp.exp(bh)
                ekl = jnp.exp(bh[CHUNK - 1:CHUNK, :] - bh)
                dqf, dkk = dqh * eb, dkh * ekl
                db = dqh * qh - dkh * kh
                dbl = jnp.sum(dkh * kh, axis=0, keepdims=True) + ebl * jnp.sum(dsth * stp, axis=0, keepdims=True)
                dst[h] = dsth * ebl + _tn(dob, qh.astype(BF16))
                dq_rows = []
                for i, (qt, kt, m, eq, ek) in enumerate(parts):
                    rs = slice(i * SUB, (i + 1) * SUB)
                    da = jnp.where(m, _nt(dob[rs], vb), 0.0)
                    dv = dv + _tn(att[i].astype(BF16), dob[rs])
                    dqt = _nn(da, kt, precision=HI)
                    dkt = _tn(da, qt, precision=HI)
                    dq_rows.append((dqt * eq, dqt * qt))
                    dkk = dkk + dkt * ek
                    db = db - dkt * kt
                dqf = dqf + jnp.concatenate([a for a, _ in dq_rows], axis=0)
                db = db + jnp.concatenate([a for _, a in dq_rows], axis=0) + jnp.where(last, dbl, 0.0)
                dqf_l.append(dqf); dkk_l.append(dkk); db_l.append(db); dv_l.append(dv)
            cat = lambda l: jnp.concatenate(l, axis=1)
            dlf = _nn(triu, cat(db_l), precision=lax.Precision.HIGHEST)
            dfg = dlf / fg - cat(dkk_l)
            dlb_ref[...] += jnp.sum(dfg * (1.0 - sg), axis=0, keepdims=True)
            dp_ref[rows, 0:512] = (cat(dqf_l) * _dsilu(hq, sq)).astype(BF16)
            dp_ref[rows, 512:1024] = (dfg * (1.0 - lbv) * sg * (1.0 - sg)).astype(BF16)
            dp_ref[rows, 1024:1536] = cat(dv_l).astype(BF16)
            dp_ref[rows, 1536:2048] = cat(dg_l).astype(BF16)
            return carry

        lax.fori_loop(0, nch, chunk, 0)

    col = lambda n: pl.BlockSpec((ts, 512), functools.partial(lambda n, i: (nt - 1 - i, n), n))
    vec = lambda n: pl.BlockSpec((1, n), lambda i: (0, 0))
    return pl.pallas_call(
        body,
        out_shape=(jax.ShapeDtypeStruct((s, 2048), BF16), jax.ShapeDtypeStruct((1, 512), F32), jax.ShapeDtypeStruct((1, HG_D), F32)),
        grid=(nt,),
        in_specs=[col(0), col(1), col(2), col(3), pl.BlockSpec((ts, 512), lambda i: (nt - 1 - i, 0)),
                  pl.BlockSpec((nch, HG_HEADS, HG_D, HG_D), lambda i: (nt - 1 - i, 0, 0, 0)), vec(512), vec(HG_D)],
        out_specs=(pl.BlockSpec((ts, 2048), lambda i: (nt - 1 - i, 0)), vec(512), vec(HG_D)),
        scratch_shapes=[pltpu.VMEM((HG_HEADS, HG_D, HG_D), F32)], name="hgrn_bwd",
        compiler_params=_params(("arbitrary",)),
    )(proj, proj, proj, proj, dout, states, lb, nw)


CV_BLK = 2048 // 512


def _halo_before(ts, halo, colblk):
    return pl.BlockSpec((halo, 512), functools.partial(lambda cb, i: (jnp.maximum(i * (ts // halo) - 1, 0), cb), colblk))


def _cv_front(a_ref, g_ref, ah_ref, gh_ref, ext, first):
    a, sg = a_ref[...], _sigmoid(g_ref[...])
    zh = ah_ref[...] * _sigmoid(gh_ref[...])
    ext[0:CV_HALO, :] = jnp.where(first, 0.0, zh)
    ext[CV_HALO:, :] = a * sg
    return a, sg


def _cv_conv_ln(ext, w_ref, b_ref, ts):
    y = jnp.zeros((ts, CV_W), F32) + b_ref[...]
    for j in range(CV_K):
        y = y + w_ref[j:j + 1, :] * ext[pl.ds(CV_HALO - (CV_K - 1) + j, ts), :]
    mu = jnp.mean(y, axis=-1, keepdims=True)
    yc = y - mu
    r = lax.rsqrt(jnp.mean(yc * yc, axis=-1, keepdims=True) + EPS)
    return yc * r, r


def _conv_fwd(proj, w, b, lnw, lnb, ts=256):
    s = proj.shape[0]
    ts = min(ts, s)

    def body(a_ref, g_ref, ah_ref, gh_ref, w_ref, b_ref, lnw_ref, lnb_ref, o_ref, ext):
        _cv_front(a_ref, g_ref, ah_ref, gh_ref, ext, pl.program_id(0) == 0)
        yh, _ = _cv_conv_ln(ext, w_ref, b_ref, ts)
        yn = yh * lnw_ref[...] + lnb_ref[...]
        o_ref[...] = (yn * _sigmoid(yn)).astype(BF16)

    col = lambda n: pl.BlockSpec((ts, 512), functools.partial(lambda n, i: (i, n), n))
    vec = pl.BlockSpec((1, CV_W), lambda i: (0, 0))
    return pl.pallas_call(
        body, out_shape=jax.ShapeDtypeStruct((s, CV_W), BF16), grid=(s // ts,),
        in_specs=[col(CV_BLK), col(CV_BLK + 1), _halo_before(ts, CV_HALO, CV_BLK), _halo_before(ts, CV_HALO, CV_BLK + 1),
                  pl.BlockSpec((32, CV_W), lambda i: (0, 0)), vec, vec, vec],
        out_specs=pl.BlockSpec((ts, CV_W), lambda i: (i, 0)), scratch_shapes=[pltpu.VMEM((ts + CV_HALO, CV_W), F32)],
        name="conv_fwd", compiler_params=_params(("parallel",)),
    )(proj, proj, proj, proj, w, b, lnw, lnb)


def _conv_bwd(proj, dout, w, b, lnw, lnb, ts=256):
    s = proj.shape[0]
    ts = min(ts, s)
    nt = s // ts

    def body(a_ref, g_ref, ah_ref, gh_ref, do_ref, w_ref, b_ref, lnw_ref, lnb_ref,
             du_ref, dw_ref, db_ref, dlnw_ref, dlnb_ref, ext, dyext, carry):
        i = pl.program_id(0)

        @pl.when(i == 0)
        def _():
            carry[...] = jnp.zeros_like(carry)
            for ref in (dw_ref, db_ref, dlnw_ref, dlnb_ref):
                ref[...] = jnp.zeros_like(ref)

        a, sg = _cv_front(a_ref, g_ref, ah_ref, gh_ref, ext, i == nt - 1)
        yh, r = _cv_conv_ln(ext, w_ref, b_ref, ts)
        yn = yh * lnw_ref[...] + lnb_ref[...]
        dyn = do_ref[...] * _dsilu(yn, _sigmoid(yn))
        dlnw_ref[...] += jnp.sum(dyn * yh, axis=0, keepdims=True)
        dlnb_ref[...] += jnp.sum(dyn, axis=0, keepdims=True)
        gl = dyn * lnw_ref[...]
        dy = r * (gl - jnp.mean(gl, axis=-1, keepdims=True) - yh * jnp.mean(gl * yh, axis=-1, keepdims=True))
        db_ref[...] += jnp.sum(dy, axis=0, keepdims=True)
        dyext[0:ts, :] = dy
        dyext[ts:, :] = carry[...]
        carry[...] = dy[0:CV_HALO, :]
        dz = jnp.zeros((ts, CV_W), F32)
        for j in range(CV_K):
            dw_ref[j:j + 1, :] += jnp.sum(dy * ext[pl.ds(CV_HALO - (CV_K - 1) + j, ts), :], axis=0, keepdims=True)
            dz = dz + w_ref[j:j + 1, :] * dyext[pl.ds(CV_K - 1 - j, ts), :]
        du_ref[:, 0:CV_W] = (dz * sg).astype(BF16)
        du_ref[:, CV_W:] = (dz * a * sg * (1.0 - sg)).astype(BF16)

    rev = lambda n: pl.BlockSpec((ts, 512), functools.partial(lambda n, i: (nt - 1 - i, n), n))
    halo = lambda n: pl.BlockSpec((CV_HALO, 512), functools.partial(
        lambda n, i: (jnp.maximum((nt - 1 - i) * (ts // CV_HALO) - 1, 0), n), n))
    vec = pl.BlockSpec((1, CV_W), lambda i: (0, 0))
    wsp = pl.BlockSpec((32, CV_W), lambda i: (0, 0))
    v1 = jax.ShapeDtypeStruct((1, CV_W), F32)
    return pl.pallas_call(
        body, out_shape=(jax.ShapeDtypeStruct((s, 2 * CV_W), BF16), jax.ShapeDtypeStruct((32, CV_W), F32), v1, v1, v1),
        grid=(nt,),
        in_specs=[rev(CV_BLK), rev(CV_BLK + 1), halo(CV_BLK), halo(CV_BLK + 1), rev(0), wsp, vec, vec, vec],
        out_specs=(pl.BlockSpec((ts, 2 * CV_W), lambda i: (nt - 1 - i, 0)), wsp, vec, vec, vec),
        scratch_shapes=[pltpu.VMEM((ts + CV_HALO, CV_W), F32), pltpu.VMEM((ts + CV_HALO, CV_W), F32), pltpu.VMEM((CV_HALO, CV_W), F32)],
        name="conv_bwd", compiler_params=_params(("arbitrary",)),
    )(proj, proj, proj, proj, dout, w, b, lnw, lnb)


PL_BLK = 3072 // 512


def _pool_windows(ext, t0, ts):
    n = ext.shape[0]
    t = t0 + lax.broadcasted_iota(jnp.int32, (ts, 1), 0)
    out = []
    for g, wdw in enumerate(POOL_WINDOWS):
        e = ext[:, g * 128:(g + 1) * 128]
        acc, k = e, 1
        while k < wdw:
            acc = acc + pltpu.roll(acc, k, 0)
            k *= 2
        cnt = jnp.minimum(t + 1, wdw).astype(F32)
        out.append(acc[POOL_HALO:] / cnt - e[POOL_HALO:])
    return out


def _pool_fwd(proj, w, sc, ts=256):
    s = proj.shape[0]
    ts = min(ts, s)

    def body(u_ref, uh_ref, w_ref, sc_ref, o_ref):
        i = pl.program_id(0)
        ext = jnp.concatenate([jnp.where(i == 0, 0.0, uh_ref[...]), u_ref[...]], axis=0)
        ps = _pool_windows(ext, i * ts, ts)
        y = jnp.concatenate([_nn(ps[g].astype(BF16), w_ref[g].astype(BF16)) for g in range(4)], axis=1)
        o_ref[...] = (y * sc_ref[...]).astype(BF16)

    return pl.pallas_call(
        body, out_shape=jax.ShapeDtypeStruct((s, 512), BF16), grid=(s // ts,),
        in_specs=[pl.BlockSpec((ts, 512), lambda i: (i, PL_BLK)), _halo_before(ts, POOL_HALO, PL_BLK),
                  pl.BlockSpec((4, 128, 128), lambda i: (0, 0, 0)), pl.BlockSpec((1, 512), lambda i: (0, 0))],
        out_specs=pl.BlockSpec((ts, 512), lambda i: (i, 0)), name="pool_fwd", compiler_params=_params(("parallel",)),
    )(proj, proj, w, sc)


def _pool_bwd(proj, dout, w, sc, ts=256):
    s = proj.shape[0]
    ts = min(ts, s)
    nt = s // ts
    n = ts + POOL_HALO

    def body(u_ref, uh_ref, do_ref, doh_ref, w_ref, sc_ref, du_ref, dw_ref, dsc_ref):
        i = pl.program_id(0)

        @pl.when(i == 0)
        def _():
            dw_ref[...] = jnp.zeros_like(dw_ref)
            dsc_ref[...] = jnp.zeros_like(dsc_ref)

        ext = jnp.concatenate([jnp.where(i == 0, 0.0, uh_ref[...]), u_ref[...]], axis=0)
        ps = _pool_windows(ext, i * ts, ts)
        dov = do_ref[...]
        dyext = jnp.concatenate([dov, jnp.where(i == nt - 1, 0.0, doh_ref[...])], axis=0) * sc_ref[...]
        t = i * ts + lax.broadcasted_iota(jnp.int32, (n, 1), 0)
        row = lax.broadcasted_iota(jnp.int32, (n, 1), 0)
        dus = []
        for g, wdw in enumerate(POOL_WINDOWS):
            cs = slice(g * 128, (g + 1) * 128)
            wg, pb = w_ref[g].astype(BF16), ps[g].astype(BF16)
            dsc_ref[:, cs] += jnp.sum(dov[:, cs] * _nn(pb, wg), axis=0, keepdims=True)
            dyg = dyext[:, cs].astype(BF16)
            dw_ref[g] += _tn(pb, dyg[0:ts])
            dp = _nt(dyg, wg)
            acc, k = dp / jnp.minimum(t + 1, wdw).astype(F32), 1
            while k < wdw:
                acc = acc + jnp.where(row < n - k, pltpu.roll(acc, n - k, 0), 0.0)
                k *= 2
            dus.append(acc[0:ts] - dp[0:ts])
        du_ref[...] = jnp.concatenate(dus, axis=1).astype(BF16)

    tile = lambda cb: pl.BlockSpec((ts, 512), functools.partial(lambda cb, i: (i, cb), cb))
    after = pl.BlockSpec((POOL_HALO, 512), lambda i: (jnp.minimum((i + 1) * (ts // POOL_HALO), s // POOL_HALO - 1), 0))
    wsp, vec = pl.BlockSpec((4, 128, 128), lambda i: (0, 0, 0)), pl.BlockSpec((1, 512), lambda i: (0, 0))
    return pl.pallas_call(
        body, out_shape=(jax.ShapeDtypeStruct((s, 512), BF16), jax.ShapeDtypeStruct((4, 128, 128), F32), jax.ShapeDtypeStruct((1, 512), F32)),
        grid=(nt,),
        in_specs=[tile(PL_BLK), _halo_before(ts, POOL_HALO, PL_BLK), tile(0), after, wsp, vec],
        out_specs=(tile(0), wsp, vec), name="pool_bwd", compiler_params=_params(("arbitrary",)),
    )(proj, proj, dout, dout, w, sc)


LX_BLK, LY_BLK = 3584 // 512, 4096 // 512
LRU_OFF = LRU_HALO - (LRU_CONV - 1)


def _scan_fwd(a, b):
    n = a.shape[0]
    row = lax.broadcasted_iota(jnp.int32, (n, 1), 0)
    k = 1
    while k < n:
        m = row >= k
        b = jnp.where(m, a * pltpu.roll(b, k, 0) + b, b)
        a = jnp.where(m, a * pltpu.roll(a, k, 0), a)
        k *= 2
    return a, b


def _scan_rev(a, b):
    n = a.shape[0]
    row = lax.broadcasted_iota(jnp.int32, (n, 1), 0)
    k = 1
    while k < n:
        m = row < n - k
        b = jnp.where(m, a * pltpu.roll(b, n - k, 0) + b, b)
        a = jnp.where(m, a * pltpu.roll(a, n - k, 0), a)
        k *= 2
    return b


def _lru_gates(x_ref, xh_ref, ext, first, cw_ref, cb_ref, wa_ref, ba_ref, wx_ref, bx_ref, lam_ref, ts):
    ext[0:LRU_HALO, :] = jnp.where(first, 0.0, xh_ref[...])
    ext[LRU_HALO:, :] = x_ref[...]
    xc = jnp.zeros((ts, LRU_W), F32) + cb_ref[...]
    for j in range(LRU_CONV):
        xc = xc + cw_ref[j:j + 1, :] * ext[pl.ds(LRU_OFF + j, ts), :]
    xb = xc.astype(BF16)
    r = _sigmoid(_nn(xb, wa_ref[...]) + ba_ref[...])
    ig = _sigmoid(_nn(xb, wx_ref[...]) + bx_ref[...])
    nl = -lam_ref[...]
    sp = jnp.maximum(nl, 0.0) + jnp.log(1.0 + jnp.exp(-jnp.abs(nl)))
    la = -LRU_C * r * sp
    a = jnp.exp(la)
    z = 2.0 * la
    em = jnp.where(z > -0.1, -z * (1.0 + z * 0.5 * (1.0 + z * (1.0 / 3) * (1.0 + z * 0.25 * (1.0 + z * 0.2)))), 1.0 - a * a)
    return xc, xb, r, ig, sp, a, jnp.sqrt(em)


def _lru_fwd(proj, cw, cb, wa, ba, wx, bx, lam, ts=256):
    s = proj.shape[0]
    ts = min(ts, s)

    def body(x_ref, xh_ref, y_ref, cw_ref, cb_ref, wa_ref, ba_ref, wx_ref, bx_ref, lam_ref, o_ref, h_ref, ext, hc):
        i = pl.program_id(0)

        @pl.when(i == 0)
        def _():
            hc[...] = jnp.zeros_like(hc)

        xc, _, _, ig, _, a, mult = _lru_gates(x_ref, xh_ref, ext, i == 0, cw_ref, cb_ref, wa_ref, ba_ref, wx_ref, bx_ref, lam_ref, ts)
        acum, h0 = _scan_fwd(a, mult * ig * xc)
        h = h0 + acum * hc[0:1, :]
        hc[...] = jnp.broadcast_to(h[ts - 1:ts, :], hc.shape)
        h_ref[...] = h
        o_ref[...] = (h * _gelu_parts(y_ref[...])[0]).astype(BF16)

    tile = lambda cb_: pl.BlockSpec((ts, 512), functools.partial(lambda c, i: (i, c), cb_))
    vec = pl.BlockSpec((1, LRU_W), lambda i: (0, 0))
    mat = pl.BlockSpec((LRU_W, LRU_W), lambda i: (0, 0))
    return pl.pallas_call(
        body, out_shape=(jax.ShapeDtypeStruct((s, LRU_W), BF16), jax.ShapeDtypeStruct((s, LRU_W), F32)), grid=(s // ts,),
        in_specs=[tile(LX_BLK), _halo_before(ts, LRU_HALO, LX_BLK), tile(LY_BLK), pl.BlockSpec((8, LRU_W), lambda i: (0, 0)),
                  vec, mat, vec, mat, vec, vec],
        out_specs=(tile(0), tile(0)), scratch_shapes=[pltpu.VMEM((ts + LRU_HALO, LRU_W), F32), pltpu.VMEM((8, LRU_W), F32)],
        name="lru_fwd", compiler_params=_params(("arbitrary",)),
    )(proj, proj, proj, cw, cb, wa, ba, wx, bx, lam)


def _lru_bwd(proj, hs, dout, cw, cb, wa, ba, wx, bx, lam, ts=256):
    s = proj.shape[0]
    ts = min(ts, s)
    nt = s // ts

    def body(x_ref, xh_ref, y_ref, h_ref, hh_ref, do_ref, cw_ref, cb_ref, wa_ref, ba_ref, wx_ref, bx_ref, lam_ref,
             dxy_ref, dcw_ref, dcb_ref, dwa_ref, dba_ref, dwx_ref, dbx_ref, dlam_ref, ext, dext, cg, cd):
        i = pl.program_id(0)
        first_tile = i == nt - 1

        @pl.when(i == 0)
        def _():
            cg[...] = jnp.zeros_like(cg)
            cd[...] = jnp.zeros_like(cd)
            for ref in (dcw_ref, dcb_ref, dwa_ref, dba_ref, dwx_ref, dbx_ref, dlam_ref):
                ref[...] = jnp.zeros_like(ref)

        xc, xb, r, ig, sp, a, mult = _lru_gates(x_ref, xh_ref, ext, first_tile, cw_ref, cb_ref, wa_ref, ba_ref, wx_ref, bx_ref, lam_ref, ts)
        row = lax.broadcasted_iota(jnp.int32, (ts, 1), 0)
        h, dov = h_ref[...], do_ref[...]
        gel, dgel = _gelu_parts(y_ref[...])
        dxy_ref[:, LRU_W:] = (dov * h * dgel).astype(BF16)
        alpha = jnp.where(row < ts - 1, pltpu.roll(a, ts - 1, 0), 0.0)
        g = _scan_rev(alpha, dov * gel + jnp.where(row == ts - 1, cg[0:1, :], 0.0))
        cg[...] = jnp.broadcast_to(a[0:1, :] * g[0:1, :], cg.shape)
        hprev = jnp.where(row == 0, jnp.where(first_tile, 0.0, hh_ref[LRU_HALO - 1:LRU_HALO, :]), pltpu.roll(h, 1, 0))
        dla = g * hprev * a - g * ig * xc * (a * a) / mult
        dpr = dla * (-LRU_C * sp) * r * (1.0 - r)
        dpi = g * mult * xc * ig * (1.0 - ig)
        dprb, dpib = dpr.astype(BF16), dpi.astype(BF16)
        dxc = g * mult * ig + _nt(dprb, wa_ref[...]) + _nt(dpib, wx_ref[...])
        dlam_ref[...] += jnp.sum(dla * (-LRU_C * r), axis=0, keepdims=True) * (-_sigmoid(-lam_ref[...]))
        dwa_ref[...] += _tn(xb, dprb)
        dwx_ref[...] += _tn(xb, dpib)
        dba_ref[...] += jnp.sum(dpr, axis=0, keepdims=True)
        dbx_ref[...] += jnp.sum(dpi, axis=0, keepdims=True)
        dcb_ref[...] += jnp.sum(dxc, axis=0, keepdims=True)
        dext[0:ts, :] = dxc
        dext[ts:, :] = cd[...]
        cd[...] = dxc[0:LRU_HALO, :]
        dx = jnp.zeros((ts, LRU_W), F32)
        for j in range(LRU_CONV):
            dcw_ref[j:j + 1, :] += jnp.sum(dxc * ext[pl.ds(LRU_OFF + j, ts), :], axis=0, keepdims=True)
            dx = dx + cw_ref[j:j + 1, :] * dext[pl.ds(LRU_CONV - 1 - j, ts), :]
        dxy_ref[:, 0:LRU_W] = dx.astype(BF16)

    rev = lambda c: pl.BlockSpec((ts, 512), functools.partial(lambda c, i: (nt - 1 - i, c), c))
    halo = lambda c: pl.BlockSpec((LRU_HALO, 512), functools.partial(
        lambda c, i: (jnp.maximum((nt - 1 - i) * (ts // LRU_HALO) - 1, 0), c), c))
    vec = pl.BlockSpec((1, LRU_W), lambda i: (0, 0))
    mat = pl.BlockSpec((LRU_W, LRU_W), lambda i: (0, 0))
    cws = pl.BlockSpec((8, LRU_W), lambda i: (0, 0))
    v1, m1 = jax.ShapeDtypeStruct((1, LRU_W), F32), jax.ShapeDtypeStruct((LRU_W, LRU_W), F32)
    return pl.pallas_call(
        body, out_shape=(jax.ShapeDtypeStruct((s, 2 * LRU_W), BF16), jax.ShapeDtypeStruct((8, LRU_W), F32), v1, m1, v1, m1, v1, v1),
        grid=(nt,),
        in_specs=[rev(LX_BLK), halo(LX_BLK), rev(LY_BLK), rev(0), halo(0), rev(0), cws, vec, mat, vec, mat, vec, vec],
        out_specs=(pl.BlockSpec((ts, 2 * LRU_W), lambda i: (nt - 1 - i, 0)), cws, vec, mat, vec, mat, vec, vec),
        scratch_shapes=[pltpu.VMEM((ts + LRU_HALO, LRU_W), F32), pltpu.VMEM((ts + LRU_HALO, LRU_W), F32),
                        pltpu.VMEM((8, LRU_W), F32), pltpu.VMEM((LRU_HALO, LRU_W), F32)],
        name="lru_bwd", compiler_params=_params(("arbitrary",)),
    )(proj, proj, proj, hs, hs, dout, cw, cb, wa, ba, wx, bx, lam)


def _final_loss(x, fw, tgt, ts=512):
    s, d = x.shape
    ts = min(ts, s)

    def body(x_ref, w_ref, t_ref, loss_ref, dx_ref, dw_ref):
        @pl.when(pl.program_id(0) == 0)
        def _():
            loss_ref[...] = jnp.zeros_like(loss_ref)
            dw_ref[...] = jnp.zeros_like(dw_ref)

        xv = x_ref[...]
        y, r = _rms_fwd(xv, w_ref[...])
        err = y - t_ref[...]
        loss_ref[...] += 0.5 * jnp.sum(jnp.mean(err * err, axis=-1, keepdims=True), axis=0, keepdims=True)
        dx, dw = _rms_bwd(xv, r, w_ref[...], err * (1.0 / d))
        dx_ref[...] = dx
        dw_ref[...] += dw

    row = pl.BlockSpec((ts, d), lambda i: (i, 0))
    vec = pl.BlockSpec((1, d), lambda i: (0, 0))
    return pl.pallas_call(
        body, out_shape=(jax.ShapeDtypeStruct((8, 128), F32), jax.ShapeDtypeStruct((s, d), F32), jax.ShapeDtypeStruct((1, d), F32)),
        grid=(s // ts,), in_specs=[row, vec, row], out_specs=(pl.BlockSpec((8, 128), lambda i: (0, 0)), row, vec),
        name="final_loss", compiler_params=_params(("arbitrary",)),
    )(x, fw, tgt)


def _lb_softmax(raw_ref):
    raw = raw_ref[...]
    e = jnp.exp(raw - jnp.max(raw, axis=0, keepdims=True))
    return e / jnp.sum(e, axis=0, keepdims=True)


def _lb_fwd(raw):
    def body(raw_ref, o_ref):
        sm = _lb_softmax(raw_ref)
        acc = jnp.zeros((1, sm.shape[1]), F32)
        o_ref[0:1, :] = acc
        for l in range(1, DEPTH):
            acc = acc + sm[l:l + 1, :]
            o_ref[l:l + 1, :] = acc

    return pl.pallas_call(body, out_shape=jax.ShapeDtypeStruct(raw.shape, F32), name="lb_fwd")(raw)


def _lb_bwd(raw, dlb):
    def body(raw_ref, d_ref, o_ref):
        sm = _lb_softmax(raw_ref)
        dlbv = d_ref[...]
        dsm, acc = [None] * DEPTH, jnp.zeros((1, sm.shape[1]), F32)
        for l in range(DEPTH - 1, 0, -1):
            acc = acc + dlbv[l:l + 1, :]
            dsm[l] = acc
        dsm[0] = jnp.zeros_like(acc)
        dsm = jnp.concatenate(dsm, axis=0)
        o_ref[...] = sm * (dsm - jnp.sum(sm * dsm, axis=0, keepdims=True))

    return pl.pallas_call(body, out_shape=jax.ShapeDtypeStruct(raw.shape, F32), name="lb_bwd")(raw, dlb)


def _adam_math(w, g, m, v):
    m = ADAM_B1 * m + (1.0 - ADAM_B1) * g
    v = ADAM_B2 * v + (1.0 - ADAM_B2) * (g * g)
    m_hat = m / (1.0 - ADAM_B1 ** ADAM_STEP)
    v_hat = v / (1.0 - ADAM_B2 ** ADAM_STEP)
    return -ADAM_LR * (m_hat / (jnp.sqrt(v_hat) + ADAM_EPS) + ADAM_WD * w), m, v


def _adamw(w, gs, m, v, name, tr=128):
    r, c = w.shape
    tr = min(tr, r)
    ng = len(gs)

    def body(*refs):
        w_ref, g_refs, m_ref, v_ref = refs[0], refs[1:1 + ng], refs[1 + ng], refs[2 + ng]
        outs = refs[3 + ng:]
        g = g_refs[0][...]
        if ng == 2:
            g = g + g_refs[1][...]
            outs[0][...] = g
            outs = outs[1:]
        for o, val in zip(outs, _adam_math(w_ref[...], g, m_ref[...], v_ref[...])):
            o[...] = val

    blk = pl.BlockSpec((tr, c), lambda i: (i, 0))
    sd = jax.ShapeDtypeStruct((r, c), F32)
    nout = 3 + (ng == 2)
    return pl.pallas_call(
        body, out_shape=(sd,) * nout, grid=(r // tr,), in_specs=[blk] * (3 + ng), out_specs=(blk,) * nout, name=name,
        compiler_params=_params(("parallel",)),
    )(w, *gs, m, v)


def _cast_bf16(w, name, tr=256):
    r, c = w.shape
    tr = min(tr, r)

    def body(w_ref, o_ref):
        o_ref[...] = w_ref[...].astype(BF16)

    blk = pl.BlockSpec((tr, c), lambda i: (i, 0))
    return pl.pallas_call(body, out_shape=jax.ShapeDtypeStruct((r, c), BF16), grid=(r // tr,), in_specs=[blk], out_specs=blk,
                          name=name, compiler_params=_params(("parallel",)))(w)


def _place():
    return lax.axis_index("x"), lax.axis_index("y"), lax.axis_index("c")


def _other_chips(x, y):
    return [(1 - x, y), (x, 1 - y), (1 - x, 1 - y)]


def _slab(ref, kind, jj):
    if kind == "col":
        c = ref.shape[2] // 4
        return ref.at[:, :, pl.ds(jj * c, c)]
    r = ref.shape[1] // 4
    return ref.at[:, pl.ds(jj * r, r), :]


def _gather_weights(shards, kinds):
    nt = len(shards)
    shapes = [jax.ShapeDtypeStruct((l, r, 4 * c) if kd == "col" else (l, 4 * r, c), BF16)
              for (l, r, c), kd in zip([a.shape for a in shards], kinds)]

    def body(*refs):
        ins, outs = refs[:nt], refs[nt:2 * nt]
        send_sems, recv_sems, local_sems = refs[2 * nt:]
        x, y, c = _place()
        chips = _other_chips(x, y)

        def push(t, k, landing):
            chip = chips[k]
            return pltpu.make_async_remote_copy(
                src_ref=ins[t], dst_ref=_slab(outs[t], kinds[t], landing), send_sem=send_sems.at[t, k],
                recv_sem=recv_sems.at[t, k], device_id=(chip[0], chip[1], c), device_id_type=MESH)

        mine = [pltpu.make_async_copy(ins[t], _slab(outs[t], kinds[t], 2 * x + y), local_sems.at[t]) for t in range(nt)]
        sends = [push(t, k, 2 * x + y) for t in range(nt) for k in range(3)]
        for cp in mine + sends:
            cp.start()
        for t in range(nt):
            for k in range(3):
                push(t, k, 2 * chips[k][0] + chips[k][1]).wait_recv()
        for cp in sends:
            cp.wait_send()
        for cp in mine:
            cp.wait()

    return pl.pallas_call(
        body, out_shape=shapes, in_specs=[ANY] * nt, out_specs=[ANY] * nt,
        scratch_shapes=[pltpu.SemaphoreType.DMA((nt, 3)), pltpu.SemaphoreType.DMA((nt, 3)), pltpu.SemaphoreType.DMA((nt,))],
        name="gather_weights",
    )(*shards)


def _scatter_grads(grads, kinds):
    nt = len(grads)
    shapes = []
    for g, kd in zip(grads, kinds):
        l, r, c = g.shape
        shapes.append(jax.ShapeDtypeStruct((3, l, r, c // 4) if kd == "col" else (3, l, r // 4, c), g.dtype))

    def body(*refs):
        ins, outs = refs[:nt], refs[nt:2 * nt]
        send_sems, recv_sems = refs[2 * nt:]
        x, y, c = _place()
        chips = _other_chips(x, y)

        def push(t, k):
            chip = chips[k]
            return pltpu.make_async_remote_copy(
                src_ref=_slab(ins[t], kinds[t], 2 * chip[0] + chip[1]), dst_ref=outs[t].at[k], send_sem=send_sems.at[t, k],
                recv_sem=recv_sems.at[t, k], device_id=(chip[0], chip[1], c), device_id_type=MESH)

        sends = [push(t, k) for t in range(nt) for k in range(3)]
        for cp in sends:
            cp.start()
        for cp in sends:
            cp.wait_recv()
        for cp in sends:
            cp.wait_send()

    return pl.pallas_call(
        body, out_shape=shapes, in_specs=[ANY] * nt, out_specs=[ANY] * nt,
        scratch_shapes=[pltpu.SemaphoreType.DMA((nt, 3)), pltpu.SemaphoreType.DMA((nt, 3))], name="scatter_grads",
    )(*grads)


def _sibling_swap(arrs):
    nt = len(arrs)

    def body(*refs):
        ins, outs = refs[:nt], refs[nt:2 * nt]
        send_sems, recv_sems = refs[2 * nt:]
        x, y, c = _place()
        sends = [pltpu.make_async_remote_copy(src_ref=ins[t], dst_ref=outs[t], send_sem=send_sems.at[t], recv_sem=recv_sems.at[t],
                                              device_id=(x, y, 1 - c), device_id_type=MESH) for t in range(nt)]
        for cp in sends:
            cp.start()
        for cp in sends:
            cp.wait_recv()
        for cp in sends:
            cp.wait_send()

    return pl.pallas_call(
        body, out_shape=[jax.ShapeDtypeStruct(a.shape, a.dtype) for a in arrs], in_specs=[ANY] * nt, out_specs=[ANY] * nt,
        scratch_shapes=[pltpu.SemaphoreType.DMA((nt,)), pltpu.SemaphoreType.DMA((nt,))], name="sibling_swap",
    )(*arrs)


def _gather_small(vec, over_c):
    n = vec.shape[0]
    flips = [(dx, dy, dc) for dx in (0, 1) for dy in (0, 1) for dc in ((0, 1) if over_c else (0,))][1:]
    np_ = len(flips)

    def body(v_ref, o_ref, send_sems, recv_sems, local_sem):
        x, y, c = _place()

        def idx(px, py, pc):
            return 4 * px + 2 * py + pc if over_c else 2 * px + py

        def peer(f):
            return (1 - x if f[0] else x, 1 - y if f[1] else y, 1 - c if f[2] else c)

        def push(k, landing):
            return pltpu.make_async_remote_copy(src_ref=v_ref, dst_ref=o_ref.at[landing], send_sem=send_sems.at[k],
                                                recv_sem=recv_sems.at[k], device_id=peer(flips[k]), device_id_type=MESH)

        mine = pltpu.make_async_copy(v_ref, o_ref.at[idx(x, y, c)], local_sem)
        sends = [push(k, idx(x, y, c)) for k in range(np_)]
        for cp in [mine] + sends:
            cp.start()
        for k in range(np_):
            push(k, idx(*peer(flips[k]))).wait_recv()
        for cp in sends:
            cp.wait_send()
        mine.wait()

    return pl.pallas_call(
        body, out_shape=jax.ShapeDtypeStruct((np_ + 1, n, 128), F32), in_specs=[ANY], out_specs=ANY,
        scratch_shapes=[pltpu.SemaphoreType.DMA((np_,)), pltpu.SemaphoreType.DMA((np_,)), pltpu.SemaphoreType.DMA(())],
        name="gather_small_all" if over_c else "gather_small_xy",
    )(vec)


def _sum_rows(buf, tr=512):
    p, n, _ = buf.shape
    tr = min(tr, n)

    def body(b_ref, o_ref):
        acc = b_ref[0]
        for k in range(1, p):
            acc = acc + b_ref[k]
        o_ref[...] = acc

    return pl.pallas_call(
        body, out_shape=jax.ShapeDtypeStruct((n, 128), F32), grid=(n // tr,),
        in_specs=[pl.BlockSpec((p, tr, 128), lambda i: (0, i, 0))], out_specs=pl.BlockSpec((tr, 128), lambda i: (i, 0)),
        name="sum_rows", compiler_params=_params(("parallel",)),
    )(buf)


def _sum_partials(grad, recv, kind, jj, name, tr=128):
    _, l, r, c = recv.shape
    tr = min(tr, r)

    def body(j_ref, g_ref, r0, r1, r2, o_ref):
        o_ref[...] = ((g_ref[...].astype(F32) + r0[...].astype(F32)) + r1[...].astype(F32)) + r2[...].astype(F32)

    if kind == "col":
        own = pl.BlockSpec((None, tr, c), lambda a, b, j: (a, b, j[0]))
    else:
        own = pl.BlockSpec((None, tr, c), lambda a, b, j: (a, j[0] * (r // tr) + b, 0))
    got = lambda k: pl.BlockSpec((None, None, tr, c), functools.partial(lambda k, a, b, j: (k, a, b, 0), k))
    return pl.pallas_call(
        body, out_shape=jax.ShapeDtypeStruct((l, r, c), F32),
        grid_spec=pltpu.PrefetchScalarGridSpec(
            num_scalar_prefetch=1, grid=(l, r // tr), in_specs=[own, got(0), got(1), got(2)],
            out_specs=pl.BlockSpec((None, tr, c), lambda a, b, j: (a, b, 0))),
        name=name, compiler_params=_params(("parallel", "parallel")),
    )(jj, grad, recv, recv, recv)


WEIGHTS = ['norm_mix_w', 'w_in', 'hg_lb_raw', 'hg_norm_w', 'cv_dw_w', 'cv_dw_b', 'cv_ln_w', 'cv_ln_b', 'pl_w', 'pl_scale',
           'lru_conv_w', 'lru_conv_b', 'lru_wa', 'lru_ba', 'lru_wx', 'lru_bx', 'lru_lambda', 'gate_b', 'w_branch', 'w_out',
           'norm_mem_w', 'mem_norm_w', 'xa_wq', 'xa_wkv', 'xa_wo', 'norm_ffn_w', 'ffn_w1', 'ffn_w2', 'final_norm_w']
BIG = {'w_in': 'col', 'w_branch': 'col', 'w_out': 'row', 'xa_wq': 'row', 'xa_wkv': 'col', 'xa_wo': 'row', 'ffn_w1': 'col', 'ffn_w2': 'row'}
SMALL_SPLIT = ('gate_b', 'cv_dw_w', 'lru_conv_w')
SMALL = [n for n in WEIGHTS if n not in BIG]
ROWS_PAD = 512


def _as3d(a):
    return a.reshape((-1,) + a.shape[-2:])


def _pack(parts):
    flat = jnp.concatenate([p.reshape(-1).astype(F32) for p in parts])
    n = -(-flat.shape[0] // (128 * ROWS_PAD)) * ROWS_PAD
    return jnp.pad(flat, (0, n * 128 - flat.shape[0])).reshape(n, 128)


def _unpack(packed, shapes):
    flat, out, o = packed.reshape(-1), [], 0
    for sh in shapes:
        sz = math.prod(sh)
        out.append(flat[o:o + sz].reshape(sh))
        o += sz
    return out


def _block_diag(w):
    h, a, b = w.shape
    eye = jnp.eye(h, dtype=w.dtype)
    return (w[:, :, None, :] * eye[:, None, :, None]).reshape(h * a, h * b)


def _diag_blocks(m, h):
    a, b = m.shape[0] // h, m.shape[1] // h
    return jnp.stack([m[i * a:(i + 1) * a, i * b:(i + 1) * b] for i in range(h)])


def kernel(x, mem, norm_mix_w, w_in, hg_lb_raw, hg_norm_w, cv_dw_w, cv_dw_b, cv_ln_w, cv_ln_b, pl_w, pl_scale, lru_conv_w, lru_conv_b, lru_wa, lru_ba, lru_wx, lru_bx, lru_lambda, gate_b, w_branch, w_out, norm_mem_w, mem_norm_w, xa_wq, xa_wkv, xa_wo, norm_ffn_w, ffn_w1, ffn_w2, final_norm_w, loss_target, m_norm_mix_w, m_w_in, m_hg_lb_raw, m_hg_norm_w, m_cv_dw_w, m_cv_dw_b, m_cv_ln_w, m_cv_ln_b, m_pl_w, m_pl_scale, m_lru_conv_w, m_lru_conv_b, m_lru_wa, m_lru_ba, m_lru_wx, m_lru_bx, m_lru_lambda, m_gate_b, m_w_branch, m_w_out, m_norm_mem_w, m_mem_norm_w, m_xa_wq, m_xa_wkv, m_xa_wo, m_norm_ffn_w, m_ffn_w1, m_ffn_w2, m_final_norm_w, v_norm_mix_w, v_w_in, v_hg_lb_raw, v_hg_norm_w, v_cv_dw_w, v_cv_dw_b, v_cv_ln_w, v_cv_ln_b, v_pl_w, v_pl_scale, v_lru_conv_w, v_lru_conv_b, v_lru_wa, v_lru_ba, v_lru_wx, v_lru_bx, v_lru_lambda, v_gate_b, v_w_branch, v_w_out, v_norm_mem_w, v_mem_norm_w, v_xa_wq, v_xa_wkv, v_xa_wo, v_norm_ffn_w, v_ffn_w1, v_ffn_w2, v_final_norm_w):
    w = dict(zip(WEIGHTS, (norm_mix_w, w_in, hg_lb_raw, hg_norm_w, cv_dw_w, cv_dw_b, cv_ln_w, cv_ln_b, pl_w, pl_scale, lru_conv_w, lru_conv_b, lru_wa, lru_ba, lru_wx, lru_bx, lru_lambda, gate_b, w_branch, w_out, norm_mem_w, mem_norm_w, xa_wq, xa_wkv, xa_wo, norm_ffn_w, ffn_w1, ffn_w2, final_norm_w)))
    m1 = dict(zip(WEIGHTS, (m_norm_mix_w, m_w_in, m_hg_lb_raw, m_hg_norm_w, m_cv_dw_w, m_cv_dw_b, m_cv_ln_w, m_cv_ln_b, m_pl_w, m_pl_scale, m_lru_conv_w, m_lru_conv_b, m_lru_wa, m_lru_ba, m_lru_wx, m_lru_bx, m_lru_lambda, m_gate_b, m_w_branch, m_w_out, m_norm_mem_w, m_mem_norm_w, m_xa_wq, m_xa_wkv, m_xa_wo, m_norm_ffn_w, m_ffn_w1, m_ffn_w2, m_final_norm_w)))
    v1 = dict(zip(WEIGHTS, (v_norm_mix_w, v_w_in, v_hg_lb_raw, v_hg_norm_w, v_cv_dw_w, v_cv_dw_b, v_cv_ln_w, v_cv_ln_b, v_pl_w, v_pl_scale, v_lru_conv_w, v_lru_conv_b, v_lru_wa, v_lru_ba, v_lru_wx, v_lru_bx, v_lru_lambda, v_gate_b, v_w_branch, v_w_out, v_norm_mem_w, v_mem_norm_w, v_xa_wq, v_xa_wkv, v_xa_wo, v_norm_ffn_w, v_ffn_w1, v_ffn_w2, v_final_norm_w)))
    seq = x.shape[1]
    xs, mems, tgt = x.reshape(seq, D_MODEL), mem.reshape(-1, D_MODEL), loss_target.reshape(seq, D_MODEL)
    jj = 2 * lax.axis_index("x") + lax.axis_index("y")
    jj1 = jnp.reshape(jj, (1,)).astype(jnp.int32)

    split_shapes = [w[n].shape for n in SMALL_SPLIT]
    got = _gather_small(_pack([w[n] for n in SMALL_SPLIT]), over_c=False)
    per_chip = [_unpack(got[k], split_shapes) for k in range(4)]
    full_small = {n: jnp.concatenate([per_chip[k][i] for k in range(4)], axis=-1) for i, n in enumerate(SMALL_SPLIT)}
    big_names = list(BIG)
    kinds = [BIG[n] for n in big_names]
    shards = []
    for n in big_names:
        a3 = _as3d(w[n])
        shards.append(_cast_bf16(a3.reshape(-1, a3.shape[-1]), "cast_" + n).reshape(a3.shape))
    wf = dict(zip(big_names, _gather_weights(shards, kinds)))
    wf['w_branch'] = wf['w_branch'].reshape(DEPTH, 4, 512, D_MODEL)

    lb = _lb_fwd(hg_lb_raw)
    row = lambda a: a.reshape(1, -1)

    def layer_params(l):
        return dict(
            nmix=row(norm_mix_w[l]), lb=row(lb[l]), hgnw=row(hg_norm_w[l]),
            cw=jnp.pad(full_small['cv_dw_w'][l], ((0, 32 - CV_K), (0, 0))), cb=row(cv_dw_b[l]), lnw=row(cv_ln_w[l]), lnb=row(cv_ln_b[l]),
            plw=pl_w[l], plsc=row(pl_scale[l]),
            lcw=jnp.pad(full_small['lru_conv_w'][l], ((0, 8 - LRU_CONV), (0, 0))), lcb=row(lru_conv_b[l]),
            wa=_block_diag(lru_wa[l]).astype(BF16), ba=row(lru_ba[l]), wx=_block_diag(lru_wx[l]).astype(BF16), bx=row(lru_bx[l]),
            lam=row(lru_lambda[l]), gb=full_small['gate_b'][l], nmem=row(norm_mem_w[l]), memw=row(mem_norm_w[l]), nffn=row(norm_ffn_w[l]))

    saved = []
    xc = xs
    for l in range(DEPTH):
        p = layer_params(l)
        h = _norm_fwd(xc, p['nmix'], "norm_mix")
        proj = _mm(h, wf['w_in'], "nn", F32, "proj", layer=l)
        b_hg, st = _hgrn_fwd(proj, p['lb'], p['hgnw'])
        b_cv = _conv_fwd(proj, p['cw'], p['cb'], p['lnw'], p['lnb'])
        b_pl = _pool_fwd(proj, p['plw'], p['plsc'])
        b_lru, hs = _lru_fwd(proj, p['lcw'], p['lcb'], p['wa'], p['ba'], p['wx'], p['bx'], p['lam'])
        branches = (b_hg, b_cv, b_pl, b_lru)
        x1 = _merge_fwd(xc, branches, proj, p['gb'], wf['w_branch'], wf['w_out'], l)
        memn = _norm_fwd(mems, p['memw'], "norm_memtok")
        kv = _mm(memn, wf['xa_wkv'], "nn", BF16, "kv_proj", layer=l)
        x2 = _attn_fwd(x1, p['nmem'], wf['xa_wq'], kv, wf['xa_wo'], l)
        x3 = _ffn_fwd(x2, p['nffn'], wf['ffn_w1'], wf['ffn_w2'], l)
        saved.append(dict(p=p, x=xc, h=h, proj=proj, st=st, hs=hs, branches=branches, x1=x1, memn=memn, kv=kv, x2=x2))
        xc = x3

    loss_blk, dx, dfinal = _final_loss(xc, row(final_norm_w), tgt)

    gs = {n: [None] * DEPTH for n in WEIGHTS if n != 'final_norm_w'}
    dlb = [None] * DEPTH
    for l in reversed(range(DEPTH)):
        sv = saved[l]
        p = sv['p']
        dx2, gs['norm_ffn_w'][l], h3, da, r, dxb = _ffn_bwd(sv['x2'], dx, p['nffn'], wf['ffn_w1'], wf['ffn_w2'], l)
        gs['ffn_w1'][l] = _mm(h3, da, "tn", BF16, "dw_ffn1")
        gs['ffn_w2'][l] = _mm(r, dxb, "tn", BF16, "dw_ffn2")
        dx1, gs['norm_mem_w'][l], h2, o, dq, dxb2, dk, dv = _attn_bwd(sv['x1'], dx2, p['nmem'], wf['xa_wq'], sv['kv'], wf['xa_wo'], l)
        gs['xa_wq'][l] = _mm(h2, dq, "tn", BF16, "dw_q")
        gs['xa_wo'][l] = _mm(o, dxb2, "tn", BF16, "dw_o")
        dkv = jnp.concatenate([dk, dv], axis=1)
        gs['xa_wkv'][l] = _mm(sv['memn'], dkv, "tn", BF16, "dw_kv")
        dmemn = _mm(dkv, wf['xa_wkv'], "nt", F32, "dmemn", layer=l)
        _, gs['mem_norm_w'][l] = _norm_bwd(mems, p['memw'], dmemn, None, "norm_memtok_bwd")
        db0, db1, db2, db3, dgp, dup, mg, dxb1, gs['gate_b'][l] = _merge_bwd(
            dx1, sv['branches'], sv['proj'], p['gb'], wf['w_branch'], wf['w_out'], l)
        gs['w_out'][l] = _mm(mg, dxb1, "tn", BF16, "dw_out")
        gs['w_branch'][l] = jnp.stack([_mm(sv['branches'][kb], dup, "tn", BF16, "dw_branch", b_col0=kb * D_MODEL, n=D_MODEL, tn=512)
                                       for kb in range(4)])
        dhg, dlb[l], gs['hg_norm_w'][l] = _hgrn_bwd(sv['proj'], db0, sv['st'], p['lb'], p['hgnw'])
        dcv, dcw, gs['cv_dw_b'][l], gs['cv_ln_w'][l], gs['cv_ln_b'][l] = _conv_bwd(sv['proj'], db1, p['cw'], p['cb'], p['lnw'], p['lnb'])
        gs['cv_dw_w'][l] = dcw[:CV_K]
        dpl, gs['pl_w'][l], gs['pl_scale'][l] = _pool_bwd(sv['proj'], db2, p['plw'], p['plsc'])
        dlru, dlcw, gs['lru_conv_b'][l], dwa, gs['lru_ba'][l], dwx, gs['lru_bx'][l], gs['lru_lambda'][l] = _lru_bwd(
            sv['proj'], sv['hs'], db3, p['lcw'], p['lcb'], p['wa'], p['ba'], p['wx'], p['bx'], p['lam'])
        gs['lru_conv_w'][l] = dlcw[:LRU_CONV]
        gs['lru_wa'][l], gs['lru_wx'][l] = _diag_blocks(dwa, LRU_HEADS), _diag_blocks(dwx, LRU_HEADS)
        dproj = jnp.concatenate([dhg, dcv, dpl, dlru, dgp], axis=1)
        gs['w_in'][l] = _mm(sv['h'], dproj, "tn", BF16, "dw_in")
        dh = _mm(dproj, wf['w_in'], "nt", F32, "dh_mix", tm=256, tn=256, layer=l)
        dx, gs['norm_mix_w'][l] = _norm_bwd(sv['x'], p['nmix'], dh, dx1, "norm_mix_bwd")
    grad_x = dx.reshape(x.shape)
    gs['hg_lb_raw'] = _lb_bwd(hg_lb_raw, jnp.concatenate(dlb, axis=0))

    def full_shape(n):
        return full_small[n].shape if n in SMALL_SPLIT else w[n].shape

    small_full = []
    for n in SMALL:
        g = gs[n] if n == 'hg_lb_raw' else dfinal if n == 'final_norm_w' else jnp.stack(gs[n])
        small_full.append(g.reshape(full_shape(n)))
    total = _sum_rows(_gather_small(_pack(small_full + [loss_blk[0:1, 0:1]]), over_c=True))
    parts = _unpack(total, [full_shape(n) for n in SMALL] + [(1,)])
    loss = parts[-1].reshape(())
    g_small = {}
    for n, g in zip(SMALL, parts[:-1]):
        if n in SMALL_SPLIT:
            width = w[n].shape[-1]
            g = lax.dynamic_slice_in_dim(g, jj * width, width, axis=g.ndim - 1)
        g_small[n] = g
    shapes = [w[n].shape for n in SMALL]
    upd = _adamw(_pack([w[n] for n in SMALL]), [_pack([g_small[n] for n in SMALL])], _pack([m1[n] for n in SMALL]),
                 _pack([v1[n] for n in SMALL]), "adamw_small")
    d_small, m_small, v_small = [dict(zip(SMALL, _unpack(u, shapes))) for u in upd]

    full_grads = [_as3d(jnp.stack(gs[n])) for n in big_names]
    recv = _scatter_grads(full_grads, kinds)
    partial = [_sum_partials(g, r, kd, jj1, "sum_" + n) for n, g, r, kd in zip(big_names, full_grads, recv, kinds)]
    theirs = _sibling_swap(partial)
    g_big, d_big, m_big, v_big = {}, {}, {}, {}
    for n, pa, pb in zip(big_names, partial, theirs):
        c2 = lambda a: a.reshape(-1, a.shape[-1])
        out = _adamw(c2(w[n]), [c2(pa), c2(pb)], c2(m1[n]), c2(v1[n]), "adamw_" + n)
        g_big[n], d_big[n], m_big[n], v_big[n] = [o.reshape(w[n].shape) for o in out]

    pick = lambda small, big: [big[n] if n in BIG else small[n] for n in WEIGHTS]
    return (loss, grad_x, *pick(g_small, g_big), *pick(d_small, d_big), *pick(m_small, m_big), *pick(v_small, v_big))
```

```python
import functools
import math

import jax
import jax.numpy as jnp
from jax import lax
from jax.experimental import pallas as pl
from jax.experimental.pallas import tpu as pltpu

F32 = jnp.float32
BF16 = jnp.bfloat16
MESH = pl.DeviceIdType.MESH
ANY = pl.BlockSpec(memory_space=pl.ANY)

D_MODEL = 1024
DEPTH = 4
CHUNK = 64
SUB = 16
EPS = 1e-6
HG_HEADS, HG_D = 4, 128
CV_W, CV_K = 512, 31
CV_HALO = 32
POOL_WINDOWS = (2, 4, 8, 16)
POOL_HALO = 16
LRU_W, LRU_HEADS, LRU_HD, LRU_CONV = 512, 8, 64, 4
LRU_HALO = 8
LRU_C = 8.0
MIX_W = 4608
IN_W = 8704
XA_HEADS, XA_HD = 4, 256
D_FF = 4096
FF_CHUNK = 1024
ADAM_LR, ADAM_B1, ADAM_B2, ADAM_EPS, ADAM_WD, ADAM_STEP = 0.001, 0.9, 0.999, 1e-08, 0.01, 10
VMEM_LIMIT = 56 * 1024 * 1024
EXP_CLAMP = 80.0
HI = lax.Precision.HIGHEST


def _params(sem=None):
    return pltpu.CompilerParams(dimension_semantics=sem, vmem_limit_bytes=VMEM_LIMIT)


def _sigmoid(x):
    return 1.0 / (1.0 + jnp.exp(-x))


def _dsilu(x, s):
    return s * (1.0 + x * (1.0 - s))


_GELU_C = math.sqrt(2.0 / math.pi)


def _gelu_parts(x):
    t = jnp.tanh(_GELU_C * (x + 0.044715 * x * x * x))
    g = 0.5 * x * (1.0 + t)
    dg = 0.5 * (1.0 + t) + 0.5 * x * (1.0 - t * t) * _GELU_C * (1.0 + 3 * 0.044715 * x * x)
    return g, dg


def _dot(a, b, dims, precision=None):
    return lax.dot_general(a, b, (dims, ((), ())), precision=precision, preferred_element_type=F32)


def _nn(a, b, **k):
    return _dot(a, b, ((1,), (0,)), **k)


def _nt(a, b, **k):
    return _dot(a, b, ((1,), (1,)), **k)


def _tn(a, b, **k):
    return _dot(a, b, ((0,), (0,)), **k)


def _rms_fwd(x, w):
    r = lax.rsqrt(jnp.mean(x * x, axis=-1, keepdims=True) + EPS)
    return x * r * w, r


def _rms_bwd(x, r, w, dy):
    xr = x * r
    g = dy * w
    dx = r * (g - xr * jnp.mean(g * xr, axis=-1, keepdims=True))
    return dx, jnp.sum(dy * xr, axis=0, keepdims=True)


def _lw(shape, index, layer):
    return pl.BlockSpec((None,) + tuple(shape), lambda *g: (layer,) + tuple(index(*g)))


def _mm(a, b, mode, out_dtype, name, tm=512, tn=512, b_col0=0, n=None, layer=None):
    bs = b.shape if layer is None else b.shape[1:]
    if mode == "nn":
        m, k = a.shape
        n = bs[1] if n is None else n
    elif mode == "nt":
        m, k = a.shape
        n = bs[0] if n is None else n
    else:
        k, m = a.shape
        n = bs[1] if n is None else n
    tm, tn = min(tm, m), min(tn, n)
    assert m % tm == 0 and n % tn == 0 and b_col0 % tn == 0
    off = b_col0 // tn

    def body(a_ref, b_ref, o_ref):
        av, bv = a_ref[...].astype(BF16), b_ref[...].astype(BF16)
        o_ref[...] = (_nn if mode == "nn" else _nt if mode == "nt" else _tn)(av, bv).astype(out_dtype)

    def bspec(shape, index):
        return pl.BlockSpec(shape, index) if layer is None else _lw(shape, index, layer)

    if mode == "tn":
        grid = (m // tm, n // tn)
        a_spec = pl.BlockSpec((k, tm), lambda i, j: (0, i))
        b_spec = bspec((k, tn), lambda i, j: (0, j + off))
        o_spec = pl.BlockSpec((tm, tn), lambda i, j: (i, j))
    else:
        grid = (n // tn, m // tm)
        a_spec = pl.BlockSpec((tm, k), lambda j, i: (i, 0))
        if mode == "nn":
            b_spec = bspec((k, tn), lambda j, i: (0, j + off))
        else:
            b_spec = bspec((tn, k), lambda j, i: (j + off, 0))
        o_spec = pl.BlockSpec((tm, tn), lambda j, i: (i, j))
    return pl.pallas_call(
        body, out_shape=jax.ShapeDtypeStruct((m, n), out_dtype), grid=grid,
        in_specs=[a_spec, b_spec], out_specs=o_spec, name=name,
        compiler_params=_params(("parallel", "parallel")),
    )(a, b)


def _norm_fwd(x, w, name, ts=512):
    s, d = x.shape
    ts = min(ts, s)

    def body(x_ref, w_ref, o_ref):
        o_ref[...] = _rms_fwd(x_ref[...], w_ref[...])[0].astype(BF16)

    return pl.pallas_call(
        body, out_shape=jax.ShapeDtypeStruct((s, d), BF16), grid=(s // ts,),
        in_specs=[pl.BlockSpec((ts, d), lambda i: (i, 0)), pl.BlockSpec((1, d), lambda i: (0, 0))],
        out_specs=pl.BlockSpec((ts, d), lambda i: (i, 0)), name=name, compiler_params=_params(("parallel",)),
    )(x, w)


def _norm_bwd(x, w, dy, dres, name, ts=512):
    s, d = x.shape
    ts = min(ts, s)
    with_res = dres is not None

    def body(*refs):
        if with_res:
            x_ref, w_ref, dy_ref, dres_ref, dx_ref, dw_ref = refs
        else:
            x_ref, w_ref, dy_ref, dx_ref, dw_ref = refs
        xv = x_ref[...]
        r = lax.rsqrt(jnp.mean(xv * xv, axis=-1, keepdims=True) + EPS)
        dx, dw = _rms_bwd(xv, r, w_ref[...], dy_ref[...])
        dx_ref[...] = dx + dres_ref[...] if with_res else dx

        @pl.when(pl.program_id(0) == 0)
        def _():
            dw_ref[...] = jnp.zeros_like(dw_ref)

        dw_ref[...] += dw

    row = pl.BlockSpec((ts, d), lambda i: (i, 0))
    vec = pl.BlockSpec((1, d), lambda i: (0, 0))
    return pl.pallas_call(
        body, out_shape=(jax.ShapeDtypeStruct((s, d), F32), jax.ShapeDtypeStruct((1, d), F32)), grid=(s // ts,),
        in_specs=[row, vec, row] + ([row] if with_res else []), out_specs=(row, vec), name=name,
        compiler_params=_params(("arbitrary",)),
    )(*([x, w, dy] + ([dres] if with_res else [])))


def _ffn_fwd(x, nw, w1, w2, layer, ts=256):
    s, d = x.shape
    ts = min(ts, s)
    nj = D_FF // FF_CHUNK

    def body(x_ref, nw_ref, w1_ref, w2_ref, o_ref, h_scr, acc):
        j = pl.program_id(1)

        @pl.when(j == 0)
        def _():
            h_scr[...] = _rms_fwd(x_ref[...], nw_ref[...])[0].astype(BF16)
            acc[...] = jnp.zeros_like(acc)

        a = _nn(h_scr[...], w1_ref[...])
        rl = jnp.maximum(a, 0.0)
        acc[...] += _nn((rl * rl).astype(BF16), w2_ref[...])

        @pl.when(j == nj - 1)
        def _():
            o_ref[...] = x_ref[...] + acc[...]

    row = pl.BlockSpec((ts, d), lambda i, j: (i, 0))
    return pl.pallas_call(
        body, out_shape=jax.ShapeDtypeStruct((s, d), F32), grid=(s // ts, nj),
        in_specs=[row, pl.BlockSpec((1, d), lambda i, j: (0, 0)),
                  _lw((d, FF_CHUNK), lambda i, j: (0, j), layer), _lw((FF_CHUNK, d), lambda i, j: (j, 0), layer)],
        out_specs=row, scratch_shapes=[pltpu.VMEM((ts, d), BF16), pltpu.VMEM((ts, d), F32)], name="ffn_fwd",
        compiler_params=_params(("parallel", "arbitrary")),
    )(x, nw, w1, w2)


def _ffn_bwd(x, dxo, nw, w1, w2, layer, ts=256):
    s, d = x.shape
    ts = min(ts, s)
    nj = D_FF // FF_CHUNK

    def body(x_ref, dxo_ref, nw_ref, w1_ref, w2_ref, dx_ref, dnw_ref, h_ref, da_ref, r_ref, dxb_ref, dh):
        i, j = pl.program_id(0), pl.program_id(1)

        @pl.when(j == 0)
        def _():
            h_ref[...] = _rms_fwd(x_ref[...], nw_ref[...])[0].astype(BF16)
            dxb_ref[...] = dxo_ref[...].astype(BF16)
            dh[...] = jnp.zeros_like(dh)

        a = _nn(h_ref[...], w1_ref[...])
        rl = jnp.maximum(a, 0.0)
        r_ref[...] = (rl * rl).astype(BF16)
        da = (_nt(dxb_ref[...], w2_ref[...]) * (2.0 * rl)).astype(BF16)
        da_ref[...] = da
        dh[...] += _nt(da, w1_ref[...])

        @pl.when(jnp.logical_and(i == 0, j == 0))
        def _():
            dnw_ref[...] = jnp.zeros_like(dnw_ref)

        @pl.when(j == nj - 1)
        def _():
            xv = x_ref[...]
            r = lax.rsqrt(jnp.mean(xv * xv, axis=-1, keepdims=True) + EPS)
            dx, dw = _rms_bwd(xv, r, nw_ref[...], dh[...])
            dx_ref[...] = dxo_ref[...] + dx
            dnw_ref[...] += dw

    row = pl.BlockSpec((ts, d), lambda i, j: (i, 0))
    vec = pl.BlockSpec((1, d), lambda i, j: (0, 0))
    ffc = pl.BlockSpec((ts, FF_CHUNK), lambda i, j: (i, j))
    return pl.pallas_call(
        body,
        out_shape=(jax.ShapeDtypeStruct((s, d), F32), jax.ShapeDtypeStruct((1, d), F32), jax.ShapeDtypeStruct((s, d), BF16),
                   jax.ShapeDtypeStruct((s, D_FF), BF16), jax.ShapeDtypeStruct((s, D_FF), BF16), jax.ShapeDtypeStruct((s, d), BF16)),
        grid=(s // ts, nj),
        in_specs=[row, row, vec, _lw((d, FF_CHUNK), lambda i, j: (0, j), layer), _lw((FF_CHUNK, d), lambda i, j: (j, 0), layer)],
        out_specs=(row, vec, row, ffc, ffc, row), scratch_shapes=[pltpu.VMEM((ts, d), F32)], name="ffn_bwd",
        compiler_params=_params(("arbitrary", "arbitrary")),
    )(x, dxo, nw, w1, w2)


def _attn_probs(q, k_ref):
    ps = []
    for hd in range(XA_HEADS):
        c = slice(hd * XA_HD, (hd + 1) * XA_HD)
        sc = _nt(q[:, c].astype(BF16), k_ref[:, c]) * (XA_HD ** -0.5)
        e = jnp.exp(sc - jnp.max(sc, axis=-1, keepdims=True))
        ps.append(e / jnp.sum(e, axis=-1, keepdims=True))
    return ps


def _attn_fwd(x, nw, wq, kv, wo, layer, ts=256):
    s, d = x.shape
    ts = min(ts, s)
    nm = kv.shape[0]

    def body(x_ref, nw_ref, wq_ref, k_ref, v_ref, wo_ref, o_ref):
        xv = x_ref[...]
        h = _rms_fwd(xv, nw_ref[...])[0].astype(BF16)
        q = _nn(h, wq_ref[...])
        ps = _attn_probs(q, k_ref)
        o = jnp.concatenate([_nn(ps[hd].astype(BF16), v_ref[:, hd * XA_HD:(hd + 1) * XA_HD]) for hd in range(XA_HEADS)], axis=1)
        o_ref[...] = xv + _nn(o.astype(BF16), wo_ref[...])

    row = pl.BlockSpec((ts, d), lambda i: (i, 0))
    full = lambda r, c: pl.BlockSpec((r, c), lambda i: (0, 0))
    wsp = _lw((d, d), lambda i: (0, 0), layer)
    return pl.pallas_call(
        body, out_shape=jax.ShapeDtypeStruct((s, d), F32), grid=(s // ts,),
        in_specs=[row, full(1, d), wsp, full(nm, d), pl.BlockSpec((nm, d), lambda i: (0, 1)), wsp], out_specs=row, name="attn_fwd",
        compiler_params=_params(("parallel",)),
    )(x, nw, wq, kv, kv, wo)


def _attn_bwd(x, dxo, nw, wq, kv, wo, layer, ts=256):
    s, d = x.shape
    ts = min(ts, s)
    nm = kv.shape[0]

    def body(x_ref, dxo_ref, nw_ref, wq_ref, k_ref, v_ref, wo_ref,
             dx_ref, dnw_ref, h_ref, o_ref, dq_ref, dxb_ref, dk_ref, dv_ref):
        xv = x_ref[...]
        hf, r = _rms_fwd(xv, nw_ref[...])
        h = hf.astype(BF16)
        h_ref[...] = h
        q = _nn(h, wq_ref[...])
        qb = q.astype(BF16)
        ps = _attn_probs(q, k_ref)
        dxb = dxo_ref[...].astype(BF16)
        dxb_ref[...] = dxb
        do = _nt(dxb, wo_ref[...])

        @pl.when(pl.program_id(0) == 0)
        def _():
            dnw_ref[...] = jnp.zeros_like(dnw_ref)
            dk_ref[...] = jnp.zeros_like(dk_ref)
            dv_ref[...] = jnp.zeros_like(dv_ref)

        dqs = []
        for hd in range(XA_HEADS):
            c = slice(hd * XA_HD, (hd + 1) * XA_HD)
            p = ps[hd]
            pb = p.astype(BF16)
            dob = do[:, c].astype(BF16)
            o_ref[:, c] = _nn(pb, v_ref[:, c]).astype(BF16)
            dp = _nt(dob, v_ref[:, c])
            ds = (p * (dp - jnp.sum(p * dp, axis=-1, keepdims=True)) * (XA_HD ** -0.5)).astype(BF16)
            dqs.append(_nn(ds, k_ref[:, c]))
            dk_ref[:, c] += _tn(ds, qb[:, c])
            dv_ref[:, c] += _tn(pb, dob)
        dq = jnp.concatenate(dqs, axis=1).astype(BF16)
        dq_ref[...] = dq
        dx, dw = _rms_bwd(xv, r, nw_ref[...], _nt(dq, wq_ref[...]))
        dx_ref[...] = dxo_ref[...] + dx
        dnw_ref[...] += dw

    row = pl.BlockSpec((ts, d), lambda i: (i, 0))
    full = lambda r, c: pl.BlockSpec((r, c), lambda i: (0, 0))
    sd = lambda dt: jax.ShapeDtypeStruct((s, d), dt)
    return pl.pallas_call(
        body,
        out_shape=(sd(F32), jax.ShapeDtypeStruct((1, d), F32), sd(BF16), sd(BF16), sd(BF16), sd(BF16),
                   jax.ShapeDtypeStruct((nm, d), F32), jax.ShapeDtypeStruct((nm, d), F32)),
        grid=(s // ts,),
        in_specs=[row, row, full(1, d), _lw((d, d), lambda i: (0, 0), layer), full(nm, d), pl.BlockSpec((nm, d), lambda i: (0, 1)),
                  _lw((d, d), lambda i: (0, 0), layer)],
        out_specs=(row, full(1, d), row, row, row, row, full(nm, d), full(nm, d)), name="attn_bwd",
        compiler_params=_params(("arbitrary",)),
    )(x, dxo, nw, wq, kv, kv, wo)


GATE_BLK0 = MIX_W // 512


def _merge_specs(ts, layer):
    row = pl.BlockSpec((ts, D_MODEL), lambda i: (i, 0))
    br = pl.BlockSpec((ts, 512), lambda i: (i, 0))
    gates = [pl.BlockSpec((ts, 512), functools.partial(lambda n, i: (i, GATE_BLK0 + n), n)) for n in range(8)]
    full = lambda *shape: pl.BlockSpec(shape, lambda i: (0,) * len(shape))
    weights = [full(4, D_MODEL), _lw((4, 512, D_MODEL), lambda i: (0, 0, 0), layer), _lw((D_MODEL, D_MODEL), lambda i: (0, 0), layer)]
    return row, br, gates, full, weights


def _merge_gates(gp_refs, gb_ref, kb):
    gp = jnp.concatenate([gp_refs[2 * kb][...], gp_refs[2 * kb + 1][...]], axis=1)
    return _sigmoid(gp + gb_ref[kb:kb + 1, :])


def _merge_fwd(x, branches, proj, gate_b, wb, wout, layer, ts=256):
    s, d = x.shape
    ts = min(ts, s)

    def body(x_ref, b0, b1, b2, b3, g0, g1, g2, g3, g4, g5, g6, g7, gb_ref, wb_ref, wo_ref, o_ref):
        brs, gps = (b0, b1, b2, b3), (g0, g1, g2, g3, g4, g5, g6, g7)
        merged = jnp.zeros((ts, d), F32)
        for kb in range(4):
            merged += _merge_gates(gps, gb_ref, kb) * _nn(brs[kb][...], wb_ref[kb])
        o_ref[...] = x_ref[...] + _nn(merged.astype(BF16), wo_ref[...])

    row, br, gates, full, weights = _merge_specs(ts, layer)
    return pl.pallas_call(
        body, out_shape=jax.ShapeDtypeStruct((s, d), F32), grid=(s // ts,),
        in_specs=[row, br, br, br, br] + gates + weights, out_specs=row, name="merge_fwd",
        compiler_params=_params(("parallel",)),
    )(x, *branches, *([proj] * 8), gate_b, wb, wout)


def _merge_bwd(dxo, branches, proj, gate_b, wb, wout, layer, ts=256):
    s, d = dxo.shape
    ts = min(ts, s)

    def body(dxo_ref, b0, b1, b2, b3, g0, g1, g2, g3, g4, g5, g6, g7, gb_ref, wb_ref, wo_ref,
             db0, db1, db2, db3, dgp_ref, dup_ref, mg_ref, dxb_ref, dgb_ref):
        brs, gps, dbs = (b0, b1, b2, b3), (g0, g1, g2, g3, g4, g5, g6, g7), (db0, db1, db2, db3)
        dxb = dxo_ref[...].astype(BF16)
        dxb_ref[...] = dxb
        dm = _nt(dxb, wo_ref[...])

        @pl.when(pl.program_id(0) == 0)
        def _():
            dgb_ref[...] = jnp.zeros_like(dgb_ref)

        merged = jnp.zeros((ts, d), F32)
        for kb in range(4):
            c = slice(kb * d, (kb + 1) * d)
            g = _merge_gates(gps, gb_ref, kb)
            up = _nn(brs[kb][...], wb_ref[kb])
            merged += g * up
            dup = (dm * g).astype(BF16)
            dup_ref[:, c] = dup
            dgp = dm * up * g * (1.0 - g)
            dgp_ref[:, c] = dgp.astype(BF16)
            dgb_ref[kb:kb + 1, :] += jnp.sum(dgp, axis=0, keepdims=True)
            dbs[kb][...] = _nt(dup, wb_ref[kb])
        mg_ref[...] = merged.astype(BF16)

    row, br, gates, full, weights = _merge_specs(ts, layer)
    wide = pl.BlockSpec((ts, 4 * d), lambda i: (i, 0))
    sb = jax.ShapeDtypeStruct((s, 512), F32)
    return pl.pallas_call(
        body,
        out_shape=(sb, sb, sb, sb, jax.ShapeDtypeStruct((s, 4 * d), BF16), jax.ShapeDtypeStruct((s, 4 * d), BF16),
                   jax.ShapeDtypeStruct((s, d), BF16), jax.ShapeDtypeStruct((s, d), BF16), jax.ShapeDtypeStruct((4, d), F32)),
        grid=(s // ts,),
        in_specs=[row, br, br, br, br] + gates + weights,
        out_specs=(br, br, br, br, wide, wide, row, row, full(4, d)), name="merge_bwd",
        compiler_params=_params(("arbitrary",)),
    )(dxo, *branches, *([proj] * 8), gate_b, wb, wout)


def _tri(n, upper=False):
    r = lax.broadcasted_iota(jnp.int32, (n, n), 0)
    c = lax.broadcasted_iota(jnp.int32, (n, n), 1)
    return jnp.where((c >= r) if upper else (c <= r), 1.0, 0.0).astype(F32)


def _hg_gates(hq, hf, lb):
    sg = _sigmoid(hf)
    fg = lb + (1.0 - lb) * sg
    sq = _sigmoid(hq)
    return sg, fg, 1.0 - fg, jnp.log(fg), hq * sq, sq


def _hg_intra(qf, kk, b):
    out = []
    col = lax.broadcasted_iota(jnp.int32, (SUB, CHUNK), 1)
    row = lax.broadcasted_iota(jnp.int32, (SUB, CHUNK), 0)
    for i in range(CHUNK // SUB):
        rs = slice(i * SUB, (i + 1) * SUB)
        ref = b[i * SUB - 1:i * SUB, :] if i else jnp.zeros((1, b.shape[1]), F32)
        eq = jnp.exp(b[rs] - ref)
        ek = jnp.exp(jnp.minimum(ref - b, EXP_CLAMP))
        out.append((qf[rs] * eq, kk * ek, col <= row + i * SUB, eq, ek))
    return out


def _hg_chunk_fwd(qf, kk, b, v, st):
    parts = _hg_intra(qf, kk, b)
    vb = v.astype(BF16)
    att = [jnp.where(m, _nt(qt.astype(BF16), kt.astype(BF16)), 0.0) for qt, kt, m, _, _ in parts]
    o = jnp.concatenate([_nn(a.astype(BF16), vb) for a in att], axis=0)
    qh = qf * jnp.exp(b)
    o = o + _nt(qh.astype(BF16), st.astype(BF16))
    bl = b[CHUNK - 1:CHUNK, :]
    kh = kk * jnp.exp(bl - b)
    return o, parts, att, qh, kh, jnp.exp(bl)


def _hgrn_fwd(proj, lb, nw, ts=256):
    s = proj.shape[0]
    ts = min(ts, s)
    nch = ts // CHUNK

    def body(q_ref, f_ref, v_ref, g_ref, lb_ref, nw_ref, o_ref, st_ref, st):
        @pl.when(pl.program_id(0) == 0)
        def _():
            st[...] = jnp.zeros_like(st)

        tri = _tri(CHUNK)

        def chunk(c, carry):
            rows = pl.ds(pl.multiple_of(c * CHUNK, CHUNK), CHUNK)
            _, _, kk, lf, qf, _ = _hg_gates(q_ref[rows, :], f_ref[rows, :], lb_ref[...])
            b = _nn(tri, lf, precision=lax.Precision.HIGHEST)
            hv, hg = v_ref[rows, :], g_ref[rows, :]
            st_ref[c] = st[...]
            for h in range(HG_HEADS):
                cs = slice(h * HG_D, (h + 1) * HG_D)
                o, _, _, _, kh, ebl = _hg_chunk_fwd(qf[:, cs], kk[:, cs], b[:, cs], hv[:, cs], st[h])
                st[h] = st[h] * ebl + _tn(hv[:, cs].astype(BF16), kh.astype(BF16))
                on = _rms_fwd(o, nw_ref[...])[0]
                gh = hg[:, cs]
                o_ref[rows, cs] = (on * gh * _sigmoid(gh)).astype(BF16)
            return carry

        lax.fori_loop(0, nch, chunk, 0)

    col = lambda n: pl.BlockSpec((ts, 512), functools.partial(lambda n, i: (i, n), n))
    return pl.pallas_call(
        body,
        out_shape=(jax.ShapeDtypeStruct((s, 512), BF16), jax.ShapeDtypeStruct((s // CHUNK, HG_HEADS, HG_D, HG_D), F32)),
        grid=(s // ts,),
        in_specs=[col(0), col(1), col(2), col(3), pl.BlockSpec((1, 512), lambda i: (0, 0)), pl.BlockSpec((1, HG_D), lambda i: (0, 0))],
        out_specs=(pl.BlockSpec((ts, 512), lambda i: (i, 0)), pl.BlockSpec((nch, HG_HEADS, HG_D, HG_D), lambda i: (i, 0, 0, 0))),
        scratch_shapes=[pltpu.VMEM((HG_HEADS, HG_D, HG_D), F32)], name="hgrn_fwd",
        compiler_params=_params(("arbitrary",)),
    )(proj, proj, proj, proj, lb, nw)


def _hgrn_bwd(proj, dout, states, lb, nw, ts=256):
    s = proj.shape[0]
    ts = min(ts, s)
    nch = ts // CHUNK
    nt = s // ts

    def body(q_ref, f_ref, v_ref, g_ref, do_ref, st_ref, lb_ref, nw_ref, dp_ref, dlb_ref, dnw_ref, dst):
        @pl.when(pl.program_id(0) == 0)
        def _():
            dst[...] = jnp.zeros_like(dst)
            dlb_ref[...] = jnp.zeros_like(dlb_ref)
            dnw_ref[...] = jnp.zeros_like(dnw_ref)

        tri, triu = _tri(CHUNK), _tri(CHUNK, upper=True)
        last = lax.broadcasted_iota(jnp.int32, (CHUNK, HG_D), 0) == CHUNK - 1
        nwv = nw_ref[...]

        def chunk(cc, carry):
            c = nch - 1 - cc
            rows = pl.ds(pl.multiple_of(c * CHUNK, CHUNK), CHUNK)
            hq, hf, hv, hg = q_ref[rows, :], f_ref[rows, :], v_ref[rows, :], g_ref[rows, :]
            lbv = lb_ref[...]
            sg, fg, kk, lf, qf, sq = _hg_gates(hq, hf, lbv)
            b = _nn(tri, lf, precision=lax.Precision.HIGHEST)
            dov = do_ref[rows, :]
            dqf_l, dkk_l, db_l, dv_l, dg_l = [], [], [], [], []
            for h in range(HG_HEADS):
                cs = slice(h * HG_D, (h + 1) * HG_D)
                stp = st_ref[c, h]
                bh, vh, gh = b[:, cs], hv[:, cs], hg[:, cs]
                o, parts, att, qh, kh, ebl = _hg_chunk_fwd(qf[:, cs], kk[:, cs], bh, vh, stp)
                sgg = _sigmoid(gh)
                on, r = _rms_fwd(o, nwv)
                d_on = dov[:, cs] * (gh * sgg)
                dg_l.append(dov[:, cs] * on * _dsilu(gh, sgg))
                do, dnw = _rms_bwd(o, r, nwv, d_on)
                dnw_ref[...] += dnw
                dob, vb = do.astype(BF16), vh.astype(BF16)
                dsth = dst[h]
                dstb = dsth.astype(BF16)
                dqh = _nn(do, stp, precision=HI)
                dkh = _nn(vh, dsth, precision=HI)
                dv = _nt(kh.astype(BF16), dstb)
                eb = jnp.exp(bh)
                ekl = jnp.exp(bh[CHUNK - 1:CHUNK, :] - bh)
                dqf, dkk = dqh * eb, dkh * ekl
                db = dqh * qh - dkh * kh
                dbl = jnp.sum(dkh * kh, axis=0, keepdims=True) + ebl * jnp.sum(dsth * stp, axis=0, keepdims=True)
                dst[h] = dsth * ebl + _tn(dob, qh.astype(BF16))
                dq_rows = []
                for i, (qt, kt, m, eq, ek) in enumerate(parts):
                    rs = slice(i * SUB, (i + 1) * SUB)
                    da = jnp.where(m, _nt(dob[rs], vb), 0.0)
                    dv = dv + _tn(att[i].astype(BF16), dob[rs])
                    dqt = _nn(da, kt, precision=HI)
                    dkt = _tn(da, qt, precision=HI)
                    dq_rows.append((dqt * eq, dqt * qt))
                    dkk = dkk + dkt * ek
                    db = db - dkt * kt
                dqf = dqf + jnp.concatenate([a for a, _ in dq_rows], axis=0)
                db = db + jnp.concatenate([a for _, a in dq_rows], axis=0) + jnp.where(last, dbl, 0.0)
                dqf_l.append(dqf); dkk_l.append(dkk); db_l.append(db); dv_l.append(dv)
            cat = lambda l: jnp.concatenate(l, axis=1)
            dlf = _nn(triu, cat(db_l), precision=lax.Precision.HIGHEST)
            dfg = dlf / fg - cat(dkk_l)
            dlb_ref[...] += jnp.sum(dfg * (1.0 - sg), axis=0, keepdims=True)
            dp_ref[rows, 0:512] = (cat(dqf_l) * _dsilu(hq, sq)).astype(BF16)
            dp_ref[rows, 512:1024] = (dfg * (1.0 - lbv) * sg * (1.0 - sg)).astype(BF16)
            dp_ref[rows, 1024:1536] = cat(dv_l).astype(BF16)
            dp_ref[rows, 1536:2048] = cat(dg_l).astype(BF16)
            return carry

        lax.fori_loop(0, nch, chunk, 0)

    col = lambda n: pl.BlockSpec((ts, 512), functools.partial(lambda n, i: (nt - 1 - i, n), n))
    vec = lambda n: pl.BlockSpec((1, n), lambda i: (0, 0))
    return pl.pallas_call(
        body,
        out_shape=(jax.ShapeDtypeStruct((s, 2048), BF16), jax.ShapeDtypeStruct((1, 512), F32), jax.ShapeDtypeStruct((1, HG_D), F32)),
        grid=(nt,),
        in_specs=[col(0), col(1), col(2), col(3), pl.BlockSpec((ts, 512), lambda i: (nt - 1 - i, 0)),
                  pl.BlockSpec((nch, HG_HEADS, HG_D, HG_D), lambda i: (nt - 1 - i, 0, 0, 0)), vec(512), vec(HG_D)],
        out_specs=(pl.BlockSpec((ts, 2048), lambda i: (nt - 1 - i, 0)), vec(512), vec(HG_D)),
        scratch_shapes=[pltpu.VMEM((HG_HEADS, HG_D, HG_D), F32)], name="hgrn_bwd",
        compiler_params=_params(("arbitrary",)),
    )(proj, proj, proj, proj, dout, states, lb, nw)


CV_BLK = 2048 // 512


def _halo_before(ts, halo, colblk):
    return pl.BlockSpec((halo, 512), functools.partial(lambda cb, i: (jnp.maximum(i * (ts // halo) - 1, 0), cb), colblk))


def _cv_front(a_ref, g_ref, ah_ref, gh_ref, ext, first):
    a, sg = a_ref[...], _sigmoid(g_ref[...])
    zh = ah_ref[...] * _sigmoid(gh_ref[...])
    ext[0:CV_HALO, :] = jnp.where(first, 0.0, zh)
    ext[CV_HALO:, :] = a * sg
    return a, sg


def _cv_conv_ln(ext, w_ref, b_ref, ts):
    y = jnp.zeros((ts, CV_W), F32) + b_ref[...]
    for j in range(CV_K):
        y = y + w_ref[j:j + 1, :] * ext[pl.ds(CV_HALO - (CV_K - 1) + j, ts), :]
    mu = jnp.mean(y, axis=-1, keepdims=True)
    yc = y - mu
    r = lax.rsqrt(jnp.mean(yc * yc, axis=-1, keepdims=True) + EPS)
    return yc * r, r


def _conv_fwd(proj, w, b, lnw, lnb, ts=256):
    s = proj.shape[0]
    ts = min(ts, s)

    def body(a_ref, g_ref, ah_ref, gh_ref, w_ref, b_ref, lnw_ref, lnb_ref, o_ref, ext):
        _cv_front(a_ref, g_ref, ah_ref, gh_ref, ext, pl.program_id(0) == 0)
        yh, _ = _cv_conv_ln(ext, w_ref, b_ref, ts)
        yn = yh * lnw_ref[...] + lnb_ref[...]
        o_ref[...] = (yn * _sigmoid(yn)).astype(BF16)

    col = lambda n: pl.BlockSpec((ts, 512), functools.partial(lambda n, i: (i, n), n))
    vec = pl.BlockSpec((1, CV_W), lambda i: (0, 0))
    return pl.pallas_call(
        body, out_shape=jax.ShapeDtypeStruct((s, CV_W), BF16), grid=(s // ts,),
        in_specs=[col(CV_BLK), col(CV_BLK + 1), _halo_before(ts, CV_HALO, CV_BLK), _halo_before(ts, CV_HALO, CV_BLK + 1),
                  pl.BlockSpec((32, CV_W), lambda i: (0, 0)), vec, vec, vec],
        out_specs=pl.BlockSpec((ts, CV_W), lambda i: (i, 0)), scratch_shapes=[pltpu.VMEM((ts + CV_HALO, CV_W), F32)],
        name="conv_fwd", compiler_params=_params(("parallel",)),
    )(proj, proj, proj, proj, w, b, lnw, lnb)


def _conv_bwd(proj, dout, w, b, lnw, lnb, ts=256):
    s = proj.shape[0]
    ts = min(ts, s)
    nt = s // ts

    def body(a_ref, g_ref, ah_ref, gh_ref, do_ref, w_ref, b_ref, lnw_ref, lnb_ref,
             du_ref, dw_ref, db_ref, dlnw_ref, dlnb_ref, ext, dyext, carry):
        i = pl.program_id(0)

        @pl.when(i == 0)
        def _():
            carry[...] = jnp.zeros_like(carry)
            for ref in (dw_ref, db_ref, dlnw_ref, dlnb_ref):
                ref[...] = jnp.zeros_like(ref)

        a, sg = _cv_front(a_ref, g_ref, ah_ref, gh_ref, ext, i == nt - 1)
        yh, r = _cv_conv_ln(ext, w_ref, b_ref, ts)
        yn = yh * lnw_ref[...] + lnb_ref[...]
        dyn = do_ref[...] * _dsilu(yn, _sigmoid(yn))
        dlnw_ref[...] += jnp.sum(dyn * yh, axis=0, keepdims=True)
        dlnb_ref[...] += jnp.sum(dyn, axis=0, keepdims=True)
        gl = dyn * lnw_ref[...]
        dy = r * (gl - jnp.mean(gl, axis=-1, keepdims=True) - yh * jnp.mean(gl * yh, axis=-1, keepdims=True))
        db_ref[...] += jnp.sum(dy, axis=0, keepdims=True)
        dyext[0:ts, :] = dy
        dyext[ts:, :] = carry[...]
        carry[...] = dy[0:CV_HALO, :]
        dz = jnp.zeros((ts, CV_W), F32)
        for j in range(CV_K):
            dw_ref[j:j + 1, :] += jnp.sum(dy * ext[pl.ds(CV_HALO - (CV_K - 1) + j, ts), :], axis=0, keepdims=True)
            dz = dz + w_ref[j:j + 1, :] * dyext[pl.ds(CV_K - 1 - j, ts), :]
        du_ref[:, 0:CV_W] = (dz * sg).astype(BF16)
        du_ref[:, CV_W:] = (dz * a * sg * (1.0 - sg)).astype(BF16)

    rev = lambda n: pl.BlockSpec((ts, 512), functools.partial(lambda n, i: (nt - 1 - i, n), n))
    halo = lambda n: pl.BlockSpec((CV_HALO, 512), functools.partial(
        lambda n, i: (jnp.maximum((nt - 1 - i) * (ts // CV_HALO) - 1, 0), n), n))
    vec = pl.BlockSpec((1, CV_W), lambda i: (0, 0))
    wsp = pl.BlockSpec((32, CV_W), lambda i: (0, 0))
    v1 = jax.ShapeDtypeStruct((1, CV_W), F32)
    return pl.pallas_call(
        body, out_shape=(jax.ShapeDtypeStruct((s, 2 * CV_W), BF16), jax.ShapeDtypeStruct((32, CV_W), F32), v1, v1, v1),
        grid=(nt,),
        in_specs=[rev(CV_BLK), rev(CV_BLK + 1), halo(CV_BLK), halo(CV_BLK + 1), rev(0), wsp, vec, vec, vec],
        out_specs=(pl.BlockSpec((ts, 2 * CV_W), lambda i: (nt - 1 - i, 0)), wsp, vec, vec, vec),
        scratch_shapes=[pltpu.VMEM((ts + CV_HALO, CV_W), F32), pltpu.VMEM((ts + CV_HALO, CV_W), F32), pltpu.VMEM((CV_HALO, CV_W), F32)],
        name="conv_bwd", compiler_params=_params(("arbitrary",)),
    )(proj, proj, proj, proj, dout, w, b, lnw, lnb)


PL_BLK = 3072 // 512


def _pool_windows(ext, t0, ts):
    n = ext.shape[0]
    t = t0 + lax.broadcasted_iota(jnp.int32, (ts, 1), 0)
    out = []
    for g, wdw in enumerate(POOL_WINDOWS):
        e = ext[:, g * 128:(g + 1) * 128]
        acc, k = e, 1
        while k < wdw:
            acc = acc + pltpu.roll(acc, k, 0)
            k *= 2
        cnt = jnp.minimum(t + 1, wdw).astype(F32)
        out.append(acc[POOL_HALO:] / cnt - e[POOL_HALO:])
    return out


def _pool_fwd(proj, w, sc, ts=256):
    s = proj.shape[0]
    ts = min(ts, s)

    def body(u_ref, uh_ref, w_ref, sc_ref, o_ref):
        i = pl.program_id(0)
        ext = jnp.concatenate([jnp.where(i == 0, 0.0, uh_ref[...]), u_ref[...]], axis=0)
        ps = _pool_windows(ext, i * ts, ts)
        y = jnp.concatenate([_nn(ps[g].astype(BF16), w_ref[g].astype(BF16)) for g in range(4)], axis=1)
        o_ref[...] = (y * sc_ref[...]).astype(BF16)

    return pl.pallas_call(
        body, out_shape=jax.ShapeDtypeStruct((s, 512), BF16), grid=(s // ts,),
        in_specs=[pl.BlockSpec((ts, 512), lambda i: (i, PL_BLK)), _halo_before(ts, POOL_HALO, PL_BLK),
                  pl.BlockSpec((4, 128, 128), lambda i: (0, 0, 0)), pl.BlockSpec((1, 512), lambda i: (0, 0))],
        out_specs=pl.BlockSpec((ts, 512), lambda i: (i, 0)), name="pool_fwd", compiler_params=_params(("parallel",)),
    )(proj, proj, w, sc)


def _pool_bwd(proj, dout, w, sc, ts=256):
    s = proj.shape[0]
    ts = min(ts, s)
    nt = s // ts
    n = ts + POOL_HALO

    def body(u_ref, uh_ref, do_ref, doh_ref, w_ref, sc_ref, du_ref, dw_ref, dsc_ref):
        i = pl.program_id(0)

        @pl.when(i == 0)
        def _():
            dw_ref[...] = jnp.zeros_like(dw_ref)
            dsc_ref[...] = jnp.zeros_like(dsc_ref)

        ext = jnp.concatenate([jnp.where(i == 0, 0.0, uh_ref[...]), u_ref[...]], axis=0)
        ps = _pool_windows(ext, i * ts, ts)
        dov = do_ref[...]
        dyext = jnp.concatenate([dov, jnp.where(i == nt - 1, 0.0, doh_ref[...])], axis=0) * sc_ref[...]
        t = i * ts + lax.broadcasted_iota(jnp.int32, (n, 1), 0)
        row = lax.broadcasted_iota(jnp.int32, (n, 1), 0)
        dus = []
        for g, wdw in enumerate(POOL_WINDOWS):
            cs = slice(g * 128, (g + 1) * 128)
            wg, pb = w_ref[g].astype(BF16), ps[g].astype(BF16)
            dsc_ref[:, cs] += jnp.sum(dov[:, cs] * _nn(pb, wg), axis=0, keepdims=True)
            dyg = dyext[:, cs].astype(BF16)
            dw_ref[g] += _tn(pb, dyg[0:ts])
            dp = _nt(dyg, wg)
            acc, k = dp / jnp.minimum(t + 1, wdw).astype(F32), 1
            while k < wdw:
                acc = acc + jnp.where(row < n - k, pltpu.roll(acc, n - k, 0), 0.0)
                k *= 2
            dus.append(acc[0:ts] - dp[0:ts])
        du_ref[...] = jnp.concatenate(dus, axis=1).astype(BF16)

    tile = lambda cb: pl.BlockSpec((ts, 512), functools.partial(lambda cb, i: (i, cb), cb))
    after = pl.BlockSpec((POOL_HALO, 512), lambda i: (jnp.minimum((i + 1) * (ts // POOL_HALO), s // POOL_HALO - 1), 0))
    wsp, vec = pl.BlockSpec((4, 128, 128), lambda i: (0, 0, 0)), pl.BlockSpec((1, 512), lambda i: (0, 0))
    return pl.pallas_call(
        body, out_shape=(jax.ShapeDtypeStruct((s, 512), BF16), jax.ShapeDtypeStruct((4, 128, 128), F32), jax.ShapeDtypeStruct((1, 512), F32)),
        grid=(nt,),
        in_specs=[tile(PL_BLK), _halo_before(ts, POOL_HALO, PL_BLK), tile(0), after, wsp, vec],
        out_specs=(tile(0), wsp, vec), name="pool_bwd", compiler_params=_params(("arbitrary",)),
    )(proj, proj, dout, dout, w, sc)


LX_BLK, LY_BLK = 3584 // 512, 4096 // 512
LRU_OFF = LRU_HALO - (LRU_CONV - 1)


def _scan_fwd(a, b):
    n = a.shape[0]
    row = lax.broadcasted_iota(jnp.int32, (n, 1), 0)
    k = 1
    while k < n:
        m = row >= k
        b = jnp.where(m, a * pltpu.roll(b, k, 0) + b, b)
        a = jnp.where(m, a * pltpu.roll(a, k, 0), a)
        k *= 2
    return a, b


def _scan_rev(a, b):
    n = a.shape[0]
    row = lax.broadcasted_iota(jnp.int32, (n, 1), 0)
    k = 1
    while k < n:
        m = row < n - k
        b = jnp.where(m, a * pltpu.roll(b, n - k, 0) + b, b)
        a = jnp.where(m, a * pltpu.roll(a, n - k, 0), a)
        k *= 2
    return b


def _lru_gates(x_ref, xh_ref, ext, first, cw_ref, cb_ref, wa_ref, ba_ref, wx_ref, bx_ref, lam_ref, ts):
    ext[0:LRU_HALO, :] = jnp.where(first, 0.0, xh_ref[...])
    ext[LRU_HALO:, :] = x_ref[...]
    xc = jnp.zeros((ts, LRU_W), F32) + cb_ref[...]
    for j in range(LRU_CONV):
        xc = xc + cw_ref[j:j + 1, :] * ext[pl.ds(LRU_OFF + j, ts), :]
    xb = xc.astype(BF16)
    r = _sigmoid(_nn(xb, wa_ref[...]) + ba_ref[...])
    ig = _sigmoid(_nn(xb, wx_ref[...]) + bx_ref[...])
    nl = -lam_ref[...]
    sp = jnp.maximum(nl, 0.0) + jnp.log(1.0 + jnp.exp(-jnp.abs(nl)))
    la = -LRU_C * r * sp
    a = jnp.exp(la)
    z = 2.0 * la
    em = jnp.where(z > -0.1, -z * (1.0 + z * 0.5 * (1.0 + z * (1.0 / 3) * (1.0 + z * 0.25 * (1.0 + z * 0.2)))), 1.0 - a * a)
    return xc, xb, r, ig, sp, a, jnp.sqrt(em)


def _lru_fwd(proj, cw, cb, wa, ba, wx, bx, lam, ts=256):
    s = proj.shape[0]
    ts = min(ts, s)

    def body(x_ref, xh_ref, y_ref, cw_ref, cb_ref, wa_ref, ba_ref, wx_ref, bx_ref, lam_ref, o_ref, h_ref, ext, hc):
        i = pl.program_id(0)

        @pl.when(i == 0)
        def _():
            hc[...] = jnp.zeros_like(hc)

        xc, _, _, ig, _, a, mult = _lru_gates(x_ref, xh_ref, ext, i == 0, cw_ref, cb_ref, wa_ref, ba_ref, wx_ref, bx_ref, lam_ref, ts)
        acum, h0 = _scan_fwd(a, mult * ig * xc)
        h = h0 + acum * hc[0:1, :]
        hc[...] = jnp.broadcast_to(h[ts - 1:ts, :], hc.shape)
        h_ref[...] = h
        o_ref[...] = (h * _gelu_parts(y_ref[...])[0]).astype(BF16)

    tile = lambda cb_: pl.BlockSpec((ts, 512), functools.partial(lambda c, i: (i, c), cb_))
    vec = pl.BlockSpec((1, LRU_W), lambda i: (0, 0))
    mat = pl.BlockSpec((LRU_W, LRU_W), lambda i: (0, 0))
    return pl.pallas_call(
        body, out_shape=(jax.ShapeDtypeStruct((s, LRU_W), BF16), jax.ShapeDtypeStruct((s, LRU_W), F32)), grid=(s // ts,),
        in_specs=[tile(LX_BLK), _halo_before(ts, LRU_HALO, LX_BLK), tile(LY_BLK), pl.BlockSpec((8, LRU_W), lambda i: (0, 0)),
                  vec, mat, vec, mat, vec, vec],
        out_specs=(tile(0), tile(0)), scratch_shapes=[pltpu.VMEM((ts + LRU_HALO, LRU_W), F32), pltpu.VMEM((8, LRU_W), F32)],
        name="lru_fwd", compiler_params=_params(("arbitrary",)),
    )(proj, proj, proj, cw, cb, wa, ba, wx, bx, lam)


def _lru_bwd(proj, hs, dout, cw, cb, wa, ba, wx, bx, lam, ts=256):
    s = proj.shape[0]
    ts = min(ts, s)
    nt = s // ts

    def body(x_ref, xh_ref, y_ref, h_ref, hh_ref, do_ref, cw_ref, cb_ref, wa_ref, ba_ref, wx_ref, bx_ref, lam_ref,
             dxy_ref, dcw_ref, dcb_ref, dwa_ref, dba_ref, dwx_ref, dbx_ref, dlam_ref, ext, dext, cg, cd):
        i = pl.program_id(0)
        first_tile = i == nt - 1

        @pl.when(i == 0)
        def _():
            cg[...] = jnp.zeros_like(cg)
            cd[...] = jnp.zeros_like(cd)
            for ref in (dcw_ref, dcb_ref, dwa_ref, dba_ref, dwx_ref, dbx_ref, dlam_ref):
                ref[...] = jnp.zeros_like(ref)

        xc, xb, r, ig, sp, a, mult = _lru_gates(x_ref, xh_ref, ext, first_tile, cw_ref, cb_ref, wa_ref, ba_ref, wx_ref, bx_ref, lam_ref, ts)
        row = lax.broadcasted_iota(jnp.int32, (ts, 1), 0)
        h, dov = h_ref[...], do_ref[...]
        gel, dgel = _gelu_parts(y_ref[...])
        dxy_ref[:, LRU_W:] = (dov * h * dgel).astype(BF16)
        alpha = jnp.where(row < ts - 1, pltpu.roll(a, ts - 1, 0), 0.0)
        g = _scan_rev(alpha, dov * gel + jnp.where(row == ts - 1, cg[0:1, :], 0.0))
        cg[...] = jnp.broadcast_to(a[0:1, :] * g[0:1, :], cg.shape)
        hprev = jnp.where(row == 0, jnp.where(first_tile, 0.0, hh_ref[LRU_HALO - 1:LRU_HALO, :]), pltpu.roll(h, 1, 0))
        dla = g * hprev * a - g * ig * xc * (a * a) / mult
        dpr = dla * (-LRU_C * sp) * r * (1.0 - r)
        dpi = g * mult * xc * ig * (1.0 - ig)
        dprb, dpib = dpr.astype(BF16), dpi.astype(BF16)
        dxc = g * mult * ig + _nt(dprb, wa_ref[...]) + _nt(dpib, wx_ref[...])
        dlam_ref[...] += jnp.sum(dla * (-LRU_C * r), axis=0, keepdims=True) * (-_sigmoid(-lam_ref[...]))
        dwa_ref[...] += _tn(xb, dprb)
        dwx_ref[...] += _tn(xb, dpib)
        dba_ref[...] += jnp.sum(dpr, axis=0, keepdims=True)
        dbx_ref[...] += jnp.sum(dpi, axis=0, keepdims=True)
        dcb_ref[...] += jnp.sum(dxc, axis=0, keepdims=True)
        dext[0:ts, :] = dxc
        dext[ts:, :] = cd[...]
        cd[...] = dxc[0:LRU_HALO, :]
        dx = jnp.zeros((ts, LRU_W), F32)
        for j in range(LRU_CONV):
            dcw_ref[j:j + 1, :] += jnp.sum(dxc * ext[pl.ds(LRU_OFF + j, ts), :], axis=0, keepdims=True)
            dx = dx + cw_ref[j:j + 1, :] * dext[pl.ds(LRU_CONV - 1 - j, ts), :]
        dxy_ref[:, 0:LRU_W] = dx.astype(BF16)

    rev = lambda c: pl.BlockSpec((ts, 512), functools.partial(lambda c, i: (nt - 1 - i, c), c))
    halo = lambda c: pl.BlockSpec((LRU_HALO, 512), functools.partial(
        lambda c, i: (jnp.maximum((nt - 1 - i) * (ts // LRU_HALO) - 1, 0), c), c))
    vec = pl.BlockSpec((1, LRU_W), lambda i: (0, 0))
    mat = pl.BlockSpec((LRU_W, LRU_W), lambda i: (0, 0))
    cws = pl.BlockSpec((8, LRU_W), lambda i: (0, 0))
    v1, m1 = jax.ShapeDtypeStruct((1, LRU_W), F32), jax.ShapeDtypeStruct((LRU_W, LRU_W), F32)
    return pl.pallas_call(
        body, out_shape=(jax.ShapeDtypeStruct((s, 2 * LRU_W), BF16), jax.ShapeDtypeStruct((8, LRU_W), F32), v1, m1, v1, m1, v1, v1),
        grid=(nt,),
        in_specs=[rev(LX_BLK), halo(LX_BLK), rev(LY_BLK), rev(0), halo(0), rev(0), cws, vec, mat, vec, mat, vec, vec],
        out_specs=(pl.BlockSpec((ts, 2 * LRU_W), lambda i: (nt - 1 - i, 0)), cws, vec, mat, vec, mat, vec, vec),
        scratch_shapes=[pltpu.VMEM((ts + LRU_HALO, LRU_W), F32), pltpu.VMEM((ts + LRU_HALO, LRU_W), F32),
                        pltpu.VMEM((8, LRU_W), F32), pltpu.VMEM((LRU_HALO, LRU_W), F32)],
        name="lru_bwd", compiler_params=_params(("arbitrary",)),
    )(proj, proj, proj, hs, hs, dout, cw, cb, wa, ba, wx, bx, lam)


def _final_loss(x, fw, tgt, ts=512):
    s, d = x.shape
    ts = min(ts, s)

    def body(x_ref, w_ref, t_ref, loss_ref, dx_ref, dw_ref):
        @pl.when(pl.program_id(0) == 0)
        def _():
            loss_ref[...] = jnp.zeros_like(loss_ref)
            dw_ref[...] = jnp.zeros_like(dw_ref)

        xv = x_ref[...]
        y, r = _rms_fwd(xv, w_ref[...])
        err = y - t_ref[...]
        loss_ref[...] += 0.5 * jnp.sum(jnp.mean(err * err, axis=-1, keepdims=True), axis=0, keepdims=True)
        dx, dw = _rms_bwd(xv, r, w_ref[...], err * (1.0 / d))
        dx_ref[...] = dx
        dw_ref[...] += dw

    row = pl.BlockSpec((ts, d), lambda i: (i, 0))
    vec = pl.BlockSpec((1, d), lambda i: (0, 0))
    return pl.pallas_call(
        body, out_shape=(jax.ShapeDtypeStruct((8, 128), F32), jax.ShapeDtypeStruct((s, d), F32), jax.ShapeDtypeStruct((1, d), F32)),
        grid=(s // ts,), in_specs=[row, vec, row], out_specs=(pl.BlockSpec((8, 128), lambda i: (0, 0)), row, vec),
        name="final_loss", compiler_params=_params(("arbitrary",)),
    )(x, fw, tgt)


def _lb_softmax(raw_ref):
    raw = raw_ref[...]
    e = jnp.exp(raw - jnp.max(raw, axis=0, keepdims=True))
    return e / jnp.sum(e, axis=0, keepdims=True)


def _lb_fwd(raw):
    def body(raw_ref, o_ref):
        sm = _lb_softmax(raw_ref)
        acc = jnp.zeros((1, sm.shape[1]), F32)
        o_ref[0:1, :] = acc
        for l in range(1, DEPTH):
            acc = acc + sm[l:l + 1, :]
            o_ref[l:l + 1, :] = acc

    return pl.pallas_call(body, out_shape=jax.ShapeDtypeStruct(raw.shape, F32), name="lb_fwd")(raw)


def _lb_bwd(raw, dlb):
    def body(raw_ref, d_ref, o_ref):
        sm = _lb_softmax(raw_ref)
        dlbv = d_ref[...]
        dsm, acc = [None] * DEPTH, jnp.zeros((1, sm.shape[1]), F32)
        for l in range(DEPTH - 1, 0, -1):
            acc = acc + dlbv[l:l + 1, :]
            dsm[l] = acc
        dsm[0] = jnp.zeros_like(acc)
        dsm = jnp.concatenate(dsm, axis=0)
        o_ref[...] = sm * (dsm - jnp.sum(sm * dsm, axis=0, keepdims=True))

    return pl.pallas_call(body, out_shape=jax.ShapeDtypeStruct(raw.shape, F32), name="lb_bwd")(raw, dlb)


def _adam_math(w, g, m, v):
    m = ADAM_B1 * m + (1.0 - ADAM_B1) * g
    v = ADAM_B2 * v + (1.0 - ADAM_B2) * (g * g)
    m_hat = m / (1.0 - ADAM_B1 ** ADAM_STEP)
    v_hat = v / (1.0 - ADAM_B2 ** ADAM_STEP)
    return -ADAM_LR * (m_hat / (jnp.sqrt(v_hat) + ADAM_EPS) + ADAM_WD * w), m, v


def _adamw(w, gs, m, v, name, tr=128):
    r, c = w.shape
    tr = min(tr, r)
    ng = len(gs)

    def body(*refs):
        w_ref, g_refs, m_ref, v_ref = refs[0], refs[1:1 + ng], refs[1 + ng], refs[2 + ng]
        outs = refs[3 + ng:]
        g = g_refs[0][...]
        if ng == 2:
            g = g + g_refs[1][...]
            outs[0][...] = g
            outs = outs[1:]
        for o, val in zip(outs, _adam_math(w_ref[...], g, m_ref[...], v_ref[...])):
            o[...] = val

    blk = pl.BlockSpec((tr, c), lambda i: (i, 0))
    sd = jax.ShapeDtypeStruct((r, c), F32)
    nout = 3 + (ng == 2)
    return pl.pallas_call(
        body, out_shape=(sd,) * nout, grid=(r // tr,), in_specs=[blk] * (3 + ng), out_specs=(blk,) * nout, name=name,
        compiler_params=_params(("parallel",)),
    )(w, *gs, m, v)


def _cast_into_full(w, kind, jj, name, tr=256):
    l, r, c = w.shape
    tr = min(tr, r)

    def body(j_ref, w_ref, o_ref):
        o_ref[...] = w_ref[...].astype(BF16)

    if kind == "col":
        full, dst = (l, r, 4 * c), pl.BlockSpec((None, tr, c), lambda a, b, j: (a, b, j[0]))
    else:
        full, dst = (l, 4 * r, c), pl.BlockSpec((None, tr, c), lambda a, b, j: (a, j[0] * (r // tr) + b, 0))
    return pl.pallas_call(
        body, out_shape=jax.ShapeDtypeStruct(full, BF16),
        grid_spec=pltpu.PrefetchScalarGridSpec(
            num_scalar_prefetch=1, grid=(l, r // tr), in_specs=[pl.BlockSpec((None, tr, c), lambda a, b, j: (a, b, 0))], out_specs=dst),
        name=name, compiler_params=_params(("parallel", "parallel")),
    )(jj, w)


def _place():
    return lax.axis_index("x"), lax.axis_index("y"), lax.axis_index("c")


def _other_chips(x, y):
    return [(1 - x, y), (x, 1 - y), (1 - x, 1 - y)]


def _slab(ref, kind, jj):
    if kind == "col":
        c = ref.shape[2] // 4
        return ref.at[:, :, pl.ds(jj * c, c)]
    r = ref.shape[1] // 4
    return ref.at[:, pl.ds(jj * r, r), :]


HBM = pl.BlockSpec(memory_space=pltpu.HBM)
SEM = pl.BlockSpec(memory_space=pltpu.SEMAPHORE)
EFFECT = pltpu.SideEffectType.DATAFLOW_SIDE_EFFECTING


def _in_hbm(a):
    return pltpu.with_memory_space_constraint(a, pltpu.HBM)


def _thru(arrs):
    return [pltpu.HBM(a.shape, a.dtype) for a in arrs]


def _gather_copy(fulls, kinds, send_sems, recv_sems, group, t, k, landing):
    x, y, c = _place()
    chip = _other_chips(x, y)[k]
    per = fulls[t].shape[0] // DEPTH
    rows = fulls[t].at[pl.ds(group * per, per)]
    idx = (group * len(fulls) + t) * 3 + k
    return pltpu.make_async_remote_copy(
        src_ref=_slab(rows, kinds[t], 2 * x + y), dst_ref=_slab(rows, kinds[t], landing), send_sem=send_sems.at[idx],
        recv_sem=recv_sems.at[idx], device_id=(chip[0], chip[1], c), device_id_type=MESH)


def _gather_start(fulls, kinds):
    nt = len(fulls)
    ncp = DEPTH * nt * 3

    def body(*refs):
        ins, send_sems, recv_sems = refs[:nt], refs[nt], refs[nt + 1]
        x, y, _ = _place()
        for group in range(DEPTH):
            for t in range(nt):
                for k in range(3):
                    _gather_copy(ins, kinds, send_sems, recv_sems, group, t, k, 2 * x + y).start()

    out = pl.pallas_call(
        body, out_shape=(pltpu.SemaphoreType.DMA((ncp,)), pltpu.SemaphoreType.DMA((ncp,)), *_thru(fulls)),
        in_specs=[HBM] * nt, out_specs=(SEM, SEM, *([HBM] * nt)), input_output_aliases={t: 2 + t for t in range(nt)},
        name="gather_start", compiler_params=pltpu.CompilerParams(has_side_effects=EFFECT),
    )(*[_in_hbm(a) for a in fulls])
    return out[0], out[1], list(out[2:])


def _gather_wait(group, send_sems, recv_sems, fulls, kinds, after):
    nt = len(fulls)

    def body(*refs):
        ins, send_ref, recv_ref = refs[:nt], refs[nt], refs[nt + 1]
        x, y, _ = _place()
        chips = _other_chips(x, y)
        for t in range(nt):
            for k in range(3):
                cp = _gather_copy(ins, kinds, send_ref, recv_ref, group, t, k, 2 * chips[k][0] + chips[k][1])
                cp.wait_send()
                cp.wait_recv()

    out = pl.pallas_call(
        body, out_shape=tuple(_thru(fulls)), in_specs=[HBM] * nt + [SEM, SEM, ANY], out_specs=tuple([HBM] * nt),
        input_output_aliases={t: t for t in range(nt)}, name="gather_wait_%d" % group,
        compiler_params=pltpu.CompilerParams(has_side_effects=EFFECT),
    )(*fulls, send_sems, recv_sems, after)
    return list(out)


def _scatter_copy(grads, lands, kinds, send_sems, recv_sems, t, k):
    x, y, c = _place()
    chip = _other_chips(x, y)[k]
    return pltpu.make_async_remote_copy(
        src_ref=_slab(grads[t], kinds[t], 2 * chip[0] + chip[1]), dst_ref=lands[t].at[k], send_sem=send_sems.at[3 * t + k],
        recv_sem=recv_sems.at[3 * t + k], device_id=(chip[0], chip[1], c), device_id_type=MESH)


def _scatter_start(grads, kinds, name):
    nt = len(grads)
    lands = []
    for g, kd in zip(grads, kinds):
        l, r, c = g.shape
        lands.append(lax.empty((3, l, r, c // 4) if kd == "col" else (3, l, r // 4, c), g.dtype))

    def body(*refs):
        ins, lnd, send_sems, recv_sems = refs[:nt], refs[nt:2 * nt], refs[2 * nt], refs[2 * nt + 1]
        for t in range(nt):
            for k in range(3):
                _scatter_copy(ins, lnd, kinds, send_sems, recv_sems, t, k).start()

    out = pl.pallas_call(
        body, out_shape=(pltpu.SemaphoreType.DMA((3 * nt,)), pltpu.SemaphoreType.DMA((3 * nt,)), *_thru(grads), *_thru(lands)),
        in_specs=[HBM] * (2 * nt), out_specs=(SEM, SEM, *([HBM] * (2 * nt))),
        input_output_aliases={t: 2 + t for t in range(2 * nt)}, name=name,
        compiler_params=pltpu.CompilerParams(has_side_effects=EFFECT),
    )(*[_in_hbm(a) for a in grads], *[_in_hbm(a) for a in lands])
    return out[0], out[1], list(out[2:2 + nt]), list(out[2 + nt:])


def _scatter_wait(send_sems, recv_sems, grads, lands, kinds, after, name):
    nt = len(grads)

    def body(*refs):
        ins, lnd, send_ref, recv_ref = refs[:nt], refs[nt:2 * nt], refs[2 * nt], refs[2 * nt + 1]
        for t in range(nt):
            for k in range(3):
                cp = _scatter_copy(ins, lnd, kinds, send_ref, recv_ref, t, k)
                cp.wait_send()
                cp.wait_recv()

    out = pl.pallas_call(
        body, out_shape=(*_thru(grads), *_thru(lands)), in_specs=[HBM] * (2 * nt) + [SEM, SEM, ANY],
        out_specs=tuple([HBM] * (2 * nt)), input_output_aliases={t: t for t in range(2 * nt)}, name=name,
        compiler_params=pltpu.CompilerParams(has_side_effects=EFFECT),
    )(*grads, *lands, send_sems, recv_sems, after)
    return list(out[:nt]), list(out[nt:])


def _sibling_swap(arrs):
    nt = len(arrs)

    def body(*refs):
        ins, outs = refs[:nt], refs[nt:2 * nt]
        send_sems, recv_sems = refs[2 * nt:]
        x, y, c = _place()
        sends = [pltpu.make_async_remote_copy(src_ref=ins[t], dst_ref=outs[t], send_sem=send_sems.at[t], recv_sem=recv_sems.at[t],
                                              device_id=(x, y, 1 - c), device_id_type=MESH) for t in range(nt)]
        for cp in sends:
            cp.start()
        for cp in sends:
            cp.wait_recv()
        for cp in sends:
            cp.wait_send()

    return pl.pallas_call(
        body, out_shape=[jax.ShapeDtypeStruct(a.shape, a.dtype) for a in arrs], in_specs=[ANY] * nt, out_specs=[ANY] * nt,
        scratch_shapes=[pltpu.SemaphoreType.DMA((nt,)), pltpu.SemaphoreType.DMA((nt,))], name="sibling_swap",
    )(*arrs)


def _gather_small(vec, over_c):
    n = vec.shape[0]
    flips = [(dx, dy, dc) for dx in (0, 1) for dy in (0, 1) for dc in ((0, 1) if over_c else (0,))][1:]
    np_ = len(flips)

    def body(v_ref, o_ref, send_sems, recv_sems, local_sem):
        x, y, c = _place()

        def idx(px, py, pc):
            return 4 * px + 2 * py + pc if over_c else 2 * px + py

        def peer(f):
            return (1 - x if f[0] else x, 1 - y if f[1] else y, 1 - c if f[2] else c)

        def push(k, landing):
            return pltpu.make_async_remote_copy(src_ref=v_ref, dst_ref=o_ref.at[landing], send_sem=send_sems.at[k],
                                                recv_sem=recv_sems.at[k], device_id=peer(flips[k]), device_id_type=MESH)

        mine = pltpu.make_async_copy(v_ref, o_ref.at[idx(x, y, c)], local_sem)
        sends = [push(k, idx(x, y, c)) for k in range(np_)]
        for cp in [mine] + sends:
            cp.start()
        for k in range(np_):
            push(k, idx(*peer(flips[k]))).wait_recv()
        for cp in sends:
            cp.wait_send()
        mine.wait()

    return pl.pallas_call(
        body, out_shape=jax.ShapeDtypeStruct((np_ + 1, n, 128), F32), in_specs=[ANY], out_specs=ANY,
        scratch_shapes=[pltpu.SemaphoreType.DMA((np_,)), pltpu.SemaphoreType.DMA((np_,)), pltpu.SemaphoreType.DMA(())],
        name="gather_small_all" if over_c else "gather_small_xy",
    )(vec)


def _sum_rows(buf, tr=512):
    p, n, _ = buf.shape
    tr = min(tr, n)

    def body(b_ref, o_ref):
        acc = b_ref[0]
        for k in range(1, p):
            acc = acc + b_ref[k]
        o_ref[...] = acc

    return pl.pallas_call(
        body, out_shape=jax.ShapeDtypeStruct((n, 128), F32), grid=(n // tr,),
        in_specs=[pl.BlockSpec((p, tr, 128), lambda i: (0, i, 0))], out_specs=pl.BlockSpec((tr, 128), lambda i: (i, 0)),
        name="sum_rows", compiler_params=_params(("parallel",)),
    )(buf)


def _sum_partials_into(stack, at, depth, grad, recv, kind, jj, name, tr=128):
    _, _, r, c = recv.shape
    tr = min(tr, r)

    def body(j_ref, g_ref, r0, r1, r2, *rest):
        rest[-1][...] = ((g_ref[...].astype(F32) + r0[...].astype(F32)) + r1[...].astype(F32)) + r2[...].astype(F32)

    if kind == "col":
        own = pl.BlockSpec((None, tr, c), lambda b, j: (0, b, j[0]))
    else:
        own = pl.BlockSpec((None, tr, c), lambda b, j: (0, j[0] * (r // tr) + b, 0))
    got = lambda k: pl.BlockSpec((None, None, tr, c), functools.partial(lambda k, b, j: (k, 0, b, 0), k))
    chained = stack is not None
    return pl.pallas_call(
        body, out_shape=jax.ShapeDtypeStruct((depth, r, c), F32),
        grid_spec=pltpu.PrefetchScalarGridSpec(
            num_scalar_prefetch=1, grid=(r // tr,), in_specs=[own, got(0), got(1), got(2)] + ([ANY] if chained else []),
            out_specs=pl.BlockSpec((None, tr, c), lambda b, j: (at, b, 0))),
        input_output_aliases={5: 0} if chained else {}, name=name, compiler_params=_params(("parallel",)),
    )(*([jj, grad, recv, recv, recv] + ([stack] if chained else [])))


WEIGHTS = ['norm_mix_w', 'w_in', 'hg_lb_raw', 'hg_norm_w', 'cv_dw_w', 'cv_dw_b', 'cv_ln_w', 'cv_ln_b', 'pl_w', 'pl_scale',
           'lru_conv_w', 'lru_conv_b', 'lru_wa', 'lru_ba', 'lru_wx', 'lru_bx', 'lru_lambda', 'gate_b', 'w_branch', 'w_out',
           'norm_mem_w', 'mem_norm_w', 'xa_wq', 'xa_wkv', 'xa_wo', 'norm_ffn_w', 'ffn_w1', 'ffn_w2', 'final_norm_w']
BIG = {'w_in': 'col', 'w_branch': 'col', 'w_out': 'row', 'xa_wq': 'row', 'xa_wkv': 'col', 'xa_wo': 'row', 'ffn_w1': 'col', 'ffn_w2': 'row'}
SMALL_SPLIT = ('gate_b', 'cv_dw_w', 'lru_conv_w')
SMALL = [n for n in WEIGHTS if n not in BIG]
PIECES = ['w_in', ('w_branch', 0), ('w_branch', 1), ('w_branch', 2), ('w_branch', 3), 'w_out', 'xa_wq', 'xa_wkv', 'xa_wo', 'ffn_w1', 'ffn_w2']
PIECE_KINDS = [BIG[k[0] if isinstance(k, tuple) else k] for k in PIECES]
ROWS_PAD = 512


def _as3d(a):
    return a.reshape((-1,) + a.shape[-2:])


def _pack(parts):
    flat = jnp.concatenate([p.reshape(-1).astype(F32) for p in parts])
    n = -(-flat.shape[0] // (128 * ROWS_PAD)) * ROWS_PAD
    return jnp.pad(flat, (0, n * 128 - flat.shape[0])).reshape(n, 128)


def _unpack(packed, shapes):
    flat, out, o = packed.reshape(-1), [], 0
    for sh in shapes:
        sz = math.prod(sh)
        out.append(flat[o:o + sz].reshape(sh))
        o += sz
    return out


def _block_diag(w):
    h, a, b = w.shape
    eye = jnp.eye(h, dtype=w.dtype)
    return (w[:, :, None, :] * eye[:, None, :, None]).reshape(h * a, h * b)


def _diag_blocks(m, h):
    a, b = m.shape[0] // h, m.shape[1] // h
    return jnp.stack([m[i * a:(i + 1) * a, i * b:(i + 1) * b] for i in range(h)])


def kernel(x, mem, norm_mix_w, w_in, hg_lb_raw, hg_norm_w, cv_dw_w, cv_dw_b, cv_ln_w, cv_ln_b, pl_w, pl_scale, lru_conv_w, lru_conv_b, lru_wa, lru_ba, lru_wx, lru_bx, lru_lambda, gate_b, w_branch, w_out, norm_mem_w, mem_norm_w, xa_wq, xa_wkv, xa_wo, norm_ffn_w, ffn_w1, ffn_w2, final_norm_w, loss_target, m_norm_mix_w, m_w_in, m_hg_lb_raw, m_hg_norm_w, m_cv_dw_w, m_cv_dw_b, m_cv_ln_w, m_cv_ln_b, m_pl_w, m_pl_scale, m_lru_conv_w, m_lru_conv_b, m_lru_wa, m_lru_ba, m_lru_wx, m_lru_bx, m_lru_lambda, m_gate_b, m_w_branch, m_w_out, m_norm_mem_w, m_mem_norm_w, m_xa_wq, m_xa_wkv, m_xa_wo, m_norm_ffn_w, m_ffn_w1, m_ffn_w2, m_final_norm_w, v_norm_mix_w, v_w_in, v_hg_lb_raw, v_hg_norm_w, v_cv_dw_w, v_cv_dw_b, v_cv_ln_w, v_cv_ln_b, v_pl_w, v_pl_scale, v_lru_conv_w, v_lru_conv_b, v_lru_wa, v_lru_ba, v_lru_wx, v_lru_bx, v_lru_lambda, v_gate_b, v_w_branch, v_w_out, v_norm_mem_w, v_mem_norm_w, v_xa_wq, v_xa_wkv, v_xa_wo, v_norm_ffn_w, v_ffn_w1, v_ffn_w2, v_final_norm_w):
    w = dict(zip(WEIGHTS, (norm_mix_w, w_in, hg_lb_raw, hg_norm_w, cv_dw_w, cv_dw_b, cv_ln_w, cv_ln_b, pl_w, pl_scale, lru_conv_w, lru_conv_b, lru_wa, lru_ba, lru_wx, lru_bx, lru_lambda, gate_b, w_branch, w_out, norm_mem_w, mem_norm_w, xa_wq, xa_wkv, xa_wo, norm_ffn_w, ffn_w1, ffn_w2, final_norm_w)))
    m1 = dict(zip(WEIGHTS, (m_norm_mix_w, m_w_in, m_hg_lb_raw, m_hg_norm_w, m_cv_dw_w, m_cv_dw_b, m_cv_ln_w, m_cv_ln_b, m_pl_w, m_pl_scale, m_lru_conv_w, m_lru_conv_b, m_lru_wa, m_lru_ba, m_lru_wx, m_lru_bx, m_lru_lambda, m_gate_b, m_w_branch, m_w_out, m_norm_mem_w, m_mem_norm_w, m_xa_wq, m_xa_wkv, m_xa_wo, m_norm_ffn_w, m_ffn_w1, m_ffn_w2, m_final_norm_w)))
    v1 = dict(zip(WEIGHTS, (v_norm_mix_w, v_w_in, v_hg_lb_raw, v_hg_norm_w, v_cv_dw_w, v_cv_dw_b, v_cv_ln_w, v_cv_ln_b, v_pl_w, v_pl_scale, v_lru_conv_w, v_lru_conv_b, v_lru_wa, v_lru_ba, v_lru_wx, v_lru_bx, v_lru_lambda, v_gate_b, v_w_branch, v_w_out, v_norm_mem_w, v_mem_norm_w, v_xa_wq, v_xa_wkv, v_xa_wo, v_norm_ffn_w, v_ffn_w1, v_ffn_w2, v_final_norm_w)))
    seq = x.shape[1]
    xs, mems, tgt = x.reshape(seq, D_MODEL), mem.reshape(-1, D_MODEL), loss_target.reshape(seq, D_MODEL)
    jj = 2 * lax.axis_index("x") + lax.axis_index("y")
    jj1 = jnp.reshape(jj, (1,)).astype(jnp.int32)

    split_shapes = [w[n].shape for n in SMALL_SPLIT]
    got = _gather_small(_pack([w[n] for n in SMALL_SPLIT]), over_c=False)
    per_chip = [_unpack(got[k], split_shapes) for k in range(4)]
    full_small = {n: jnp.concatenate([per_chip[k][i] for k in range(4)], axis=-1) for i, n in enumerate(SMALL_SPLIT)}
    big_names = list(BIG)
    kinds = [BIG[n] for n in big_names]
    g_send, g_recv, fulls = _gather_start([_cast_into_full(_as3d(w[n]), BIG[n], jj1, "cast_" + n) for n in big_names], kinds)

    lb = _lb_fwd(hg_lb_raw)
    row = lambda a: a.reshape(1, -1)

    def layer_params(l):
        return dict(
            nmix=row(norm_mix_w[l]), lb=row(lb[l]), hgnw=row(hg_norm_w[l]),
            cw=jnp.pad(full_small['cv_dw_w'][l], ((0, 32 - CV_K), (0, 0))), cb=row(cv_dw_b[l]), lnw=row(cv_ln_w[l]), lnb=row(cv_ln_b[l]),
            plw=pl_w[l], plsc=row(pl_scale[l]),
            lcw=jnp.pad(full_small['lru_conv_w'][l], ((0, 8 - LRU_CONV), (0, 0))), lcb=row(lru_conv_b[l]),
            wa=_block_diag(lru_wa[l]).astype(BF16), ba=row(lru_ba[l]), wx=_block_diag(lru_wx[l]).astype(BF16), bx=row(lru_bx[l]),
            lam=row(lru_lambda[l]), gb=full_small['gate_b'][l], nmem=row(norm_mem_w[l]), memw=row(mem_norm_w[l]), nffn=row(norm_ffn_w[l]))

    saved = []
    xc = xs
    for l in range(DEPTH):
        fulls = _gather_wait(l, g_send, g_recv, fulls, kinds, xc)
        wf = dict(zip(big_names, fulls))
        wf['w_branch'] = wf['w_branch'].reshape(DEPTH, 4, 512, D_MODEL)
        p = layer_params(l)
        h = _norm_fwd(xc, p['nmix'], "norm_mix")
        proj = _mm(h, wf['w_in'], "nn", F32, "proj", layer=l)
        b_hg, st = _hgrn_fwd(proj, p['lb'], p['hgnw'])
        b_cv = _conv_fwd(proj, p['cw'], p['cb'], p['lnw'], p['lnb'])
        b_pl = _pool_fwd(proj, p['plw'], p['plsc'])
        b_lru, hs = _lru_fwd(proj, p['lcw'], p['lcb'], p['wa'], p['ba'], p['wx'], p['bx'], p['lam'])
        branches = (b_hg, b_cv, b_pl, b_lru)
        x1 = _merge_fwd(xc, branches, proj, p['gb'], wf['w_branch'], wf['w_out'], l)
        memn = _norm_fwd(mems, p['memw'], "norm_memtok")
        kv = _mm(memn, wf['xa_wkv'], "nn", BF16, "kv_proj", layer=l)
        x2 = _attn_fwd(x1, p['nmem'], wf['xa_wq'], kv, wf['xa_wo'], l)
        x3 = _ffn_fwd(x2, p['nffn'], wf['ffn_w1'], wf['ffn_w2'], l)
        saved.append(dict(p=p, x=xc, h=h, proj=proj, st=st, hs=hs, branches=branches, x1=x1, memn=memn, kv=kv, x2=x2))
        xc = x3

    loss_blk, dx, dfinal = _final_loss(xc, row(final_norm_w), tgt)

    gs = {n: [None] * DEPTH for n in SMALL if n != 'final_norm_w'}
    dlb = [None] * DEPTH
    in_flight = [None] * DEPTH
    for l in reversed(range(DEPTH)):
        sv = saved[l]
        p = sv['p']
        gb = {}
        dx2, gs['norm_ffn_w'][l], h3, da, r, dxb = _ffn_bwd(sv['x2'], dx, p['nffn'], wf['ffn_w1'], wf['ffn_w2'], l)
        gb['ffn_w1'] = _mm(h3, da, "tn", BF16, "dw_ffn1")
        gb['ffn_w2'] = _mm(r, dxb, "tn", BF16, "dw_ffn2")
        dx1, gs['norm_mem_w'][l], h2, o, dq, dxb2, dk, dv = _attn_bwd(sv['x1'], dx2, p['nmem'], wf['xa_wq'], sv['kv'], wf['xa_wo'], l)
        gb['xa_wq'] = _mm(h2, dq, "tn", BF16, "dw_q")
        gb['xa_wo'] = _mm(o, dxb2, "tn", BF16, "dw_o")
        dkv = jnp.concatenate([dk, dv], axis=1)
        gb['xa_wkv'] = _mm(sv['memn'], dkv, "tn", BF16, "dw_kv")
        dmemn = _mm(dkv, wf['xa_wkv'], "nt", F32, "dmemn", layer=l)
        _, gs['mem_norm_w'][l] = _norm_bwd(mems, p['memw'], dmemn, None, "norm_memtok_bwd")
        db0, db1, db2, db3, dgp, dup, mg, dxb1, gs['gate_b'][l] = _merge_bwd(
            dx1, sv['branches'], sv['proj'], p['gb'], wf['w_branch'], wf['w_out'], l)
        gb['w_out'] = _mm(mg, dxb1, "tn", BF16, "dw_out")
        for kb in range(4):
            gb['w_branch', kb] = _mm(sv['branches'][kb], dup, "tn", BF16, "dw_branch", b_col0=kb * D_MODEL, n=D_MODEL, tn=512)
        dhg, dlb[l], gs['hg_norm_w'][l] = _hgrn_bwd(sv['proj'], db0, sv['st'], p['lb'], p['hgnw'])
        dcv, dcw, gs['cv_dw_b'][l], gs['cv_ln_w'][l], gs['cv_ln_b'][l] = _conv_bwd(sv['proj'], db1, p['cw'], p['cb'], p['lnw'], p['lnb'])
        gs['cv_dw_w'][l] = dcw[:CV_K]
        dpl, gs['pl_w'][l], gs['pl_scale'][l] = _pool_bwd(sv['proj'], db2, p['plw'], p['plsc'])
        dlru, dlcw, gs['lru_conv_b'][l], dwa, gs['lru_ba'][l], dwx, gs['lru_bx'][l], gs['lru_lambda'][l] = _lru_bwd(
            sv['proj'], sv['hs'], db3, p['lcw'], p['lcb'], p['wa'], p['ba'], p['wx'], p['bx'], p['lam'])
        gs['lru_conv_w'][l] = dlcw[:LRU_CONV]
        gs['lru_wa'][l], gs['lru_wx'][l] = _diag_blocks(dwa, LRU_HEADS), _diag_blocks(dwx, LRU_HEADS)
        dproj = jnp.concatenate([dhg, dcv, dpl, dlru, dgp], axis=1)
        gb['w_in'] = _mm(sv['h'], dproj, "tn", BF16, "dw_in")
        dh = _mm(dproj, wf['w_in'], "nt", F32, "dh_mix", tm=256, tn=256, layer=l)
        dx, gs['norm_mix_w'][l] = _norm_bwd(sv['x'], p['nmix'], dh, dx1, "norm_mix_bwd")
        in_flight[l] = _scatter_start([gb[key][None] for key in PIECES], PIECE_KINDS, "scatter_start_%d" % l)
    grad_x = dx.reshape(x.shape)
    gs['hg_lb_raw'] = _lb_bwd(hg_lb_raw, jnp.concatenate(dlb, axis=0))

    def full_shape(n):
        return full_small[n].shape if n in SMALL_SPLIT else w[n].shape

    small_full = []
    for n in SMALL:
        g = gs[n] if n == 'hg_lb_raw' else dfinal if n == 'final_norm_w' else jnp.stack(gs[n])
        small_full.append(g.reshape(full_shape(n)))
    total = _sum_rows(_gather_small(_pack(small_full + [loss_blk[0:1, 0:1]]), over_c=True))
    parts = _unpack(total, [full_shape(n) for n in SMALL] + [(1,)])
    loss = parts[-1].reshape(())
    g_small = {}
    for n, g in zip(SMALL, parts[:-1]):
        if n in SMALL_SPLIT:
            width = w[n].shape[-1]
            g = lax.dynamic_slice_in_dim(g, jj * width, width, axis=g.ndim - 1)
        g_small[n] = g
    shapes = [w[n].shape for n in SMALL]
    upd = _adamw(_pack([w[n] for n in SMALL]), [_pack([g_small[n] for n in SMALL])], _pack([m1[n] for n in SMALL]),
                 _pack([v1[n] for n in SMALL]), "adamw_small")
    d_small, m_small, v_small = [dict(zip(SMALL, _unpack(u, shapes))) for u in upd]

    stacks = {n: None for n in big_names}
    for l in reversed(range(DEPTH)):
        s_send, s_recv, g_thru, lands = in_flight[l]
        g_thru, lands = _scatter_wait(s_send, s_recv, g_thru, lands, PIECE_KINDS, dx, "scatter_wait_%d" % l)
        for key, g, r in zip(PIECES, g_thru, lands):
            n, kb = key if isinstance(key, tuple) else (key, None)
            per = 1 if kb is None else 4
            stacks[n] = _sum_partials_into(stacks[n], l * per + (kb or 0), DEPTH * per, g, r, BIG[n], jj1, "sum_" + n)
    partial = [stacks[n] for n in big_names]
    theirs = _sibling_swap(partial)
    g_big, d_big, m_big, v_big = {}, {}, {}, {}
    for n, pa, pb in zip(big_names, partial, theirs):
        c2 = lambda a: a.reshape(-1, a.shape[-1])
        out = _adamw(c2(w[n]), [c2(pa), c2(pb)], c2(m1[n]), c2(v1[n]), "adamw_" + n)
        g_big[n], d_big[n], m_big[n], v_big[n] = [o.reshape(w[n].shape) for o in out]

    pick = lambda small, big: [big[n] if n in BIG else small[n] for n in WEIGHTS]
    return (loss, grad_x, *pick(g_small, g_big), *pick(d_small, d_big), *pick(m_small, m_big), *pick(v_small, v_big))
```

```python
import functools
import math

import jax
import jax.numpy as jnp
from jax import lax
from jax.experimental import pallas as pl
from jax.experimental.pallas import tpu as pltpu

F32 = jnp.float32
BF16 = jnp.bfloat16
MESH = pl.DeviceIdType.MESH
ANY = pl.BlockSpec(memory_space=pl.ANY)

D_MODEL = 1024
DEPTH = 4
CHUNK = 64
SUB = 16
EPS = 1e-6
HG_HEADS, HG_D = 4, 128
CV_W, CV_K = 512, 31
CV_HALO = 32
POOL_WINDOWS = (2, 4, 8, 16)
POOL_HALO = 16
LRU_W, LRU_HEADS, LRU_HD, LRU_CONV = 512, 8, 64, 4
LRU_HALO = 8
LRU_C = 8.0
MIX_W = 4608
IN_W = 8704
XA_HEADS, XA_HD = 4, 256
D_FF = 4096
FF_CHUNK = 1024
ADAM_LR, ADAM_B1, ADAM_B2, ADAM_EPS, ADAM_WD, ADAM_STEP = 0.001, 0.9, 0.999, 1e-08, 0.01, 10
VMEM_LIMIT = 56 * 1024 * 1024
EXP_CLAMP = 80.0
HI = lax.Precision.HIGHEST


def _params(sem=None):
    return pltpu.CompilerParams(dimension_semantics=sem, vmem_limit_bytes=VMEM_LIMIT)


def _sigmoid(x):
    return 1.0 / (1.0 + jnp.exp(-x))


def _dsilu(x, s):
    return s * (1.0 + x * (1.0 - s))


_GELU_C = math.sqrt(2.0 / math.pi)


def _gelu_parts(x):
    t = jnp.tanh(_GELU_C * (x + 0.044715 * x * x * x))
    g = 0.5 * x * (1.0 + t)
    dg = 0.5 * (1.0 + t) + 0.5 * x * (1.0 - t * t) * _GELU_C * (1.0 + 3 * 0.044715 * x * x)
    return g, dg


def _dot(a, b, dims, precision=None):
    return lax.dot_general(a, b, (dims, ((), ())), precision=precision, preferred_element_type=F32)


def _nn(a, b, **k):
    return _dot(a, b, ((1,), (0,)), **k)


def _nt(a, b, **k):
    return _dot(a, b, ((1,), (1,)), **k)


def _tn(a, b, **k):
    return _dot(a, b, ((0,), (0,)), **k)


def _rms_fwd(x, w):
    r = lax.rsqrt(jnp.mean(x * x, axis=-1, keepdims=True) + EPS)
    return x * r * w, r


def _rms_bwd(x, r, w, dy):
    xr = x * r
    g = dy * w
    dx = r * (g - xr * jnp.mean(g * xr, axis=-1, keepdims=True))
    return dx, jnp.sum(dy * xr, axis=0, keepdims=True)


def _lw(shape, index, layer):
    return pl.BlockSpec((None,) + tuple(shape), lambda *g: (layer,) + tuple(index(*g)))


def _mm(a, b, mode, out_dtype, name, tm=512, tn=512, b_col0=0, n=None, layer=None):
    bs = b.shape if layer is None else b.shape[1:]
    if mode == "nn":
        m, k = a.shape
        n = bs[1] if n is None else n
    elif mode == "nt":
        m, k = a.shape
        n = bs[0] if n is None else n
    else:
        k, m = a.shape
        n = bs[1] if n is None else n
    tm, tn = min(tm, m), min(tn, n)
    assert m % tm == 0 and n % tn == 0 and b_col0 % tn == 0
    off = b_col0 // tn

    def body(a_ref, b_ref, o_ref):
        av, bv = a_ref[...].astype(BF16), b_ref[...].astype(BF16)
        o_ref[...] = (_nn if mode == "nn" else _nt if mode == "nt" else _tn)(av, bv).astype(out_dtype)

    def bspec(shape, index):
        return pl.BlockSpec(shape, index) if layer is None else _lw(shape, index, layer)

    if mode == "tn":
        grid = (m // tm, n // tn)
        a_spec = pl.BlockSpec((k, tm), lambda i, j: (0, i))
        b_spec = bspec((k, tn), lambda i, j: (0, j + off))
        o_spec = pl.BlockSpec((tm, tn), lambda i, j: (i, j))
    else:
        grid = (n // tn, m // tm)
        a_spec = pl.BlockSpec((tm, k), lambda j, i: (i, 0))
        if mode == "nn":
            b_spec = bspec((k, tn), lambda j, i: (0, j + off))
        else:
            b_spec = bspec((tn, k), lambda j, i: (j + off, 0))
        o_spec = pl.BlockSpec((tm, tn), lambda j, i: (i, j))
    return pl.pallas_call(
        body, out_shape=jax.ShapeDtypeStruct((m, n), out_dtype), grid=grid,
        in_specs=[a_spec, b_spec], out_specs=o_spec, name=name,
        compiler_params=_params(("parallel", "parallel")),
    )(a, b)


def _norm_fwd(x, w, name, ts=512):
    s, d = x.shape
    ts = min(ts, s)

    def body(x_ref, w_ref, o_ref):
        o_ref[...] = _rms_fwd(x_ref[...], w_ref[...])[0].astype(BF16)

    return pl.pallas_call(
        body, out_shape=jax.ShapeDtypeStruct((s, d), BF16), grid=(s // ts,),
        in_specs=[pl.BlockSpec((ts, d), lambda i: (i, 0)), pl.BlockSpec((1, d), lambda i: (0, 0))],
        out_specs=pl.BlockSpec((ts, d), lambda i: (i, 0)), name=name, compiler_params=_params(("parallel",)),
    )(x, w)


def _norm_bwd(x, w, dy, dres, name, ts=512):
    s, d = x.shape
    ts = min(ts, s)
    with_res = dres is not None

    def body(*refs):
        if with_res:
            x_ref, w_ref, dy_ref, dres_ref, dx_ref, dw_ref = refs
        else:
            x_ref, w_ref, dy_ref, dx_ref, dw_ref = refs
        xv = x_ref[...]
        r = lax.rsqrt(jnp.mean(xv * xv, axis=-1, keepdims=True) + EPS)
        dx, dw = _rms_bwd(xv, r, w_ref[...], dy_ref[...])
        dx_ref[...] = dx + dres_ref[...] if with_res else dx

        @pl.when(pl.program_id(0) == 0)
        def _():
            dw_ref[...] = jnp.zeros_like(dw_ref)

        dw_ref[...] += dw

    row = pl.BlockSpec((ts, d), lambda i: (i, 0))
    vec = pl.BlockSpec((1, d), lambda i: (0, 0))
    return pl.pallas_call(
        body, out_shape=(jax.ShapeDtypeStruct((s, d), F32), jax.ShapeDtypeStruct((1, d), F32)), grid=(s // ts,),
        in_specs=[row, vec, row] + ([row] if with_res else []), out_specs=(row, vec), name=name,
        compiler_params=_params(("arbitrary",)),
    )(*([x, w, dy] + ([dres] if with_res else [])))


def _ffn_fwd(x, nw, w1, w2, layer, ts=256):
    s, d = x.shape
    ts = min(ts, s)
    nj = D_FF // FF_CHUNK

    def body(x_ref, nw_ref, w1_ref, w2_ref, o_ref, h_scr, acc):
        j = pl.program_id(1)

        @pl.when(j == 0)
        def _():
            h_scr[...] = _rms_fwd(x_ref[...], nw_ref[...])[0].astype(BF16)
            acc[...] = jnp.zeros_like(acc)

        a = _nn(h_scr[...], w1_ref[...])
        rl = jnp.maximum(a, 0.0)
        acc[...] += _nn((rl * rl).astype(BF16), w2_ref[...])

        @pl.when(j == nj - 1)
        def _():
            o_ref[...] = x_ref[...] + acc[...]

    row = pl.BlockSpec((ts, d), lambda i, j: (i, 0))
    return pl.pallas_call(
        body, out_shape=jax.ShapeDtypeStruct((s, d), F32), grid=(s // ts, nj),
        in_specs=[row, pl.BlockSpec((1, d), lambda i, j: (0, 0)),
                  _lw((d, FF_CHUNK), lambda i, j: (0, j), layer), _lw((FF_CHUNK, d), lambda i, j: (j, 0), layer)],
        out_specs=row, scratch_shapes=[pltpu.VMEM((ts, d), BF16), pltpu.VMEM((ts, d), F32)], name="ffn_fwd",
        compiler_params=_params(("parallel", "arbitrary")),
    )(x, nw, w1, w2)


def _ffn_bwd(x, dxo, nw, w1, w2, layer, after, ts=256):
    s, d = x.shape
    ts = min(ts, s)
    nj = D_FF // FF_CHUNK

    def body(x_ref, dxo_ref, nw_ref, w1_ref, w2_ref, after_ref, dx_ref, dnw_ref, h_ref, da_ref, r_ref, dxb_ref, dh):
        i, j = pl.program_id(0), pl.program_id(1)

        @pl.when(j == 0)
        def _():
            h_ref[...] = _rms_fwd(x_ref[...], nw_ref[...])[0].astype(BF16)
            dxb_ref[...] = dxo_ref[...].astype(BF16)
            dh[...] = jnp.zeros_like(dh)

        a = _nn(h_ref[...], w1_ref[...])
        rl = jnp.maximum(a, 0.0)
        r_ref[...] = (rl * rl).astype(BF16)
        da = (_nt(dxb_ref[...], w2_ref[...]) * (2.0 * rl)).astype(BF16)
        da_ref[...] = da
        dh[...] += _nt(da, w1_ref[...])

        @pl.when(jnp.logical_and(i == 0, j == 0))
        def _():
            dnw_ref[...] = jnp.zeros_like(dnw_ref)

        @pl.when(j == nj - 1)
        def _():
            xv = x_ref[...]
            r = lax.rsqrt(jnp.mean(xv * xv, axis=-1, keepdims=True) + EPS)
            dx, dw = _rms_bwd(xv, r, nw_ref[...], dh[...])
            dx_ref[...] = dxo_ref[...] + dx
            dnw_ref[...] += dw

    row = pl.BlockSpec((ts, d), lambda i, j: (i, 0))
    vec = pl.BlockSpec((1, d), lambda i, j: (0, 0))
    ffc = pl.BlockSpec((ts, FF_CHUNK), lambda i, j: (i, j))
    return pl.pallas_call(
        body,
        out_shape=(jax.ShapeDtypeStruct((s, d), F32), jax.ShapeDtypeStruct((1, d), F32), jax.ShapeDtypeStruct((s, d), BF16),
                   jax.ShapeDtypeStruct((s, D_FF), BF16), jax.ShapeDtypeStruct((s, D_FF), BF16), jax.ShapeDtypeStruct((s, d), BF16)),
        grid=(s // ts, nj),
        in_specs=[row, row, vec, _lw((d, FF_CHUNK), lambda i, j: (0, j), layer), _lw((FF_CHUNK, d), lambda i, j: (j, 0), layer), ANY],
        out_specs=(row, vec, row, ffc, ffc, row), scratch_shapes=[pltpu.VMEM((ts, d), F32)], name="ffn_bwd",
        compiler_params=_params(("arbitrary", "arbitrary")),
    )(x, dxo, nw, w1, w2, after)


def _attn_probs(q, k_ref):
    ps = []
    for hd in range(XA_HEADS):
        c = slice(hd * XA_HD, (hd + 1) * XA_HD)
        sc = _nt(q[:, c].astype(BF16), k_ref[:, c]) * (XA_HD ** -0.5)
        e = jnp.exp(sc - jnp.max(sc, axis=-1, keepdims=True))
        ps.append(e / jnp.sum(e, axis=-1, keepdims=True))
    return ps


def _attn_fwd(x, nw, wq, kv, wo, layer, ts=256):
    s, d = x.shape
    ts = min(ts, s)
    nm = kv.shape[0]

    def body(x_ref, nw_ref, wq_ref, k_ref, v_ref, wo_ref, o_ref):
        xv = x_ref[...]
        h = _rms_fwd(xv, nw_ref[...])[0].astype(BF16)
        q = _nn(h, wq_ref[...])
        ps = _attn_probs(q, k_ref)
        o = jnp.concatenate([_nn(ps[hd].astype(BF16), v_ref[:, hd * XA_HD:(hd + 1) * XA_HD]) for hd in range(XA_HEADS)], axis=1)
        o_ref[...] = xv + _nn(o.astype(BF16), wo_ref[...])

    row = pl.BlockSpec((ts, d), lambda i: (i, 0))
    full = lambda r, c: pl.BlockSpec((r, c), lambda i: (0, 0))
    wsp = _lw((d, d), lambda i: (0, 0), layer)
    return pl.pallas_call(
        body, out_shape=jax.ShapeDtypeStruct((s, d), F32), grid=(s // ts,),
        in_specs=[row, full(1, d), wsp, full(nm, d), pl.BlockSpec((nm, d), lambda i: (0, 1)), wsp], out_specs=row, name="attn_fwd",
        compiler_params=_params(("parallel",)),
    )(x, nw, wq, kv, kv, wo)


def _attn_bwd(x, dxo, nw, wq, kv, wo, layer, ts=256):
    s, d = x.shape
    ts = min(ts, s)
    nm = kv.shape[0]

    def body(x_ref, dxo_ref, nw_ref, wq_ref, k_ref, v_ref, wo_ref,
             dx_ref, dnw_ref, h_ref, o_ref, dq_ref, dxb_ref, dk_ref, dv_ref):
        xv = x_ref[...]
        hf, r = _rms_fwd(xv, nw_ref[...])
        h = hf.astype(BF16)
        h_ref[...] = h
        q = _nn(h, wq_ref[...])
        qb = q.astype(BF16)
        ps = _attn_probs(q, k_ref)
        dxb = dxo_ref[...].astype(BF16)
        dxb_ref[...] = dxb
        do = _nt(dxb, wo_ref[...])

        @pl.when(pl.program_id(0) == 0)
        def _():
            dnw_ref[...] = jnp.zeros_like(dnw_ref)
            dk_ref[...] = jnp.zeros_like(dk_ref)
            dv_ref[...] = jnp.zeros_like(dv_ref)

        dqs = []
        for hd in range(XA_HEADS):
            c = slice(hd * XA_HD, (hd + 1) * XA_HD)
            p = ps[hd]
            pb = p.astype(BF16)
            dob = do[:, c].astype(BF16)
            o_ref[:, c] = _nn(pb, v_ref[:, c]).astype(BF16)
            dp = _nt(dob, v_ref[:, c])
            ds = (p * (dp - jnp.sum(p * dp, axis=-1, keepdims=True)) * (XA_HD ** -0.5)).astype(BF16)
            dqs.append(_nn(ds, k_ref[:, c]))
            dk_ref[:, c] += _tn(ds, qb[:, c])
            dv_ref[:, c] += _tn(pb, dob)
        dq = jnp.concatenate(dqs, axis=1).astype(BF16)
        dq_ref[...] = dq
        dx, dw = _rms_bwd(xv, r, nw_ref[...], _nt(dq, wq_ref[...]))
        dx_ref[...] = dxo_ref[...] + dx
        dnw_ref[...] += dw

    row = pl.BlockSpec((ts, d), lambda i: (i, 0))
    full = lambda r, c: pl.BlockSpec((r, c), lambda i: (0, 0))
    sd = lambda dt: jax.ShapeDtypeStruct((s, d), dt)
    return pl.pallas_call(
        body,
        out_shape=(sd(F32), jax.ShapeDtypeStruct((1, d), F32), sd(BF16), sd(BF16), sd(BF16), sd(BF16),
                   jax.ShapeDtypeStruct((nm, d), F32), jax.ShapeDtypeStruct((nm, d), F32)),
        grid=(s // ts,),
        in_specs=[row, row, full(1, d), _lw((d, d), lambda i: (0, 0), layer), full(nm, d), pl.BlockSpec((nm, d), lambda i: (0, 1)),
                  _lw((d, d), lambda i: (0, 0), layer)],
        out_specs=(row, full(1, d), row, row, row, row, full(nm, d), full(nm, d)), name="attn_bwd",
        compiler_params=_params(("arbitrary",)),
    )(x, dxo, nw, wq, kv, kv, wo)


GATE_BLK0 = MIX_W // 512


def _merge_specs(ts, layer):
    row = pl.BlockSpec((ts, D_MODEL), lambda i: (i, 0))
    br = pl.BlockSpec((ts, 512), lambda i: (i, 0))
    gates = [pl.BlockSpec((ts, 512), functools.partial(lambda n, i: (i, GATE_BLK0 + n), n)) for n in range(8)]
    full = lambda *shape: pl.BlockSpec(shape, lambda i: (0,) * len(shape))
    weights = [full(4, D_MODEL), _lw((4, 512, D_MODEL), lambda i: (0, 0, 0), layer), _lw((D_MODEL, D_MODEL), lambda i: (0, 0), layer)]
    return row, br, gates, full, weights


def _merge_gates(gp_refs, gb_ref, kb):
    gp = jnp.concatenate([gp_refs[2 * kb][...], gp_refs[2 * kb + 1][...]], axis=1)
    return _sigmoid(gp + gb_ref[kb:kb + 1, :])


def _merge_fwd(x, branches, proj, gate_b, wb, wout, layer, ts=256):
    s, d = x.shape
    ts = min(ts, s)

    def body(x_ref, b0, b1, b2, b3, g0, g1, g2, g3, g4, g5, g6, g7, gb_ref, wb_ref, wo_ref, o_ref):
        brs, gps = (b0, b1, b2, b3), (g0, g1, g2, g3, g4, g5, g6, g7)
        merged = jnp.zeros((ts, d), F32)
        for kb in range(4):
            merged += _merge_gates(gps, gb_ref, kb) * _nn(brs[kb][...], wb_ref[kb])
        o_ref[...] = x_ref[...] + _nn(merged.astype(BF16), wo_ref[...])

    row, br, gates, full, weights = _merge_specs(ts, layer)
    return pl.pallas_call(
        body, out_shape=jax.ShapeDtypeStruct((s, d), F32), grid=(s // ts,),
        in_specs=[row, br, br, br, br] + gates + weights, out_specs=row, name="merge_fwd",
        compiler_params=_params(("parallel",)),
    )(x, *branches, *([proj] * 8), gate_b, wb, wout)


def _merge_bwd(dxo, branches, proj, gate_b, wb, wout, layer, ts=256):
    s, d = dxo.shape
    ts = min(ts, s)

    def body(dxo_ref, b0, b1, b2, b3, g0, g1, g2, g3, g4, g5, g6, g7, gb_ref, wb_ref, wo_ref,
             db0, db1, db2, db3, dgp_ref, dup_ref, mg_ref, dxb_ref, dgb_ref):
        brs, gps, dbs = (b0, b1, b2, b3), (g0, g1, g2, g3, g4, g5, g6, g7), (db0, db1, db2, db3)
        dxb = dxo_ref[...].astype(BF16)
        dxb_ref[...] = dxb
        dm = _nt(dxb, wo_ref[...])

        @pl.when(pl.program_id(0) == 0)
        def _():
            dgb_ref[...] = jnp.zeros_like(dgb_ref)

        merged = jnp.zeros((ts, d), F32)
        for kb in range(4):
            c = slice(kb * d, (kb + 1) * d)
            g = _merge_gates(gps, gb_ref, kb)
            up = _nn(brs[kb][...], wb_ref[kb])
            merged += g * up
            dup = (dm * g).astype(BF16)
            dup_ref[:, c] = dup
            dgp = dm * up * g * (1.0 - g)
            dgp_ref[:, c] = dgp.astype(BF16)
            dgb_ref[kb:kb + 1, :] += jnp.sum(dgp, axis=0, keepdims=True)
            dbs[kb][...] = _nt(dup, wb_ref[kb])
        mg_ref[...] = merged.astype(BF16)

    row, br, gates, full, weights = _merge_specs(ts, layer)
    wide = pl.BlockSpec((ts, 4 * d), lambda i: (i, 0))
    sb = jax.ShapeDtypeStruct((s, 512), F32)
    return pl.pallas_call(
        body,
        out_shape=(sb, sb, sb, sb, jax.ShapeDtypeStruct((s, 4 * d), BF16), jax.ShapeDtypeStruct((s, 4 * d), BF16),
                   jax.ShapeDtypeStruct((s, d), BF16), jax.ShapeDtypeStruct((s, d), BF16), jax.ShapeDtypeStruct((4, d), F32)),
        grid=(s // ts,),
        in_specs=[row, br, br, br, br] + gates + weights,
        out_specs=(br, br, br, br, wide, wide, row, row, full(4, d)), name="merge_bwd",
        compiler_params=_params(("arbitrary",)),
    )(dxo, *branches, *([proj] * 8), gate_b, wb, wout)


def _tri(n, upper=False):
    r = lax.broadcasted_iota(jnp.int32, (n, n), 0)
    c = lax.broadcasted_iota(jnp.int32, (n, n), 1)
    return jnp.where((c >= r) if upper else (c <= r), 1.0, 0.0).astype(F32)


def _hg_gates(hq, hf, lb):
    sg = _sigmoid(hf)
    fg = lb + (1.0 - lb) * sg
    sq = _sigmoid(hq)
    return sg, fg, 1.0 - fg, jnp.log(fg), hq * sq, sq


def _hg_intra(qf, kk, b):
    out = []
    col = lax.broadcasted_iota(jnp.int32, (SUB, CHUNK), 1)
    row = lax.broadcasted_iota(jnp.int32, (SUB, CHUNK), 0)
    for i in range(CHUNK // SUB):
        rs = slice(i * SUB, (i + 1) * SUB)
        ref = b[i * SUB - 1:i * SUB, :] if i else jnp.zeros((1, b.shape[1]), F32)
        eq = jnp.exp(b[rs] - ref)
        ek = jnp.exp(jnp.minimum(ref - b, EXP_CLAMP))
        out.append((qf[rs] * eq, kk * ek, col <= row + i * SUB, eq, ek))
    return out


def _hg_chunk_fwd(qf, kk, b, v, st):
    parts = _hg_intra(qf, kk, b)
    vb = v.astype(BF16)
    att = [jnp.where(m, _nt(qt.astype(BF16), kt.astype(BF16)), 0.0) for qt, kt, m, _, _ in parts]
    o = jnp.concatenate([_nn(a.astype(BF16), vb) for a in att], axis=0)
    qh = qf * jnp.exp(b)
    o = o + _nt(qh.astype(BF16), st.astype(BF16))
    bl = b[CHUNK - 1:CHUNK, :]
    kh = kk * jnp.exp(bl - b)
    return o, parts, att, qh, kh, jnp.exp(bl)


def _hgrn_fwd(proj, lb, nw, ts=256):
    s = proj.shape[0]
    ts = min(ts, s)
    nch = ts // CHUNK

    def body(q_ref, f_ref, v_ref, g_ref, lb_ref, nw_ref, o_ref, st_ref, st):
        @pl.when(pl.program_id(0) == 0)
        def _():
            st[...] = jnp.zeros_like(st)

        tri = _tri(CHUNK)

        def chunk(c, carry):
            rows = pl.ds(pl.multiple_of(c * CHUNK, CHUNK), CHUNK)
            _, _, kk, lf, qf, _ = _hg_gates(q_ref[rows, :], f_ref[rows, :], lb_ref[...])
            b = _nn(tri, lf, precision=lax.Precision.HIGHEST)
            hv, hg = v_ref[rows, :], g_ref[rows, :]
            st_ref[c] = st[...]
            for h in range(HG_HEADS):
                cs = slice(h * HG_D, (h + 1) * HG_D)
                o, _, _, _, kh, ebl = _hg_chunk_fwd(qf[:, cs], kk[:, cs], b[:, cs], hv[:, cs], st[h])
                st[h] = st[h] * ebl + _tn(hv[:, cs].astype(BF16), kh.astype(BF16))
                on = _rms_fwd(o, nw_ref[...])[0]
                gh = hg[:, cs]
                o_ref[rows, cs] = (on * gh * _sigmoid(gh)).astype(BF16)
            return carry

        lax.fori_loop(0, nch, chunk, 0, unroll=2)

    col = lambda n: pl.BlockSpec((ts, 512), functools.partial(lambda n, i: (i, n), n))
    return pl.pallas_call(
        body,
        out_shape=(jax.ShapeDtypeStruct((s, 512), BF16), jax.ShapeDtypeStruct((s // CHUNK, HG_HEADS, HG_D, HG_D), F32)),
        grid=(s // ts,),
        in_specs=[col(0), col(1), col(2), col(3), pl.BlockSpec((1, 512), lambda i: (0, 0)), pl.BlockSpec((1, HG_D), lambda i: (0, 0))],
        out_specs=(pl.BlockSpec((ts, 512), lambda i: (i, 0)), pl.BlockSpec((nch, HG_HEADS, HG_D, HG_D), lambda i: (i, 0, 0, 0))),
        scratch_shapes=[pltpu.VMEM((HG_HEADS, HG_D, HG_D), F32)], name="hgrn_fwd",
        compiler_params=_params(("arbitrary",)),
    )(proj, proj, proj, proj, lb, nw)


def _hgrn_bwd(proj, dout, states, lb, nw, ts=256):
    s = proj.shape[0]
    ts = min(ts, s)
    nch = ts // CHUNK
    nt = s // ts

    def body(q_ref, f_ref, v_ref, g_ref, do_ref, st_ref, lb_ref, nw_ref, dp_ref, dlb_ref, dnw_ref, dst):
        @pl.when(pl.program_id(0) == 0)
        def _():
            dst[...] = jnp.zeros_like(dst)
            dlb_ref[...] = jnp.zeros_like(dlb_ref)
            dnw_ref[...] = jnp.zeros_like(dnw_ref)

        tri, triu = _tri(CHUNK), _tri(CHUNK, upper=True)
        last = lax.broadcasted_iota(jnp.int32, (CHUNK, HG_D), 0) == CHUNK - 1
        nwv = nw_ref[...]

        def chunk(cc, carry):
            c = nch - 1 - cc
            rows = pl.ds(pl.multiple_of(c * CHUNK, CHUNK), CHUNK)
            hq, hf, hv, hg = q_ref[rows, :], f_ref[rows, :], v_ref[rows, :], g_ref[rows, :]
            lbv = lb_ref[...]
            sg, fg, kk, lf, qf, sq = _hg_gates(hq, hf, lbv)
            b = _nn(tri, lf, precision=lax.Precision.HIGHEST)
            dov = do_ref[rows, :]
            dqf_l, dkk_l, db_l, dv_l, dg_l = [], [], [], [], []
            for h in range(HG_HEADS):
                cs = slice(h * HG_D, (h + 1) * HG_D)
                stp = st_ref[c, h]
                bh, vh, gh = b[:, cs], hv[:, cs], hg[:, cs]
                o, parts, att, qh, kh, ebl = _hg_chunk_fwd(qf[:, cs], kk[:, cs], bh, vh, stp)
                sgg = _sigmoid(gh)
                on, r = _rms_fwd(o, nwv)
                d_on = dov[:, cs] * (gh * sgg)
                dg_l.append(dov[:, cs] * on * _dsilu(gh, sgg))
                do, dnw = _rms_bwd(o, r, nwv, d_on)
                dnw_ref[...] += dnw
                dob, vb = do.astype(BF16), vh.astype(BF16)
                dsth = dst[h]
                dstb = dsth.astype(BF16)
                dqh = _nn(do, stp, precision=HI)
                dkh = _nn(vh, dsth, precision=HI)
                dv = _nt(kh.astype(BF16), dstb)
                eb = jnp.exp(bh)
                ekl = jnp.exp(bh[CHUNK - 1:CHUNK, :] - bh)
                dqf, dkk = dqh * eb, dkh * ekl
                db = dqh * qh - dkh * kh
                dbl = jnp.sum(dkh * kh, axis=0, keepdims=True) + ebl * jnp.sum(dsth * stp, axis=0, keepdims=True)
                dst[h] = dsth * ebl + _tn(dob, qh.astype(BF16))
                dq_rows = []
                for i, (qt, kt, m, eq, ek) in enumerate(parts):
                    rs = slice(i * SUB, (i + 1) * SUB)
                    da = jnp.where(m, _nt(dob[rs], vb), 0.0)
                    dv = dv + _tn(att[i].astype(BF16), dob[rs])
                    dqt = _nn(da, kt, precision=HI)
                    dkt = _tn(da, qt, precision=HI)
                    dq_rows.append((dqt * eq, dqt * qt))
                    dkk = dkk + dkt * ek
                    db = db - dkt * kt
                dqf = dqf + jnp.concatenate([a for a, _ in dq_rows], axis=0)
                db = db + jnp.concatenate([a for _, a in dq_rows], axis=0) + jnp.where(last, dbl, 0.0)
                dqf_l.append(dqf); dkk_l.append(dkk); db_l.append(db); dv_l.append(dv)
            cat = lambda l: jnp.concatenate(l, axis=1)
            dlf = _nn(triu, cat(db_l), precision=lax.Precision.HIGHEST)
            dfg = dlf / fg - cat(dkk_l)
            dlb_ref[...] += jnp.sum(dfg * (1.0 - sg), axis=0, keepdims=True)
            dp_ref[rows, 0:512] = (cat(dqf_l) * _dsilu(hq, sq)).astype(BF16)
            dp_ref[rows, 512:1024] = (dfg * (1.0 - lbv) * sg * (1.0 - sg)).astype(BF16)
            dp_ref[rows, 1024:1536] = cat(dv_l).astype(BF16)
            dp_ref[rows, 1536:2048] = cat(dg_l).astype(BF16)
            return carry

        lax.fori_loop(0, nch, chunk, 0, unroll=2)

    col = lambda n: pl.BlockSpec((ts, 512), functools.partial(lambda n, i: (nt - 1 - i, n), n))
    vec = lambda n: pl.BlockSpec((1, n), lambda i: (0, 0))
    return pl.pallas_call(
        body,
        out_shape=(jax.ShapeDtypeStruct((s, 2048), BF16), jax.ShapeDtypeStruct((1, 512), F32), jax.ShapeDtypeStruct((1, HG_D), F32)),
        grid=(nt,),
        in_specs=[col(0), col(1), col(2), col(3), pl.BlockSpec((ts, 512), lambda i: (nt - 1 - i, 0)),
                  pl.BlockSpec((nch, HG_HEADS, HG_D, HG_D), lambda i: (nt - 1 - i, 0, 0, 0)), vec(512), vec(HG_D)],
        out_specs=(pl.BlockSpec((ts, 2048), lambda i: (nt - 1 - i, 0)), vec(512), vec(HG_D)),
        scratch_shapes=[pltpu.VMEM((HG_HEADS, HG_D, HG_D), F32)], name="hgrn_bwd",
        compiler_params=_params(("arbitrary",)),
    )(proj, proj, proj, proj, dout, states, lb, nw)


CV_BLK = 2048 // 512


def _halo_before(ts, halo, colblk):
    return pl.BlockSpec((halo, 512), functools.partial(lambda cb, i: (jnp.maximum(i * (ts // halo) - 1, 0), cb), colblk))


def _cv_front(a_ref, g_ref, ah_ref, gh_ref, ext, first):
    a, sg = a_ref[...], _sigmoid(g_ref[...])
    zh = ah_ref[...] * _sigmoid(gh_ref[...])
    ext[0:CV_HALO, :] = jnp.where(first, 0.0, zh)
    ext[CV_HALO:, :] = a * sg
    return a, sg


def _cv_conv_ln(ext, w_ref, b_ref, ts):
    y = jnp.zeros((ts, CV_W), F32) + b_ref[...]
    for j in range(CV_K):
        y = y + w_ref[j:j + 1, :] * ext[pl.ds(CV_HALO - (CV_K - 1) + j, ts), :]
    mu = jnp.mean(y, axis=-1, keepdims=True)
    yc = y - mu
    r = lax.rsqrt(jnp.mean(yc * yc, axis=-1, keepdims=True) + EPS)
    return yc * r, r


def _conv_fwd(proj, w, b, lnw, lnb, ts=256):
    s = proj.shape[0]
    ts = min(ts, s)

    def body(a_ref, g_ref, ah_ref, gh_ref, w_ref, b_ref, lnw_ref, lnb_ref, o_ref, ext):
        _cv_front(a_ref, g_ref, ah_ref, gh_ref, ext, pl.program_id(0) == 0)
        yh, _ = _cv_conv_ln(ext, w_ref, b_ref, ts)
        yn = yh * lnw_ref[...] + lnb_ref[...]
        o_ref[...] = (yn * _sigmoid(yn)).astype(BF16)

    col = lambda n: pl.BlockSpec((ts, 512), functools.partial(lambda n, i: (i, n), n))
    vec = pl.BlockSpec((1, CV_W), lambda i: (0, 0))
    return pl.pallas_call(
        body, out_shape=jax.ShapeDtypeStruct((s, CV_W), BF16), grid=(s // ts,),
        in_specs=[col(CV_BLK), col(CV_BLK + 1), _halo_before(ts, CV_HALO, CV_BLK), _halo_before(ts, CV_HALO, CV_BLK + 1),
                  pl.BlockSpec((32, CV_W), lambda i: (0, 0)), vec, vec, vec],
        out_specs=pl.BlockSpec((ts, CV_W), lambda i: (i, 0)), scratch_shapes=[pltpu.VMEM((ts + CV_HALO, CV_W), F32)],
        name="conv_fwd", compiler_params=_params(("parallel",)),
    )(proj, proj, proj, proj, w, b, lnw, lnb)


def _conv_bwd(proj, dout, w, b, lnw, lnb, ts=256):
    s = proj.shape[0]
    ts = min(ts, s)
    nt = s // ts

    def body(a_ref, g_ref, ah_ref, gh_ref, do_ref, w_ref, b_ref, lnw_ref, lnb_ref,
             du_ref, dw_ref, db_ref, dlnw_ref, dlnb_ref, ext, dyext, carry):
        i = pl.program_id(0)

        @pl.when(i == 0)
        def _():
            carry[...] = jnp.zeros_like(carry)
            for ref in (dw_ref, db_ref, dlnw_ref, dlnb_ref):
                ref[...] = jnp.zeros_like(ref)

        a, sg = _cv_front(a_ref, g_ref, ah_ref, gh_ref, ext, i == nt - 1)
        yh, r = _cv_conv_ln(ext, w_ref, b_ref, ts)
        yn = yh * lnw_ref[...] + lnb_ref[...]
        dyn = do_ref[...] * _dsilu(yn, _sigmoid(yn))
        dlnw_ref[...] += jnp.sum(dyn * yh, axis=0, keepdims=True)
        dlnb_ref[...] += jnp.sum(dyn, axis=0, keepdims=True)
        gl = dyn * lnw_ref[...]
        dy = r * (gl - jnp.mean(gl, axis=-1, keepdims=True) - yh * jnp.mean(gl * yh, axis=-1, keepdims=True))
        db_ref[...] += jnp.sum(dy, axis=0, keepdims=True)
        dyext[0:ts, :] = dy
        dyext[ts:, :] = carry[...]
        carry[...] = dy[0:CV_HALO, :]
        dz = jnp.zeros((ts, CV_W), F32)
        for j in range(CV_K):
            dw_ref[j:j + 1, :] += jnp.sum(dy * ext[pl.ds(CV_HALO - (CV_K - 1) + j, ts), :], axis=0, keepdims=True)
            dz = dz + w_ref[j:j + 1, :] * dyext[pl.ds(CV_K - 1 - j, ts), :]
        du_ref[:, 0:CV_W] = (dz * sg).astype(BF16)
        du_ref[:, CV_W:] = (dz * a * sg * (1.0 - sg)).astype(BF16)

    rev = lambda n: pl.BlockSpec((ts, 512), functools.partial(lambda n, i: (nt - 1 - i, n), n))
    halo = lambda n: pl.BlockSpec((CV_HALO, 512), functools.partial(
        lambda n, i: (jnp.maximum((nt - 1 - i) * (ts // CV_HALO) - 1, 0), n), n))
    vec = pl.BlockSpec((1, CV_W), lambda i: (0, 0))
    wsp = pl.BlockSpec((32, CV_W), lambda i: (0, 0))
    v1 = jax.ShapeDtypeStruct((1, CV_W), F32)
    return pl.pallas_call(
        body, out_shape=(jax.ShapeDtypeStruct((s, 2 * CV_W), BF16), jax.ShapeDtypeStruct((32, CV_W), F32), v1, v1, v1),
        grid=(nt,),
        in_specs=[rev(CV_BLK), rev(CV_BLK + 1), halo(CV_BLK), halo(CV_BLK + 1), rev(0), wsp, vec, vec, vec],
        out_specs=(pl.BlockSpec((ts, 2 * CV_W), lambda i: (nt - 1 - i, 0)), wsp, vec, vec, vec),
        scratch_shapes=[pltpu.VMEM((ts + CV_HALO, CV_W), F32), pltpu.VMEM((ts + CV_HALO, CV_W), F32), pltpu.VMEM((CV_HALO, CV_W), F32)],
        name="conv_bwd", compiler_params=_params(("arbitrary",)),
    )(proj, proj, proj, proj, dout, w, b, lnw, lnb)


PL_BLK = 3072 // 512


def _pool_windows(ext, t0, ts):
    n = ext.shape[0]
    t = t0 + lax.broadcasted_iota(jnp.int32, (ts, 1), 0)
    out = []
    for g, wdw in enumerate(POOL_WINDOWS):
        e = ext[:, g * 128:(g + 1) * 128]
        acc, k = e, 1
        while k < wdw:
            acc = acc + pltpu.roll(acc, k, 0)
            k *= 2
        cnt = jnp.minimum(t + 1, wdw).astype(F32)
        out.append(acc[POOL_HALO:] / cnt - e[POOL_HALO:])
    return out


def _pool_fwd(proj, w, sc, ts=256):
    s = proj.shape[0]
    ts = min(ts, s)

    def body(u_ref, uh_ref, w_ref, sc_ref, o_ref):
        i = pl.program_id(0)
        ext = jnp.concatenate([jnp.where(i == 0, 0.0, uh_ref[...]), u_ref[...]], axis=0)
        ps = _pool_windows(ext, i * ts, ts)
        y = jnp.concatenate([_nn(ps[g].astype(BF16), w_ref[g].astype(BF16)) for g in range(4)], axis=1)
        o_ref[...] = (y * sc_ref[...]).astype(BF16)

    return pl.pallas_call(
        body, out_shape=jax.ShapeDtypeStruct((s, 512), BF16), grid=(s // ts,),
        in_specs=[pl.BlockSpec((ts, 512), lambda i: (i, PL_BLK)), _halo_before(ts, POOL_HALO, PL_BLK),
                  pl.BlockSpec((4, 128, 128), lambda i: (0, 0, 0)), pl.BlockSpec((1, 512), lambda i: (0, 0))],
        out_specs=pl.BlockSpec((ts, 512), lambda i: (i, 0)), name="pool_fwd", compiler_params=_params(("parallel",)),
    )(proj, proj, w, sc)


def _pool_bwd(proj, dout, w, sc, ts=256):
    s = proj.shape[0]
    ts = min(ts, s)
    nt = s // ts
    n = ts + POOL_HALO

    def body(u_ref, uh_ref, do_ref, doh_ref, w_ref, sc_ref, du_ref, dw_ref, dsc_ref):
        i = pl.program_id(0)

        @pl.when(i == 0)
        def _():
            dw_ref[...] = jnp.zeros_like(dw_ref)
            dsc_ref[...] = jnp.zeros_like(dsc_ref)

        ext = jnp.concatenate([jnp.where(i == 0, 0.0, uh_ref[...]), u_ref[...]], axis=0)
        ps = _pool_windows(ext, i * ts, ts)
        dov = do_ref[...]
        dyext = jnp.concatenate([dov, jnp.where(i == nt - 1, 0.0, doh_ref[...])], axis=0) * sc_ref[...]
        t = i * ts + lax.broadcasted_iota(jnp.int32, (n, 1), 0)
        row = lax.broadcasted_iota(jnp.int32, (n, 1), 0)
        dus = []
        for g, wdw in enumerate(POOL_WINDOWS):
            cs = slice(g * 128, (g + 1) * 128)
            wg, pb = w_ref[g].astype(BF16), ps[g].astype(BF16)
            dsc_ref[:, cs] += jnp.sum(dov[:, cs] * _nn(pb, wg), axis=0, keepdims=True)
            dyg = dyext[:, cs].astype(BF16)
            dw_ref[g] += _tn(pb, dyg[0:ts])
            dp = _nt(dyg, wg)
            acc, k = dp / jnp.minimum(t + 1, wdw).astype(F32), 1
            while k < wdw:
                acc = acc + jnp.where(row < n - k, pltpu.roll(acc, n - k, 0), 0.0)
                k *= 2
            dus.append(acc[0:ts] - dp[0:ts])
        du_ref[...] = jnp.concatenate(dus, axis=1).astype(BF16)

    tile = lambda cb: pl.BlockSpec((ts, 512), functools.partial(lambda cb, i: (i, cb), cb))
    after = pl.BlockSpec((POOL_HALO, 512), lambda i: (jnp.minimum((i + 1) * (ts // POOL_HALO), s // POOL_HALO - 1), 0))
    wsp, vec = pl.BlockSpec((4, 128, 128), lambda i: (0, 0, 0)), pl.BlockSpec((1, 512), lambda i: (0, 0))
    return pl.pallas_call(
        body, out_shape=(jax.ShapeDtypeStruct((s, 512), BF16), jax.ShapeDtypeStruct((4, 128, 128), F32), jax.ShapeDtypeStruct((1, 512), F32)),
        grid=(nt,),
        in_specs=[tile(PL_BLK), _halo_before(ts, POOL_HALO, PL_BLK), tile(0), after, wsp, vec],
        out_specs=(tile(0), wsp, vec), name="pool_bwd", compiler_params=_params(("arbitrary",)),
    )(proj, proj, dout, dout, w, sc)


LX_BLK, LY_BLK = 3584 // 512, 4096 // 512
LRU_OFF = LRU_HALO - (LRU_CONV - 1)


def _scan_fwd(a, b):
    n = a.shape[0]
    row = lax.broadcasted_iota(jnp.int32, (n, 1), 0)
    k = 1
    while k < n:
        m = row >= k
        b = jnp.where(m, a * pltpu.roll(b, k, 0) + b, b)
        a = jnp.where(m, a * pltpu.roll(a, k, 0), a)
        k *= 2
    return a, b


def _scan_rev(a, b):
    n = a.shape[0]
    row = lax.broadcasted_iota(jnp.int32, (n, 1), 0)
    k = 1
    while k < n:
        m = row < n - k
        b = jnp.where(m, a * pltpu.roll(b, n - k, 0) + b, b)
        a = jnp.where(m, a * pltpu.roll(a, n - k, 0), a)
        k *= 2
    return b


def _lru_gates(x_ref, xh_ref, ext, first, cw_ref, cb_ref, wa_ref, ba_ref, wx_ref, bx_ref, lam_ref, ts):
    ext[0:LRU_HALO, :] = jnp.where(first, 0.0, xh_ref[...])
    ext[LRU_HALO:, :] = x_ref[...]
    xc = jnp.zeros((ts, LRU_W), F32) + cb_ref[...]
    for j in range(LRU_CONV):
        xc = xc + cw_ref[j:j + 1, :] * ext[pl.ds(LRU_OFF + j, ts), :]
    xb = xc.astype(BF16)
    r = _sigmoid(_nn(xb, wa_ref[...]) + ba_ref[...])
    ig = _sigmoid(_nn(xb, wx_ref[...]) + bx_ref[...])
    nl = -lam_ref[...]
    sp = jnp.maximum(nl, 0.0) + jnp.log(1.0 + jnp.exp(-jnp.abs(nl)))
    la = -LRU_C * r * sp
    a = jnp.exp(la)
    z = 2.0 * la
    em = jnp.where(z > -0.1, -z * (1.0 + z * 0.5 * (1.0 + z * (1.0 / 3) * (1.0 + z * 0.25 * (1.0 + z * 0.2)))), 1.0 - a * a)
    return xc, xb, r, ig, sp, a, jnp.sqrt(em)


def _lru_fwd(proj, cw, cb, wa, ba, wx, bx, lam, ts=256):
    s = proj.shape[0]
    ts = min(ts, s)

    def body(x_ref, xh_ref, y_ref, cw_ref, cb_ref, wa_ref, ba_ref, wx_ref, bx_ref, lam_ref, o_ref, h_ref, ext, hc):
        i = pl.program_id(0)

        @pl.when(i == 0)
        def _():
            hc[...] = jnp.zeros_like(hc)

        xc, _, _, ig, _, a, mult = _lru_gates(x_ref, xh_ref, ext, i == 0, cw_ref, cb_ref, wa_ref, ba_ref, wx_ref, bx_ref, lam_ref, ts)
        acum, h0 = _scan_fwd(a, mult * ig * xc)
        h = h0 + acum * hc[0:1, :]
        hc[...] = jnp.broadcast_to(h[ts - 1:ts, :], hc.shape)
        h_ref[...] = h
        o_ref[...] = (h * _gelu_parts(y_ref[...])[0]).astype(BF16)

    tile = lambda cb_: pl.BlockSpec((ts, 512), functools.partial(lambda c, i: (i, c), cb_))
    vec = pl.BlockSpec((1, LRU_W), lambda i: (0, 0))
    mat = pl.BlockSpec((LRU_W, LRU_W), lambda i: (0, 0))
    return pl.pallas_call(
        body, out_shape=(jax.ShapeDtypeStruct((s, LRU_W), BF16), jax.ShapeDtypeStruct((s, LRU_W), F32)), grid=(s // ts,),
        in_specs=[tile(LX_BLK), _halo_before(ts, LRU_HALO, LX_BLK), tile(LY_BLK), pl.BlockSpec((8, LRU_W), lambda i: (0, 0)),
                  vec, mat, vec, mat, vec, vec],
        out_specs=(tile(0), tile(0)), scratch_shapes=[pltpu.VMEM((ts + LRU_HALO, LRU_W), F32), pltpu.VMEM((8, LRU_W), F32)],
        name="lru_fwd", compiler_params=_params(("arbitrary",)),
    )(proj, proj, proj, cw, cb, wa, ba, wx, bx, lam)


def _lru_bwd(proj, hs, dout, cw, cb, wa, ba, wx, bx, lam, ts=256):
    s = proj.shape[0]
    ts = min(ts, s)
    nt = s // ts

    def body(x_ref, xh_ref, y_ref, h_ref, hh_ref, do_ref, cw_ref, cb_ref, wa_ref, ba_ref, wx_ref, bx_ref, lam_ref,
             dxy_ref, dcw_ref, dcb_ref, dwa_ref, dba_ref, dwx_ref, dbx_ref, dlam_ref, ext, dext, cg, cd):
        i = pl.program_id(0)
        first_tile = i == nt - 1

        @pl.when(i == 0)
        def _():
            cg[...] = jnp.zeros_like(cg)
            cd[...] = jnp.zeros_like(cd)
            for ref in (dcw_ref, dcb_ref, dwa_ref, dba_ref, dwx_ref, dbx_ref, dlam_ref):
                ref[...] = jnp.zeros_like(ref)

        xc, xb, r, ig, sp, a, mult = _lru_gates(x_ref, xh_ref, ext, first_tile, cw_ref, cb_ref, wa_ref, ba_ref, wx_ref, bx_ref, lam_ref, ts)
        row = lax.broadcasted_iota(jnp.int32, (ts, 1), 0)
        h, dov = h_ref[...], do_ref[...]
        gel, dgel = _gelu_parts(y_ref[...])
        dxy_ref[:, LRU_W:] = (dov * h * dgel).astype(BF16)
        alpha = jnp.where(row < ts - 1, pltpu.roll(a, ts - 1, 0), 0.0)
        g = _scan_rev(alpha, dov * gel + jnp.where(row == ts - 1, cg[0:1, :], 0.0))
        cg[...] = jnp.broadcast_to(a[0:1, :] * g[0:1, :], cg.shape)
        hprev = jnp.where(row == 0, jnp.where(first_tile, 0.0, hh_ref[LRU_HALO - 1:LRU_HALO, :]), pltpu.roll(h, 1, 0))
        dla = g * hprev * a - g * ig * xc * (a * a) / mult
        dpr = dla * (-LRU_C * sp) * r * (1.0 - r)
        dpi = g * mult * xc * ig * (1.0 - ig)
        dprb, dpib = dpr.astype(BF16), dpi.astype(BF16)
        dxc = g * mult * ig + _nt(dprb, wa_ref[...]) + _nt(dpib, wx_ref[...])
        dlam_ref[...] += jnp.sum(dla * (-LRU_C * r), axis=0, keepdims=True) * (-_sigmoid(-lam_ref[...]))
        dwa_ref[...] += _tn(xb, dprb)
        dwx_ref[...] += _tn(xb, dpib)
        dba_ref[...] += jnp.sum(dpr, axis=0, keepdims=True)
        dbx_ref[...] += jnp.sum(dpi, axis=0, keepdims=True)
        dcb_ref[...] += jnp.sum(dxc, axis=0, keepdims=True)
        dext[0:ts, :] = dxc
        dext[ts:, :] = cd[...]
        cd[...] = dxc[0:LRU_HALO, :]
        dx = jnp.zeros((ts, LRU_W), F32)
        for j in range(LRU_CONV):
            dcw_ref[j:j + 1, :] += jnp.sum(dxc * ext[pl.ds(LRU_OFF + j, ts), :], axis=0, keepdims=True)
            dx = dx + cw_ref[j:j + 1, :] * dext[pl.ds(LRU_CONV - 1 - j, ts), :]
        dxy_ref[:, 0:LRU_W] = dx.astype(BF16)

    rev = lambda c: pl.BlockSpec((ts, 512), functools.partial(lambda c, i: (nt - 1 - i, c), c))
    halo = lambda c: pl.BlockSpec((LRU_HALO, 512), functools.partial(
        lambda c, i: (jnp.maximum((nt - 1 - i) * (ts // LRU_HALO) - 1, 0), c), c))
    vec = pl.BlockSpec((1, LRU_W), lambda i: (0, 0))
    mat = pl.BlockSpec((LRU_W, LRU_W), lambda i: (0, 0))
    cws = pl.BlockSpec((8, LRU_W), lambda i: (0, 0))
    v1, m1 = jax.ShapeDtypeStruct((1, LRU_W), F32), jax.ShapeDtypeStruct((LRU_W, LRU_W), F32)
    return pl.pallas_call(
        body, out_shape=(jax.ShapeDtypeStruct((s, 2 * LRU_W), BF16), jax.ShapeDtypeStruct((8, LRU_W), F32), v1, m1, v1, m1, v1, v1),
        grid=(nt,),
        in_specs=[rev(LX_BLK), halo(LX_BLK), rev(LY_BLK), rev(0), halo(0), rev(0), cws, vec, mat, vec, mat, vec, vec],
        out_specs=(pl.BlockSpec((ts, 2 * LRU_W), lambda i: (nt - 1 - i, 0)), cws, vec, mat, vec, mat, vec, vec),
        scratch_shapes=[pltpu.VMEM((ts + LRU_HALO, LRU_W), F32), pltpu.VMEM((ts + LRU_HALO, LRU_W), F32),
                        pltpu.VMEM((8, LRU_W), F32), pltpu.VMEM((LRU_HALO, LRU_W), F32)],
        name="lru_bwd", compiler_params=_params(("arbitrary",)),
    )(proj, proj, proj, hs, hs, dout, cw, cb, wa, ba, wx, bx, lam)


def _final_loss(x, fw, tgt, ts=512):
    s, d = x.shape
    ts = min(ts, s)

    def body(x_ref, w_ref, t_ref, loss_ref, dx_ref, dw_ref):
        @pl.when(pl.program_id(0) == 0)
        def _():
            loss_ref[...] = jnp.zeros_like(loss_ref)
            dw_ref[...] = jnp.zeros_like(dw_ref)

        xv = x_ref[...]
        y, r = _rms_fwd(xv, w_ref[...])
        err = y - t_ref[...]
        loss_ref[...] += 0.5 * jnp.sum(jnp.mean(err * err, axis=-1, keepdims=True), axis=0, keepdims=True)
        dx, dw = _rms_bwd(xv, r, w_ref[...], err * (1.0 / d))
        dx_ref[...] = dx
        dw_ref[...] += dw

    row = pl.BlockSpec((ts, d), lambda i: (i, 0))
    vec = pl.BlockSpec((1, d), lambda i: (0, 0))
    return pl.pallas_call(
        body, out_shape=(jax.ShapeDtypeStruct((8, 128), F32), jax.ShapeDtypeStruct((s, d), F32), jax.ShapeDtypeStruct((1, d), F32)),
        grid=(s // ts,), in_specs=[row, vec, row], out_specs=(pl.BlockSpec((8, 128), lambda i: (0, 0)), row, vec),
        name="final_loss", compiler_params=_params(("arbitrary",)),
    )(x, fw, tgt)


def _lb_softmax(raw_ref):
    raw = raw_ref[...]
    e = jnp.exp(raw - jnp.max(raw, axis=0, keepdims=True))
    return e / jnp.sum(e, axis=0, keepdims=True)


def _lb_fwd(raw):
    def body(raw_ref, o_ref):
        sm = _lb_softmax(raw_ref)
        acc = jnp.zeros((1, sm.shape[1]), F32)
        o_ref[0:1, :] = acc
        for l in range(1, DEPTH):
            acc = acc + sm[l:l + 1, :]
            o_ref[l:l + 1, :] = acc

    return pl.pallas_call(body, out_shape=jax.ShapeDtypeStruct(raw.shape, F32), name="lb_fwd")(raw)


def _lb_bwd(raw, dlb):
    def body(raw_ref, d_ref, o_ref):
        sm = _lb_softmax(raw_ref)
        dlbv = d_ref[...]
        dsm, acc = [None] * DEPTH, jnp.zeros((1, sm.shape[1]), F32)
        for l in range(DEPTH - 1, 0, -1):
            acc = acc + dlbv[l:l + 1, :]
            dsm[l] = acc
        dsm[0] = jnp.zeros_like(acc)
        dsm = jnp.concatenate(dsm, axis=0)
        o_ref[...] = sm * (dsm - jnp.sum(sm * dsm, axis=0, keepdims=True))

    return pl.pallas_call(body, out_shape=jax.ShapeDtypeStruct(raw.shape, F32), name="lb_bwd")(raw, dlb)


def _adam_math(w, g, m, v):
    m = ADAM_B1 * m + (1.0 - ADAM_B1) * g
    v = ADAM_B2 * v + (1.0 - ADAM_B2) * (g * g)
    m_hat = m / (1.0 - ADAM_B1 ** ADAM_STEP)
    v_hat = v / (1.0 - ADAM_B2 ** ADAM_STEP)
    return -ADAM_LR * (m_hat / (jnp.sqrt(v_hat) + ADAM_EPS) + ADAM_WD * w), m, v


def _adamw(w, gs, m, v, name, tr=128):
    r, c = w.shape
    tr = min(tr, r)
    ng = len(gs)

    def body(*refs):
        w_ref, g_refs, m_ref, v_ref = refs[0], refs[1:1 + ng], refs[1 + ng], refs[2 + ng]
        outs = refs[3 + ng:]
        g = g_refs[0][...]
        if ng == 2:
            g = g + g_refs[1][...]
            outs[0][...] = g
            outs = outs[1:]
        for o, val in zip(outs, _adam_math(w_ref[...], g, m_ref[...], v_ref[...])):
            o[...] = val

    blk = pl.BlockSpec((tr, c), lambda i: (i, 0))
    sd = jax.ShapeDtypeStruct((r, c), F32)
    nout = 3 + (ng == 2)
    return pl.pallas_call(
        body, out_shape=(sd,) * nout, grid=(r // tr,), in_specs=[blk] * (3 + ng), out_specs=(blk,) * nout, name=name,
        compiler_params=_params(("parallel",)),
    )(w, *gs, m, v)


def _cast_into_full(w, kind, jj, name, tr=256):
    l, r, c = w.shape
    tr = min(tr, r)

    def body(j_ref, w_ref, o_ref):
        o_ref[...] = w_ref[...].astype(BF16)

    if kind == "col":
        full, dst = (l, r, 4 * c), pl.BlockSpec((None, tr, c), lambda a, b, j: (a, b, j[0]))
    else:
        full, dst = (l, 4 * r, c), pl.BlockSpec((None, tr, c), lambda a, b, j: (a, j[0] * (r // tr) + b, 0))
    return pl.pallas_call(
        body, out_shape=jax.ShapeDtypeStruct(full, BF16),
        grid_spec=pltpu.PrefetchScalarGridSpec(
            num_scalar_prefetch=1, grid=(l, r // tr), in_specs=[pl.BlockSpec((None, tr, c), lambda a, b, j: (a, b, 0))], out_specs=dst),
        name=name, compiler_params=_params(("parallel", "parallel")),
    )(jj, w)


def _place():
    return lax.axis_index("x"), lax.axis_index("y"), lax.axis_index("c")


def _other_chips(x, y):
    return [(1 - x, y), (x, 1 - y), (1 - x, 1 - y)]


def _slab(ref, kind, jj):
    if kind == "col":
        c = ref.shape[2] // 4
        return ref.at[:, :, pl.ds(jj * c, c)]
    r = ref.shape[1] // 4
    return ref.at[:, pl.ds(jj * r, r), :]


HBM = pl.BlockSpec(memory_space=pltpu.HBM)
SEM = pl.BlockSpec(memory_space=pltpu.SEMAPHORE)
EFFECT = pltpu.SideEffectType.DATAFLOW_SIDE_EFFECTING


def _in_hbm(a):
    return pltpu.with_memory_space_constraint(a, pltpu.HBM)


def _thru(arrs):
    return [pltpu.HBM(a.shape, a.dtype) for a in arrs]


def _gather_copy(fulls, kinds, send_sems, recv_sems, group, t, k, landing):
    x, y, c = _place()
    chip = _other_chips(x, y)[k]
    per = fulls[t].shape[0] // DEPTH
    rows = fulls[t].at[pl.ds(group * per, per)]
    idx = (group * len(fulls) + t) * 3 + k
    return pltpu.make_async_remote_copy(
        src_ref=_slab(rows, kinds[t], 2 * x + y), dst_ref=_slab(rows, kinds[t], landing), send_sem=send_sems.at[idx],
        recv_sem=recv_sems.at[idx], device_id=(chip[0], chip[1], c), device_id_type=MESH)


def _gather_start(fulls, kinds, after):
    nt = len(fulls)
    ncp = DEPTH * nt * 3

    def body(*refs):
        ins, send_sems, recv_sems = refs[:nt], refs[nt + 1], refs[nt + 2]
        x, y, _ = _place()
        for group in range(DEPTH):
            for t in range(nt):
                for k in range(3):
                    _gather_copy(ins, kinds, send_sems, recv_sems, group, t, k, 2 * x + y).start()

    out = pl.pallas_call(
        body, out_shape=(pltpu.SemaphoreType.DMA((ncp,)), pltpu.SemaphoreType.DMA((ncp,)), *_thru(fulls)),
        in_specs=[HBM] * nt + [ANY], out_specs=(SEM, SEM, *([HBM] * nt)), input_output_aliases={t: 2 + t for t in range(nt)},
        name="gather_start", compiler_params=pltpu.CompilerParams(has_side_effects=EFFECT),
    )(*[_in_hbm(a) for a in fulls], after)
    return out[0], out[1], list(out[2:])


def _gather_wait(group, which, send_sems, recv_sems, fulls, kinds, after):
    nt = len(fulls)

    def body(*refs):
        ins, send_ref, recv_ref = refs[:nt], refs[nt], refs[nt + 1]
        x, y, _ = _place()
        chips = _other_chips(x, y)
        for t in which:
            for k in range(3):
                cp = _gather_copy(ins, kinds, send_ref, recv_ref, group, t, k, 2 * chips[k][0] + chips[k][1])
                cp.wait_send()
                cp.wait_recv()

    out = pl.pallas_call(
        body, out_shape=tuple(_thru(fulls)), in_specs=[HBM] * nt + [SEM, SEM, ANY], out_specs=tuple([HBM] * nt),
        input_output_aliases={t: t for t in range(nt)}, name="gather_wait_%d_%d" % (group, which[0]),
        compiler_params=pltpu.CompilerParams(has_side_effects=EFFECT),
    )(*fulls, send_sems, recv_sems, after)
    return list(out)


def _scatter_copy(grads, lands, kinds, send_sems, recv_sems, t, k):
    x, y, c = _place()
    chip = _other_chips(x, y)[k]
    return pltpu.make_async_remote_copy(
        src_ref=_slab(grads[t], kinds[t], 2 * chip[0] + chip[1]), dst_ref=lands[t].at[k], send_sem=send_sems.at[3 * t + k],
        recv_sem=recv_sems.at[3 * t + k], device_id=(chip[0], chip[1], c), device_id_type=MESH)


def _scatter_start(grads, kinds, after, name):
    nt = len(grads)
    lands = []
    for g, kd in zip(grads, kinds):
        l, r, c = g.shape
        lands.append(lax.empty((3, l, r, c // 4) if kd == "col" else (3, l, r // 4, c), g.dtype))

    def body(*refs):
        ins, lnd, send_sems, recv_sems = refs[:nt], refs[nt:2 * nt], refs[2 * nt + 1], refs[2 * nt + 2]
        for t in range(nt):
            for k in range(3):
                _scatter_copy(ins, lnd, kinds, send_sems, recv_sems, t, k).start()
        refs[-1][...] = jnp.zeros_like(refs[-1])

    out = pl.pallas_call(
        body, out_shape=(pltpu.SemaphoreType.DMA((3 * nt,)), pltpu.SemaphoreType.DMA((3 * nt,)), *_thru(grads), *_thru(lands),
                         jax.ShapeDtypeStruct((8, 128), F32)),
        in_specs=[HBM] * (2 * nt) + [ANY], out_specs=(SEM, SEM, *([HBM] * (2 * nt)), pl.BlockSpec(memory_space=pltpu.VMEM)),
        input_output_aliases={t: 2 + t for t in range(2 * nt)}, name=name,
        compiler_params=pltpu.CompilerParams(has_side_effects=EFFECT),
    )(*[_in_hbm(a) for a in grads], *[_in_hbm(a) for a in lands], after)
    return (out[0], out[1], list(out[2:2 + nt]), list(out[2 + nt:2 + 2 * nt])), out[-1]


def _scatter_wait(send_sems, recv_sems, grads, lands, kinds, after, name):
    nt = len(grads)

    def body(*refs):
        ins, lnd, send_ref, recv_ref = refs[:nt], refs[nt:2 * nt], refs[2 * nt], refs[2 * nt + 1]
        for t in range(nt):
            for k in range(3):
                cp = _scatter_copy(ins, lnd, kinds, send_ref, recv_ref, t, k)
                cp.wait_send()
                cp.wait_recv()

    out = pl.pallas_call(
        body, out_shape=(*_thru(grads), *_thru(lands)), in_specs=[HBM] * (2 * nt) + [SEM, SEM, ANY],
        out_specs=tuple([HBM] * (2 * nt)), input_output_aliases={t: t for t in range(2 * nt)}, name=name,
        compiler_params=pltpu.CompilerParams(has_side_effects=EFFECT),
    )(*grads, *lands, send_sems, recv_sems, after)
    return list(out[:nt]), list(out[nt:])


def _sibling_swap(arrs):
    nt = len(arrs)

    def body(*refs):
        ins, outs = refs[:nt], refs[nt:2 * nt]
        send_sems, recv_sems = refs[2 * nt:]
        x, y, c = _place()
        sends = [pltpu.make_async_remote_copy(src_ref=ins[t], dst_ref=outs[t], send_sem=send_sems.at[t], recv_sem=recv_sems.at[t],
                                              device_id=(x, y, 1 - c), device_id_type=MESH) for t in range(nt)]
        for cp in sends:
            cp.start()
        for cp in sends:
            cp.wait_recv()
        for cp in sends:
            cp.wait_send()

    return pl.pallas_call(
        body, out_shape=[jax.ShapeDtypeStruct(a.shape, a.dtype) for a in arrs], in_specs=[ANY] * nt, out_specs=[ANY] * nt,
        scratch_shapes=[pltpu.SemaphoreType.DMA((nt,)), pltpu.SemaphoreType.DMA((nt,))], name="sibling_swap",
    )(*arrs)


def _gather_small(vec, over_c):
    n = vec.shape[0]
    flips = [(dx, dy, dc) for dx in (0, 1) for dy in (0, 1) for dc in ((0, 1) if over_c else (0,))][1:]
    np_ = len(flips)

    def body(v_ref, o_ref, send_sems, recv_sems, local_sem):
        x, y, c = _place()

        def idx(px, py, pc):
            return 4 * px + 2 * py + pc if over_c else 2 * px + py

        def peer(f):
            return (1 - x if f[0] else x, 1 - y if f[1] else y, 1 - c if f[2] else c)

        def push(k, landing):
            return pltpu.make_async_remote_copy(src_ref=v_ref, dst_ref=o_ref.at[landing], send_sem=send_sems.at[k],
                                                recv_sem=recv_sems.at[k], device_id=peer(flips[k]), device_id_type=MESH)

        mine = pltpu.make_async_copy(v_ref, o_ref.at[idx(x, y, c)], local_sem)
        sends = [push(k, idx(x, y, c)) for k in range(np_)]
        for cp in [mine] + sends:
            cp.start()
        for k in range(np_):
            push(k, idx(*peer(flips[k]))).wait_recv()
        for cp in sends:
            cp.wait_send()
        mine.wait()

    return pl.pallas_call(
        body, out_shape=jax.ShapeDtypeStruct((np_ + 1, n, 128), F32), in_specs=[ANY], out_specs=ANY,
        scratch_shapes=[pltpu.SemaphoreType.DMA((np_,)), pltpu.SemaphoreType.DMA((np_,)), pltpu.SemaphoreType.DMA(())],
        name="gather_small_all" if over_c else "gather_small_xy",
    )(vec)


def _sum_rows(buf, after, tr=512):
    p, n, _ = buf.shape
    tr = min(tr, n)

    def body(b_ref, after_ref, o_ref):
        acc = b_ref[0]
        for k in range(1, p):
            acc = acc + b_ref[k]
        o_ref[...] = acc

    return pl.pallas_call(
        body, out_shape=jax.ShapeDtypeStruct((n, 128), F32), grid=(n // tr,),
        in_specs=[pl.BlockSpec((p, tr, 128), lambda i: (0, i, 0)), ANY], out_specs=pl.BlockSpec((tr, 128), lambda i: (i, 0)),
        name="sum_rows", compiler_params=_params(("parallel",)),
    )(buf, after)


def _sum_partials_into(stack, at, depth, grad, recv, kind, jj, name, tr=128):
    _, _, r, c = recv.shape
    tr = min(tr, r)

    def body(j_ref, g_ref, r0, r1, r2, *rest):
        rest[-1][...] = ((g_ref[...].astype(F32) + r0[...].astype(F32)) + r1[...].astype(F32)) + r2[...].astype(F32)

    if kind == "col":
        own = pl.BlockSpec((None, tr, c), lambda b, j: (0, b, j[0]))
    else:
        own = pl.BlockSpec((None, tr, c), lambda b, j: (0, j[0] * (r // tr) + b, 0))
    got = lambda k: pl.BlockSpec((None, None, tr, c), functools.partial(lambda k, b, j: (k, 0, b, 0), k))
    chained = stack is not None
    return pl.pallas_call(
        body, out_shape=jax.ShapeDtypeStruct((depth, r, c), F32),
        grid_spec=pltpu.PrefetchScalarGridSpec(
            num_scalar_prefetch=1, grid=(r // tr,), in_specs=[own, got(0), got(1), got(2)] + ([ANY] if chained else []),
            out_specs=pl.BlockSpec((None, tr, c), lambda b, j: (at, b, 0))),
        input_output_aliases={5: 0} if chained else {}, name=name, compiler_params=_params(("parallel",)),
    )(*([jj, grad, recv, recv, recv] + ([stack] if chained else [])))


WEIGHTS = ['norm_mix_w', 'w_in', 'hg_lb_raw', 'hg_norm_w', 'cv_dw_w', 'cv_dw_b', 'cv_ln_w', 'cv_ln_b', 'pl_w', 'pl_scale',
           'lru_conv_w', 'lru_conv_b', 'lru_wa', 'lru_ba', 'lru_wx', 'lru_bx', 'lru_lambda', 'gate_b', 'w_branch', 'w_out',
           'norm_mem_w', 'mem_norm_w', 'xa_wq', 'xa_wkv', 'xa_wo', 'norm_ffn_w', 'ffn_w1', 'ffn_w2', 'final_norm_w']
BIG = {'w_in': 'col', 'w_branch': 'col', 'w_out': 'row', 'xa_wq': 'row', 'xa_wkv': 'col', 'xa_wo': 'row', 'ffn_w1': 'col', 'ffn_w2': 'row'}
SMALL_SPLIT = ('gate_b', 'cv_dw_w', 'lru_conv_w')
SMALL = [n for n in WEIGHTS if n not in BIG]
PIECES = ['w_in', ('w_branch', 0), ('w_branch', 1), ('w_branch', 2), ('w_branch', 3), 'w_out', 'xa_wq', 'xa_wkv', 'xa_wo', 'ffn_w1', 'ffn_w2']
PIECE_KINDS = [BIG[k[0] if isinstance(k, tuple) else k] for k in PIECES]
ROWS_PAD = 512


def _as3d(a):
    return a.reshape((-1,) + a.shape[-2:])


def _pack(parts):
    flat = jnp.concatenate([p.reshape(-1).astype(F32) for p in parts])
    n = -(-flat.shape[0] // (128 * ROWS_PAD)) * ROWS_PAD
    return jnp.pad(flat, (0, n * 128 - flat.shape[0])).reshape(n, 128)


def _unpack(packed, shapes):
    flat, out, o = packed.reshape(-1), [], 0
    for sh in shapes:
        sz = math.prod(sh)
        out.append(flat[o:o + sz].reshape(sh))
        o += sz
    return out


def _block_diag(w):
    h, a, b = w.shape
    eye = jnp.eye(h, dtype=w.dtype)
    return (w[:, :, None, :] * eye[:, None, :, None]).reshape(h * a, h * b)


def _diag_blocks(m, h):
    a, b = m.shape[0] // h, m.shape[1] // h
    return jnp.stack([m[i * a:(i + 1) * a, i * b:(i + 1) * b] for i in range(h)])


def kernel(x, mem, norm_mix_w, w_in, hg_lb_raw, hg_norm_w, cv_dw_w, cv_dw_b, cv_ln_w, cv_ln_b, pl_w, pl_scale, lru_conv_w, lru_conv_b, lru_wa, lru_ba, lru_wx, lru_bx, lru_lambda, gate_b, w_branch, w_out, norm_mem_w, mem_norm_w, xa_wq, xa_wkv, xa_wo, norm_ffn_w, ffn_w1, ffn_w2, final_norm_w, loss_target, m_norm_mix_w, m_w_in, m_hg_lb_raw, m_hg_norm_w, m_cv_dw_w, m_cv_dw_b, m_cv_ln_w, m_cv_ln_b, m_pl_w, m_pl_scale, m_lru_conv_w, m_lru_conv_b, m_lru_wa, m_lru_ba, m_lru_wx, m_lru_bx, m_lru_lambda, m_gate_b, m_w_branch, m_w_out, m_norm_mem_w, m_mem_norm_w, m_xa_wq, m_xa_wkv, m_xa_wo, m_norm_ffn_w, m_ffn_w1, m_ffn_w2, m_final_norm_w, v_norm_mix_w, v_w_in, v_hg_lb_raw, v_hg_norm_w, v_cv_dw_w, v_cv_dw_b, v_cv_ln_w, v_cv_ln_b, v_pl_w, v_pl_scale, v_lru_conv_w, v_lru_conv_b, v_lru_wa, v_lru_ba, v_lru_wx, v_lru_bx, v_lru_lambda, v_gate_b, v_w_branch, v_w_out, v_norm_mem_w, v_mem_norm_w, v_xa_wq, v_xa_wkv, v_xa_wo, v_norm_ffn_w, v_ffn_w1, v_ffn_w2, v_final_norm_w):
    w = dict(zip(WEIGHTS, (norm_mix_w, w_in, hg_lb_raw, hg_norm_w, cv_dw_w, cv_dw_b, cv_ln_w, cv_ln_b, pl_w, pl_scale, lru_conv_w, lru_conv_b, lru_wa, lru_ba, lru_wx, lru_bx, lru_lambda, gate_b, w_branch, w_out, norm_mem_w, mem_norm_w, xa_wq, xa_wkv, xa_wo, norm_ffn_w, ffn_w1, ffn_w2, final_norm_w)))
    m1 = dict(zip(WEIGHTS, (m_norm_mix_w, m_w_in, m_hg_lb_raw, m_hg_norm_w, m_cv_dw_w, m_cv_dw_b, m_cv_ln_w, m_cv_ln_b, m_pl_w, m_pl_scale, m_lru_conv_w, m_lru_conv_b, m_lru_wa, m_lru_ba, m_lru_wx, m_lru_bx, m_lru_lambda, m_gate_b, m_w_branch, m_w_out, m_norm_mem_w, m_mem_norm_w, m_xa_wq, m_xa_wkv, m_xa_wo, m_norm_ffn_w, m_ffn_w1, m_ffn_w2, m_final_norm_w)))
    v1 = dict(zip(WEIGHTS, (v_norm_mix_w, v_w_in, v_hg_lb_raw, v_hg_norm_w, v_cv_dw_w, v_cv_dw_b, v_cv_ln_w, v_cv_ln_b, v_pl_w, v_pl_scale, v_lru_conv_w, v_lru_conv_b, v_lru_wa, v_lru_ba, v_lru_wx, v_lru_bx, v_lru_lambda, v_gate_b, v_w_branch, v_w_out, v_norm_mem_w, v_mem_norm_w, v_xa_wq, v_xa_wkv, v_xa_wo, v_norm_ffn_w, v_ffn_w1, v_ffn_w2, v_final_norm_w)))
    seq = x.shape[1]
    xs, mems, tgt = x.reshape(seq, D_MODEL), mem.reshape(-1, D_MODEL), loss_target.reshape(seq, D_MODEL)
    jj = 2 * lax.axis_index("x") + lax.axis_index("y")
    jj1 = jnp.reshape(jj, (1,)).astype(jnp.int32)

    split_shapes = [w[n].shape for n in SMALL_SPLIT]
    got = _gather_small(_pack([w[n] for n in SMALL_SPLIT]), over_c=False)
    per_chip = [_unpack(got[k], split_shapes) for k in range(4)]
    full_small = {n: jnp.concatenate([per_chip[k][i] for k in range(4)], axis=-1) for i, n in enumerate(SMALL_SPLIT)}
    big_names = list(BIG)
    kinds = [BIG[n] for n in big_names]
    g_send, g_recv, fulls = _gather_start([_cast_into_full(_as3d(w[n]), BIG[n], jj1, "cast_" + n) for n in big_names], kinds, got)
    tix = {n: t for t, n in enumerate(big_names)}

    lb = _lb_fwd(hg_lb_raw)
    row = lambda a: a.reshape(1, -1)

    def layer_params(l):
        return dict(
            nmix=row(norm_mix_w[l]), lb=row(lb[l]), hgnw=row(hg_norm_w[l]),
            cw=jnp.pad(full_small['cv_dw_w'][l], ((0, 32 - CV_K), (0, 0))), cb=row(cv_dw_b[l]), lnw=row(cv_ln_w[l]), lnb=row(cv_ln_b[l]),
            plw=pl_w[l], plsc=row(pl_scale[l]),
            lcw=jnp.pad(full_small['lru_conv_w'][l], ((0, 8 - LRU_CONV), (0, 0))), lcb=row(lru_conv_b[l]),
            wa=_block_diag(lru_wa[l]).astype(BF16), ba=row(lru_ba[l]), wx=_block_diag(lru_wx[l]).astype(BF16), bx=row(lru_bx[l]),
            lam=row(lru_lambda[l]), gb=full_small['gate_b'][l], nmem=row(norm_mem_w[l]), memw=row(mem_norm_w[l]), nffn=row(norm_ffn_w[l]))

    saved = []
    xc = xs
    def arrived(l, names, after):
        got_ = _gather_wait(l, [tix[n] for n in names], g_send, g_recv, fulls, kinds, after)
        wf_ = dict(zip(big_names, got_))
        wf_['w_branch'] = wf_['w_branch'].reshape(DEPTH, 4, 512, D_MODEL)
        return got_, wf_

    for l in range(DEPTH):
        first = l == 0
        fulls, wf = arrived(l, ['w_in'] if first else big_names, xc)
        p = layer_params(l)
        h = _norm_fwd(xc, p['nmix'], "norm_mix")
        proj = _mm(h, wf['w_in'], "nn", F32, "proj", layer=l)
        b_hg, st = _hgrn_fwd(proj, p['lb'], p['hgnw'])
        b_cv = _conv_fwd(proj, p['cw'], p['cb'], p['lnw'], p['lnb'])
        b_pl = _pool_fwd(proj, p['plw'], p['plsc'])
        b_lru, hs = _lru_fwd(proj, p['lcw'], p['lcb'], p['wa'], p['ba'], p['wx'], p['bx'], p['lam'])
        branches = (b_hg, b_cv, b_pl, b_lru)
        if first:
            fulls, wf = arrived(l, ['w_branch', 'w_out'], b_lru)
        x1 = _merge_fwd(xc, branches, proj, p['gb'], wf['w_branch'], wf['w_out'], l)
        if first:
            fulls, wf = arrived(l, ['xa_wq', 'xa_wkv', 'xa_wo'], x1)
        memn = _norm_fwd(mems, p['memw'], "norm_memtok")
        kv = _mm(memn, wf['xa_wkv'], "nn", BF16, "kv_proj", layer=l)
        x2 = _attn_fwd(x1, p['nmem'], wf['xa_wq'], kv, wf['xa_wo'], l)
        if first:
            fulls, wf = arrived(l, ['ffn_w1', 'ffn_w2'], x2)
        x3 = _ffn_fwd(x2, p['nffn'], wf['ffn_w1'], wf['ffn_w2'], l)
        saved.append(dict(p=p, x=xc, h=h, proj=proj, st=st, hs=hs, branches=branches, x1=x1, memn=memn, kv=kv, x2=x2))
        xc = x3

    loss_blk, dx, dfinal = _final_loss(xc, row(final_norm_w), tgt)

    gs = {n: [None] * DEPTH for n in SMALL if n != 'final_norm_w'}
    dlb = [None] * DEPTH
    in_flight = [None] * DEPTH
    token = loss_blk
    for l in reversed(range(DEPTH)):
        sv = saved[l]
        p = sv['p']
        gb = {}
        dx2, gs['norm_ffn_w'][l], h3, da, r, dxb = _ffn_bwd(sv['x2'], dx, p['nffn'], wf['ffn_w1'], wf['ffn_w2'], l, token)
        gb['ffn_w1'] = _mm(h3, da, "tn", BF16, "dw_ffn1")
        gb['ffn_w2'] = _mm(r, dxb, "tn", BF16, "dw_ffn2")
        dx1, gs['norm_mem_w'][l], h2, o, dq, dxb2, dk, dv = _attn_bwd(sv['x1'], dx2, p['nmem'], wf['xa_wq'], sv['kv'], wf['xa_wo'], l)
        gb['xa_wq'] = _mm(h2, dq, "tn", BF16, "dw_q")
        gb['xa_wo'] = _mm(o, dxb2, "tn", BF16, "dw_o")
        dkv = jnp.concatenate([dk, dv], axis=1)
        gb['xa_wkv'] = _mm(sv['memn'], dkv, "tn", BF16, "dw_kv")
        dmemn = _mm(dkv, wf['xa_wkv'], "nt", F32, "dmemn", layer=l)
        _, gs['mem_norm_w'][l] = _norm_bwd(mems, p['memw'], dmemn, None, "norm_memtok_bwd")
        db0, db1, db2, db3, dgp, dup, mg, dxb1, gs['gate_b'][l] = _merge_bwd(
            dx1, sv['branches'], sv['proj'], p['gb'], wf['w_branch'], wf['w_out'], l)
        gb['w_out'] = _mm(mg, dxb1, "tn", BF16, "dw_out")
        for kb in range(4):
            gb['w_branch', kb] = _mm(sv['branches'][kb], dup, "tn", BF16, "dw_branch", b_col0=kb * D_MODEL, n=D_MODEL, tn=512)
        dhg, dlb[l], gs['hg_norm_w'][l] = _hgrn_bwd(sv['proj'], db0, sv['st'], p['lb'], p['hgnw'])
        dcv, dcw, gs['cv_dw_b'][l], gs['cv_ln_w'][l], gs['cv_ln_b'][l] = _conv_bwd(sv['proj'], db1, p['cw'], p['cb'], p['lnw'], p['lnb'])
        gs['cv_dw_w'][l] = dcw[:CV_K]
        dpl, gs['pl_w'][l], gs['pl_scale'][l] = _pool_bwd(sv['proj'], db2, p['plw'], p['plsc'])
        dlru, dlcw, gs['lru_conv_b'][l], dwa, gs['lru_ba'][l], dwx, gs['lru_bx'][l], gs['lru_lambda'][l] = _lru_bwd(
            sv['proj'], sv['hs'], db3, p['lcw'], p['lcb'], p['wa'], p['ba'], p['wx'], p['bx'], p['lam'])
        gs['lru_conv_w'][l] = dlcw[:LRU_CONV]
        gs['lru_wa'][l], gs['lru_wx'][l] = _diag_blocks(dwa, LRU_HEADS), _diag_blocks(dwx, LRU_HEADS)
        dproj = jnp.concatenate([dhg, dcv, dpl, dlru, dgp], axis=1)
        gb['w_in'] = _mm(sv['h'], dproj, "tn", BF16, "dw_in")
        dh = _mm(dproj, wf['w_in'], "nt", F32, "dh_mix", tm=256, tn=256, layer=l)
        dx, gs['norm_mix_w'][l] = _norm_bwd(sv['x'], p['nmix'], dh, dx1, "norm_mix_bwd")
        if l:
            in_flight[l], token = _scatter_start([gb[key][None] for key in PIECES], PIECE_KINDS, dx, "scatter_start_%d" % l)
    grad_x = dx.reshape(x.shape)
    gs['hg_lb_raw'] = _lb_bwd(hg_lb_raw, jnp.concatenate(dlb, axis=0))

    def full_shape(n):
        return full_small[n].shape if n in SMALL_SPLIT else w[n].shape

    small_full = []
    for n in SMALL:
        g = gs[n] if n == 'hg_lb_raw' else dfinal if n == 'final_norm_w' else jnp.stack(gs[n])
        small_full.append(g.reshape(full_shape(n)))
    everyone = _gather_small(_pack(small_full + [loss_blk[0:1, 0:1]]), over_c=True)
    in_flight[0], token = _scatter_start([gb[key][None] for key in PIECES], PIECE_KINDS, everyone, "scatter_start_0")
    total = _sum_rows(everyone, token)
    parts = _unpack(total, [full_shape(n) for n in SMALL] + [(1,)])
    loss = parts[-1].reshape(())
    g_small = {}
    for n, g in zip(SMALL, parts[:-1]):
        if n in SMALL_SPLIT:
            width = w[n].shape[-1]
            g = lax.dynamic_slice_in_dim(g, jj * width, width, axis=g.ndim - 1)
        g_small[n] = g
    shapes = [w[n].shape for n in SMALL]
    upd = _adamw(_pack([w[n] for n in SMALL]), [_pack([g_small[n] for n in SMALL])], _pack([m1[n] for n in SMALL]),
                 _pack([v1[n] for n in SMALL]), "adamw_small")
    d_small, m_small, v_small = [dict(zip(SMALL, _unpack(u, shapes))) for u in upd]

    stacks = {n: None for n in big_names}
    for l in reversed(range(DEPTH)):
        s_send, s_recv, g_thru, lands = in_flight[l]
        done_before = upd[0] if stacks['ffn_w2'] is None else stacks['ffn_w2']
        g_thru, lands = _scatter_wait(s_send, s_recv, g_thru, lands, PIECE_KINDS, done_before, "scatter_wait_%d" % l)
        for key, g, r in zip(PIECES, g_thru, lands):
            n, kb = key if isinstance(key, tuple) else (key, None)
            per = 1 if kb is None else 4
            stacks[n] = _sum_partials_into(stacks[n], l * per + (kb or 0), DEPTH * per, g, r, BIG[n], jj1, "sum_" + n)
    partial = [stacks[n] for n in big_names]
    theirs = _sibling_swap(partial)
    g_big, d_big, m_big, v_big = {}, {}, {}, {}
    for n, pa, pb in zip(big_names, partial, theirs):
        c2 = lambda a: a.reshape(-1, a.shape[-1])
        out = _adamw(c2(w[n]), [c2(pa), c2(pb)], c2(m1[n]), c2(v1[n]), "adamw_" + n)
        g_big[n], d_big[n], m_big[n], v_big[n] = [o.reshape(w[n].shape) for o in out]

    pick = lambda small, big: [big[n] if n in BIG else small[n] for n in WEIGHTS]
    return (loss, grad_x, *pick(g_small, g_big), *pick(d_small, d_big), *pick(m_small, m_big), *pick(v_small, v_big))
```

```python
import functools
import math

import jax
import jax.numpy as jnp
from jax import lax
from jax.experimental import pallas as pl
from jax.experimental.pallas import tpu as pltpu

F32 = jnp.float32
BF16 = jnp.bfloat16
MESH = pl.DeviceIdType.MESH
ANY = pl.BlockSpec(memory_space=pl.ANY)

D_MODEL = 1024
DEPTH = 4
CHUNK = 64
SUB = 16
EPS = 1e-6
HG_HEADS, HG_D = 4, 128
CV_W, CV_K = 512, 31
CV_HALO = 32
POOL_WINDOWS = (2, 4, 8, 16)
POOL_HALO = 16
LRU_W, LRU_HEADS, LRU_HD, LRU_CONV = 512, 8, 64, 4
LRU_HALO = 8
LRU_C = 8.0
MIX_W = 4608
IN_W = 8704
XA_HEADS, XA_HD = 4, 256
D_FF = 4096
FF_CHUNK = 1024
ADAM_LR, ADAM_B1, ADAM_B2, ADAM_EPS, ADAM_WD, ADAM_STEP = 0.001, 0.9, 0.999, 1e-08, 0.01, 10
VMEM_LIMIT = 56 * 1024 * 1024
EXP_CLAMP = 80.0
HI = lax.Precision.HIGHEST


def _params(sem=None):
    return pltpu.CompilerParams(dimension_semantics=sem, vmem_limit_bytes=VMEM_LIMIT)


def _sigmoid(x):
    return 1.0 / (1.0 + jnp.exp(-x))


def _dsilu(x, s):
    return s * (1.0 + x * (1.0 - s))


_GELU_C = math.sqrt(2.0 / math.pi)


def _gelu_parts(x):
    t = jnp.tanh(_GELU_C * (x + 0.044715 * x * x * x))
    g = 0.5 * x * (1.0 + t)
    dg = 0.5 * (1.0 + t) + 0.5 * x * (1.0 - t * t) * _GELU_C * (1.0 + 3 * 0.044715 * x * x)
    return g, dg


def _dot(a, b, dims, precision=None):
    return lax.dot_general(a, b, (dims, ((), ())), precision=precision, preferred_element_type=F32)


def _nn(a, b, **k):
    return _dot(a, b, ((1,), (0,)), **k)


def _nt(a, b, **k):
    return _dot(a, b, ((1,), (1,)), **k)


def _tn(a, b, **k):
    return _dot(a, b, ((0,), (0,)), **k)


def _split(x):
    hi = x.astype(BF16)
    return hi, (x - hi.astype(F32)).astype(BF16)


def _nn3(a, b):
    (ah, al), (bh, bl) = _split(a), _split(b)
    return _nn(jnp.concatenate([ah, ah, al], axis=1), jnp.concatenate([bh, bl, bh], axis=0))


def _tn3(a, b):
    (ah, al), (bh, bl) = _split(a), _split(b)
    return _tn(jnp.concatenate([ah, ah, al], axis=0), jnp.concatenate([bh, bl, bh], axis=0))


def _rms_fwd(x, w):
    r = lax.rsqrt(jnp.mean(x * x, axis=-1, keepdims=True) + EPS)
    return x * r * w, r


def _rms_bwd(x, r, w, dy):
    xr = x * r
    g = dy * w
    dx = r * (g - xr * jnp.mean(g * xr, axis=-1, keepdims=True))
    return dx, jnp.sum(dy * xr, axis=0, keepdims=True)


def _lw(shape, index, layer):
    return pl.BlockSpec((None,) + tuple(shape), lambda *g: (layer,) + tuple(index(*g)))


def _mm(a, b, mode, out_dtype, name, tm=512, tn=512, b_col0=0, n=None, layer=None):
    bs = b.shape if layer is None else b.shape[1:]
    if mode == "nn":
        m, k = a.shape
        n = bs[1] if n is None else n
    elif mode == "nt":
        m, k = a.shape
        n = bs[0] if n is None else n
    else:
        k, m = a.shape
        n = bs[1] if n is None else n
    tm, tn = min(tm, m), min(tn, n)
    assert m % tm == 0 and n % tn == 0 and b_col0 % tn == 0
    off = b_col0 // tn

    def body(a_ref, b_ref, o_ref):
        av, bv = a_ref[...].astype(BF16), b_ref[...].astype(BF16)
        o_ref[...] = (_nn if mode == "nn" else _nt if mode == "nt" else _tn)(av, bv).astype(out_dtype)

    def bspec(shape, index):
        return pl.BlockSpec(shape, index) if layer is None else _lw(shape, index, layer)

    if mode == "tn":
        grid = (m // tm, n // tn)
        a_spec = pl.BlockSpec((k, tm), lambda i, j: (0, i))
        b_spec = bspec((k, tn), lambda i, j: (0, j + off))
        o_spec = pl.BlockSpec((tm, tn), lambda i, j: (i, j))
    else:
        grid = (n // tn, m // tm)
        a_spec = pl.BlockSpec((tm, k), lambda j, i: (i, 0))
        if mode == "nn":
            b_spec = bspec((k, tn), lambda j, i: (0, j + off))
        else:
            b_spec = bspec((tn, k), lambda j, i: (j + off, 0))
        o_spec = pl.BlockSpec((tm, tn), lambda j, i: (i, j))
    return pl.pallas_call(
        body, out_shape=jax.ShapeDtypeStruct((m, n), out_dtype), grid=grid,
        in_specs=[a_spec, b_spec], out_specs=o_spec, name=name,
        compiler_params=_params(("parallel", "parallel")),
    )(a, b)


def _norm_fwd(x, w, name, ts=512):
    s, d = x.shape
    ts = min(ts, s)

    def body(x_ref, w_ref, o_ref):
        o_ref[...] = _rms_fwd(x_ref[...], w_ref[...])[0].astype(BF16)

    return pl.pallas_call(
        body, out_shape=jax.ShapeDtypeStruct((s, d), BF16), grid=(s // ts,),
        in_specs=[pl.BlockSpec((ts, d), lambda i: (i, 0)), pl.BlockSpec((1, d), lambda i: (0, 0))],
        out_specs=pl.BlockSpec((ts, d), lambda i: (i, 0)), name=name, compiler_params=_params(("parallel",)),
    )(x, w)


def _norm_bwd(x, w, dy, dres, name, ts=512):
    s, d = x.shape
    ts = min(ts, s)
    with_res = dres is not None

    def body(*refs):
        if with_res:
            x_ref, w_ref, dy_ref, dres_ref, dx_ref, dw_ref = refs
        else:
            x_ref, w_ref, dy_ref, dx_ref, dw_ref = refs
        xv = x_ref[...]
        r = lax.rsqrt(jnp.mean(xv * xv, axis=-1, keepdims=True) + EPS)
        dx, dw = _rms_bwd(xv, r, w_ref[...], dy_ref[...])
        dx_ref[...] = dx + dres_ref[...] if with_res else dx

        @pl.when(pl.program_id(0) == 0)
        def _():
            dw_ref[...] = jnp.zeros_like(dw_ref)

        dw_ref[...] += dw

    row = pl.BlockSpec((ts, d), lambda i: (i, 0))
    vec = pl.BlockSpec((1, d), lambda i: (0, 0))
    return pl.pallas_call(
        body, out_shape=(jax.ShapeDtypeStruct((s, d), F32), jax.ShapeDtypeStruct((1, d), F32)), grid=(s // ts,),
        in_specs=[row, vec, row] + ([row] if with_res else []), out_specs=(row, vec), name=name,
        compiler_params=_params(("arbitrary",)),
    )(*([x, w, dy] + ([dres] if with_res else [])))


def _ffn_fwd(x, nw, w1, w2, layer, ts=256):
    s, d = x.shape
    ts = min(ts, s)
    nj = D_FF // FF_CHUNK

    def body(x_ref, nw_ref, w1_ref, w2_ref, o_ref, h_scr, acc):
        j = pl.program_id(1)

        @pl.when(j == 0)
        def _():
            h_scr[...] = _rms_fwd(x_ref[...], nw_ref[...])[0].astype(BF16)
            acc[...] = jnp.zeros_like(acc)

        a = _nn(h_scr[...], w1_ref[...])
        rl = jnp.maximum(a, 0.0)
        acc[...] += _nn((rl * rl).astype(BF16), w2_ref[...])

        @pl.when(j == nj - 1)
        def _():
            o_ref[...] = x_ref[...] + acc[...]

    row = pl.BlockSpec((ts, d), lambda i, j: (i, 0))
    return pl.pallas_call(
        body, out_shape=jax.ShapeDtypeStruct((s, d), F32), grid=(s // ts, nj),
        in_specs=[row, pl.BlockSpec((1, d), lambda i, j: (0, 0)),
                  _lw((d, FF_CHUNK), lambda i, j: (0, j), layer), _lw((FF_CHUNK, d), lambda i, j: (j, 0), layer)],
        out_specs=row, scratch_shapes=[pltpu.VMEM((ts, d), BF16), pltpu.VMEM((ts, d), F32)], name="ffn_fwd",
        compiler_params=_params(("parallel", "arbitrary")),
    )(x, nw, w1, w2)


def _ffn_bwd(x, dxo, nw, w1, w2, layer, after, ts=256):
    s, d = x.shape
    ts = min(ts, s)
    nj = D_FF // FF_CHUNK

    def body(x_ref, dxo_ref, nw_ref, w1_ref, w2_ref, after_ref, dx_ref, dnw_ref, h_ref, da_ref, r_ref, dxb_ref, dh):
        i, j = pl.program_id(0), pl.program_id(1)

        @pl.when(j == 0)
        def _():
            h_ref[...] = _rms_fwd(x_ref[...], nw_ref[...])[0].astype(BF16)
            dxb_ref[...] = dxo_ref[...].astype(BF16)
            dh[...] = jnp.zeros_like(dh)

        a = _nn(h_ref[...], w1_ref[...])
        rl = jnp.maximum(a, 0.0)
        r_ref[...] = (rl * rl).astype(BF16)
        da = (_nt(dxb_ref[...], w2_ref[...]) * (2.0 * rl)).astype(BF16)
        da_ref[...] = da
        dh[...] += _nt(da, w1_ref[...])

        @pl.when(jnp.logical_and(i == 0, j == 0))
        def _():
            dnw_ref[...] = jnp.zeros_like(dnw_ref)

        @pl.when(j == nj - 1)
        def _():
            xv = x_ref[...]
            r = lax.rsqrt(jnp.mean(xv * xv, axis=-1, keepdims=True) + EPS)
            dx, dw = _rms_bwd(xv, r, nw_ref[...], dh[...])
            dx_ref[...] = dxo_ref[...] + dx
            dnw_ref[...] += dw

    row = pl.BlockSpec((ts, d), lambda i, j: (i, 0))
    vec = pl.BlockSpec((1, d), lambda i, j: (0, 0))
    ffc = pl.BlockSpec((ts, FF_CHUNK), lambda i, j: (i, j))
    return pl.pallas_call(
        body,
        out_shape=(jax.ShapeDtypeStruct((s, d), F32), jax.ShapeDtypeStruct((1, d), F32), jax.ShapeDtypeStruct((s, d), BF16),
                   jax.ShapeDtypeStruct((s, D_FF), BF16), jax.ShapeDtypeStruct((s, D_FF), BF16), jax.ShapeDtypeStruct((s, d), BF16)),
        grid=(s // ts, nj),
        in_specs=[row, row, vec, _lw((d, FF_CHUNK), lambda i, j: (0, j), layer), _lw((FF_CHUNK, d), lambda i, j: (j, 0), layer), ANY],
        out_specs=(row, vec, row, ffc, ffc, row), scratch_shapes=[pltpu.VMEM((ts, d), F32)], name="ffn_bwd",
        compiler_params=_params(("arbitrary", "arbitrary")),
    )(x, dxo, nw, w1, w2, after)


def _attn_probs(q, k_ref):
    ps = []
    for hd in range(XA_HEADS):
        c = slice(hd * XA_HD, (hd + 1) * XA_HD)
        sc = _nt(q[:, c].astype(BF16), k_ref[:, c]) * (XA_HD ** -0.5)
        e = jnp.exp(sc - jnp.max(sc, axis=-1, keepdims=True))
        ps.append(e / jnp.sum(e, axis=-1, keepdims=True))
    return ps


def _attn_fwd(x, nw, wq, kv, wo, layer, ts=256):
    s, d = x.shape
    ts = min(ts, s)
    nm = kv.shape[0]

    def body(x_ref, nw_ref, wq_ref, k_ref, v_ref, wo_ref, o_ref):
        xv = x_ref[...]
        h = _rms_fwd(xv, nw_ref[...])[0].astype(BF16)
        q = _nn(h, wq_ref[...])
        ps = _attn_probs(q, k_ref)
        o = jnp.concatenate([_nn(ps[hd].astype(BF16), v_ref[:, hd * XA_HD:(hd + 1) * XA_HD]) for hd in range(XA_HEADS)], axis=1)
        o_ref[...] = xv + _nn(o.astype(BF16), wo_ref[...])

    row = pl.BlockSpec((ts, d), lambda i: (i, 0))
    full = lambda r, c: pl.BlockSpec((r, c), lambda i: (0, 0))
    wsp = _lw((d, d), lambda i: (0, 0), layer)
    return pl.pallas_call(
        body, out_shape=jax.ShapeDtypeStruct((s, d), F32), grid=(s // ts,),
        in_specs=[row, full(1, d), wsp, full(nm, d), pl.BlockSpec((nm, d), lambda i: (0, 1)), wsp], out_specs=row, name="attn_fwd",
        compiler_params=_params(("parallel",)),
    )(x, nw, wq, kv, kv, wo)


def _attn_bwd(x, dxo, nw, wq, kv, wo, layer, ts=256):
    s, d = x.shape
    ts = min(ts, s)
    nm = kv.shape[0]

    def body(x_ref, dxo_ref, nw_ref, wq_ref, k_ref, v_ref, wo_ref,
             dx_ref, dnw_ref, h_ref, o_ref, dq_ref, dxb_ref, dk_ref, dv_ref):
        xv = x_ref[...]
        hf, r = _rms_fwd(xv, nw_ref[...])
        h = hf.astype(BF16)
        h_ref[...] = h
        q = _nn(h, wq_ref[...])
        qb = q.astype(BF16)
        ps = _attn_probs(q, k_ref)
        dxb = dxo_ref[...].astype(BF16)
        dxb_ref[...] = dxb
        do = _nt(dxb, wo_ref[...])

        @pl.when(pl.program_id(0) == 0)
        def _():
            dnw_ref[...] = jnp.zeros_like(dnw_ref)
            dk_ref[...] = jnp.zeros_like(dk_ref)
            dv_ref[...] = jnp.zeros_like(dv_ref)

        dqs = []
        for hd in range(XA_HEADS):
            c = slice(hd * XA_HD, (hd + 1) * XA_HD)
            p = ps[hd]
            pb = p.astype(BF16)
            dob = do[:, c].astype(BF16)
            o_ref[:, c] = _nn(pb, v_ref[:, c]).astype(BF16)
            dp = _nt(dob, v_ref[:, c])
            ds = (p * (dp - jnp.sum(p * dp, axis=-1, keepdims=True)) * (XA_HD ** -0.5)).astype(BF16)
            dqs.append(_nn(ds, k_ref[:, c]))
            dk_ref[:, c] += _tn(ds, qb[:, c])
            dv_ref[:, c] += _tn(pb, dob)
        dq = jnp.concatenate(dqs, axis=1).astype(BF16)
        dq_ref[...] = dq
        dx, dw = _rms_bwd(xv, r, nw_ref[...], _nt(dq, wq_ref[...]))
        dx_ref[...] = dxo_ref[...] + dx
        dnw_ref[...] += dw

    row = pl.BlockSpec((ts, d), lambda i: (i, 0))
    full = lambda r, c: pl.BlockSpec((r, c), lambda i: (0, 0))
    sd = lambda dt: jax.ShapeDtypeStruct((s, d), dt)
    return pl.pallas_call(
        body,
        out_shape=(sd(F32), jax.ShapeDtypeStruct((1, d), F32), sd(BF16), sd(BF16), sd(BF16), sd(BF16),
                   jax.ShapeDtypeStruct((nm, d), F32), jax.ShapeDtypeStruct((nm, d), F32)),
        grid=(s // ts,),
        in_specs=[row, row, full(1, d), _lw((d, d), lambda i: (0, 0), layer), full(nm, d), pl.BlockSpec((nm, d), lambda i: (0, 1)),
                  _lw((d, d), lambda i: (0, 0), layer)],
        out_specs=(row, full(1, d), row, row, row, row, full(nm, d), full(nm, d)), name="attn_bwd",
        compiler_params=_params(("arbitrary",)),
    )(x, dxo, nw, wq, kv, kv, wo)


GATE_BLK0 = MIX_W // 512


def _merge_specs(ts, layer):
    row = pl.BlockSpec((ts, D_MODEL), lambda i: (i, 0))
    br = pl.BlockSpec((ts, 512), lambda i: (i, 0))
    gates = [pl.BlockSpec((ts, 512), functools.partial(lambda n, i: (i, GATE_BLK0 + n), n)) for n in range(8)]
    full = lambda *shape: pl.BlockSpec(shape, lambda i: (0,) * len(shape))
    weights = [full(4, D_MODEL), _lw((4, 512, D_MODEL), lambda i: (0, 0, 0), layer), _lw((D_MODEL, D_MODEL), lambda i: (0, 0), layer)]
    return row, br, gates, full, weights


def _merge_gates(gp_refs, gb_ref, kb):
    gp = jnp.concatenate([gp_refs[2 * kb][...], gp_refs[2 * kb + 1][...]], axis=1)
    return _sigmoid(gp + gb_ref[kb:kb + 1, :])


def _merge_fwd(x, branches, proj, gate_b, wb, wout, layer, ts=256):
    s, d = x.shape
    ts = min(ts, s)

    def body(x_ref, b0, b1, b2, b3, g0, g1, g2, g3, g4, g5, g6, g7, gb_ref, wb_ref, wo_ref, o_ref):
        brs, gps = (b0, b1, b2, b3), (g0, g1, g2, g3, g4, g5, g6, g7)
        merged = jnp.zeros((ts, d), F32)
        for kb in range(4):
            merged += _merge_gates(gps, gb_ref, kb) * _nn(brs[kb][...], wb_ref[kb])
        o_ref[...] = x_ref[...] + _nn(merged.astype(BF16), wo_ref[...])

    row, br, gates, full, weights = _merge_specs(ts, layer)
    return pl.pallas_call(
        body, out_shape=jax.ShapeDtypeStruct((s, d), F32), grid=(s // ts,),
        in_specs=[row, br, br, br, br] + gates + weights, out_specs=row, name="merge_fwd",
        compiler_params=_params(("parallel",)),
    )(x, *branches, *([proj] * 8), gate_b, wb, wout)


def _merge_bwd(dxo, branches, proj, gate_b, wb, wout, layer, ts=256):
    s, d = dxo.shape
    ts = min(ts, s)

    def body(dxo_ref, b0, b1, b2, b3, g0, g1, g2, g3, g4, g5, g6, g7, gb_ref, wb_ref, wo_ref,
             db0, db1, db2, db3, dgp_ref, dup_ref, mg_ref, dxb_ref, dgb_ref):
        brs, gps, dbs = (b0, b1, b2, b3), (g0, g1, g2, g3, g4, g5, g6, g7), (db0, db1, db2, db3)
        dxb = dxo_ref[...].astype(BF16)
        dxb_ref[...] = dxb
        dm = _nt(dxb, wo_ref[...])

        @pl.when(pl.program_id(0) == 0)
        def _():
            dgb_ref[...] = jnp.zeros_like(dgb_ref)

        merged = jnp.zeros((ts, d), F32)
        for kb in range(4):
            c = slice(kb * d, (kb + 1) * d)
            g = _merge_gates(gps, gb_ref, kb)
            up = _nn(brs[kb][...], wb_ref[kb])
            merged += g * up
            dup = (dm * g).astype(BF16)
            dup_ref[:, c] = dup
            dgp = dm * up * g * (1.0 - g)
            dgp_ref[:, c] = dgp.astype(BF16)
            dgb_ref[kb:kb + 1, :] += jnp.sum(dgp, axis=0, keepdims=True)
            dbs[kb][...] = _nt(dup, wb_ref[kb])
        mg_ref[...] = merged.astype(BF16)

    row, br, gates, full, weights = _merge_specs(ts, layer)
    wide = pl.BlockSpec((ts, 4 * d), lambda i: (i, 0))
    sb = jax.ShapeDtypeStruct((s, 512), F32)
    return pl.pallas_call(
        body,
        out_shape=(sb, sb, sb, sb, jax.ShapeDtypeStruct((s, 4 * d), BF16), jax.ShapeDtypeStruct((s, 4 * d), BF16),
                   jax.ShapeDtypeStruct((s, d), BF16), jax.ShapeDtypeStruct((s, d), BF16), jax.ShapeDtypeStruct((4, d), F32)),
        grid=(s // ts,),
        in_specs=[row, br, br, br, br] + gates + weights,
        out_specs=(br, br, br, br, wide, wide, row, row, full(4, d)), name="merge_bwd",
        compiler_params=_params(("arbitrary",)),
    )(dxo, *branches, *([proj] * 8), gate_b, wb, wout)


def _tri(n, upper=False):
    r = lax.broadcasted_iota(jnp.int32, (n, 3 * n), 0)
    c = lax.broadcasted_iota(jnp.int32, (n, 3 * n), 1) % n
    return jnp.where((c >= r) if upper else (c <= r), 1.0, 0.0).astype(BF16)


def _cum(tri3, x):
    hi = x.astype(BF16)
    r1 = x - hi.astype(F32)
    mid = r1.astype(BF16)
    lo = (r1 - mid.astype(F32)).astype(BF16)
    return _nn(tri3, jnp.concatenate([hi, mid, lo], axis=0))


def _hg_gates(hq, hf, lb):
    sg = _sigmoid(hf)
    fg = lb + (1.0 - lb) * sg
    sq = _sigmoid(hq)
    return sg, fg, 1.0 - fg, jnp.log(fg), hq * sq, sq


NSUB = CHUNK // SUB


def _hg_intra(qf, kk, b):
    row = lax.broadcasted_iota(jnp.int32, (CHUNK, 1), 0)
    refs = [b[i * SUB - 1:i * SUB, :] if i else jnp.zeros((1, b.shape[1]), F32) for i in range(NSUB)]
    mine = [jnp.logical_and(row >= i * SUB, row < (i + 1) * SUB) for i in range(NSUB)]
    ref_rows = refs[0]
    for i in range(1, NSUB):
        ref_rows = jnp.where(mine[i], refs[i], ref_rows)
    eq = jnp.exp(b - ref_rows)
    qt = qf * eq
    ek = jnp.concatenate([jnp.exp(jnp.minimum(r - b, EXP_CLAMP)) for r in refs], axis=1)
    kbig = jnp.concatenate([kk] * NSUB, axis=1) * ek
    qbig = jnp.concatenate([jnp.where(m, qt, 0.0) for m in mine], axis=1)
    return qt, qbig, kbig, eq, ek, mine


def _causal(n, upper=False):
    r, c = lax.broadcasted_iota(jnp.int32, (n, n), 0), lax.broadcasted_iota(jnp.int32, (n, n), 1)
    return (c >= r) if upper else (c <= r)


def _hg_chunk_fwd(qf, kk, b, v, st):
    parts = _hg_intra(qf, kk, b)
    att = jnp.where(_causal(CHUNK), _nt(parts[1].astype(BF16), parts[2].astype(BF16)), 0.0)
    qh = qf * jnp.exp(b)
    o = _nn(att.astype(BF16), v.astype(BF16)) + _nt(qh.astype(BF16), st.astype(BF16))
    bl = b[CHUNK - 1:CHUNK, :]
    kh = kk * jnp.exp(bl - b)
    return o, parts, att, qh, kh, jnp.exp(bl)


def _hgrn_fwd(proj, lb, nw, ts=256):
    s = proj.shape[0]
    ts = min(ts, s)
    nch = ts // CHUNK

    def body(q_ref, f_ref, v_ref, g_ref, lb_ref, nw_ref, o_ref, st_ref, st):
        @pl.when(pl.program_id(0) == 0)
        def _():
            st[...] = jnp.zeros_like(st)

        tri = _tri(CHUNK)

        def chunk(c, carry):
            rows = pl.ds(pl.multiple_of(c * CHUNK, CHUNK), CHUNK)
            _, _, kk, lf, qf, _ = _hg_gates(q_ref[rows, :], f_ref[rows, :], lb_ref[...])
            b = _cum(tri, lf)
            hv, hg = v_ref[rows, :], g_ref[rows, :]
            st_ref[c] = st[...]
            for h in range(HG_HEADS):
                cs = slice(h * HG_D, (h + 1) * HG_D)
                o, _, _, _, kh, ebl = _hg_chunk_fwd(qf[:, cs], kk[:, cs], b[:, cs], hv[:, cs], st[h])
                st[h] = st[h] * ebl + _tn(hv[:, cs].astype(BF16), kh.astype(BF16))
                on = _rms_fwd(o, nw_ref[...])[0]
                gh = hg[:, cs]
                o_ref[rows, cs] = (on * gh * _sigmoid(gh)).astype(BF16)
            return carry

        lax.fori_loop(0, nch, chunk, 0, unroll=4)

    col = lambda n: pl.BlockSpec((ts, 512), functools.partial(lambda n, i: (i, n), n))
    return pl.pallas_call(
        body,
        out_shape=(jax.ShapeDtypeStruct((s, 512), BF16), jax.ShapeDtypeStruct((s // CHUNK, HG_HEADS, HG_D, HG_D), F32)),
        grid=(s // ts,),
        in_specs=[col(0), col(1), col(2), col(3), pl.BlockSpec((1, 512), lambda i: (0, 0)), pl.BlockSpec((1, HG_D), lambda i: (0, 0))],
        out_specs=(pl.BlockSpec((ts, 512), lambda i: (i, 0)), pl.BlockSpec((nch, HG_HEADS, HG_D, HG_D), lambda i: (i, 0, 0, 0))),
        scratch_shapes=[pltpu.VMEM((HG_HEADS, HG_D, HG_D), F32)], name="hgrn_fwd",
        compiler_params=_params(("arbitrary",)),
    )(proj, proj, proj, proj, lb, nw)


def _hgrn_bwd(proj, dout, states, lb, nw, ts=256):
    s = proj.shape[0]
    ts = min(ts, s)
    nch = ts // CHUNK
    nt = s // ts

    def body(q_ref, f_ref, v_ref, g_ref, do_ref, st_ref, lb_ref, nw_ref, dp_ref, dlb_ref, dnw_ref, dst):
        @pl.when(pl.program_id(0) == 0)
        def _():
            dst[...] = jnp.zeros_like(dst)
            dlb_ref[...] = jnp.zeros_like(dlb_ref)
            dnw_ref[...] = jnp.zeros_like(dnw_ref)

        tri, triu = _tri(CHUNK), _tri(CHUNK, upper=True)
        last = lax.broadcasted_iota(jnp.int32, (CHUNK, HG_D), 0) == CHUNK - 1
        nwv = nw_ref[...]

        def chunk(cc, carry):
            c = nch - 1 - cc
            rows = pl.ds(pl.multiple_of(c * CHUNK, CHUNK), CHUNK)
            hq, hf, hv, hg = q_ref[rows, :], f_ref[rows, :], v_ref[rows, :], g_ref[rows, :]
            lbv = lb_ref[...]
            sg, fg, kk, lf, qf, sq = _hg_gates(hq, hf, lbv)
            b = _cum(tri, lf)
            dov = do_ref[rows, :]
            dqf_l, dkk_l, db_l, dv_l, dg_l = [], [], [], [], []
            for h in range(HG_HEADS):
                cs = slice(h * HG_D, (h + 1) * HG_D)
                stp = st_ref[c, h]
                bh, vh, gh = b[:, cs], hv[:, cs], hg[:, cs]
                o, parts, att, qh, kh, ebl = _hg_chunk_fwd(qf[:, cs], kk[:, cs], bh, vh, stp)
                sgg = _sigmoid(gh)
                on, r = _rms_fwd(o, nwv)
                d_on = dov[:, cs] * (gh * sgg)
                dg_l.append(dov[:, cs] * on * _dsilu(gh, sgg))
                do, dnw = _rms_bwd(o, r, nwv, d_on)
                dnw_ref[...] += dnw
                dob, vb = do.astype(BF16), vh.astype(BF16)
                dsth = dst[h]
                dstb = dsth.astype(BF16)
                dqh = _nn3(do, stp)
                dkh = _nn3(vh, dsth)
                dv = _nt(kh.astype(BF16), dstb)
                eb = jnp.exp(bh)
                ekl = jnp.exp(bh[CHUNK - 1:CHUNK, :] - bh)
                dqf, dkk = dqh * eb, dkh * ekl
                db = dqh * qh - dkh * kh
                dbl = jnp.sum(dkh * kh, axis=0, keepdims=True) + ebl * jnp.sum(dsth * stp, axis=0, keepdims=True)
                dst[h] = dsth * ebl + _tn(dob, qh.astype(BF16))
                qt, qbig, kbig, eq, ek, mine = parts
                da = jnp.where(_causal(CHUNK), _nt(dob, vb), 0.0)
                da_t = jnp.where(_causal(CHUNK, upper=True), _nt(vb, dob), 0.0)
                dv = dv + _tn(att.astype(BF16), dob)
                dqbig = _tn3(da_t, kbig)
                dkbig = _tn3(da, qbig)
                dkek, dkkb = dkbig * ek, dkbig * kbig
                dqt = jnp.zeros_like(qt)
                for i in range(NSUB):
                    bs = slice(i * HG_D, (i + 1) * HG_D)
                    dqt = dqt + jnp.where(mine[i], dqbig[:, bs], 0.0)
                    dkk = dkk + dkek[:, bs]
                    db = db - dkkb[:, bs]
                dqf = dqf + dqt * eq
                db = db + dqt * qt + jnp.where(last, dbl, 0.0)
                dqf_l.append(dqf); dkk_l.append(dkk); db_l.append(db); dv_l.append(dv)
            cat = lambda l: jnp.concatenate(l, axis=1)
            dlf = _cum(triu, cat(db_l))
            dfg = dlf / fg - cat(dkk_l)
            dlb_ref[...] += jnp.sum(dfg * (1.0 - sg), axis=0, keepdims=True)
            dp_ref[rows, 0:512] = (cat(dqf_l) * _dsilu(hq, sq)).astype(BF16)
            dp_ref[rows, 512:1024] = (dfg * (1.0 - lbv) * sg * (1.0 - sg)).astype(BF16)
            dp_ref[rows, 1024:1536] = cat(dv_l).astype(BF16)
            dp_ref[rows, 1536:2048] = cat(dg_l).astype(BF16)
            return carry

        lax.fori_loop(0, nch, chunk, 0, unroll=4)

    col = lambda n: pl.BlockSpec((ts, 512), functools.partial(lambda n, i: (nt - 1 - i, n), n))
    vec = lambda n: pl.BlockSpec((1, n), lambda i: (0, 0))
    return pl.pallas_call(
        body,
        out_shape=(jax.ShapeDtypeStruct((s, 2048), BF16), jax.ShapeDtypeStruct((1, 512), F32), jax.ShapeDtypeStruct((1, HG_D), F32)),
        grid=(nt,),
        in_specs=[col(0), col(1), col(2), col(3), pl.BlockSpec((ts, 512), lambda i: (nt - 1 - i, 0)),
                  pl.BlockSpec((nch, HG_HEADS, HG_D, HG_D), lambda i: (nt - 1 - i, 0, 0, 0)), vec(512), vec(HG_D)],
        out_specs=(pl.BlockSpec((ts, 2048), lambda i: (nt - 1 - i, 0)), vec(512), vec(HG_D)),
        scratch_shapes=[pltpu.VMEM((HG_HEADS, HG_D, HG_D), F32)], name="hgrn_bwd",
        compiler_params=_params(("arbitrary",)),
    )(proj, proj, proj, proj, dout, states, lb, nw)


CV_BLK = 2048 // 512


def _halo_before(ts, halo, colblk):
    return pl.BlockSpec((halo, 512), functools.partial(lambda cb, i: (jnp.maximum(i * (ts // halo) - 1, 0), cb), colblk))


def _cv_front(a_ref, g_ref, ah_ref, gh_ref, ext, first):
    a, sg = a_ref[...], _sigmoid(g_ref[...])
    zh = ah_ref[...] * _sigmoid(gh_ref[...])
    ext[0:CV_HALO, :] = jnp.where(first, 0.0, zh)
    ext[CV_HALO:, :] = a * sg
    return a, sg


def _windows(ref, base, ntaps, ts):
    out = []
    for phase in range(8):
        taps = [j for j in range(ntaps) if (base + j) % 8 == phase]
        if taps:
            span = max(base + j - phase for j in taps)
            big = ref[pl.ds(phase, ts + span), :]
            out += [(j, big[base + j - phase:base + j - phase + ts]) for j in taps]
    return out


def _cv_conv_ln(ext, w_ref, b_ref, ts):
    y = jnp.zeros((ts, CV_W), F32) + b_ref[...]
    for j, win in _windows(ext, CV_HALO - (CV_K - 1), CV_K, ts):
        y = y + w_ref[j:j + 1, :] * win
    mu = jnp.mean(y, axis=-1, keepdims=True)
    yc = y - mu
    r = lax.rsqrt(jnp.mean(yc * yc, axis=-1, keepdims=True) + EPS)
    return yc * r, r


def _conv_fwd(proj, w, b, lnw, lnb, ts=256):
    s = proj.shape[0]
    ts = min(ts, s)

    def body(a_ref, g_ref, ah_ref, gh_ref, w_ref, b_ref, lnw_ref, lnb_ref, o_ref, ext):
        _cv_front(a_ref, g_ref, ah_ref, gh_ref, ext, pl.program_id(0) == 0)
        yh, _ = _cv_conv_ln(ext, w_ref, b_ref, ts)
        yn = yh * lnw_ref[...] + lnb_ref[...]
        o_ref[...] = (yn * _sigmoid(yn)).astype(BF16)

    col = lambda n: pl.BlockSpec((ts, 512), functools.partial(lambda n, i: (i, n), n))
    vec = pl.BlockSpec((1, CV_W), lambda i: (0, 0))
    return pl.pallas_call(
        body, out_shape=jax.ShapeDtypeStruct((s, CV_W), BF16), grid=(s // ts,),
        in_specs=[col(CV_BLK), col(CV_BLK + 1), _halo_before(ts, CV_HALO, CV_BLK), _halo_before(ts, CV_HALO, CV_BLK + 1),
                  pl.BlockSpec((32, CV_W), lambda i: (0, 0)), vec, vec, vec],
        out_specs=pl.BlockSpec((ts, CV_W), lambda i: (i, 0)), scratch_shapes=[pltpu.VMEM((ts + CV_HALO, CV_W), F32)],
        name="conv_fwd", compiler_params=_params(("parallel",)),
    )(proj, proj, proj, proj, w, b, lnw, lnb)


def _conv_bwd(proj, dout, w, b, lnw, lnb, ts=256):
    s = proj.shape[0]
    ts = min(ts, s)
    nt = s // ts

    def body(a_ref, g_ref, ah_ref, gh_ref, do_ref, w_ref, b_ref, lnw_ref, lnb_ref,
             du_ref, dw_ref, db_ref, dlnw_ref, dlnb_ref, ext, dyext, carry):
        i = pl.program_id(0)

        @pl.when(i == 0)
        def _():
            carry[...] = jnp.zeros_like(carry)
            for ref in (dw_ref, db_ref, dlnw_ref, dlnb_ref):
                ref[...] = jnp.zeros_like(ref)

        a, sg = _cv_front(a_ref, g_ref, ah_ref, gh_ref, ext, i == nt - 1)
        yh, r = _cv_conv_ln(ext, w_ref, b_ref, ts)
        yn = yh * lnw_ref[...] + lnb_ref[...]
        dyn = do_ref[...] * _dsilu(yn, _sigmoid(yn))
        dlnw_ref[...] += jnp.sum(dyn * yh, axis=0, keepdims=True)
        dlnb_ref[...] += jnp.sum(dyn, axis=0, keepdims=True)
        gl = dyn * lnw_ref[...]
        dy = r * (gl - jnp.mean(gl, axis=-1, keepdims=True) - yh * jnp.mean(gl * yh, axis=-1, keepdims=True))
        db_ref[...] += jnp.sum(dy, axis=0, keepdims=True)
        dyext[0:ts, :] = dy
        dyext[ts:, :] = carry[...]
        carry[...] = dy[0:CV_HALO, :]
        dz = jnp.zeros((ts, CV_W), F32)
        for j, win in _windows(ext, CV_HALO - (CV_K - 1), CV_K, ts):
            dw_ref[j:j + 1, :] += jnp.sum(dy * win, axis=0, keepdims=True)
        for j, win in _windows(dyext, 0, CV_K, ts):
            dz = dz + w_ref[CV_K - 1 - j:CV_K - j, :] * win
        du_ref[:, 0:CV_W] = (dz * sg).astype(BF16)
        du_ref[:, CV_W:] = (dz * a * sg * (1.0 - sg)).astype(BF16)

    rev = lambda n: pl.BlockSpec((ts, 512), functools.partial(lambda n, i: (nt - 1 - i, n), n))
    halo = lambda n: pl.BlockSpec((CV_HALO, 512), functools.partial(
        lambda n, i: (jnp.maximum((nt - 1 - i) * (ts // CV_HALO) - 1, 0), n), n))
    vec = pl.BlockSpec((1, CV_W), lambda i: (0, 0))
    wsp = pl.BlockSpec((32, CV_W), lambda i: (0, 0))
    v1 = jax.ShapeDtypeStruct((1, CV_W), F32)
    return pl.pallas_call(
        body, out_shape=(jax.ShapeDtypeStruct((s, 2 * CV_W), BF16), jax.ShapeDtypeStruct((32, CV_W), F32), v1, v1, v1),
        grid=(nt,),
        in_specs=[rev(CV_BLK), rev(CV_BLK + 1), halo(CV_BLK), halo(CV_BLK + 1), rev(0), wsp, vec, vec, vec],
        out_specs=(pl.BlockSpec((ts, 2 * CV_W), lambda i: (nt - 1 - i, 0)), wsp, vec, vec, vec),
        scratch_shapes=[pltpu.VMEM((ts + CV_HALO, CV_W), F32), pltpu.VMEM((ts + CV_HALO, CV_W), F32), pltpu.VMEM((CV_HALO, CV_W), F32)],
        name="conv_bwd", compiler_params=_params(("arbitrary",)),
    )(proj, proj, proj, proj, dout, w, b, lnw, lnb)


PL_BLK = 3072 // 512


def _pool_windows(ext, t0, ts):
    n = ext.shape[0]
    t = t0 + lax.broadcasted_iota(jnp.int32, (ts, 1), 0)
    out = []
    for g, wdw in enumerate(POOL_WINDOWS):
        e = ext[:, g * 128:(g + 1) * 128]
        acc, k = e, 1
        while k < wdw:
            acc = acc + pltpu.roll(acc, k, 0)
            k *= 2
        cnt = jnp.minimum(t + 1, wdw).astype(F32)
        out.append(acc[POOL_HALO:] / cnt - e[POOL_HALO:])
    return out


def _pool_fwd(proj, w, sc, ts=256):
    s = proj.shape[0]
    ts = min(ts, s)

    def body(u_ref, uh_ref, w_ref, sc_ref, o_ref):
        i = pl.program_id(0)
        ext = jnp.concatenate([jnp.where(i == 0, 0.0, uh_ref[...]), u_ref[...]], axis=0)
        ps = _pool_windows(ext, i * ts, ts)
        y = jnp.concatenate([_nn(ps[g].astype(BF16), w_ref[g].astype(BF16)) for g in range(4)], axis=1)
        o_ref[...] = (y * sc_ref[...]).astype(BF16)

    return pl.pallas_call(
        body, out_shape=jax.ShapeDtypeStruct((s, 512), BF16), grid=(s // ts,),
        in_specs=[pl.BlockSpec((ts, 512), lambda i: (i, PL_BLK)), _halo_before(ts, POOL_HALO, PL_BLK),
                  pl.BlockSpec((4, 128, 128), lambda i: (0, 0, 0)), pl.BlockSpec((1, 512), lambda i: (0, 0))],
        out_specs=pl.BlockSpec((ts, 512), lambda i: (i, 0)), name="pool_fwd", compiler_params=_params(("parallel",)),
    )(proj, proj, w, sc)


def _pool_bwd(proj, dout, w, sc, ts=256):
    s = proj.shape[0]
    ts = min(ts, s)
    nt = s // ts
    n = ts + POOL_HALO

    def body(u_ref, uh_ref, do_ref, doh_ref, w_ref, sc_ref, du_ref, dw_ref, dsc_ref):
        i = pl.program_id(0)

        @pl.when(i == 0)
        def _():
            dw_ref[...] = jnp.zeros_like(dw_ref)
            dsc_ref[...] = jnp.zeros_like(dsc_ref)

        ext = jnp.concatenate([jnp.where(i == 0, 0.0, uh_ref[...]), u_ref[...]], axis=0)
        ps = _pool_windows(ext, i * ts, ts)
        dov = do_ref[...]
        dyext = jnp.concatenate([dov, jnp.where(i == nt - 1, 0.0, doh_ref[...])], axis=0) * sc_ref[...]
        t = i * ts + lax.broadcasted_iota(jnp.int32, (n, 1), 0)
        row = lax.broadcasted_iota(jnp.int32, (n, 1), 0)
        dus = []
        for g, wdw in enumerate(POOL_WINDOWS):
            cs = slice(g * 128, (g + 1) * 128)
            wg, pb = w_ref[g].astype(BF16), ps[g].astype(BF16)
            dsc_ref[:, cs] += jnp.sum(dov[:, cs] * _nn(pb, wg), axis=0, keepdims=True)
            dyg = dyext[:, cs].astype(BF16)
            dw_ref[g] += _tn(pb, dyg[0:ts])
            dp = _nt(dyg, wg)
            acc, k = dp / jnp.minimum(t + 1, wdw).astype(F32), 1
            while k < wdw:
                acc = acc + jnp.where(row < n - k, pltpu.roll(acc, n - k, 0), 0.0)
                k *= 2
            dus.append(acc[0:ts] - dp[0:ts])
        du_ref[...] = jnp.concatenate(dus, axis=1).astype(BF16)

    tile = lambda cb: pl.BlockSpec((ts, 512), functools.partial(lambda cb, i: (i, cb), cb))
    after = pl.BlockSpec((POOL_HALO, 512), lambda i: (jnp.minimum((i + 1) * (ts // POOL_HALO), s // POOL_HALO - 1), 0))
    wsp, vec = pl.BlockSpec((4, 128, 128), lambda i: (0, 0, 0)), pl.BlockSpec((1, 512), lambda i: (0, 0))
    return pl.pallas_call(
        body, out_shape=(jax.ShapeDtypeStruct((s, 512), BF16), jax.ShapeDtypeStruct((4, 128, 128), F32), jax.ShapeDtypeStruct((1, 512), F32)),
        grid=(nt,),
        in_specs=[tile(PL_BLK), _halo_before(ts, POOL_HALO, PL_BLK), tile(0), after, wsp, vec],
        out_specs=(tile(0), wsp, vec), name="pool_bwd", compiler_params=_params(("arbitrary",)),
    )(proj, proj, dout, dout, w, sc)


LX_BLK, LY_BLK = 3584 // 512, 4096 // 512
LRU_OFF = LRU_HALO - (LRU_CONV - 1)


def _scan_fwd(a, b):
    n = a.shape[0]
    row = lax.broadcasted_iota(jnp.int32, (n, 1), 0)
    k = 1
    while k < n:
        m = row >= k
        b = jnp.where(m, a * pltpu.roll(b, k, 0) + b, b)
        a = jnp.where(m, a * pltpu.roll(a, k, 0), a)
        k *= 2
    return a, b


def _scan_rev(a, b):
    n = a.shape[0]
    row = lax.broadcasted_iota(jnp.int32, (n, 1), 0)
    k = 1
    while k < n:
        m = row < n - k
        b = jnp.where(m, a * pltpu.roll(b, n - k, 0) + b, b)
        a = jnp.where(m, a * pltpu.roll(a, n - k, 0), a)
        k *= 2
    return b


def _lru_gates(x_ref, xh_ref, ext, first, cw_ref, cb_ref, wa_ref, ba_ref, wx_ref, bx_ref, lam_ref, ts):
    ext[0:LRU_HALO, :] = jnp.where(first, 0.0, xh_ref[...])
    ext[LRU_HALO:, :] = x_ref[...]
    xc = jnp.zeros((ts, LRU_W), F32) + cb_ref[...]
    for j in range(LRU_CONV):
        xc = xc + cw_ref[j:j + 1, :] * ext[pl.ds(LRU_OFF + j, ts), :]
    xb = xc.astype(BF16)
    r = _sigmoid(_nn(xb, wa_ref[...]) + ba_ref[...])
    ig = _sigmoid(_nn(xb, wx_ref[...]) + bx_ref[...])
    nl = -lam_ref[...]
    sp = jnp.maximum(nl, 0.0) + jnp.log(1.0 + jnp.exp(-jnp.abs(nl)))
    la = -LRU_C * r * sp
    a = jnp.exp(la)
    z = 2.0 * la
    em = jnp.where(z > -0.1, -z * (1.0 + z * 0.5 * (1.0 + z * (1.0 / 3) * (1.0 + z * 0.25 * (1.0 + z * 0.2)))), 1.0 - a * a)
    return xc, xb, r, ig, sp, a, jnp.sqrt(em)


def _lru_fwd(proj, cw, cb, wa, ba, wx, bx, lam, ts=256):
    s = proj.shape[0]
    ts = min(ts, s)

    def body(x_ref, xh_ref, y_ref, cw_ref, cb_ref, wa_ref, ba_ref, wx_ref, bx_ref, lam_ref, o_ref, h_ref, ext, hc):
        i = pl.program_id(0)

        @pl.when(i == 0)
        def _():
            hc[...] = jnp.zeros_like(hc)

        xc, _, _, ig, _, a, mult = _lru_gates(x_ref, xh_ref, ext, i == 0, cw_ref, cb_ref, wa_ref, ba_ref, wx_ref, bx_ref, lam_ref, ts)
        acum, h0 = _scan_fwd(a, mult * ig * xc)
        h = h0 + acum * hc[0:1, :]
        hc[...] = jnp.broadcast_to(h[ts - 1:ts, :], hc.shape)
        h_ref[...] = h
        o_ref[...] = (h * _gelu_parts(y_ref[...])[0]).astype(BF16)

    tile = lambda cb_: pl.BlockSpec((ts, 512), functools.partial(lambda c, i: (i, c), cb_))
    vec = pl.BlockSpec((1, LRU_W), lambda i: (0, 0))
    mat = pl.BlockSpec((LRU_W, LRU_W), lambda i: (0, 0))
    return pl.pallas_call(
        body, out_shape=(jax.ShapeDtypeStruct((s, LRU_W), BF16), jax.ShapeDtypeStruct((s, LRU_W), F32)), grid=(s // ts,),
        in_specs=[tile(LX_BLK), _halo_before(ts, LRU_HALO, LX_BLK), tile(LY_BLK), pl.BlockSpec((8, LRU_W), lambda i: (0, 0)),
                  vec, mat, vec, mat, vec, vec],
        out_specs=(tile(0), tile(0)), scratch_shapes=[pltpu.VMEM((ts + LRU_HALO, LRU_W), F32), pltpu.VMEM((8, LRU_W), F32)],
        name="lru_fwd", compiler_params=_params(("arbitrary",)),
    )(proj, proj, proj, cw, cb, wa, ba, wx, bx, lam)


def _lru_bwd(proj, hs, dout, cw, cb, wa, ba, wx, bx, lam, ts=256):
    s = proj.shape[0]
    ts = min(ts, s)
    nt = s // ts

    def body(x_ref, xh_ref, y_ref, h_ref, hh_ref, do_ref, cw_ref, cb_ref, wa_ref, ba_ref, wx_ref, bx_ref, lam_ref,
             dxy_ref, dcw_ref, dcb_ref, dwa_ref, dba_ref, dwx_ref, dbx_ref, dlam_ref, ext, dext, cg, cd):
        i = pl.program_id(0)
        first_tile = i == nt - 1

        @pl.when(i == 0)
        def _():
            cg[...] = jnp.zeros_like(cg)
            cd[...] = jnp.zeros_like(cd)
            for ref in (dcw_ref, dcb_ref, dwa_ref, dba_ref, dwx_ref, dbx_ref, dlam_ref):
                ref[...] = jnp.zeros_like(ref)

        xc, xb, r, ig, sp, a, mult = _lru_gates(x_ref, xh_ref, ext, first_tile, cw_ref, cb_ref, wa_ref, ba_ref, wx_ref, bx_ref, lam_ref, ts)
        row = lax.broadcasted_iota(jnp.int32, (ts, 1), 0)
        h, dov = h_ref[...], do_ref[...]
        gel, dgel = _gelu_parts(y_ref[...])
        dxy_ref[:, LRU_W:] = (dov * h * dgel).astype(BF16)
        alpha = jnp.where(row < ts - 1, pltpu.roll(a, ts - 1, 0), 0.0)
        g = _scan_rev(alpha, dov * gel + jnp.where(row == ts - 1, cg[0:1, :], 0.0))
        cg[...] = jnp.broadcast_to(a[0:1, :] * g[0:1, :], cg.shape)
        hprev = jnp.where(row == 0, jnp.where(first_tile, 0.0, hh_ref[LRU_HALO - 1:LRU_HALO, :]), pltpu.roll(h, 1, 0))
        dla = g * hprev * a - g * ig * xc * (a * a) / mult
        dpr = dla * (-LRU_C * sp) * r * (1.0 - r)
        dpi = g * mult * xc * ig * (1.0 - ig)
        dprb, dpib = dpr.astype(BF16), dpi.astype(BF16)
        dxc = g * mult * ig + _nt(dprb, wa_ref[...]) + _nt(dpib, wx_ref[...])
        dlam_ref[...] += jnp.sum(dla * (-LRU_C * r), axis=0, keepdims=True) * (-_sigmoid(-lam_ref[...]))
        dwa_ref[...] += _tn(xb, dprb)
        dwx_ref[...] += _tn(xb, dpib)
        dba_ref[...] += jnp.sum(dpr, axis=0, keepdims=True)
        dbx_ref[...] += jnp.sum(dpi, axis=0, keepdims=True)
        dcb_ref[...] += jnp.sum(dxc, axis=0, keepdims=True)
        dext[0:ts, :] = dxc
        dext[ts:, :] = cd[...]
        cd[...] = dxc[0:LRU_HALO, :]
        dx = jnp.zeros((ts, LRU_W), F32)
        for j in range(LRU_CONV):
            dcw_ref[j:j + 1, :] += jnp.sum(dxc * ext[pl.ds(LRU_OFF + j, ts), :], axis=0, keepdims=True)
            dx = dx + cw_ref[j:j + 1, :] * dext[pl.ds(LRU_CONV - 1 - j, ts), :]
        dxy_ref[:, 0:LRU_W] = dx.astype(BF16)

    rev = lambda c: pl.BlockSpec((ts, 512), functools.partial(lambda c, i: (nt - 1 - i, c), c))
    halo = lambda c: pl.BlockSpec((LRU_HALO, 512), functools.partial(
        lambda c, i: (jnp.maximum((nt - 1 - i) * (ts // LRU_HALO) - 1, 0), c), c))
    vec = pl.BlockSpec((1, LRU_W), lambda i: (0, 0))
    mat = pl.BlockSpec((LRU_W, LRU_W), lambda i: (0, 0))
    cws = pl.BlockSpec((8, LRU_W), lambda i: (0, 0))
    v1, m1 = jax.ShapeDtypeStruct((1, LRU_W), F32), jax.ShapeDtypeStruct((LRU_W, LRU_W), F32)
    return pl.pallas_call(
        body, out_shape=(jax.ShapeDtypeStruct((s, 2 * LRU_W), BF16), jax.ShapeDtypeStruct((8, LRU_W), F32), v1, m1, v1, m1, v1, v1),
        grid=(nt,),
        in_specs=[rev(LX_BLK), halo(LX_BLK), rev(LY_BLK), rev(0), halo(0), rev(0), cws, vec, mat, vec, mat, vec, vec],
        out_specs=(pl.BlockSpec((ts, 2 * LRU_W), lambda i: (nt - 1 - i, 0)), cws, vec, mat, vec, mat, vec, vec),
        scratch_shapes=[pltpu.VMEM((ts + LRU_HALO, LRU_W), F32), pltpu.VMEM((ts + LRU_HALO, LRU_W), F32),
                        pltpu.VMEM((8, LRU_W), F32), pltpu.VMEM((LRU_HALO, LRU_W), F32)],
        name="lru_bwd", compiler_params=_params(("arbitrary",)),
    )(proj, proj, proj, hs, hs, dout, cw, cb, wa, ba, wx, bx, lam)


def _final_loss(x, fw, tgt, ts=512):
    s, d = x.shape
    ts = min(ts, s)

    def body(x_ref, w_ref, t_ref, loss_ref, dx_ref, dw_ref):
        @pl.when(pl.program_id(0) == 0)
        def _():
            loss_ref[...] = jnp.zeros_like(loss_ref)
            dw_ref[...] = jnp.zeros_like(dw_ref)

        xv = x_ref[...]
        y, r = _rms_fwd(xv, w_ref[...])
        err = y - t_ref[...]
        loss_ref[...] += 0.5 * jnp.sum(jnp.mean(err * err, axis=-1, keepdims=True), axis=0, keepdims=True)
        dx, dw = _rms_bwd(xv, r, w_ref[...], err * (1.0 / d))
        dx_ref[...] = dx
        dw_ref[...] += dw

    row = pl.BlockSpec((ts, d), lambda i: (i, 0))
    vec = pl.BlockSpec((1, d), lambda i: (0, 0))
    return pl.pallas_call(
        body, out_shape=(jax.ShapeDtypeStruct((8, 128), F32), jax.ShapeDtypeStruct((s, d), F32), jax.ShapeDtypeStruct((1, d), F32)),
        grid=(s // ts,), in_specs=[row, vec, row], out_specs=(pl.BlockSpec((8, 128), lambda i: (0, 0)), row, vec),
        name="final_loss", compiler_params=_params(("arbitrary",)),
    )(x, fw, tgt)


def _lb_softmax(raw_ref):
    raw = raw_ref[...]
    e = jnp.exp(raw - jnp.max(raw, axis=0, keepdims=True))
    return e / jnp.sum(e, axis=0, keepdims=True)


def _lb_fwd(raw):
    def body(raw_ref, o_ref):
        sm = _lb_softmax(raw_ref)
        acc = jnp.zeros((1, sm.shape[1]), F32)
        o_ref[0:1, :] = acc
        for l in range(1, DEPTH):
            acc = acc + sm[l:l + 1, :]
            o_ref[l:l + 1, :] = acc

    return pl.pallas_call(body, out_shape=jax.ShapeDtypeStruct(raw.shape, F32), name="lb_fwd")(raw)


def _lb_bwd(raw, dlb):
    def body(raw_ref, d_ref, o_ref):
        sm = _lb_softmax(raw_ref)
        dlbv = d_ref[...]
        dsm, acc = [None] * DEPTH, jnp.zeros((1, sm.shape[1]), F32)
        for l in range(DEPTH - 1, 0, -1):
            acc = acc + dlbv[l:l + 1, :]
            dsm[l] = acc
        dsm[0] = jnp.zeros_like(acc)
        dsm = jnp.concatenate(dsm, axis=0)
        o_ref[...] = sm * (dsm - jnp.sum(sm * dsm, axis=0, keepdims=True))

    return pl.pallas_call(body, out_shape=jax.ShapeDtypeStruct(raw.shape, F32), name="lb_bwd")(raw, dlb)


def _adam_math(w, g, m, v):
    m = ADAM_B1 * m + (1.0 - ADAM_B1) * g
    v = ADAM_B2 * v + (1.0 - ADAM_B2) * (g * g)
    m_hat = m / (1.0 - ADAM_B1 ** ADAM_STEP)
    v_hat = v / (1.0 - ADAM_B2 ** ADAM_STEP)
    return -ADAM_LR * (m_hat / (jnp.sqrt(v_hat) + ADAM_EPS) + ADAM_WD * w), m, v


def _adamw(w, gs, m, v, name, tr=128):
    r, c = w.shape
    tr = min(tr, r)
    ng = len(gs)

    def body(*refs):
        w_ref, g_refs, m_ref, v_ref = refs[0], refs[1:1 + ng], refs[1 + ng], refs[2 + ng]
        outs = refs[3 + ng:]
        g = g_refs[0][...]
        if ng == 2:
            g = g + g_refs[1][...]
            outs[0][...] = g
            outs = outs[1:]
        for o, val in zip(outs, _adam_math(w_ref[...], g, m_ref[...], v_ref[...])):
            o[...] = val

    blk = pl.BlockSpec((tr, c), lambda i: (i, 0))
    sd = jax.ShapeDtypeStruct((r, c), F32)
    nout = 3 + (ng == 2)
    return pl.pallas_call(
        body, out_shape=(sd,) * nout, grid=(r // tr,), in_specs=[blk] * (3 + ng), out_specs=(blk,) * nout, name=name,
        compiler_params=_params(("parallel",)),
    )(w, *gs, m, v)


def _cast_into_full(w, kind, jj, name, tr=256):
    l, r, c = w.shape
    tr = min(tr, r)

    def body(j_ref, w_ref, o_ref):
        o_ref[...] = w_ref[...].astype(BF16)

    if kind == "col":
        full, dst = (l, r, 4 * c), pl.BlockSpec((None, tr, c), lambda a, b, j: (a, b, j[0]))
    else:
        full, dst = (l, 4 * r, c), pl.BlockSpec((None, tr, c), lambda a, b, j: (a, j[0] * (r // tr) + b, 0))
    return pl.pallas_call(
        body, out_shape=jax.ShapeDtypeStruct(full, BF16),
        grid_spec=pltpu.PrefetchScalarGridSpec(
            num_scalar_prefetch=1, grid=(l, r // tr), in_specs=[pl.BlockSpec((None, tr, c), lambda a, b, j: (a, b, 0))], out_specs=dst),
        name=name, compiler_params=_params(("parallel", "parallel")),
    )(jj, w)


def _place():
    return lax.axis_index("x"), lax.axis_index("y"), lax.axis_index("c")


def _other_chips(x, y):
    return [(1 - x, y), (x, 1 - y), (1 - x, 1 - y)]


def _slab(ref, kind, jj):
    if kind == "col":
        c = ref.shape[2] // 4
        return ref.at[:, :, pl.ds(jj * c, c)]
    r = ref.shape[1] // 4
    return ref.at[:, pl.ds(jj * r, r), :]


HBM = pl.BlockSpec(memory_space=pltpu.HBM)
SEM = pl.BlockSpec(memory_space=pltpu.SEMAPHORE)
EFFECT = pltpu.SideEffectType.DATAFLOW_SIDE_EFFECTING


def _in_hbm(a):
    return pltpu.with_memory_space_constraint(a, pltpu.HBM)


def _thru(arrs):
    return [pltpu.HBM(a.shape, a.dtype) for a in arrs]


def _gather_copy(fulls, kinds, send_sems, recv_sems, group, t, k, landing):
    x, y, c = _place()
    chip = _other_chips(x, y)[k]
    per = fulls[t].shape[0] // DEPTH
    rows = fulls[t].at[pl.ds(group * per, per)]
    idx = (group * len(fulls) + t) * 3 + k
    return pltpu.make_async_remote_copy(
        src_ref=_slab(rows, kinds[t], 2 * x + y), dst_ref=_slab(rows, kinds[t], landing), send_sem=send_sems.at[idx],
        recv_sem=recv_sems.at[idx], device_id=(chip[0], chip[1], c), device_id_type=MESH)


def _gather_start(fulls, kinds, after):
    nt = len(fulls)
    ncp = DEPTH * nt * 3

    def body(*refs):
        ins, send_sems, recv_sems = refs[:nt], refs[nt + 1], refs[nt + 2]
        x, y, _ = _place()
        for group in range(DEPTH):
            for t in range(nt):
                for k in range(3):
                    _gather_copy(ins, kinds, send_sems, recv_sems, group, t, k, 2 * x + y).start()

    out = pl.pallas_call(
        body, out_shape=(pltpu.SemaphoreType.DMA((ncp,)), pltpu.SemaphoreType.DMA((ncp,)), *_thru(fulls)),
        in_specs=[HBM] * nt + [ANY], out_specs=(SEM, SEM, *([HBM] * nt)), input_output_aliases={t: 2 + t for t in range(nt)},
        name="gather_start", compiler_params=pltpu.CompilerParams(has_side_effects=EFFECT),
    )(*[_in_hbm(a) for a in fulls], after)
    return out[0], out[1], list(out[2:])


def _gather_wait(group, which, send_sems, recv_sems, fulls, kinds, after):
    nt = len(fulls)

    def body(*refs):
        ins, send_ref, recv_ref = refs[:nt], refs[nt], refs[nt + 1]
        x, y, _ = _place()
        chips = _other_chips(x, y)
        for t in which:
            for k in range(3):
                cp = _gather_copy(ins, kinds, send_ref, recv_ref, group, t, k, 2 * chips[k][0] + chips[k][1])
                cp.wait_send()
                cp.wait_recv()

    out = pl.pallas_call(
        body, out_shape=tuple(_thru(fulls)), in_specs=[HBM] * nt + [SEM, SEM, ANY], out_specs=tuple([HBM] * nt),
        input_output_aliases={t: t for t in range(nt)}, name="gather_wait_%d_%d" % (group, which[0]),
        compiler_params=pltpu.CompilerParams(has_side_effects=EFFECT),
    )(*fulls, send_sems, recv_sems, after)
    return list(out)


def _scatter_copy(grads, lands, kinds, send_sems, recv_sems, t, k):
    x, y, c = _place()
    chip = _other_chips(x, y)[k]
    return pltpu.make_async_remote_copy(
        src_ref=_slab(grads[t], kinds[t], 2 * chip[0] + chip[1]), dst_ref=lands[t].at[k], send_sem=send_sems.at[3 * t + k],
        recv_sem=recv_sems.at[3 * t + k], device_id=(chip[0], chip[1], c), device_id_type=MESH)


def _scatter_start(grads, kinds, after, name):
    nt = len(grads)
    lands = []
    for g, kd in zip(grads, kinds):
        l, r, c = g.shape
        lands.append(lax.empty((3, l, r, c // 4) if kd == "col" else (3, l, r // 4, c), g.dtype))

    def body(*refs):
        ins, lnd, send_sems, recv_sems = refs[:nt], refs[nt:2 * nt], refs[2 * nt + 1], refs[2 * nt + 2]
        for t in range(nt):
            for k in range(3):
                _scatter_copy(ins, lnd, kinds, send_sems, recv_sems, t, k).start()
        refs[-1][...] = jnp.zeros_like(refs[-1])

    out = pl.pallas_call(
        body, out_shape=(pltpu.SemaphoreType.DMA((3 * nt,)), pltpu.SemaphoreType.DMA((3 * nt,)), *_thru(grads), *_thru(lands),
                         jax.ShapeDtypeStruct((8, 128), F32)),
        in_specs=[HBM] * (2 * nt) + [ANY], out_specs=(SEM, SEM, *([HBM] * (2 * nt)), pl.BlockSpec(memory_space=pltpu.VMEM)),
        input_output_aliases={t: 2 + t for t in range(2 * nt)}, name=name,
        compiler_params=pltpu.CompilerParams(has_side_effects=EFFECT),
    )(*[_in_hbm(a) for a in grads], *[_in_hbm(a) for a in lands], after)
    return (out[0], out[1], list(out[2:2 + nt]), list(out[2 + nt:2 + 2 * nt])), out[-1]


def _scatter_wait(send_sems, recv_sems, grads, lands, kinds, after, name):
    nt = len(grads)

    def body(*refs):
        ins, lnd, send_ref, recv_ref = refs[:nt], refs[nt:2 * nt], refs[2 * nt], refs[2 * nt + 1]
        for t in range(nt):
            for k in range(3):
                cp = _scatter_copy(ins, lnd, kinds, send_ref, recv_ref, t, k)
                cp.wait_send()
                cp.wait_recv()

    out = pl.pallas_call(
        body, out_shape=(*_thru(grads), *_thru(lands)), in_specs=[HBM] * (2 * nt) + [SEM, SEM, ANY],
        out_specs=tuple([HBM] * (2 * nt)), input_output_aliases={t: t for t in range(2 * nt)}, name=name,
        compiler_params=pltpu.CompilerParams(has_side_effects=EFFECT),
    )(*grads, *lands, send_sems, recv_sems, after)
    return list(out[:nt]), list(out[nt:])


def _sibling_swap(arrs):
    nt = len(arrs)

    def body(*refs):
        ins, outs = refs[:nt], refs[nt:2 * nt]
        send_sems, recv_sems = refs[2 * nt:]
        x, y, c = _place()
        sends = [pltpu.make_async_remote_copy(src_ref=ins[t], dst_ref=outs[t], send_sem=send_sems.at[t], recv_sem=recv_sems.at[t],
                                              device_id=(x, y, 1 - c), device_id_type=MESH) for t in range(nt)]
        for cp in sends:
            cp.start()
        for cp in sends:
            cp.wait_recv()
        for cp in sends:
            cp.wait_send()

    return pl.pallas_call(
        body, out_shape=[jax.ShapeDtypeStruct(a.shape, a.dtype) for a in arrs], in_specs=[ANY] * nt, out_specs=[ANY] * nt,
        scratch_shapes=[pltpu.SemaphoreType.DMA((nt,)), pltpu.SemaphoreType.DMA((nt,))], name="sibling_swap",
    )(*arrs)


def _gather_small(vec, over_c):
    n = vec.shape[0]
    flips = [(dx, dy, dc) for dx in (0, 1) for dy in (0, 1) for dc in ((0, 1) if over_c else (0,))][1:]
    np_ = len(flips)

    def body(v_ref, o_ref, send_sems, recv_sems, local_sem):
        x, y, c = _place()

        def idx(px, py, pc):
            return 4 * px + 2 * py + pc if over_c else 2 * px + py

        def peer(f):
            return (1 - x if f[0] else x, 1 - y if f[1] else y, 1 - c if f[2] else c)

        def push(k, landing):
            return pltpu.make_async_remote_copy(src_ref=v_ref, dst_ref=o_ref.at[landing], send_sem=send_sems.at[k],
                                                recv_sem=recv_sems.at[k], device_id=peer(flips[k]), device_id_type=MESH)

        mine = pltpu.make_async_copy(v_ref, o_ref.at[idx(x, y, c)], local_sem)
        sends = [push(k, idx(x, y, c)) for k in range(np_)]
        for cp in [mine] + sends:
            cp.start()
        for k in range(np_):
            push(k, idx(*peer(flips[k]))).wait_recv()
        for cp in sends:
            cp.wait_send()
        mine.wait()

    return pl.pallas_call(
        body, out_shape=jax.ShapeDtypeStruct((np_ + 1, n, 128), F32), in_specs=[ANY], out_specs=ANY,
        scratch_shapes=[pltpu.SemaphoreType.DMA((np_,)), pltpu.SemaphoreType.DMA((np_,)), pltpu.SemaphoreType.DMA(())],
        name="gather_small_all" if over_c else "gather_small_xy",
    )(vec)


def _sum_rows(buf, after, tr=512):
    p, n, _ = buf.shape
    tr = min(tr, n)

    def body(b_ref, after_ref, o_ref):
        acc = b_ref[0]
        for k in range(1, p):
            acc = acc + b_ref[k]
        o_ref[...] = acc

    return pl.pallas_call(
        body, out_shape=jax.ShapeDtypeStruct((n, 128), F32), grid=(n // tr,),
        in_specs=[pl.BlockSpec((p, tr, 128), lambda i: (0, i, 0)), ANY], out_specs=pl.BlockSpec((tr, 128), lambda i: (i, 0)),
        name="sum_rows", compiler_params=_params(("parallel",)),
    )(buf, after)


def _sum_partials_into(stack, at, depth, grad, recv, kind, jj, name, tr=128):
    _, _, r, c = recv.shape
    tr = min(tr, r)

    def body(j_ref, g_ref, r0, r1, r2, *rest):
        rest[-1][...] = ((g_ref[...].astype(F32) + r0[...].astype(F32)) + r1[...].astype(F32)) + r2[...].astype(F32)

    if kind == "col":
        own = pl.BlockSpec((None, tr, c), lambda b, j: (0, b, j[0]))
    else:
        own = pl.BlockSpec((None, tr, c), lambda b, j: (0, j[0] * (r // tr) + b, 0))
    got = lambda k: pl.BlockSpec((None, None, tr, c), functools.partial(lambda k, b, j: (k, 0, b, 0), k))
    chained = stack is not None
    return pl.pallas_call(
        body, out_shape=jax.ShapeDtypeStruct((depth, r, c), F32),
        grid_spec=pltpu.PrefetchScalarGridSpec(
            num_scalar_prefetch=1, grid=(r // tr,), in_specs=[own, got(0), got(1), got(2)] + ([ANY] if chained else []),
            out_specs=pl.BlockSpec((None, tr, c), lambda b, j: (at, b, 0))),
        input_output_aliases={5: 0} if chained else {}, name=name, compiler_params=_params(("parallel",)),
    )(*([jj, grad, recv, recv, recv] + ([stack] if chained else [])))


WEIGHTS = ['norm_mix_w', 'w_in', 'hg_lb_raw', 'hg_norm_w', 'cv_dw_w', 'cv_dw_b', 'cv_ln_w', 'cv_ln_b', 'pl_w', 'pl_scale',
           'lru_conv_w', 'lru_conv_b', 'lru_wa', 'lru_ba', 'lru_wx', 'lru_bx', 'lru_lambda', 'gate_b', 'w_branch', 'w_out',
           'norm_mem_w', 'mem_norm_w', 'xa_wq', 'xa_wkv', 'xa_wo', 'norm_ffn_w', 'ffn_w1', 'ffn_w2', 'final_norm_w']
BIG = {'w_in': 'col', 'w_branch': 'col', 'w_out': 'row', 'xa_wq': 'row', 'xa_wkv': 'col', 'xa_wo': 'row', 'ffn_w1': 'col', 'ffn_w2': 'row'}
SMALL_SPLIT = ('gate_b', 'cv_dw_w', 'lru_conv_w')
SMALL = [n for n in WEIGHTS if n not in BIG]
PIECES = ['w_in', ('w_branch', 0), ('w_branch', 1), ('w_branch', 2), ('w_branch', 3), 'w_out', 'xa_wq', 'xa_wkv', 'xa_wo', 'ffn_w1', 'ffn_w2']
PIECE_KINDS = [BIG[k[0] if isinstance(k, tuple) else k] for k in PIECES]
ROWS_PAD = 512


def _as3d(a):
    return a.reshape((-1,) + a.shape[-2:])


def _pack(parts):
    flat = jnp.concatenate([p.reshape(-1).astype(F32) for p in parts])
    n = -(-flat.shape[0] // (128 * ROWS_PAD)) * ROWS_PAD
    return jnp.pad(flat, (0, n * 128 - flat.shape[0])).reshape(n, 128)


def _unpack(packed, shapes):
    flat, out, o = packed.reshape(-1), [], 0
    for sh in shapes:
        sz = math.prod(sh)
        out.append(flat[o:o + sz].reshape(sh))
        o += sz
    return out


def _block_diag(w):
    h, a, b = w.shape
    eye = jnp.eye(h, dtype=w.dtype)
    return (w[:, :, None, :] * eye[:, None, :, None]).reshape(h * a, h * b)


def _diag_blocks(m, h):
    a, b = m.shape[0] // h, m.shape[1] // h
    return jnp.stack([m[i * a:(i + 1) * a, i * b:(i + 1) * b] for i in range(h)])


def kernel(x, mem, norm_mix_w, w_in, hg_lb_raw, hg_norm_w, cv_dw_w, cv_dw_b, cv_ln_w, cv_ln_b, pl_w, pl_scale, lru_conv_w, lru_conv_b, lru_wa, lru_ba, lru_wx, lru_bx, lru_lambda, gate_b, w_branch, w_out, norm_mem_w, mem_norm_w, xa_wq, xa_wkv, xa_wo, norm_ffn_w, ffn_w1, ffn_w2, final_norm_w, loss_target, m_norm_mix_w, m_w_in, m_hg_lb_raw, m_hg_norm_w, m_cv_dw_w, m_cv_dw_b, m_cv_ln_w, m_cv_ln_b, m_pl_w, m_pl_scale, m_lru_conv_w, m_lru_conv_b, m_lru_wa, m_lru_ba, m_lru_wx, m_lru_bx, m_lru_lambda, m_gate_b, m_w_branch, m_w_out, m_norm_mem_w, m_mem_norm_w, m_xa_wq, m_xa_wkv, m_xa_wo, m_norm_ffn_w, m_ffn_w1, m_ffn_w2, m_final_norm_w, v_norm_mix_w, v_w_in, v_hg_lb_raw, v_hg_norm_w, v_cv_dw_w, v_cv_dw_b, v_cv_ln_w, v_cv_ln_b, v_pl_w, v_pl_scale, v_lru_conv_w, v_lru_conv_b, v_lru_wa, v_lru_ba, v_lru_wx, v_lru_bx, v_lru_lambda, v_gate_b, v_w_branch, v_w_out, v_norm_mem_w, v_mem_norm_w, v_xa_wq, v_xa_wkv, v_xa_wo, v_norm_ffn_w, v_ffn_w1, v_ffn_w2, v_final_norm_w):
    w = dict(zip(WEIGHTS, (norm_mix_w, w_in, hg_lb_raw, hg_norm_w, cv_dw_w, cv_dw_b, cv_ln_w, cv_ln_b, pl_w, pl_scale, lru_conv_w, lru_conv_b, lru_wa, lru_ba, lru_wx, lru_bx, lru_lambda, gate_b, w_branch, w_out, norm_mem_w, mem_norm_w, xa_wq, xa_wkv, xa_wo, norm_ffn_w, ffn_w1, ffn_w2, final_norm_w)))
    m1 = dict(zip(WEIGHTS, (m_norm_mix_w, m_w_in, m_hg_lb_raw, m_hg_norm_w, m_cv_dw_w, m_cv_dw_b, m_cv_ln_w, m_cv_ln_b, m_pl_w, m_pl_scale, m_lru_conv_w, m_lru_conv_b, m_lru_wa, m_lru_ba, m_lru_wx, m_lru_bx, m_lru_lambda, m_gate_b, m_w_branch, m_w_out, m_norm_mem_w, m_mem_norm_w, m_xa_wq, m_xa_wkv, m_xa_wo, m_norm_ffn_w, m_ffn_w1, m_ffn_w2, m_final_norm_w)))
    v1 = dict(zip(WEIGHTS, (v_norm_mix_w, v_w_in, v_hg_lb_raw, v_hg_norm_w, v_cv_dw_w, v_cv_dw_b, v_cv_ln_w, v_cv_ln_b, v_pl_w, v_pl_scale, v_lru_conv_w, v_lru_conv_b, v_lru_wa, v_lru_ba, v_lru_wx, v_lru_bx, v_lru_lambda, v_gate_b, v_w_branch, v_w_out, v_norm_mem_w, v_mem_norm_w, v_xa_wq, v_xa_wkv, v_xa_wo, v_norm_ffn_w, v_ffn_w1, v_ffn_w2, v_final_norm_w)))
    seq = x.shape[1]
    xs, mems, tgt = x.reshape(seq, D_MODEL), mem.reshape(-1, D_MODEL), loss_target.reshape(seq, D_MODEL)
    jj = 2 * lax.axis_index("x") + lax.axis_index("y")
    jj1 = jnp.reshape(jj, (1,)).astype(jnp.int32)

    split_shapes = [w[n].shape for n in SMALL_SPLIT]
    got = _gather_small(_pack([w[n] for n in SMALL_SPLIT]), over_c=False)
    per_chip = [_unpack(got[k], split_shapes) for k in range(4)]
    full_small = {n: jnp.concatenate([per_chip[k][i] for k in range(4)], axis=-1) for i, n in enumerate(SMALL_SPLIT)}
    big_names = list(BIG)
    kinds = [BIG[n] for n in big_names]
    g_send, g_recv, fulls = _gather_start([_cast_into_full(_as3d(w[n]), BIG[n], jj1, "cast_" + n) for n in big_names], kinds, got)
    tix = {n: t for t, n in enumerate(big_names)}

    lb = _lb_fwd(hg_lb_raw)
    row = lambda a: a.reshape(1, -1)

    def layer_params(l):
        return dict(
            nmix=row(norm_mix_w[l]), lb=row(lb[l]), hgnw=row(hg_norm_w[l]),
            cw=jnp.pad(full_small['cv_dw_w'][l], ((0, 32 - CV_K), (0, 0))), cb=row(cv_dw_b[l]), lnw=row(cv_ln_w[l]), lnb=row(cv_ln_b[l]),
            plw=pl_w[l], plsc=row(pl_scale[l]),
            lcw=jnp.pad(full_small['lru_conv_w'][l], ((0, 8 - LRU_CONV), (0, 0))), lcb=row(lru_conv_b[l]),
            wa=_block_diag(lru_wa[l]).astype(BF16), ba=row(lru_ba[l]), wx=_block_diag(lru_wx[l]).astype(BF16), bx=row(lru_bx[l]),
            lam=row(lru_lambda[l]), gb=full_small['gate_b'][l], nmem=row(norm_mem_w[l]), memw=row(mem_norm_w[l]), nffn=row(norm_ffn_w[l]))

    saved = []
    xc = xs
    def arrived(l, names, after):
        got_ = _gather_wait(l, [tix[n] for n in names], g_send, g_recv, fulls, kinds, after)
        wf_ = dict(zip(big_names, got_))
        wf_['w_branch'] = wf_['w_branch'].reshape(DEPTH, 4, 512, D_MODEL)
        return got_, wf_

    for l in range(DEPTH):
        first = l == 0
        fulls, wf = arrived(l, ['w_in'] if first else big_names, xc)
        p = layer_params(l)
        h = _norm_fwd(xc, p['nmix'], "norm_mix")
        proj = _mm(h, wf['w_in'], "nn", F32, "proj", tm=seq, layer=l)
        b_hg, st = _hgrn_fwd(proj, p['lb'], p['hgnw'])
        b_cv = _conv_fwd(proj, p['cw'], p['cb'], p['lnw'], p['lnb'])
        b_pl = _pool_fwd(proj, p['plw'], p['plsc'])
        b_lru, hs = _lru_fwd(proj, p['lcw'], p['lcb'], p['wa'], p['ba'], p['wx'], p['bx'], p['lam'])
        branches = (b_hg, b_cv, b_pl, b_lru)
        if first:
            fulls, wf = arrived(l, ['w_branch', 'w_out'], b_lru)
        x1 = _merge_fwd(xc, branches, proj, p['gb'], wf['w_branch'], wf['w_out'], l)
        if first:
            fulls, wf = arrived(l, ['xa_wq', 'xa_wkv', 'xa_wo'], x1)
        memn = _norm_fwd(mems, p['memw'], "norm_memtok")
        kv = _mm(memn, wf['xa_wkv'], "nn", BF16, "kv_proj", layer=l)
        x2 = _attn_fwd(x1, p['nmem'], wf['xa_wq'], kv, wf['xa_wo'], l)
        if first:
            fulls, wf = arrived(l, ['ffn_w1', 'ffn_w2'], x2)
        x3 = _ffn_fwd(x2, p['nffn'], wf['ffn_w1'], wf['ffn_w2'], l, ts=1024)
        saved.append(dict(p=p, x=xc, h=h, proj=proj, st=st, hs=hs, branches=branches, x1=x1, memn=memn, kv=kv, x2=x2))
        xc = x3

    loss_blk, dx, dfinal = _final_loss(xc, row(final_norm_w), tgt)

    gs = {n: [None] * DEPTH for n in SMALL if n != 'final_norm_w'}
    dlb = [None] * DEPTH
    in_flight = [None] * DEPTH
    token = loss_blk
    for l in reversed(range(DEPTH)):
        sv = saved[l]
        p = sv['p']
        gb = {}
        dx2, gs['norm_ffn_w'][l], h3, da, r, dxb = _ffn_bwd(sv['x2'], dx, p['nffn'], wf['ffn_w1'], wf['ffn_w2'], l, token, ts=512)
        gb['ffn_w1'] = _mm(h3, da, "tn", BF16, "dw_ffn1", tm=1024)
        gb['ffn_w2'] = _mm(r, dxb, "tn", BF16, "dw_ffn2", tn=1024)
        dx1, gs['norm_mem_w'][l], h2, o, dq, dxb2, dk, dv = _attn_bwd(sv['x1'], dx2, p['nmem'], wf['xa_wq'], sv['kv'], wf['xa_wo'], l)
        gb['xa_wq'] = _mm(h2, dq, "tn", BF16, "dw_q")
        gb['xa_wo'] = _mm(o, dxb2, "tn", BF16, "dw_o")
        dkv = jnp.concatenate([dk, dv], axis=1)
        gb['xa_wkv'] = _mm(sv['memn'], dkv, "tn", BF16, "dw_kv")
        dmemn = _mm(dkv, wf['xa_wkv'], "nt", F32, "dmemn", layer=l)
        _, gs['mem_norm_w'][l] = _norm_bwd(mems, p['memw'], dmemn, None, "norm_memtok_bwd")
        db0, db1, db2, db3, dgp, dup, mg, dxb1, gs['gate_b'][l] = _merge_bwd(
            dx1, sv['branches'], sv['proj'], p['gb'], wf['w_branch'], wf['w_out'], l)
        gb['w_out'] = _mm(mg, dxb1, "tn", BF16, "dw_out")
        for kb in range(4):
            gb['w_branch', kb] = _mm(sv['branches'][kb], dup, "tn", BF16, "dw_branch", b_col0=kb * D_MODEL, n=D_MODEL, tn=512)
        dhg, dlb[l], gs['hg_norm_w'][l] = _hgrn_bwd(sv['proj'], db0, sv['st'], p['lb'], p['hgnw'])
        dcv, dcw, gs['cv_dw_b'][l], gs['cv_ln_w'][l], gs['cv_ln_b'][l] = _conv_bwd(sv['proj'], db1, p['cw'], p['cb'], p['lnw'], p['lnb'])
        gs['cv_dw_w'][l] = dcw[:CV_K]
        dpl, gs['pl_w'][l], gs['pl_scale'][l] = _pool_bwd(sv['proj'], db2, p['plw'], p['plsc'])
        dlru, dlcw, gs['lru_conv_b'][l], dwa, gs['lru_ba'][l], dwx, gs['lru_bx'][l], gs['lru_lambda'][l] = _lru_bwd(
            sv['proj'], sv['hs'], db3, p['lcw'], p['lcb'], p['wa'], p['ba'], p['wx'], p['bx'], p['lam'])
        gs['lru_conv_w'][l] = dlcw[:LRU_CONV]
        gs['lru_wa'][l], gs['lru_wx'][l] = _diag_blocks(dwa, LRU_HEADS), _diag_blocks(dwx, LRU_HEADS)
        dproj = jnp.concatenate([dhg, dcv, dpl, dlru, dgp], axis=1)
        gb['w_in'] = _mm(sv['h'], dproj, "tn", BF16, "dw_in", tm=1024)
        dh = _mm(dproj, wf['w_in'], "nt", F32, "dh_mix", tm=256, tn=512, layer=l)
        dx, gs['norm_mix_w'][l] = _norm_bwd(sv['x'], p['nmix'], dh, dx1, "norm_mix_bwd")
        if l:
            in_flight[l], token = _scatter_start([gb[key][None] for key in PIECES], PIECE_KINDS, dx, "scatter_start_%d" % l)
    grad_x = dx.reshape(x.shape)
    gs['hg_lb_raw'] = _lb_bwd(hg_lb_raw, jnp.concatenate(dlb, axis=0))

    def full_shape(n):
        return full_small[n].shape if n in SMALL_SPLIT else w[n].shape

    small_full = []
    for n in SMALL:
        g = gs[n] if n == 'hg_lb_raw' else dfinal if n == 'final_norm_w' else jnp.stack(gs[n])
        small_full.append(g.reshape(full_shape(n)))
    everyone = _gather_small(_pack(small_full + [loss_blk[0:1, 0:1]]), over_c=True)
    in_flight[0], token = _scatter_start([gb[key][None] for key in PIECES], PIECE_KINDS, everyone, "scatter_start_0")
    total = _sum_rows(everyone, token)
    parts = _unpack(total, [full_shape(n) for n in SMALL] + [(1,)])
    loss = parts[-1].reshape(())
    g_small = {}
    for n, g in zip(SMALL, parts[:-1]):
        if n in SMALL_SPLIT:
            width = w[n].shape[-1]
            g = lax.dynamic_slice_in_dim(g, jj * width, width, axis=g.ndim - 1)
        g_small[n] = g
    shapes = [w[n].shape for n in SMALL]
    upd = _adamw(_pack([w[n] for n in SMALL]), [_pack([g_small[n] for n in SMALL])], _pack([m1[n] for n in SMALL]),
                 _pack([v1[n] for n in SMALL]), "adamw_small")
    d_small, m_small, v_small = [dict(zip(SMALL, _unpack(u, shapes))) for u in upd]

    stacks = {n: None for n in big_names}
    for l in reversed(range(DEPTH)):
        s_send, s_recv, g_thru, lands = in_flight[l]
        done_before = upd[0] if stacks['ffn_w2'] is None else stacks['ffn_w2']
        g_thru, lands = _scatter_wait(s_send, s_recv, g_thru, lands, PIECE_KINDS, done_before, "scatter_wait_%d" % l)
        for key, g, r in zip(PIECES, g_thru, lands):
            n, kb = key if isinstance(key, tuple) else (key, None)
            per = 1 if kb is None else 4
            stacks[n] = _sum_partials_into(stacks[n], l * per + (kb or 0), DEPTH * per, g, r, BIG[n], jj1, "sum_" + n)
    partial = [stacks[n] for n in big_names]
    theirs = _sibling_swap(partial)
    g_big, d_big, m_big, v_big = {}, {}, {}, {}
    for n, pa, pb in zip(big_names, partial, theirs):
        c2 = lambda a: a.reshape(-1, a.shape[-1])
        out = _adamw(c2(w[n]), [c2(pa), c2(pb)], c2(m1[n]), c2(v1[n]), "adamw_" + n)
        g_big[n], d_big[n], m_big[n], v_big[n] = [o.reshape(w[n].shape) for o in out]

    pick = lambda small, big: [big[n] if n in BIG else small[n] for n in WEIGHTS]
    return (loss, grad_x, *pick(g_small, g_big), *pick(d_small, d_big), *pick(m_small, m_big), *pick(v_small, v_big))
```

```python
import functools
import math

import jax
import jax.numpy as jnp
from jax import lax
from jax.experimental import pallas as pl
from jax.experimental.pallas import tpu as pltpu

F32 = jnp.float32
BF16 = jnp.bfloat16
MESH = pl.DeviceIdType.MESH
ANY = pl.BlockSpec(memory_space=pl.ANY)

D_MODEL = 1024
DEPTH = 4
CHUNK = 64
SUB = 16
EPS = 1e-6
HG_HEADS, HG_D = 4, 128
CV_W, CV_K = 512, 31
CV_HALO = 32
POOL_WINDOWS = (2, 4, 8, 16)
POOL_HALO = 16
LRU_W, LRU_HEADS, LRU_HD, LRU_CONV = 512, 8, 64, 4
LRU_HALO = 8
LRU_C = 8.0
MIX_W = 4608
IN_W = 8704
XA_HEADS, XA_HD = 4, 256
D_FF = 4096
FF_CHUNK = 1024
ADAM_LR, ADAM_B1, ADAM_B2, ADAM_EPS, ADAM_WD, ADAM_STEP = 0.001, 0.9, 0.999, 1e-08, 0.01, 10
VMEM_LIMIT = 56 * 1024 * 1024
EXP_CLAMP = 80.0
HI = lax.Precision.HIGHEST


def _params(sem=None):
    return pltpu.CompilerParams(dimension_semantics=sem, vmem_limit_bytes=VMEM_LIMIT)


def _sigmoid(x):
    return 1.0 / (1.0 + jnp.exp(-x))


def _dsilu(x, s):
    return s * (1.0 + x * (1.0 - s))


_GELU_C = math.sqrt(2.0 / math.pi)


def _gelu_parts(x):
    t = jnp.tanh(_GELU_C * (x + 0.044715 * x * x * x))
    g = 0.5 * x * (1.0 + t)
    dg = 0.5 * (1.0 + t) + 0.5 * x * (1.0 - t * t) * _GELU_C * (1.0 + 3 * 0.044715 * x * x)
    return g, dg


def _dot(a, b, dims, precision=None):
    return lax.dot_general(a, b, (dims, ((), ())), precision=precision, preferred_element_type=F32)


def _nn(a, b, **k):
    return _dot(a, b, ((1,), (0,)), **k)


def _nt(a, b, **k):
    return _dot(a, b, ((1,), (1,)), **k)


def _tn(a, b, **k):
    return _dot(a, b, ((0,), (0,)), **k)


def _split(x):
    hi = x.astype(BF16)
    return hi, (x - hi.astype(F32)).astype(BF16)


def _nn3(a, b):
    (ah, al), (bh, bl) = _split(a), _split(b)
    return _nn(jnp.concatenate([ah, ah, al], axis=1), jnp.concatenate([bh, bl, bh], axis=0))


def _tn3(a, b):
    (ah, al), (bh, bl) = _split(a), _split(b)
    return _tn(jnp.concatenate([ah, ah, al], axis=0), jnp.concatenate([bh, bl, bh], axis=0))


def _rms_fwd(x, w):
    r = lax.rsqrt(jnp.mean(x * x, axis=-1, keepdims=True) + EPS)
    return x * r * w, r


def _rms_bwd(x, r, w, dy):
    xr = x * r
    g = dy * w
    dx = r * (g - xr * jnp.mean(g * xr, axis=-1, keepdims=True))
    return dx, jnp.sum(dy * xr, axis=0, keepdims=True)


def _lw(shape, index, layer):
    return pl.BlockSpec((None,) + tuple(shape), lambda *g: (layer,) + tuple(index(*g)))


def _mm(a, b, mode, out_dtype, name, tm=512, tn=512, b_col0=0, n=None, layer=None):
    bs = b.shape if layer is None else b.shape[1:]
    if mode == "nn":
        m, k = a.shape
        n = bs[1] if n is None else n
    elif mode == "nt":
        m, k = a.shape
        n = bs[0] if n is None else n
    else:
        k, m = a.shape
        n = bs[1] if n is None else n
    tm, tn = min(tm, m), min(tn, n)
    assert m % tm == 0 and n % tn == 0 and b_col0 % tn == 0
    off = b_col0 // tn

    def body(a_ref, b_ref, o_ref):
        av, bv = a_ref[...].astype(BF16), b_ref[...].astype(BF16)
        o_ref[...] = (_nn if mode == "nn" else _nt if mode == "nt" else _tn)(av, bv).astype(out_dtype)

    def bspec(shape, index):
        return pl.BlockSpec(shape, index) if layer is None else _lw(shape, index, layer)

    if mode == "tn":
        grid = (m // tm, n // tn)
        a_spec = pl.BlockSpec((k, tm), lambda i, j: (0, i))
        b_spec = bspec((k, tn), lambda i, j: (0, j + off))
        o_spec = pl.BlockSpec((tm, tn), lambda i, j: (i, j))
    else:
        grid = (n // tn, m // tm)
        a_spec = pl.BlockSpec((tm, k), lambda j, i: (i, 0))
        if mode == "nn":
            b_spec = bspec((k, tn), lambda j, i: (0, j + off))
        else:
            b_spec = bspec((tn, k), lambda j, i: (j + off, 0))
        o_spec = pl.BlockSpec((tm, tn), lambda j, i: (i, j))
    return pl.pallas_call(
        body, out_shape=jax.ShapeDtypeStruct((m, n), out_dtype), grid=grid,
        in_specs=[a_spec, b_spec], out_specs=o_spec, name=name,
        compiler_params=_params(("parallel", "parallel")),
    )(a, b)


def _norm_fwd(x, w, name, ts=512):
    s, d = x.shape
    ts = min(ts, s)

    def body(x_ref, w_ref, o_ref):
        o_ref[...] = _rms_fwd(x_ref[...], w_ref[...])[0].astype(BF16)

    return pl.pallas_call(
        body, out_shape=jax.ShapeDtypeStruct((s, d), BF16), grid=(s // ts,),
        in_specs=[pl.BlockSpec((ts, d), lambda i: (i, 0)), pl.BlockSpec((1, d), lambda i: (0, 0))],
        out_specs=pl.BlockSpec((ts, d), lambda i: (i, 0)), name=name, compiler_params=_params(("parallel",)),
    )(x, w)


def _norm_bwd(x, w, dy, dres, name, ts=512):
    s, d = x.shape
    ts = min(ts, s)
    with_res = dres is not None

    def body(*refs):
        if with_res:
            x_ref, w_ref, dy_ref, dres_ref, dx_ref, dw_ref = refs
        else:
            x_ref, w_ref, dy_ref, dx_ref, dw_ref = refs
        xv = x_ref[...]
        r = lax.rsqrt(jnp.mean(xv * xv, axis=-1, keepdims=True) + EPS)
        dx, dw = _rms_bwd(xv, r, w_ref[...], dy_ref[...])
        dx_ref[...] = dx + dres_ref[...] if with_res else dx

        @pl.when(pl.program_id(0) == 0)
        def _():
            dw_ref[...] = jnp.zeros_like(dw_ref)

        dw_ref[...] += dw

    row = pl.BlockSpec((ts, d), lambda i: (i, 0))
    vec = pl.BlockSpec((1, d), lambda i: (0, 0))
    return pl.pallas_call(
        body, out_shape=(jax.ShapeDtypeStruct((s, d), F32), jax.ShapeDtypeStruct((1, d), F32)), grid=(s // ts,),
        in_specs=[row, vec, row] + ([row] if with_res else []), out_specs=(row, vec), name=name,
        compiler_params=_params(("arbitrary",)),
    )(*([x, w, dy] + ([dres] if with_res else [])))


def _ffn_fwd(x, nw, w1, w2, layer, ts=256):
    s, d = x.shape
    ts = min(ts, s)
    nj = D_FF // FF_CHUNK

    def body(x_ref, nw_ref, w1_ref, w2_ref, o_ref, h_scr, acc):
        j = pl.program_id(1)

        @pl.when(j == 0)
        def _():
            h_scr[...] = _rms_fwd(x_ref[...], nw_ref[...])[0].astype(BF16)
            acc[...] = jnp.zeros_like(acc)

        a = _nn(h_scr[...], w1_ref[...])
        rl = jnp.maximum(a, 0.0)
        acc[...] += _nn((rl * rl).astype(BF16), w2_ref[...])

        @pl.when(j == nj - 1)
        def _():
            o_ref[...] = x_ref[...] + acc[...]

    row = pl.BlockSpec((ts, d), lambda i, j: (i, 0))
    return pl.pallas_call(
        body, out_shape=jax.ShapeDtypeStruct((s, d), F32), grid=(s // ts, nj),
        in_specs=[row, pl.BlockSpec((1, d), lambda i, j: (0, 0)),
                  _lw((d, FF_CHUNK), lambda i, j: (0, j), layer), _lw((FF_CHUNK, d), lambda i, j: (j, 0), layer)],
        out_specs=row, scratch_shapes=[pltpu.VMEM((ts, d), BF16), pltpu.VMEM((ts, d), F32)], name="ffn_fwd",
        compiler_params=_params(("parallel", "arbitrary")),
    )(x, nw, w1, w2)


def _ffn_bwd(x, dxo, nw, w1, w2, layer, after, ts=256):
    s, d = x.shape
    ts = min(ts, s)
    nj = D_FF // FF_CHUNK

    def body(x_ref, dxo_ref, nw_ref, w1_ref, w2_ref, after_ref, dx_ref, dnw_ref, h_ref, da_ref, r_ref, dxb_ref, dh):
        i, j = pl.program_id(0), pl.program_id(1)

        @pl.when(j == 0)
        def _():
            h_ref[...] = _rms_fwd(x_ref[...], nw_ref[...])[0].astype(BF16)
            dxb_ref[...] = dxo_ref[...].astype(BF16)
            dh[...] = jnp.zeros_like(dh)

        a = _nn(h_ref[...], w1_ref[...])
        rl = jnp.maximum(a, 0.0)
        r_ref[...] = (rl * rl).astype(BF16)
        da = (_nt(dxb_ref[...], w2_ref[...]) * (2.0 * rl)).astype(BF16)
        da_ref[...] = da
        dh[...] += _nt(da, w1_ref[...])

        @pl.when(jnp.logical_and(i == 0, j == 0))
        def _():
            dnw_ref[...] = jnp.zeros_like(dnw_ref)

        @pl.when(j == nj - 1)
        def _():
            xv = x_ref[...]
            r = lax.rsqrt(jnp.mean(xv * xv, axis=-1, keepdims=True) + EPS)
            dx, dw = _rms_bwd(xv, r, nw_ref[...], dh[...])
            dx_ref[...] = dxo_ref[...] + dx
            dnw_ref[...] += dw

    row = pl.BlockSpec((ts, d), lambda i, j: (i, 0))
    vec = pl.BlockSpec((1, d), lambda i, j: (0, 0))
    ffc = pl.BlockSpec((ts, FF_CHUNK), lambda i, j: (i, j))
    return pl.pallas_call(
        body,
        out_shape=(jax.ShapeDtypeStruct((s, d), F32), jax.ShapeDtypeStruct((1, d), F32), jax.ShapeDtypeStruct((s, d), BF16),
                   jax.ShapeDtypeStruct((s, D_FF), BF16), jax.ShapeDtypeStruct((s, D_FF), BF16), jax.ShapeDtypeStruct((s, d), BF16)),
        grid=(s // ts, nj),
        in_specs=[row, row, vec, _lw((d, FF_CHUNK), lambda i, j: (0, j), layer), _lw((FF_CHUNK, d), lambda i, j: (j, 0), layer), ANY],
        out_specs=(row, vec, row, ffc, ffc, row), scratch_shapes=[pltpu.VMEM((ts, d), F32)], name="ffn_bwd",
        compiler_params=_params(("arbitrary", "arbitrary")),
    )(x, dxo, nw, w1, w2, after)


def _attn_probs(q, k_ref):
    ps = []
    for hd in range(XA_HEADS):
        c = slice(hd * XA_HD, (hd + 1) * XA_HD)
        sc = _nt(q[:, c].astype(BF16), k_ref[:, c]) * (XA_HD ** -0.5)
        e = jnp.exp(sc - jnp.max(sc, axis=-1, keepdims=True))
        ps.append(e / jnp.sum(e, axis=-1, keepdims=True))
    return ps


def _attn_fwd(x, nw, wq, kv, wo, layer, ts=256):
    s, d = x.shape
    ts = min(ts, s)
    nm = kv.shape[0]

    def body(x_ref, nw_ref, wq_ref, k_ref, v_ref, wo_ref, o_ref):
        xv = x_ref[...]
        h = _rms_fwd(xv, nw_ref[...])[0].astype(BF16)
        q = _nn(h, wq_ref[...])
        ps = _attn_probs(q, k_ref)
        o = jnp.concatenate([_nn(ps[hd].astype(BF16), v_ref[:, hd * XA_HD:(hd + 1) * XA_HD]) for hd in range(XA_HEADS)], axis=1)
        o_ref[...] = xv + _nn(o.astype(BF16), wo_ref[...])

    row = pl.BlockSpec((ts, d), lambda i: (i, 0))
    full = lambda r, c: pl.BlockSpec((r, c), lambda i: (0, 0))
    wsp = _lw((d, d), lambda i: (0, 0), layer)
    return pl.pallas_call(
        body, out_shape=jax.ShapeDtypeStruct((s, d), F32), grid=(s // ts,),
        in_specs=[row, full(1, d), wsp, full(nm, d), pl.BlockSpec((nm, d), lambda i: (0, 1)), wsp], out_specs=row, name="attn_fwd",
        compiler_params=_params(("parallel",)),
    )(x, nw, wq, kv, kv, wo)


def _attn_bwd(x, dxo, nw, wq, kv, wo, layer, ts=256):
    s, d = x.shape
    ts = min(ts, s)
    nm = kv.shape[0]

    def body(x_ref, dxo_ref, nw_ref, wq_ref, k_ref, v_ref, wo_ref,
             dx_ref, dnw_ref, h_ref, o_ref, dq_ref, dxb_ref, dk_ref, dv_ref):
        xv = x_ref[...]
        hf, r = _rms_fwd(xv, nw_ref[...])
        h = hf.astype(BF16)
        h_ref[...] = h
        q = _nn(h, wq_ref[...])
        qb = q.astype(BF16)
        ps = _attn_probs(q, k_ref)
        dxb = dxo_ref[...].astype(BF16)
        dxb_ref[...] = dxb
        do = _nt(dxb, wo_ref[...])

        @pl.when(pl.program_id(0) == 0)
        def _():
            dnw_ref[...] = jnp.zeros_like(dnw_ref)
            dk_ref[...] = jnp.zeros_like(dk_ref)
            dv_ref[...] = jnp.zeros_like(dv_ref)

        dqs = []
        for hd in range(XA_HEADS):
            c = slice(hd * XA_HD, (hd + 1) * XA_HD)
            p = ps[hd]
            pb = p.astype(BF16)
            dob = do[:, c].astype(BF16)
            o_ref[:, c] = _nn(pb, v_ref[:, c]).astype(BF16)
            dp = _nt(dob, v_ref[:, c])
            ds = (p * (dp - jnp.sum(p * dp, axis=-1, keepdims=True)) * (XA_HD ** -0.5)).astype(BF16)
            dqs.append(_nn(ds, k_ref[:, c]))
            dk_ref[:, c] += _tn(ds, qb[:, c])
            dv_ref[:, c] += _tn(pb, dob)
        dq = jnp.concatenate(dqs, axis=1).astype(BF16)
        dq_ref[...] = dq
        dx, dw = _rms_bwd(xv, r, nw_ref[...], _nt(dq, wq_ref[...]))
        dx_ref[...] = dxo_ref[...] + dx
        dnw_ref[...] += dw

    row = pl.BlockSpec((ts, d), lambda i: (i, 0))
    full = lambda r, c: pl.BlockSpec((r, c), lambda i: (0, 0))
    sd = lambda dt: jax.ShapeDtypeStruct((s, d), dt)
    return pl.pallas_call(
        body,
        out_shape=(sd(F32), jax.ShapeDtypeStruct((1, d), F32), sd(BF16), sd(BF16), sd(BF16), sd(BF16),
                   jax.ShapeDtypeStruct((nm, d), F32), jax.ShapeDtypeStruct((nm, d), F32)),
        grid=(s // ts,),
        in_specs=[row, row, full(1, d), _lw((d, d), lambda i: (0, 0), layer), full(nm, d), pl.BlockSpec((nm, d), lambda i: (0, 1)),
                  _lw((d, d), lambda i: (0, 0), layer)],
        out_specs=(row, full(1, d), row, row, row, row, full(nm, d), full(nm, d)), name="attn_bwd",
        compiler_params=_params(("arbitrary",)),
    )(x, dxo, nw, wq, kv, kv, wo)


GATE_BLK0 = MIX_W // 512


def _merge_specs(ts, layer):
    row = pl.BlockSpec((ts, D_MODEL), lambda i: (i, 0))
    br = pl.BlockSpec((ts, 512), lambda i: (i, 0))
    gates = [pl.BlockSpec((ts, 512), functools.partial(lambda n, i: (i, GATE_BLK0 + n), n)) for n in range(8)]
    full = lambda *shape: pl.BlockSpec(shape, lambda i: (0,) * len(shape))
    weights = [full(4, D_MODEL), _lw((4, 512, D_MODEL), lambda i: (0, 0, 0), layer), _lw((D_MODEL, D_MODEL), lambda i: (0, 0), layer)]
    return row, br, gates, full, weights


def _merge_gates(gp_refs, gb_ref, kb):
    gp = jnp.concatenate([gp_refs[2 * kb][...], gp_refs[2 * kb + 1][...]], axis=1)
    return _sigmoid(gp + gb_ref[kb:kb + 1, :])


def _merge_fwd(x, branches, proj, gate_b, wb, wout, layer, ts=256):
    s, d = x.shape
    ts = min(ts, s)

    def body(x_ref, b0, b1, b2, b3, g0, g1, g2, g3, g4, g5, g6, g7, gb_ref, wb_ref, wo_ref, o_ref):
        brs, gps = (b0, b1, b2, b3), (g0, g1, g2, g3, g4, g5, g6, g7)
        merged = jnp.zeros((ts, d), F32)
        for kb in range(4):
            merged += _merge_gates(gps, gb_ref, kb) * _nn(brs[kb][...], wb_ref[kb])
        o_ref[...] = x_ref[...] + _nn(merged.astype(BF16), wo_ref[...])

    row, br, gates, full, weights = _merge_specs(ts, layer)
    return pl.pallas_call(
        body, out_shape=jax.ShapeDtypeStruct((s, d), F32), grid=(s // ts,),
        in_specs=[row, br, br, br, br] + gates + weights, out_specs=row, name="merge_fwd",
        compiler_params=_params(("parallel",)),
    )(x, *branches, *([proj] * 8), gate_b, wb, wout)


def _merge_bwd(dxo, branches, proj, gate_b, wb, wout, layer, after, ts=256):
    s, d = dxo.shape
    ts = min(ts, s)

    def body(dxo_ref, b0, b1, b2, b3, g0, g1, g2, g3, g4, g5, g6, g7, gb_ref, wb_ref, wo_ref, after_ref,
             db0, db1, db2, db3, dgp_ref, dup_ref, mg_ref, dxb_ref, dgb_ref):
        brs, gps, dbs = (b0, b1, b2, b3), (g0, g1, g2, g3, g4, g5, g6, g7), (db0, db1, db2, db3)
        dxb = dxo_ref[...].astype(BF16)
        dxb_ref[...] = dxb
        dm = _nt(dxb, wo_ref[...])

        @pl.when(pl.program_id(0) == 0)
        def _():
            dgb_ref[...] = jnp.zeros_like(dgb_ref)

        merged = jnp.zeros((ts, d), F32)
        for kb in range(4):
            c = slice(kb * d, (kb + 1) * d)
            g = _merge_gates(gps, gb_ref, kb)
            up = _nn(brs[kb][...], wb_ref[kb])
            merged += g * up
            dup = (dm * g).astype(BF16)
            dup_ref[:, c] = dup
            dgp = dm * up * g * (1.0 - g)
            dgp_ref[:, c] = dgp.astype(BF16)
            dgb_ref[kb:kb + 1, :] += jnp.sum(dgp, axis=0, keepdims=True)
            dbs[kb][...] = _nt(dup, wb_ref[kb])
        mg_ref[...] = merged.astype(BF16)

    row, br, gates, full, weights = _merge_specs(ts, layer)
    wide = pl.BlockSpec((ts, 4 * d), lambda i: (i, 0))
    sb = jax.ShapeDtypeStruct((s, 512), F32)
    return pl.pallas_call(
        body,
        out_shape=(sb, sb, sb, sb, jax.ShapeDtypeStruct((s, 4 * d), BF16), jax.ShapeDtypeStruct((s, 4 * d), BF16),
                   jax.ShapeDtypeStruct((s, d), BF16), jax.ShapeDtypeStruct((s, d), BF16), jax.ShapeDtypeStruct((4, d), F32)),
        grid=(s // ts,),
        in_specs=[row, br, br, br, br] + gates + weights + [ANY],
        out_specs=(br, br, br, br, wide, wide, row, row, full(4, d)), name="merge_bwd",
        compiler_params=_params(("arbitrary",)),
    )(dxo, *branches, *([proj] * 8), gate_b, wb, wout, after)


def _tri(n, upper=False):
    r = lax.broadcasted_iota(jnp.int32, (n, 3 * n), 0)
    c = lax.broadcasted_iota(jnp.int32, (n, 3 * n), 1) % n
    return jnp.where((c >= r) if upper else (c <= r), 1.0, 0.0).astype(BF16)


def _cum(tri3, x):
    hi = x.astype(BF16)
    r1 = x - hi.astype(F32)
    mid = r1.astype(BF16)
    lo = (r1 - mid.astype(F32)).astype(BF16)
    return _nn(tri3, jnp.concatenate([hi, mid, lo], axis=0))


def _hg_gates(hq, hf, lb):
    sg = _sigmoid(hf)
    fg = lb + (1.0 - lb) * sg
    sq = _sigmoid(hq)
    return sg, fg, 1.0 - fg, jnp.log(fg), hq * sq, sq


NSUB = CHUNK // SUB


def _hg_intra(qf, kk, b):
    row = lax.broadcasted_iota(jnp.int32, (CHUNK, 1), 0)
    refs = [b[i * SUB - 1:i * SUB, :] if i else jnp.zeros((1, b.shape[1]), F32) for i in range(NSUB)]
    mine = [jnp.logical_and(row >= i * SUB, row < (i + 1) * SUB) for i in range(NSUB)]
    ref_rows = refs[0]
    for i in range(1, NSUB):
        ref_rows = jnp.where(mine[i], refs[i], ref_rows)
    eq = jnp.exp(b - ref_rows)
    qt = qf * eq
    ek = jnp.concatenate([jnp.exp(jnp.minimum(r - b, EXP_CLAMP)) for r in refs], axis=1)
    kbig = jnp.concatenate([kk] * NSUB, axis=1) * ek
    qbig = jnp.concatenate([jnp.where(m, qt, 0.0) for m in mine], axis=1)
    return qt, qbig, kbig, eq, ek, mine


def _causal(n, upper=False):
    r, c = lax.broadcasted_iota(jnp.int32, (n, n), 0), lax.broadcasted_iota(jnp.int32, (n, n), 1)
    return (c >= r) if upper else (c <= r)


def _hg_chunk_fwd(qf, kk, b, v, st):
    parts = _hg_intra(qf, kk, b)
    att = jnp.where(_causal(CHUNK), _nt(parts[1].astype(BF16), parts[2].astype(BF16)), 0.0)
    qh = qf * jnp.exp(b)
    o = _nn(att.astype(BF16), v.astype(BF16)) + _nt(qh.astype(BF16), st.astype(BF16))
    bl = b[CHUNK - 1:CHUNK, :]
    kh = kk * jnp.exp(bl - b)
    return o, parts, att, qh, kh, jnp.exp(bl)


def _hgrn_fwd(proj, lb, nw, ts=256):
    s = proj.shape[0]
    ts = min(ts, s)
    nch = ts // CHUNK

    def body(q_ref, f_ref, v_ref, g_ref, lb_ref, nw_ref, o_ref, st_ref, st):
        @pl.when(pl.program_id(0) == 0)
        def _():
            st[...] = jnp.zeros_like(st)

        tri = _tri(CHUNK)

        def chunk(c, carry):
            rows = pl.ds(pl.multiple_of(c * CHUNK, CHUNK), CHUNK)
            _, _, kk, lf, qf, _ = _hg_gates(q_ref[rows, :], f_ref[rows, :], lb_ref[...])
            b = _cum(tri, lf)
            hv, hg = v_ref[rows, :], g_ref[rows, :]
            st_ref[c] = st[...]
            for h in range(HG_HEADS):
                cs = slice(h * HG_D, (h + 1) * HG_D)
                o, _, _, _, kh, ebl = _hg_chunk_fwd(qf[:, cs], kk[:, cs], b[:, cs], hv[:, cs], st[h])
                st[h] = st[h] * ebl + _tn(hv[:, cs].astype(BF16), kh.astype(BF16))
                on = _rms_fwd(o, nw_ref[...])[0]
                gh = hg[:, cs]
                o_ref[rows, cs] = (on * gh * _sigmoid(gh)).astype(BF16)
            return carry

        lax.fori_loop(0, nch, chunk, 0, unroll=4)

    col = lambda n: pl.BlockSpec((ts, 512), functools.partial(lambda n, i: (i, n), n))
    return pl.pallas_call(
        body,
        out_shape=(jax.ShapeDtypeStruct((s, 512), BF16), jax.ShapeDtypeStruct((s // CHUNK, HG_HEADS, HG_D, HG_D), F32)),
        grid=(s // ts,),
        in_specs=[col(0), col(1), col(2), col(3), pl.BlockSpec((1, 512), lambda i: (0, 0)), pl.BlockSpec((1, HG_D), lambda i: (0, 0))],
        out_specs=(pl.BlockSpec((ts, 512), lambda i: (i, 0)), pl.BlockSpec((nch, HG_HEADS, HG_D, HG_D), lambda i: (i, 0, 0, 0))),
        scratch_shapes=[pltpu.VMEM((HG_HEADS, HG_D, HG_D), F32)], name="hgrn_fwd",
        compiler_params=_params(("arbitrary",)),
    )(proj, proj, proj, proj, lb, nw)


def _hgrn_bwd(proj, dout, states, lb, nw, ts=256):
    s = proj.shape[0]
    ts = min(ts, s)
    nch = ts // CHUNK
    nt = s // ts

    def body(q_ref, f_ref, v_ref, g_ref, do_ref, st_ref, lb_ref, nw_ref, dp_ref, dlb_ref, dnw_ref, dst):
        @pl.when(pl.program_id(0) == 0)
        def _():
            dst[...] = jnp.zeros_like(dst)
            dlb_ref[...] = jnp.zeros_like(dlb_ref)
            dnw_ref[...] = jnp.zeros_like(dnw_ref)

        tri, triu = _tri(CHUNK), _tri(CHUNK, upper=True)
        last = lax.broadcasted_iota(jnp.int32, (CHUNK, HG_D), 0) == CHUNK - 1
        nwv = nw_ref[...]

        def chunk(cc, carry):
            c = nch - 1 - cc
            rows = pl.ds(pl.multiple_of(c * CHUNK, CHUNK), CHUNK)
            hq, hf, hv, hg = q_ref[rows, :], f_ref[rows, :], v_ref[rows, :], g_ref[rows, :]
            lbv = lb_ref[...]
            sg, fg, kk, lf, qf, sq = _hg_gates(hq, hf, lbv)
            b = _cum(tri, lf)
            dov = do_ref[rows, :]
            dqf_l, dkk_l, db_l, dv_l, dg_l = [], [], [], [], []
            for h in range(HG_HEADS):
                cs = slice(h * HG_D, (h + 1) * HG_D)
                stp = st_ref[c, h]
                bh, vh, gh = b[:, cs], hv[:, cs], hg[:, cs]
                o, parts, att, qh, kh, ebl = _hg_chunk_fwd(qf[:, cs], kk[:, cs], bh, vh, stp)
                sgg = _sigmoid(gh)
                on, r = _rms_fwd(o, nwv)
                d_on = dov[:, cs] * (gh * sgg)
                dg_l.append(dov[:, cs] * on * _dsilu(gh, sgg))
                do, dnw = _rms_bwd(o, r, nwv, d_on)
                dnw_ref[...] += dnw
                dob, vb = do.astype(BF16), vh.astype(BF16)
                dsth = dst[h]
                dstb = dsth.astype(BF16)
                dqh = _nn3(do, stp)
                dkh = _nn3(vh, dsth)
                dv = _nt(kh.astype(BF16), dstb)
                eb = jnp.exp(bh)
                ekl = jnp.exp(bh[CHUNK - 1:CHUNK, :] - bh)
                dqf, dkk = dqh * eb, dkh * ekl
                db = dqh * qh - dkh * kh
                dbl = jnp.sum(dkh * kh, axis=0, keepdims=True) + ebl * jnp.sum(dsth * stp, axis=0, keepdims=True)
                dst[h] = dsth * ebl + _tn(dob, qh.astype(BF16))
                qt, qbig, kbig, eq, ek, mine = parts
                da = jnp.where(_causal(CHUNK), _nt(dob, vb), 0.0)
                da_t = jnp.where(_causal(CHUNK, upper=True), _nt(vb, dob), 0.0)
                dv = dv + _tn(att.astype(BF16), dob)
                dqbig = _tn3(da_t, kbig)
                dkbig = _tn3(da, qbig)
                dkek, dkkb = dkbig * ek, dkbig * kbig
                dqt = jnp.zeros_like(qt)
                for i in range(NSUB):
                    bs = slice(i * HG_D, (i + 1) * HG_D)
                    dqt = dqt + jnp.where(mine[i], dqbig[:, bs], 0.0)
                    dkk = dkk + dkek[:, bs]
                    db = db - dkkb[:, bs]
                dqf = dqf + dqt * eq
                db = db + dqt * qt + jnp.where(last, dbl, 0.0)
                dqf_l.append(dqf); dkk_l.append(dkk); db_l.append(db); dv_l.append(dv)
            cat = lambda l: jnp.concatenate(l, axis=1)
            dlf = _cum(triu, cat(db_l))
            dfg = dlf / fg - cat(dkk_l)
            dlb_ref[...] += jnp.sum(dfg * (1.0 - sg), axis=0, keepdims=True)
            dp_ref[rows, 0:512] = (cat(dqf_l) * _dsilu(hq, sq)).astype(BF16)
            dp_ref[rows, 512:1024] = (dfg * (1.0 - lbv) * sg * (1.0 - sg)).astype(BF16)
            dp_ref[rows, 1024:1536] = cat(dv_l).astype(BF16)
            dp_ref[rows, 1536:2048] = cat(dg_l).astype(BF16)
            return carry

        lax.fori_loop(0, nch, chunk, 0, unroll=4)

    col = lambda n: pl.BlockSpec((ts, 512), functools.partial(lambda n, i: (nt - 1 - i, n), n))
    vec = lambda n: pl.BlockSpec((1, n), lambda i: (0, 0))
    return pl.pallas_call(
        body,
        out_shape=(jax.ShapeDtypeStruct((s, 2048), BF16), jax.ShapeDtypeStruct((1, 512), F32), jax.ShapeDtypeStruct((1, HG_D), F32)),
        grid=(nt,),
        in_specs=[col(0), col(1), col(2), col(3), pl.BlockSpec((ts, 512), lambda i: (nt - 1 - i, 0)),
                  pl.BlockSpec((nch, HG_HEADS, HG_D, HG_D), lambda i: (nt - 1 - i, 0, 0, 0)), vec(512), vec(HG_D)],
        out_specs=(pl.BlockSpec((ts, 2048), lambda i: (nt - 1 - i, 0)), vec(512), vec(HG_D)),
        scratch_shapes=[pltpu.VMEM((HG_HEADS, HG_D, HG_D), F32)], name="hgrn_bwd",
        compiler_params=_params(("arbitrary",)),
    )(proj, proj, proj, proj, dout, states, lb, nw)


CV_BLK = 2048 // 512


def _halo_before(ts, halo, colblk):
    return pl.BlockSpec((halo, 512), functools.partial(lambda cb, i: (jnp.maximum(i * (ts // halo) - 1, 0), cb), colblk))


def _cv_front(a_ref, g_ref, ah_ref, gh_ref, ext, first):
    a, sg = a_ref[...], _sigmoid(g_ref[...])
    zh = ah_ref[...] * _sigmoid(gh_ref[...])
    ext[0:CV_HALO, :] = jnp.where(first, 0.0, zh)
    ext[CV_HALO:, :] = a * sg
    return a, sg


def _windows(ref, base, ntaps, ts):
    out = []
    for phase in range(8):
        taps = [j for j in range(ntaps) if (base + j) % 8 == phase]
        if taps:
            span = max(base + j - phase for j in taps)
            big = ref[pl.ds(phase, ts + span), :]
            out += [(j, big[base + j - phase:base + j - phase + ts]) for j in taps]
    return out


def _cv_conv_ln(ext, w_ref, b_ref, ts):
    y = jnp.zeros((ts, CV_W), F32) + b_ref[...]
    for j, win in _windows(ext, CV_HALO - (CV_K - 1), CV_K, ts):
        y = y + w_ref[j:j + 1, :] * win
    mu = jnp.mean(y, axis=-1, keepdims=True)
    yc = y - mu
    r = lax.rsqrt(jnp.mean(yc * yc, axis=-1, keepdims=True) + EPS)
    return yc * r, r


def _conv_fwd(proj, w, b, lnw, lnb, ts=256):
    s = proj.shape[0]
    ts = min(ts, s)

    def body(a_ref, g_ref, ah_ref, gh_ref, w_ref, b_ref, lnw_ref, lnb_ref, o_ref, ext):
        _cv_front(a_ref, g_ref, ah_ref, gh_ref, ext, pl.program_id(0) == 0)
        yh, _ = _cv_conv_ln(ext, w_ref, b_ref, ts)
        yn = yh * lnw_ref[...] + lnb_ref[...]
        o_ref[...] = (yn * _sigmoid(yn)).astype(BF16)

    col = lambda n: pl.BlockSpec((ts, 512), functools.partial(lambda n, i: (i, n), n))
    vec = pl.BlockSpec((1, CV_W), lambda i: (0, 0))
    return pl.pallas_call(
        body, out_shape=jax.ShapeDtypeStruct((s, CV_W), BF16), grid=(s // ts,),
        in_specs=[col(CV_BLK), col(CV_BLK + 1), _halo_before(ts, CV_HALO, CV_BLK), _halo_before(ts, CV_HALO, CV_BLK + 1),
                  pl.BlockSpec((32, CV_W), lambda i: (0, 0)), vec, vec, vec],
        out_specs=pl.BlockSpec((ts, CV_W), lambda i: (i, 0)), scratch_shapes=[pltpu.VMEM((ts + CV_HALO, CV_W), F32)],
        name="conv_fwd", compiler_params=_params(("parallel",)),
    )(proj, proj, proj, proj, w, b, lnw, lnb)


def _conv_bwd(proj, dout, w, b, lnw, lnb, ts=256):
    s = proj.shape[0]
    ts = min(ts, s)
    nt = s // ts

    def body(a_ref, g_ref, ah_ref, gh_ref, do_ref, w_ref, b_ref, lnw_ref, lnb_ref,
             du_ref, dw_ref, db_ref, dlnw_ref, dlnb_ref, ext, dyext, carry):
        i = pl.program_id(0)

        @pl.when(i == 0)
        def _():
            carry[...] = jnp.zeros_like(carry)
            for ref in (dw_ref, db_ref, dlnw_ref, dlnb_ref):
                ref[...] = jnp.zeros_like(ref)

        a, sg = _cv_front(a_ref, g_ref, ah_ref, gh_ref, ext, i == nt - 1)
        yh, r = _cv_conv_ln(ext, w_ref, b_ref, ts)
        yn = yh * lnw_ref[...] + lnb_ref[...]
        dyn = do_ref[...] * _dsilu(yn, _sigmoid(yn))
        dlnw_ref[...] += jnp.sum(dyn * yh, axis=0, keepdims=True)
        dlnb_ref[...] += jnp.sum(dyn, axis=0, keepdims=True)
        gl = dyn * lnw_ref[...]
        dy = r * (gl - jnp.mean(gl, axis=-1, keepdims=True) - yh * jnp.mean(gl * yh, axis=-1, keepdims=True))
        db_ref[...] += jnp.sum(dy, axis=0, keepdims=True)
        dyext[0:ts, :] = dy
        dyext[ts:, :] = carry[...]
        carry[...] = dy[0:CV_HALO, :]
        dz = jnp.zeros((ts, CV_W), F32)
        for j, win in _windows(ext, CV_HALO - (CV_K - 1), CV_K, ts):
            dw_ref[j:j + 1, :] += jnp.sum(dy * win, axis=0, keepdims=True)
        for j, win in _windows(dyext, 0, CV_K, ts):
            dz = dz + w_ref[CV_K - 1 - j:CV_K - j, :] * win
        du_ref[:, 0:CV_W] = (dz * sg).astype(BF16)
        du_ref[:, CV_W:] = (dz * a * sg * (1.0 - sg)).astype(BF16)

    rev = lambda n: pl.BlockSpec((ts, 512), functools.partial(lambda n, i: (nt - 1 - i, n), n))
    halo = lambda n: pl.BlockSpec((CV_HALO, 512), functools.partial(
        lambda n, i: (jnp.maximum((nt - 1 - i) * (ts // CV_HALO) - 1, 0), n), n))
    vec = pl.BlockSpec((1, CV_W), lambda i: (0, 0))
    wsp = pl.BlockSpec((32, CV_W), lambda i: (0, 0))
    v1 = jax.ShapeDtypeStruct((1, CV_W), F32)
    return pl.pallas_call(
        body, out_shape=(jax.ShapeDtypeStruct((s, 2 * CV_W), BF16), jax.ShapeDtypeStruct((32, CV_W), F32), v1, v1, v1),
        grid=(nt,),
        in_specs=[rev(CV_BLK), rev(CV_BLK + 1), halo(CV_BLK), halo(CV_BLK + 1), rev(0), wsp, vec, vec, vec],
        out_specs=(pl.BlockSpec((ts, 2 * CV_W), lambda i: (nt - 1 - i, 0)), wsp, vec, vec, vec),
        scratch_shapes=[pltpu.VMEM((ts + CV_HALO, CV_W), F32), pltpu.VMEM((ts + CV_HALO, CV_W), F32), pltpu.VMEM((CV_HALO, CV_W), F32)],
        name="conv_bwd", compiler_params=_params(("arbitrary",)),
    )(proj, proj, proj, proj, dout, w, b, lnw, lnb)


PL_BLK = 3072 // 512


def _pool_windows(ext, t0, ts):
    n = ext.shape[0]
    t = t0 + lax.broadcasted_iota(jnp.int32, (ts, 1), 0)
    out = []
    for g, wdw in enumerate(POOL_WINDOWS):
        e = ext[:, g * 128:(g + 1) * 128]
        acc, k = e, 1
        while k < wdw:
            acc = acc + pltpu.roll(acc, k, 0)
            k *= 2
        cnt = jnp.minimum(t + 1, wdw).astype(F32)
        out.append(acc[POOL_HALO:] / cnt - e[POOL_HALO:])
    return out


def _pool_fwd(proj, w, sc, ts=256):
    s = proj.shape[0]
    ts = min(ts, s)

    def body(u_ref, uh_ref, w_ref, sc_ref, o_ref):
        i = pl.program_id(0)
        ext = jnp.concatenate([jnp.where(i == 0, 0.0, uh_ref[...]), u_ref[...]], axis=0)
        ps = _pool_windows(ext, i * ts, ts)
        y = jnp.concatenate([_nn(ps[g].astype(BF16), w_ref[g].astype(BF16)) for g in range(4)], axis=1)
        o_ref[...] = (y * sc_ref[...]).astype(BF16)

    return pl.pallas_call(
        body, out_shape=jax.ShapeDtypeStruct((s, 512), BF16), grid=(s // ts,),
        in_specs=[pl.BlockSpec((ts, 512), lambda i: (i, PL_BLK)), _halo_before(ts, POOL_HALO, PL_BLK),
                  pl.BlockSpec((4, 128, 128), lambda i: (0, 0, 0)), pl.BlockSpec((1, 512), lambda i: (0, 0))],
        out_specs=pl.BlockSpec((ts, 512), lambda i: (i, 0)), name="pool_fwd", compiler_params=_params(("parallel",)),
    )(proj, proj, w, sc)


def _pool_bwd(proj, dout, w, sc, ts=256):
    s = proj.shape[0]
    ts = min(ts, s)
    nt = s // ts
    n = ts + POOL_HALO

    def body(u_ref, uh_ref, do_ref, doh_ref, w_ref, sc_ref, du_ref, dw_ref, dsc_ref):
        i = pl.program_id(0)

        @pl.when(i == 0)
        def _():
            dw_ref[...] = jnp.zeros_like(dw_ref)
            dsc_ref[...] = jnp.zeros_like(dsc_ref)

        ext = jnp.concatenate([jnp.where(i == 0, 0.0, uh_ref[...]), u_ref[...]], axis=0)
        ps = _pool_windows(ext, i * ts, ts)
        dov = do_ref[...]
        dyext = jnp.concatenate([dov, jnp.where(i == nt - 1, 0.0, doh_ref[...])], axis=0) * sc_ref[...]
        t = i * ts + lax.broadcasted_iota(jnp.int32, (n, 1), 0)
        row = lax.broadcasted_iota(jnp.int32, (n, 1), 0)
        dus = []
        for g, wdw in enumerate(POOL_WINDOWS):
            cs = slice(g * 128, (g + 1) * 128)
            wg, pb = w_ref[g].astype(BF16), ps[g].astype(BF16)
            dsc_ref[:, cs] += jnp.sum(dov[:, cs] * _nn(pb, wg), axis=0, keepdims=True)
            dyg = dyext[:, cs].astype(BF16)
            dw_ref[g] += _tn(pb, dyg[0:ts])
            dp = _nt(dyg, wg)
            acc, k = dp / jnp.minimum(t + 1, wdw).astype(F32), 1
            while k < wdw:
                acc = acc + jnp.where(row < n - k, pltpu.roll(acc, n - k, 0), 0.0)
                k *= 2
            dus.append(acc[0:ts] - dp[0:ts])
        du_ref[...] = jnp.concatenate(dus, axis=1).astype(BF16)

    tile = lambda cb: pl.BlockSpec((ts, 512), functools.partial(lambda cb, i: (i, cb), cb))
    after = pl.BlockSpec((POOL_HALO, 512), lambda i: (jnp.minimum((i + 1) * (ts // POOL_HALO), s // POOL_HALO - 1), 0))
    wsp, vec = pl.BlockSpec((4, 128, 128), lambda i: (0, 0, 0)), pl.BlockSpec((1, 512), lambda i: (0, 0))
    return pl.pallas_call(
        body, out_shape=(jax.ShapeDtypeStruct((s, 512), BF16), jax.ShapeDtypeStruct((4, 128, 128), F32), jax.ShapeDtypeStruct((1, 512), F32)),
        grid=(nt,),
        in_specs=[tile(PL_BLK), _halo_before(ts, POOL_HALO, PL_BLK), tile(0), after, wsp, vec],
        out_specs=(tile(0), wsp, vec), name="pool_bwd", compiler_params=_params(("arbitrary",)),
    )(proj, proj, dout, dout, w, sc)


LX_BLK, LY_BLK = 3584 // 512, 4096 // 512
LRU_OFF = LRU_HALO - (LRU_CONV - 1)


def _scan_fwd(a, b):
    n = a.shape[0]
    row = lax.broadcasted_iota(jnp.int32, (n, 1), 0)
    k = 1
    while k < n:
        m = row >= k
        b = jnp.where(m, a * pltpu.roll(b, k, 0) + b, b)
        a = jnp.where(m, a * pltpu.roll(a, k, 0), a)
        k *= 2
    return a, b


def _scan_rev(a, b):
    n = a.shape[0]
    row = lax.broadcasted_iota(jnp.int32, (n, 1), 0)
    k = 1
    while k < n:
        m = row < n - k
        b = jnp.where(m, a * pltpu.roll(b, n - k, 0) + b, b)
        a = jnp.where(m, a * pltpu.roll(a, n - k, 0), a)
        k *= 2
    return b


def _lru_gates(x_ref, xh_ref, ext, first, cw_ref, cb_ref, wa_ref, ba_ref, wx_ref, bx_ref, lam_ref, ts):
    ext[0:LRU_HALO, :] = jnp.where(first, 0.0, xh_ref[...])
    ext[LRU_HALO:, :] = x_ref[...]
    xc = jnp.zeros((ts, LRU_W), F32) + cb_ref[...]
    for j in range(LRU_CONV):
        xc = xc + cw_ref[j:j + 1, :] * ext[pl.ds(LRU_OFF + j, ts), :]
    xb = xc.astype(BF16)
    r = _sigmoid(_nn(xb, wa_ref[...]) + ba_ref[...])
    ig = _sigmoid(_nn(xb, wx_ref[...]) + bx_ref[...])
    nl = -lam_ref[...]
    sp = jnp.maximum(nl, 0.0) + jnp.log(1.0 + jnp.exp(-jnp.abs(nl)))
    la = -LRU_C * r * sp
    a = jnp.exp(la)
    z = 2.0 * la
    em = jnp.where(z > -0.1, -z * (1.0 + z * 0.5 * (1.0 + z * (1.0 / 3) * (1.0 + z * 0.25 * (1.0 + z * 0.2)))), 1.0 - a * a)
    return xc, xb, r, ig, sp, a, jnp.sqrt(em)


def _lru_fwd(proj, cw, cb, wa, ba, wx, bx, lam, ts=256):
    s = proj.shape[0]
    ts = min(ts, s)

    def body(x_ref, xh_ref, y_ref, cw_ref, cb_ref, wa_ref, ba_ref, wx_ref, bx_ref, lam_ref, o_ref, h_ref, ext, hc):
        i = pl.program_id(0)

        @pl.when(i == 0)
        def _():
            hc[...] = jnp.zeros_like(hc)

        xc, _, _, ig, _, a, mult = _lru_gates(x_ref, xh_ref, ext, i == 0, cw_ref, cb_ref, wa_ref, ba_ref, wx_ref, bx_ref, lam_ref, ts)
        acum, h0 = _scan_fwd(a, mult * ig * xc)
        h = h0 + acum * hc[0:1, :]
        hc[...] = jnp.broadcast_to(h[ts - 1:ts, :], hc.shape)
        h_ref[...] = h
        o_ref[...] = (h * _gelu_parts(y_ref[...])[0]).astype(BF16)

    tile = lambda cb_: pl.BlockSpec((ts, 512), functools.partial(lambda c, i: (i, c), cb_))
    vec = pl.BlockSpec((1, LRU_W), lambda i: (0, 0))
    mat = pl.BlockSpec((LRU_W, LRU_W), lambda i: (0, 0))
    return pl.pallas_call(
        body, out_shape=(jax.ShapeDtypeStruct((s, LRU_W), BF16), jax.ShapeDtypeStruct((s, LRU_W), F32)), grid=(s // ts,),
        in_specs=[tile(LX_BLK), _halo_before(ts, LRU_HALO, LX_BLK), tile(LY_BLK), pl.BlockSpec((8, LRU_W), lambda i: (0, 0)),
                  vec, mat, vec, mat, vec, vec],
        out_specs=(tile(0), tile(0)), scratch_shapes=[pltpu.VMEM((ts + LRU_HALO, LRU_W), F32), pltpu.VMEM((8, LRU_W), F32)],
        name="lru_fwd", compiler_params=_params(("arbitrary",)),
    )(proj, proj, proj, cw, cb, wa, ba, wx, bx, lam)


def _lru_bwd(proj, hs, dout, cw, cb, wa, ba, wx, bx, lam, ts=256):
    s = proj.shape[0]
    ts = min(ts, s)
    nt = s // ts

    def body(x_ref, xh_ref, y_ref, h_ref, hh_ref, do_ref, cw_ref, cb_ref, wa_ref, ba_ref, wx_ref, bx_ref, lam_ref,
             dxy_ref, dcw_ref, dcb_ref, dwa_ref, dba_ref, dwx_ref, dbx_ref, dlam_ref, ext, dext, cg, cd):
        i = pl.program_id(0)
        first_tile = i == nt - 1

        @pl.when(i == 0)
        def _():
            cg[...] = jnp.zeros_like(cg)
            cd[...] = jnp.zeros_like(cd)
            for ref in (dcw_ref, dcb_ref, dwa_ref, dba_ref, dwx_ref, dbx_ref, dlam_ref):
                ref[...] = jnp.zeros_like(ref)

        xc, xb, r, ig, sp, a, mult = _lru_gates(x_ref, xh_ref, ext, first_tile, cw_ref, cb_ref, wa_ref, ba_ref, wx_ref, bx_ref, lam_ref, ts)
        row = lax.broadcasted_iota(jnp.int32, (ts, 1), 0)
        h, dov = h_ref[...], do_ref[...]
        gel, dgel = _gelu_parts(y_ref[...])
        dxy_ref[:, LRU_W:] = (dov * h * dgel).astype(BF16)
        alpha = jnp.where(row < ts - 1, pltpu.roll(a, ts - 1, 0), 0.0)
        g = _scan_rev(alpha, dov * gel + jnp.where(row == ts - 1, cg[0:1, :], 0.0))
        cg[...] = jnp.broadcast_to(a[0:1, :] * g[0:1, :], cg.shape)
        hprev = jnp.where(row == 0, jnp.where(first_tile, 0.0, hh_ref[LRU_HALO - 1:LRU_HALO, :]), pltpu.roll(h, 1, 0))
        dla = g * hprev * a - g * ig * xc * (a * a) / mult
        dpr = dla * (-LRU_C * sp) * r * (1.0 - r)
        dpi = g * mult * xc * ig * (1.0 - ig)
        dprb, dpib = dpr.astype(BF16), dpi.astype(BF16)
        dxc = g * mult * ig + _nt(dprb, wa_ref[...]) + _nt(dpib, wx_ref[...])
        dlam_ref[...] += jnp.sum(dla * (-LRU_C * r), axis=0, keepdims=True) * (-_sigmoid(-lam_ref[...]))
        dwa_ref[...] += _tn(xb, dprb)
        dwx_ref[...] += _tn(xb, dpib)
        dba_ref[...] += jnp.sum(dpr, axis=0, keepdims=True)
        dbx_ref[...] += jnp.sum(dpi, axis=0, keepdims=True)
        dcb_ref[...] += jnp.sum(dxc, axis=0, keepdims=True)
        dext[0:ts, :] = dxc
        dext[ts:, :] = cd[...]
        cd[...] = dxc[0:LRU_HALO, :]
        dx = jnp.zeros((ts, LRU_W), F32)
        for j in range(LRU_CONV):
            dcw_ref[j:j + 1, :] += jnp.sum(dxc * ext[pl.ds(LRU_OFF + j, ts), :], axis=0, keepdims=True)
            dx = dx + cw_ref[j:j + 1, :] * dext[pl.ds(LRU_CONV - 1 - j, ts), :]
        dxy_ref[:, 0:LRU_W] = dx.astype(BF16)

    rev = lambda c: pl.BlockSpec((ts, 512), functools.partial(lambda c, i: (nt - 1 - i, c), c))
    halo = lambda c: pl.BlockSpec((LRU_HALO, 512), functools.partial(
        lambda c, i: (jnp.maximum((nt - 1 - i) * (ts // LRU_HALO) - 1, 0), c), c))
    vec = pl.BlockSpec((1, LRU_W), lambda i: (0, 0))
    mat = pl.BlockSpec((LRU_W, LRU_W), lambda i: (0, 0))
    cws = pl.BlockSpec((8, LRU_W), lambda i: (0, 0))
    v1, m1 = jax.ShapeDtypeStruct((1, LRU_W), F32), jax.ShapeDtypeStruct((LRU_W, LRU_W), F32)
    return pl.pallas_call(
        body, out_shape=(jax.ShapeDtypeStruct((s, 2 * LRU_W), BF16), jax.ShapeDtypeStruct((8, LRU_W), F32), v1, m1, v1, m1, v1, v1),
        grid=(nt,),
        in_specs=[rev(LX_BLK), halo(LX_BLK), rev(LY_BLK), rev(0), halo(0), rev(0), cws, vec, mat, vec, mat, vec, vec],
        out_specs=(pl.BlockSpec((ts, 2 * LRU_W), lambda i: (nt - 1 - i, 0)), cws, vec, mat, vec, mat, vec, vec),
        scratch_shapes=[pltpu.VMEM((ts + LRU_HALO, LRU_W), F32), pltpu.VMEM((ts + LRU_HALO, LRU_W), F32),
                        pltpu.VMEM((8, LRU_W), F32), pltpu.VMEM((LRU_HALO, LRU_W), F32)],
        name="lru_bwd", compiler_params=_params(("arbitrary",)),
    )(proj, proj, proj, hs, hs, dout, cw, cb, wa, ba, wx, bx, lam)


def _final_loss(x, fw, tgt, ts=512):
    s, d = x.shape
    ts = min(ts, s)

    def body(x_ref, w_ref, t_ref, loss_ref, dx_ref, dw_ref):
        @pl.when(pl.program_id(0) == 0)
        def _():
            loss_ref[...] = jnp.zeros_like(loss_ref)
            dw_ref[...] = jnp.zeros_like(dw_ref)

        xv = x_ref[...]
        y, r = _rms_fwd(xv, w_ref[...])
        err = y - t_ref[...]
        loss_ref[...] += 0.5 * jnp.sum(jnp.mean(err * err, axis=-1, keepdims=True), axis=0, keepdims=True)
        dx, dw = _rms_bwd(xv, r, w_ref[...], err * (1.0 / d))
        dx_ref[...] = dx
        dw_ref[...] += dw

    row = pl.BlockSpec((ts, d), lambda i: (i, 0))
    vec = pl.BlockSpec((1, d), lambda i: (0, 0))
    return pl.pallas_call(
        body, out_shape=(jax.ShapeDtypeStruct((8, 128), F32), jax.ShapeDtypeStruct((s, d), F32), jax.ShapeDtypeStruct((1, d), F32)),
        grid=(s // ts,), in_specs=[row, vec, row], out_specs=(pl.BlockSpec((8, 128), lambda i: (0, 0)), row, vec),
        name="final_loss", compiler_params=_params(("arbitrary",)),
    )(x, fw, tgt)


def _lb_softmax(raw_ref):
    raw = raw_ref[...]
    e = jnp.exp(raw - jnp.max(raw, axis=0, keepdims=True))
    return e / jnp.sum(e, axis=0, keepdims=True)


def _lb_fwd(raw):
    def body(raw_ref, o_ref):
        sm = _lb_softmax(raw_ref)
        acc = jnp.zeros((1, sm.shape[1]), F32)
        o_ref[0:1, :] = acc
        for l in range(1, DEPTH):
            acc = acc + sm[l:l + 1, :]
            o_ref[l:l + 1, :] = acc

    return pl.pallas_call(body, out_shape=jax.ShapeDtypeStruct(raw.shape, F32), name="lb_fwd")(raw)


def _lb_bwd(raw, dlb):
    def body(raw_ref, d_ref, o_ref):
        sm = _lb_softmax(raw_ref)
        dlbv = d_ref[...]
        dsm, acc = [None] * DEPTH, jnp.zeros((1, sm.shape[1]), F32)
        for l in range(DEPTH - 1, 0, -1):
            acc = acc + dlbv[l:l + 1, :]
            dsm[l] = acc
        dsm[0] = jnp.zeros_like(acc)
        dsm = jnp.concatenate(dsm, axis=0)
        o_ref[...] = sm * (dsm - jnp.sum(sm * dsm, axis=0, keepdims=True))

    return pl.pallas_call(body, out_shape=jax.ShapeDtypeStruct(raw.shape, F32), name="lb_bwd")(raw, dlb)


def _adam_math(w, g, m, v):
    m = ADAM_B1 * m + (1.0 - ADAM_B1) * g
    v = ADAM_B2 * v + (1.0 - ADAM_B2) * (g * g)
    m_hat = m / (1.0 - ADAM_B1 ** ADAM_STEP)
    v_hat = v / (1.0 - ADAM_B2 ** ADAM_STEP)
    return -ADAM_LR * (m_hat / (jnp.sqrt(v_hat) + ADAM_EPS) + ADAM_WD * w), m, v


def _adamw(w, gs, m, v, name, tr=128):
    r, c = w.shape
    tr = min(tr, r)
    ng = len(gs)

    def body(*refs):
        w_ref, g_refs, m_ref, v_ref = refs[0], refs[1:1 + ng], refs[1 + ng], refs[2 + ng]
        outs = refs[3 + ng:]
        g = g_refs[0][...]
        if ng == 2:
            g = g + g_refs[1][...]
            outs[0][...] = g
            outs = outs[1:]
        for o, val in zip(outs, _adam_math(w_ref[...], g, m_ref[...], v_ref[...])):
            o[...] = val

    blk = pl.BlockSpec((tr, c), lambda i: (i, 0))
    sd = jax.ShapeDtypeStruct((r, c), F32)
    nout = 3 + (ng == 2)
    return pl.pallas_call(
        body, out_shape=(sd,) * nout, grid=(r // tr,), in_specs=[blk] * (3 + ng), out_specs=(blk,) * nout, name=name,
        compiler_params=_params(("parallel",)),
    )(w, *gs, m, v)


def _cast_into_full(w, kind, jj, name, tr=256):
    l, r, c = w.shape
    tr = min(tr, r)

    def body(j_ref, w_ref, o_ref):
        o_ref[...] = w_ref[...].astype(BF16)

    if kind == "col":
        full, dst = (l, r, 4 * c), pl.BlockSpec((None, tr, c), lambda a, b, j: (a, b, j[0]))
    else:
        full, dst = (l, 4 * r, c), pl.BlockSpec((None, tr, c), lambda a, b, j: (a, j[0] * (r // tr) + b, 0))
    return pl.pallas_call(
        body, out_shape=jax.ShapeDtypeStruct(full, BF16),
        grid_spec=pltpu.PrefetchScalarGridSpec(
            num_scalar_prefetch=1, grid=(l, r // tr), in_specs=[pl.BlockSpec((None, tr, c), lambda a, b, j: (a, b, 0))], out_specs=dst),
        name=name, compiler_params=_params(("parallel", "parallel")),
    )(jj, w)


def _place():
    return lax.axis_index("x"), lax.axis_index("y"), lax.axis_index("c")


def _other_chips(x, y):
    return [(1 - x, y), (x, 1 - y), (1 - x, 1 - y)]


def _slab(ref, kind, jj):
    if kind == "col":
        c = ref.shape[2] // 4
        return ref.at[:, :, pl.ds(jj * c, c)]
    r = ref.shape[1] // 4
    return ref.at[:, pl.ds(jj * r, r), :]


HBM = pl.BlockSpec(memory_space=pltpu.HBM)
SEM = pl.BlockSpec(memory_space=pltpu.SEMAPHORE)
EFFECT = pltpu.SideEffectType.DATAFLOW_SIDE_EFFECTING


def _in_hbm(a):
    return pltpu.with_memory_space_constraint(a, pltpu.HBM)


def _thru(arrs):
    return [pltpu.HBM(a.shape, a.dtype) for a in arrs]


def _half_slab(ref, kind, group, jj, half):
    per = ref.shape[0] // DEPTH
    layers = pl.ds(group * per, per)
    if kind == "col":
        r, c = ref.shape[1] // 2, ref.shape[2] // 4
        return ref.at[layers, pl.ds(half * r, r), pl.ds(jj * c, c)]
    r = ref.shape[1] // 8
    return ref.at[layers, pl.ds((2 * jj + half) * r, r), :]


def _gather_copy(fulls, kinds, send_sems, recv_sems, group, t, k, landing):
    x, y, c = _place()
    chip = _other_chips(x, y)[k]
    idx = (group * len(fulls) + t) * 3 + k
    return pltpu.make_async_remote_copy(
        src_ref=_half_slab(fulls[t], kinds[t], group, 2 * x + y, c), dst_ref=_half_slab(fulls[t], kinds[t], group, landing, c),
        send_sem=send_sems.at[idx], recv_sem=recv_sems.at[idx], device_id=(chip[0], chip[1], c), device_id_type=MESH)


def _fill_copy(fulls, kinds, send_sems, recv_sems, group, which, t, k, half):
    x, y, c = _place()
    chip = _other_chips(x, y)[k]
    idx = which.index(t) * 3 + k
    jj = 2 * chip[0] + chip[1]
    return pltpu.make_async_remote_copy(
        src_ref=_half_slab(fulls[t], kinds[t], group, jj, c), dst_ref=_half_slab(fulls[t], kinds[t], group, jj, half),
        send_sem=send_sems.at[idx], recv_sem=recv_sems.at[idx], device_id=(x, y, 1 - c), device_id_type=MESH)


def _fill_start(group, which, fulls, kinds):
    nt = len(fulls)
    ncp = 3 * len(which)

    def body(*refs):
        ins, send_sems, recv_sems = refs[:nt], refs[nt], refs[nt + 1]
        _, _, c = _place()
        for t in which:
            for k in range(3):
                _fill_copy(ins, kinds, send_sems, recv_sems, group, which, t, k, c).start()

    out = pl.pallas_call(
        body, out_shape=(pltpu.SemaphoreType.DMA((ncp,)), pltpu.SemaphoreType.DMA((ncp,)), *_thru(fulls)),
        in_specs=[HBM] * nt, out_specs=(SEM, SEM, *([HBM] * nt)), input_output_aliases={t: 2 + t for t in range(nt)},
        name="fill_start_%d_%d" % (group, which[0]), compiler_params=pltpu.CompilerParams(has_side_effects=EFFECT),
    )(*fulls)
    return out[0], out[1], list(out[2:])


def _fill_wait(group, which, send_sems, recv_sems, fulls, kinds, after):
    nt = len(fulls)

    def body(*refs):
        ins, send_ref, recv_ref = refs[:nt], refs[nt], refs[nt + 1]
        _, _, c = _place()
        for t in which:
            for k in range(3):
                cp = _fill_copy(ins, kinds, send_ref, recv_ref, group, which, t, k, 1 - c)
                cp.wait_send()
                cp.wait_recv()

    out = pl.pallas_call(
        body, out_shape=tuple(_thru(fulls)), in_specs=[HBM] * nt + [SEM, SEM, ANY], out_specs=tuple([HBM] * nt),
        input_output_aliases={t: t for t in range(nt)}, name="fill_wait_%d_%d" % (group, which[0]),
        compiler_params=pltpu.CompilerParams(has_side_effects=EFFECT),
    )(*fulls, send_sems, recv_sems, after)
    return list(out)


def _gather_start(fulls, kinds, after):
    nt = len(fulls)
    ncp = DEPTH * nt * 3

    def body(*refs):
        ins, send_sems, recv_sems = refs[:nt], refs[nt + 1], refs[nt + 2]
        x, y, _ = _place()
        for group in range(DEPTH):
            for t in range(nt):
                for k in range(3):
                    _gather_copy(ins, kinds, send_sems, recv_sems, group, t, k, 2 * x + y).start()

    out = pl.pallas_call(
        body, out_shape=(pltpu.SemaphoreType.DMA((ncp,)), pltpu.SemaphoreType.DMA((ncp,)), *_thru(fulls)),
        in_specs=[HBM] * nt + [ANY], out_specs=(SEM, SEM, *([HBM] * nt)), input_output_aliases={t: 2 + t for t in range(nt)},
        name="gather_start", compiler_params=pltpu.CompilerParams(has_side_effects=EFFECT),
    )(*[_in_hbm(a) for a in fulls], after)
    return out[0], out[1], list(out[2:])


def _gather_wait(group, which, send_sems, recv_sems, fulls, kinds, after):
    nt = len(fulls)

    def body(*refs):
        ins, send_ref, recv_ref = refs[:nt], refs[nt], refs[nt + 1]
        x, y, _ = _place()
        chips = _other_chips(x, y)
        for t in which:
            for k in range(3):
                cp = _gather_copy(ins, kinds, send_ref, recv_ref, group, t, k, 2 * chips[k][0] + chips[k][1])
                cp.wait_send()
                cp.wait_recv()

    out = pl.pallas_call(
        body, out_shape=tuple(_thru(fulls)), in_specs=[HBM] * nt + [SEM, SEM, ANY], out_specs=tuple([HBM] * nt),
        input_output_aliases={t: t for t in range(nt)}, name="gather_wait_%d_%d" % (group, which[0]),
        compiler_params=pltpu.CompilerParams(has_side_effects=EFFECT),
    )(*fulls, send_sems, recv_sems, after)
    return list(out)


def _scatter_copy(grads, lands, kinds, send_sems, recv_sems, t, k):
    x, y, c = _place()
    chip = _other_chips(x, y)[k]
    return pltpu.make_async_remote_copy(
        src_ref=_slab(grads[t], kinds[t], 2 * chip[0] + chip[1]), dst_ref=lands[t].at[k], send_sem=send_sems.at[3 * t + k],
        recv_sem=recv_sems.at[3 * t + k], device_id=(chip[0], chip[1], c), device_id_type=MESH)


def _scatter_start(grads, kinds, after, name):
    nt = len(grads)
    lands = []
    for g, kd in zip(grads, kinds):
        l, r, c = g.shape
        lands.append(lax.empty((3, l, r, c // 4) if kd == "col" else (3, l, r // 4, c), g.dtype))

    def body(*refs):
        ins, lnd, send_sems, recv_sems = refs[:nt], refs[nt:2 * nt], refs[2 * nt + 1], refs[2 * nt + 2]
        for t in range(nt):
            for k in range(3):
                _scatter_copy(ins, lnd, kinds, send_sems, recv_sems, t, k).start()
        refs[-1][...] = jnp.zeros_like(refs[-1])

    out = pl.pallas_call(
        body, out_shape=(pltpu.SemaphoreType.DMA((3 * nt,)), pltpu.SemaphoreType.DMA((3 * nt,)), *_thru(grads), *_thru(lands),
                         jax.ShapeDtypeStruct((8, 128), F32)),
        in_specs=[HBM] * (2 * nt) + [ANY], out_specs=(SEM, SEM, *([HBM] * (2 * nt)), pl.BlockSpec(memory_space=pltpu.VMEM)),
        input_output_aliases={t: 2 + t for t in range(2 * nt)}, name=name,
        compiler_params=pltpu.CompilerParams(has_side_effects=EFFECT),
    )(*[_in_hbm(a) for a in grads], *[_in_hbm(a) for a in lands], after)
    return (out[0], out[1], list(out[2:2 + nt]), list(out[2 + nt:2 + 2 * nt])), out[-1]


def _scatter_wait(send_sems, recv_sems, grads, lands, kinds, after, name):
    nt = len(grads)

    def body(*refs):
        ins, lnd, send_ref, recv_ref = refs[:nt], refs[nt:2 * nt], refs[2 * nt], refs[2 * nt + 1]
        for t in range(nt):
            for k in range(3):
                cp = _scatter_copy(ins, lnd, kinds, send_ref, recv_ref, t, k)
                cp.wait_send()
                cp.wait_recv()

    out = pl.pallas_call(
        body, out_shape=(*_thru(grads), *_thru(lands)), in_specs=[HBM] * (2 * nt) + [SEM, SEM, ANY],
        out_specs=tuple([HBM] * (2 * nt)), input_output_aliases={t: t for t in range(2 * nt)}, name=name,
        compiler_params=pltpu.CompilerParams(has_side_effects=EFFECT),
    )(*grads, *lands, send_sems, recv_sems, after)
    return list(out[:nt]), list(out[nt:])


def _sibling_swap(arrs):
    nt = len(arrs)

    def body(*refs):
        ins, outs = refs[:nt], refs[nt:2 * nt]
        send_sems, recv_sems = refs[2 * nt:]
        x, y, c = _place()
        sends = [pltpu.make_async_remote_copy(src_ref=ins[t], dst_ref=outs[t], send_sem=send_sems.at[t], recv_sem=recv_sems.at[t],
                                              device_id=(x, y, 1 - c), device_id_type=MESH) for t in range(nt)]
        for cp in sends:
            cp.start()
        for cp in sends:
            cp.wait_recv()
        for cp in sends:
            cp.wait_send()

    return pl.pallas_call(
        body, out_shape=[jax.ShapeDtypeStruct(a.shape, a.dtype) for a in arrs], in_specs=[ANY] * nt, out_specs=[ANY] * nt,
        scratch_shapes=[pltpu.SemaphoreType.DMA((nt,)), pltpu.SemaphoreType.DMA((nt,))], name="sibling_swap",
    )(*arrs)


def _gather_small(vec, over_c):
    n = vec.shape[0]
    flips = [(dx, dy, dc) for dx in (0, 1) for dy in (0, 1) for dc in ((0, 1) if over_c else (0,))][1:]
    np_ = len(flips)

    def body(v_ref, o_ref, send_sems, recv_sems, local_sem):
        x, y, c = _place()

        def idx(px, py, pc):
            return 4 * px + 2 * py + pc if over_c else 2 * px + py

        def peer(f):
            return (1 - x if f[0] else x, 1 - y if f[1] else y, 1 - c if f[2] else c)

        def push(k, landing):
            return pltpu.make_async_remote_copy(src_ref=v_ref, dst_ref=o_ref.at[landing], send_sem=send_sems.at[k],
                                                recv_sem=recv_sems.at[k], device_id=peer(flips[k]), device_id_type=MESH)

        mine = pltpu.make_async_copy(v_ref, o_ref.at[idx(x, y, c)], local_sem)
        sends = [push(k, idx(x, y, c)) for k in range(np_)]
        for cp in [mine] + sends:
            cp.start()
        for k in range(np_):
            push(k, idx(*peer(flips[k]))).wait_recv()
        for cp in sends:
            cp.wait_send()
        mine.wait()

    return pl.pallas_call(
        body, out_shape=jax.ShapeDtypeStruct((np_ + 1, n, 128), F32), in_specs=[ANY], out_specs=ANY,
        scratch_shapes=[pltpu.SemaphoreType.DMA((np_,)), pltpu.SemaphoreType.DMA((np_,)), pltpu.SemaphoreType.DMA(())],
        name="gather_small_all" if over_c else "gather_small_xy",
    )(vec)


def _sum_rows(buf, after, tr=512):
    p, n, _ = buf.shape
    tr = min(tr, n)

    def body(b_ref, after_ref, o_ref):
        acc = b_ref[0]
        for k in range(1, p):
            acc = acc + b_ref[k]
        o_ref[...] = acc

    return pl.pallas_call(
        body, out_shape=jax.ShapeDtypeStruct((n, 128), F32), grid=(n // tr,),
        in_specs=[pl.BlockSpec((p, tr, 128), lambda i: (0, i, 0)), ANY], out_specs=pl.BlockSpec((tr, 128), lambda i: (i, 0)),
        name="sum_rows", compiler_params=_params(("parallel",)),
    )(buf, after)


def _sum_partials_into(stack, at, depth, grad, recv, kind, jj, name, tr=128):
    _, _, r, c = recv.shape
    tr = min(tr, r)

    def body(j_ref, g_ref, r0, r1, r2, *rest):
        rest[-1][...] = ((g_ref[...].astype(F32) + r0[...].astype(F32)) + r1[...].astype(F32)) + r2[...].astype(F32)

    if kind == "col":
        own = pl.BlockSpec((None, tr, c), lambda b, j: (0, b, j[0]))
    else:
        own = pl.BlockSpec((None, tr, c), lambda b, j: (0, j[0] * (r // tr) + b, 0))
    got = lambda k: pl.BlockSpec((None, None, tr, c), functools.partial(lambda k, b, j: (k, 0, b, 0), k))
    chained = stack is not None
    return pl.pallas_call(
        body, out_shape=jax.ShapeDtypeStruct((depth, r, c), F32),
        grid_spec=pltpu.PrefetchScalarGridSpec(
            num_scalar_prefetch=1, grid=(r // tr,), in_specs=[own, got(0), got(1), got(2)] + ([ANY] if chained else []),
            out_specs=pl.BlockSpec((None, tr, c), lambda b, j: (at, b, 0))),
        input_output_aliases={5: 0} if chained else {}, name=name, compiler_params=_params(("parallel",)),
    )(*([jj, grad, recv, recv, recv] + ([stack] if chained else [])))


WEIGHTS = ['norm_mix_w', 'w_in', 'hg_lb_raw', 'hg_norm_w', 'cv_dw_w', 'cv_dw_b', 'cv_ln_w', 'cv_ln_b', 'pl_w', 'pl_scale',
           'lru_conv_w', 'lru_conv_b', 'lru_wa', 'lru_ba', 'lru_wx', 'lru_bx', 'lru_lambda', 'gate_b', 'w_branch', 'w_out',
           'norm_mem_w', 'mem_norm_w', 'xa_wq', 'xa_wkv', 'xa_wo', 'norm_ffn_w', 'ffn_w1', 'ffn_w2', 'final_norm_w']
BIG = {'w_in': 'col', 'w_branch': 'col', 'w_out': 'row', 'xa_wq': 'row', 'xa_wkv': 'col', 'xa_wo': 'row', 'ffn_w1': 'col', 'ffn_w2': 'row'}
SMALL_SPLIT = ('gate_b', 'cv_dw_w', 'lru_conv_w')
SMALL = [n for n in WEIGHTS if n not in BIG]
PIECE_GROUPS = [['ffn_w1', 'ffn_w2', 'xa_wq', 'xa_wkv', 'xa_wo'],
                ['w_out', ('w_branch', 0), ('w_branch', 1), ('w_branch', 2), ('w_branch', 3), 'w_in']]


def _piece_kinds(pieces):
    return [BIG[k[0] if isinstance(k, tuple) else k] for k in pieces]
ROWS_PAD = 512


def _as3d(a):
    return a.reshape((-1,) + a.shape[-2:])


def _pack(parts):
    flat = jnp.concatenate([p.reshape(-1).astype(F32) for p in parts])
    n = -(-flat.shape[0] // (128 * ROWS_PAD)) * ROWS_PAD
    return jnp.pad(flat, (0, n * 128 - flat.shape[0])).reshape(n, 128)


def _unpack(packed, shapes):
    flat, out, o = packed.reshape(-1), [], 0
    for sh in shapes:
        sz = math.prod(sh)
        out.append(flat[o:o + sz].reshape(sh))
        o += sz
    return out


def _block_diag(w):
    h, a, b = w.shape
    eye = jnp.eye(h, dtype=w.dtype)
    return (w[:, :, None, :] * eye[:, None, :, None]).reshape(h * a, h * b)


def _diag_blocks(m, h):
    a, b = m.shape[0] // h, m.shape[1] // h
    return jnp.stack([m[i * a:(i + 1) * a, i * b:(i + 1) * b] for i in range(h)])


def kernel(x, mem, norm_mix_w, w_in, hg_lb_raw, hg_norm_w, cv_dw_w, cv_dw_b, cv_ln_w, cv_ln_b, pl_w, pl_scale, lru_conv_w, lru_conv_b, lru_wa, lru_ba, lru_wx, lru_bx, lru_lambda, gate_b, w_branch, w_out, norm_mem_w, mem_norm_w, xa_wq, xa_wkv, xa_wo, norm_ffn_w, ffn_w1, ffn_w2, final_norm_w, loss_target, m_norm_mix_w, m_w_in, m_hg_lb_raw, m_hg_norm_w, m_cv_dw_w, m_cv_dw_b, m_cv_ln_w, m_cv_ln_b, m_pl_w, m_pl_scale, m_lru_conv_w, m_lru_conv_b, m_lru_wa, m_lru_ba, m_lru_wx, m_lru_bx, m_lru_lambda, m_gate_b, m_w_branch, m_w_out, m_norm_mem_w, m_mem_norm_w, m_xa_wq, m_xa_wkv, m_xa_wo, m_norm_ffn_w, m_ffn_w1, m_ffn_w2, m_final_norm_w, v_norm_mix_w, v_w_in, v_hg_lb_raw, v_hg_norm_w, v_cv_dw_w, v_cv_dw_b, v_cv_ln_w, v_cv_ln_b, v_pl_w, v_pl_scale, v_lru_conv_w, v_lru_conv_b, v_lru_wa, v_lru_ba, v_lru_wx, v_lru_bx, v_lru_lambda, v_gate_b, v_w_branch, v_w_out, v_norm_mem_w, v_mem_norm_w, v_xa_wq, v_xa_wkv, v_xa_wo, v_norm_ffn_w, v_ffn_w1, v_ffn_w2, v_final_norm_w):
    w = dict(zip(WEIGHTS, (norm_mix_w, w_in, hg_lb_raw, hg_norm_w, cv_dw_w, cv_dw_b, cv_ln_w, cv_ln_b, pl_w, pl_scale, lru_conv_w, lru_conv_b, lru_wa, lru_ba, lru_wx, lru_bx, lru_lambda, gate_b, w_branch, w_out, norm_mem_w, mem_norm_w, xa_wq, xa_wkv, xa_wo, norm_ffn_w, ffn_w1, ffn_w2, final_norm_w)))
    m1 = dict(zip(WEIGHTS, (m_norm_mix_w, m_w_in, m_hg_lb_raw, m_hg_norm_w, m_cv_dw_w, m_cv_dw_b, m_cv_ln_w, m_cv_ln_b, m_pl_w, m_pl_scale, m_lru_conv_w, m_lru_conv_b, m_lru_wa, m_lru_ba, m_lru_wx, m_lru_bx, m_lru_lambda, m_gate_b, m_w_branch, m_w_out, m_norm_mem_w, m_mem_norm_w, m_xa_wq, m_xa_wkv, m_xa_wo, m_norm_ffn_w, m_ffn_w1, m_ffn_w2, m_final_norm_w)))
    v1 = dict(zip(WEIGHTS, (v_norm_mix_w, v_w_in, v_hg_lb_raw, v_hg_norm_w, v_cv_dw_w, v_cv_dw_b, v_cv_ln_w, v_cv_ln_b, v_pl_w, v_pl_scale, v_lru_conv_w, v_lru_conv_b, v_lru_wa, v_lru_ba, v_lru_wx, v_lru_bx, v_lru_lambda, v_gate_b, v_w_branch, v_w_out, v_norm_mem_w, v_mem_norm_w, v_xa_wq, v_xa_wkv, v_xa_wo, v_norm_ffn_w, v_ffn_w1, v_ffn_w2, v_final_norm_w)))
    seq = x.shape[1]
    xs, mems, tgt = x.reshape(seq, D_MODEL), mem.reshape(-1, D_MODEL), loss_target.reshape(seq, D_MODEL)
    jj = 2 * lax.axis_index("x") + lax.axis_index("y")
    jj1 = jnp.reshape(jj, (1,)).astype(jnp.int32)

    split_shapes = [w[n].shape for n in SMALL_SPLIT]
    got = _gather_small(_pack([w[n] for n in SMALL_SPLIT]), over_c=False)
    per_chip = [_unpack(got[k], split_shapes) for k in range(4)]
    full_small = {n: jnp.concatenate([per_chip[k][i] for k in range(4)], axis=-1) for i, n in enumerate(SMALL_SPLIT)}
    big_names = list(BIG)
    kinds = [BIG[n] for n in big_names]
    g_send, g_recv, fulls = _gather_start([_cast_into_full(_as3d(w[n]), BIG[n], jj1, "cast_" + n) for n in big_names], kinds, got)
    tix = {n: t for t, n in enumerate(big_names)}

    lb = _lb_fwd(hg_lb_raw)
    row = lambda a: a.reshape(1, -1)

    def layer_params(l):
        return dict(
            nmix=row(norm_mix_w[l]), lb=row(lb[l]), hgnw=row(hg_norm_w[l]),
            cw=jnp.pad(full_small['cv_dw_w'][l], ((0, 32 - CV_K), (0, 0))), cb=row(cv_dw_b[l]), lnw=row(cv_ln_w[l]), lnb=row(cv_ln_b[l]),
            plw=pl_w[l], plsc=row(pl_scale[l]),
            lcw=jnp.pad(full_small['lru_conv_w'][l], ((0, 8 - LRU_CONV), (0, 0))), lcb=row(lru_conv_b[l]),
            wa=_block_diag(lru_wa[l]).astype(BF16), ba=row(lru_ba[l]), wx=_block_diag(lru_wx[l]).astype(BF16), bx=row(lru_bx[l]),
            lam=row(lru_lambda[l]), gb=full_small['gate_b'][l], nmem=row(norm_mem_w[l]), memw=row(mem_norm_w[l]), nffn=row(norm_ffn_w[l]))

    saved = []
    xc = xs
    def by_name(arrays):
        wf_ = dict(zip(big_names, arrays))
        wf_['w_branch'] = wf_['w_branch'].reshape(DEPTH, 4, 512, D_MODEL)
        return wf_

    def landed(l, names, arrays, after):
        which = [tix[n] for n in names]
        arrays = _gather_wait(l, which, g_send, g_recv, arrays, kinds, after)
        f_send, f_recv, arrays = _fill_start(l, which, arrays, kinds)
        return (which, f_send, f_recv), arrays

    def complete(l, pending, arrays, after):
        which, f_send, f_recv = pending
        return _fill_wait(l, which, f_send, f_recv, arrays, kinds, after)

    rest = [n for n in big_names if n != 'w_in']
    for l in range(DEPTH):
        if l == 0:
            pending, fulls = landed(0, ['w_in'], fulls, xc)
        fulls = complete(l, pending, fulls, xc)
        wf = by_name(fulls)
        p = layer_params(l)
        h = _norm_fwd(xc, p['nmix'], "norm_mix")
        proj = _mm(h, wf['w_in'], "nn", F32, "proj", tm=seq, layer=l)
        if l == 0:
            pending, fulls = landed(0, rest, fulls, proj)
        b_hg, st = _hgrn_fwd(proj, p['lb'], p['hgnw'])
        b_cv = _conv_fwd(proj, p['cw'], p['cb'], p['lnw'], p['lnb'])
        b_pl = _pool_fwd(proj, p['plw'], p['plsc'])
        b_lru, hs = _lru_fwd(proj, p['lcw'], p['lcb'], p['wa'], p['ba'], p['wx'], p['bx'], p['lam'])
        branches = (b_hg, b_cv, b_pl, b_lru)
        if l == 0:
            fulls = complete(0, pending, fulls, b_lru)
            wf = by_name(fulls)
        x1 = _merge_fwd(xc, branches, proj, p['gb'], wf['w_branch'], wf['w_out'], l)
        if l + 1 < DEPTH:
            pending, fulls = landed(l + 1, big_names, fulls, x1)
            wf = by_name(fulls)
        memn = _norm_fwd(mems, p['memw'], "norm_memtok")
        kv = _mm(memn, wf['xa_wkv'], "nn", BF16, "kv_proj", layer=l)
        x2 = _attn_fwd(x1, p['nmem'], wf['xa_wq'], kv, wf['xa_wo'], l)
        x3 = _ffn_fwd(x2, p['nffn'], wf['ffn_w1'], wf['ffn_w2'], l, ts=1024)
        saved.append(dict(p=p, x=xc, h=h, proj=proj, st=st, hs=hs, branches=branches, x1=x1, memn=memn, kv=kv, x2=x2))
        xc = x3

    loss_blk, dx, dfinal = _final_loss(xc, row(final_norm_w), tgt)

    gs = {n: [None] * DEPTH for n in SMALL if n != 'final_norm_w'}
    dlb = [None] * DEPTH
    in_flight = [[None, None] for _ in range(DEPTH)]

    def scatter(grads, grp, after, name):
        pieces = PIECE_GROUPS[grp]
        return _scatter_start([grads[key][None] for key in pieces], _piece_kinds(pieces), after, name)

    token = loss_blk
    for l in reversed(range(DEPTH)):
        sv = saved[l]
        p = sv['p']
        gb = {}
        dx2, gs['norm_ffn_w'][l], h3, da, r, dxb = _ffn_bwd(sv['x2'], dx, p['nffn'], wf['ffn_w1'], wf['ffn_w2'], l, token, ts=512)
        gb['ffn_w1'] = _mm(h3, da, "tn", BF16, "dw_ffn1", tm=1024)
        gb['ffn_w2'] = _mm(r, dxb, "tn", BF16, "dw_ffn2", tn=1024)
        dx1, gs['norm_mem_w'][l], h2, o, dq, dxb2, dk, dv = _attn_bwd(sv['x1'], dx2, p['nmem'], wf['xa_wq'], sv['kv'], wf['xa_wo'], l)
        gb['xa_wq'] = _mm(h2, dq, "tn", BF16, "dw_q")
        gb['xa_wo'] = _mm(o, dxb2, "tn", BF16, "dw_o")
        dkv = jnp.concatenate([dk, dv], axis=1)
        gb['xa_wkv'] = _mm(sv['memn'], dkv, "tn", BF16, "dw_kv")
        dmemn = _mm(dkv, wf['xa_wkv'], "nt", F32, "dmemn", layer=l)
        _, gs['mem_norm_w'][l] = _norm_bwd(mems, p['memw'], dmemn, None, "norm_memtok_bwd")
        in_flight[l][0], token = scatter(gb, 0, dx1, "scatter_start_%d_0" % l)
        db0, db1, db2, db3, dgp, dup, mg, dxb1, gs['gate_b'][l] = _merge_bwd(
            dx1, sv['branches'], sv['proj'], p['gb'], wf['w_branch'], wf['w_out'], l, token)
        gb['w_out'] = _mm(mg, dxb1, "tn", BF16, "dw_out")
        for kb in range(4):
            gb['w_branch', kb] = _mm(sv['branches'][kb], dup, "tn", BF16, "dw_branch", b_col0=kb * D_MODEL, n=D_MODEL, tn=512)
        dhg, dlb[l], gs['hg_norm_w'][l] = _hgrn_bwd(sv['proj'], db0, sv['st'], p['lb'], p['hgnw'])
        dcv, dcw, gs['cv_dw_b'][l], gs['cv_ln_w'][l], gs['cv_ln_b'][l] = _conv_bwd(sv['proj'], db1, p['cw'], p['cb'], p['lnw'], p['lnb'])
        gs['cv_dw_w'][l] = dcw[:CV_K]
        dpl, gs['pl_w'][l], gs['pl_scale'][l] = _pool_bwd(sv['proj'], db2, p['plw'], p['plsc'])
        dlru, dlcw, gs['lru_conv_b'][l], dwa, gs['lru_ba'][l], dwx, gs['lru_bx'][l], gs['lru_lambda'][l] = _lru_bwd(
            sv['proj'], sv['hs'], db3, p['lcw'], p['lcb'], p['wa'], p['ba'], p['wx'], p['bx'], p['lam'])
        gs['lru_conv_w'][l] = dlcw[:LRU_CONV]
        gs['lru_wa'][l], gs['lru_wx'][l] = _diag_blocks(dwa, LRU_HEADS), _diag_blocks(dwx, LRU_HEADS)
        dproj = jnp.concatenate([dhg, dcv, dpl, dlru, dgp], axis=1)
        gb['w_in'] = _mm(sv['h'], dproj, "tn", BF16, "dw_in", tm=1024)
        dh = _mm(dproj, wf['w_in'], "nt", F32, "dh_mix", tm=256, tn=512, layer=l)
        dx, gs['norm_mix_w'][l] = _norm_bwd(sv['x'], p['nmix'], dh, dx1, "norm_mix_bwd")
        if l:
            in_flight[l][1], token = scatter(gb, 1, dx, "scatter_start_%d_1" % l)
    grad_x = dx.reshape(x.shape)
    gs['hg_lb_raw'] = _lb_bwd(hg_lb_raw, jnp.concatenate(dlb, axis=0))

    def full_shape(n):
        return full_small[n].shape if n in SMALL_SPLIT else w[n].shape

    small_full = []
    for n in SMALL:
        g = gs[n] if n == 'hg_lb_raw' else dfinal if n == 'final_norm_w' else jnp.stack(gs[n])
        small_full.append(g.reshape(full_shape(n)))
    everyone = _gather_small(_pack(small_full + [loss_blk[0:1, 0:1]]), over_c=True)
    in_flight[0][1], token = scatter(gb, 1, everyone, "scatter_start_0_1")
    total = _sum_rows(everyone, token)
    parts = _unpack(total, [full_shape(n) for n in SMALL] + [(1,)])
    loss = parts[-1].reshape(())
    g_small = {}
    for n, g in zip(SMALL, parts[:-1]):
        if n in SMALL_SPLIT:
            width = w[n].shape[-1]
            g = lax.dynamic_slice_in_dim(g, jj * width, width, axis=g.ndim - 1)
        g_small[n] = g
    shapes = [w[n].shape for n in SMALL]
    upd = _adamw(_pack([w[n] for n in SMALL]), [_pack([g_small[n] for n in SMALL])], _pack([m1[n] for n in SMALL]),
                 _pack([v1[n] for n in SMALL]), "adamw_small")
    d_small, m_small, v_small = [dict(zip(SMALL, _unpack(u, shapes))) for u in upd]

    stacks = {n: None for n in big_names}
    done_before = upd[0]
    for l in reversed(range(DEPTH)):
        for grp, pieces in enumerate(PIECE_GROUPS):
            s_send, s_recv, g_thru, lands = in_flight[l][grp]
            g_thru, lands = _scatter_wait(s_send, s_recv, g_thru, lands, _piece_kinds(pieces), done_before,
                                          "scatter_wait_%d_%d" % (l, grp))
            for key, g, r in zip(pieces, g_thru, lands):
                n, kb = key if isinstance(key, tuple) else (key, None)
                per = 1 if kb is None else 4
                stacks[n] = _sum_partials_into(stacks[n], l * per + (kb or 0), DEPTH * per, g, r, BIG[n], jj1, "sum_" + n)
                done_before = stacks[n]
    partial = [stacks[n] for n in big_names]
    theirs = _sibling_swap(partial)
    g_big, d_big, m_big, v_big = {}, {}, {}, {}
    for n, pa, pb in zip(big_names, partial, theirs):
        c2 = lambda a: a.reshape(-1, a.shape[-1])
        out = _adamw(c2(w[n]), [c2(pa), c2(pb)], c2(m1[n]), c2(v1[n]), "adamw_" + n)
        g_big[n], d_big[n], m_big[n], v_big[n] = [o.reshape(w[n].shape) for o in out]

    pick = lambda small, big: [big[n] if n in BIG else small[n] for n in WEIGHTS]
    return (loss, grad_x, *pick(g_small, g_big), *pick(d_small, d_big), *pick(m_small, m_big), *pick(v_small, v_big))
```

```python
import functools
import math

import jax
import jax.numpy as jnp
from jax import lax
from jax.experimental import pallas as pl
from jax.experimental.pallas import tpu as pltpu

F32 = jnp.float32
BF16 = jnp.bfloat16
MESH = pl.DeviceIdType.MESH
ANY = pl.BlockSpec(memory_space=pl.ANY)

D_MODEL = 1024
DEPTH = 4
CHUNK = 64
SUB = 16
EPS = 1e-6
HG_HEADS, HG_D = 4, 128
CV_W, CV_K = 512, 31
CV_HALO = 32
POOL_WINDOWS = (2, 4, 8, 16)
POOL_HALO = 16
LRU_W, LRU_HEADS, LRU_HD, LRU_CONV = 512, 8, 64, 4
LRU_HALO = 8
LRU_C = 8.0
MIX_W = 4608
IN_W = 8704
XA_HEADS, XA_HD = 4, 256
D_FF = 4096
FF_CHUNK = 1024
ADAM_LR, ADAM_B1, ADAM_B2, ADAM_EPS, ADAM_WD, ADAM_STEP = 0.001, 0.9, 0.999, 1e-08, 0.01, 10
VMEM_LIMIT = 56 * 1024 * 1024
EXP_CLAMP = 80.0
HI = lax.Precision.HIGHEST


def _params(sem=None):
    return pltpu.CompilerParams(dimension_semantics=sem, vmem_limit_bytes=VMEM_LIMIT)


def _sigmoid(x):
    return 1.0 / (1.0 + jnp.exp(-x))


def _dsilu(x, s):
    return s * (1.0 + x * (1.0 - s))


_GELU_C = math.sqrt(2.0 / math.pi)


def _gelu_parts(x):
    t = jnp.tanh(_GELU_C * (x + 0.044715 * x * x * x))
    g = 0.5 * x * (1.0 + t)
    dg = 0.5 * (1.0 + t) + 0.5 * x * (1.0 - t * t) * _GELU_C * (1.0 + 3 * 0.044715 * x * x)
    return g, dg


def _dot(a, b, dims, precision=None):
    return lax.dot_general(a, b, (dims, ((), ())), precision=precision, preferred_element_type=F32)


def _nn(a, b, **k):
    return _dot(a, b, ((1,), (0,)), **k)


def _nt(a, b, **k):
    return _dot(a, b, ((1,), (1,)), **k)


def _tn(a, b, **k):
    return _dot(a, b, ((0,), (0,)), **k)


def _split(x):
    hi = x.astype(BF16)
    return hi, (x - hi.astype(F32)).astype(BF16)


def _nn3(a, b):
    (ah, al), (bh, bl) = _split(a), _split(b)
    return _nn(jnp.concatenate([ah, ah, al], axis=1), jnp.concatenate([bh, bl, bh], axis=0))


def _tn3(a, b):
    (ah, al), (bh, bl) = _split(a), _split(b)
    return _tn(jnp.concatenate([ah, ah, al], axis=0), jnp.concatenate([bh, bl, bh], axis=0))


def _rms_fwd(x, w):
    r = lax.rsqrt(jnp.mean(x * x, axis=-1, keepdims=True) + EPS)
    return x * r * w, r


def _rms_bwd(x, r, w, dy):
    xr = x * r
    g = dy * w
    dx = r * (g - xr * jnp.mean(g * xr, axis=-1, keepdims=True))
    return dx, jnp.sum(dy * xr, axis=0, keepdims=True)


def _lw(shape, index, layer):
    return pl.BlockSpec((None,) + tuple(shape), lambda *g: (layer,) + tuple(index(*g)))


def _mm(a, b, mode, out_dtype, name, tm=512, tn=512, b_col0=0, n=None, layer=None):
    bs = b.shape if layer is None else b.shape[1:]
    if mode == "nn":
        m, k = a.shape
        n = bs[1] if n is None else n
    elif mode == "nt":
        m, k = a.shape
        n = bs[0] if n is None else n
    else:
        k, m = a.shape
        n = bs[1] if n is None else n
    tm, tn = min(tm, m), min(tn, n)
    assert m % tm == 0 and n % tn == 0 and b_col0 % tn == 0
    off = b_col0 // tn

    def body(a_ref, b_ref, o_ref):
        av, bv = a_ref[...].astype(BF16), b_ref[...].astype(BF16)
        o_ref[...] = (_nn if mode == "nn" else _nt if mode == "nt" else _tn)(av, bv).astype(out_dtype)

    def bspec(shape, index):
        return pl.BlockSpec(shape, index) if layer is None else _lw(shape, index, layer)

    if mode == "tn":
        grid = (m // tm, n // tn)
        a_spec = pl.BlockSpec((k, tm), lambda i, j: (0, i))
        b_spec = bspec((k, tn), lambda i, j: (0, j + off))
        o_spec = pl.BlockSpec((tm, tn), lambda i, j: (i, j))
    else:
        grid = (n // tn, m // tm)
        a_spec = pl.BlockSpec((tm, k), lambda j, i: (i, 0))
        if mode == "nn":
            b_spec = bspec((k, tn), lambda j, i: (0, j + off))
        else:
            b_spec = bspec((tn, k), lambda j, i: (j + off, 0))
        o_spec = pl.BlockSpec((tm, tn), lambda j, i: (i, j))
    return pl.pallas_call(
        body, out_shape=jax.ShapeDtypeStruct((m, n), out_dtype), grid=grid,
        in_specs=[a_spec, b_spec], out_specs=o_spec, name=name,
        compiler_params=_params(("parallel", "parallel")),
    )(a, b)


def _mm_tn_pieces(a, pieces, name, tn=256):
    k, m = a.shape
    starts, o = [], 0
    for pc in pieces:
        assert pc.shape[1] % tn == 0
        starts.append(o // tn)
        o += pc.shape[1]
    counts = [pc.shape[1] // tn for pc in pieces]
    npc = len(pieces)

    def body(a_ref, *refs):
        o_ref = refs[npc]
        j = pl.program_id(0)
        for pi in range(npc):
            @pl.when(jnp.logical_and(j >= starts[pi], j < starts[pi] + counts[pi]))
            def _(pi=pi):
                o_ref[...] = _tn(a_ref[...], refs[pi][...]).astype(o_ref.dtype)

    def piece_spec(pi):
        return pl.BlockSpec((k, tn), lambda j: (0, jnp.clip(j - starts[pi], 0, counts[pi] - 1)))

    return pl.pallas_call(
        body, out_shape=jax.ShapeDtypeStruct((m, o), BF16), grid=(o // tn,),
        in_specs=[pl.BlockSpec((k, m), lambda j: (0, 0))] + [piece_spec(pi) for pi in range(npc)],
        out_specs=pl.BlockSpec((m, tn), lambda j: (0, j)), name=name, compiler_params=_params(("parallel",)),
    )(a, *pieces)


def _mm_nt_pieces(pieces, b, layer, name, tm=256, tn=512):
    m = pieces[0].shape[0]
    n, k = b.shape[1:]
    offs, o = [], 0
    for pc in pieces:
        offs.append(o)
        o += pc.shape[1]
    assert o == k
    npc = len(pieces)

    def body(*refs):
        b_ref, o_ref = refs[npc], refs[npc + 1]
        acc = jnp.zeros((tm, tn), F32)
        for pi in range(npc):
            acc += _nt(refs[pi][...], b_ref[:, offs[pi]:offs[pi] + pieces[pi].shape[1]])
        o_ref[...] = acc

    return pl.pallas_call(
        body, out_shape=jax.ShapeDtypeStruct((m, n), F32), grid=(n // tn, m // tm),
        in_specs=[pl.BlockSpec((tm, pc.shape[1]), lambda j, i: (i, 0)) for pc in pieces] + [_lw((tn, k), lambda j, i: (j, 0), layer)],
        out_specs=pl.BlockSpec((tm, tn), lambda j, i: (i, j)), name=name, compiler_params=_params(("parallel", "parallel")),
    )(*pieces, b)


def _norm_fwd(x, w, name, ts=512):
    s, d = x.shape
    ts = min(ts, s)

    def body(x_ref, w_ref, o_ref):
        o_ref[...] = _rms_fwd(x_ref[...], w_ref[...])[0].astype(BF16)

    return pl.pallas_call(
        body, out_shape=jax.ShapeDtypeStruct((s, d), BF16), grid=(s // ts,),
        in_specs=[pl.BlockSpec((ts, d), lambda i: (i, 0)), pl.BlockSpec((1, d), lambda i: (0, 0))],
        out_specs=pl.BlockSpec((ts, d), lambda i: (i, 0)), name=name, compiler_params=_params(("parallel",)),
    )(x, w)


def _norm_bwd(x, w, dy, dres, name, ts=512):
    s, d = x.shape
    ts = min(ts, s)
    with_res = dres is not None

    def body(*refs):
        if with_res:
            x_ref, w_ref, dy_ref, dres_ref, dx_ref, dw_ref = refs
        else:
            x_ref, w_ref, dy_ref, dx_ref, dw_ref = refs
        xv = x_ref[...]
        r = lax.rsqrt(jnp.mean(xv * xv, axis=-1, keepdims=True) + EPS)
        dx, dw = _rms_bwd(xv, r, w_ref[...], dy_ref[...])
        dx_ref[...] = dx + dres_ref[...] if with_res else dx

        @pl.when(pl.program_id(0) == 0)
        def _():
            dw_ref[...] = jnp.zeros_like(dw_ref)

        dw_ref[...] += dw

    row = pl.BlockSpec((ts, d), lambda i: (i, 0))
    vec = pl.BlockSpec((1, d), lambda i: (0, 0))
    return pl.pallas_call(
        body, out_shape=(jax.ShapeDtypeStruct((s, d), F32), jax.ShapeDtypeStruct((1, d), F32)), grid=(s // ts,),
        in_specs=[row, vec, row] + ([row] if with_res else []), out_specs=(row, vec), name=name,
        compiler_params=_params(("arbitrary",)),
    )(*([x, w, dy] + ([dres] if with_res else [])))


def _ffn_fwd(x, nw, w1, w2, layer, ts=256):
    s, d = x.shape
    ts = min(ts, s)
    nj = D_FF // FF_CHUNK

    def body(x_ref, nw_ref, w1_ref, w2_ref, o_ref, h_scr, acc):
        j = pl.program_id(1)

        @pl.when(j == 0)
        def _():
            h_scr[...] = _rms_fwd(x_ref[...], nw_ref[...])[0].astype(BF16)
            acc[...] = jnp.zeros_like(acc)

        a = _nn(h_scr[...], w1_ref[...])
        rl = jnp.maximum(a, 0.0)
        acc[...] += _nn((rl * rl).astype(BF16), w2_ref[...])

        @pl.when(j == nj - 1)
        def _():
            o_ref[...] = x_ref[...] + acc[...]

    row = pl.BlockSpec((ts, d), lambda i, j: (i, 0))
    return pl.pallas_call(
        body, out_shape=jax.ShapeDtypeStruct((s, d), F32), grid=(s // ts, nj),
        in_specs=[row, pl.BlockSpec((1, d), lambda i, j: (0, 0)),
                  _lw((d, FF_CHUNK), lambda i, j: (0, j), layer), _lw((FF_CHUNK, d), lambda i, j: (j, 0), layer)],
        out_specs=row, scratch_shapes=[pltpu.VMEM((ts, d), BF16), pltpu.VMEM((ts, d), F32)], name="ffn_fwd",
        compiler_params=_params(("parallel", "arbitrary")),
    )(x, nw, w1, w2)


def _ffn_bwd(x, dxo, nw, w1, w2, layer, after, ts=256):
    s, d = x.shape
    ts = min(ts, s)
    nj = D_FF // FF_CHUNK

    def body(x_ref, dxo_ref, nw_ref, w1_ref, w2_ref, after_ref, dx_ref, dnw_ref, h_ref, da_ref, r_ref, dxb_ref, dh):
        i, j = pl.program_id(0), pl.program_id(1)

        @pl.when(j == 0)
        def _():
            h_ref[...] = _rms_fwd(x_ref[...], nw_ref[...])[0].astype(BF16)
            dxb_ref[...] = dxo_ref[...].astype(BF16)
            dh[...] = jnp.zeros_like(dh)

        a = _nn(h_ref[...], w1_ref[...])
        rl = jnp.maximum(a, 0.0)
        r_ref[...] = (rl * rl).astype(BF16)
        da = (_nt(dxb_ref[...], w2_ref[...]) * (2.0 * rl)).astype(BF16)
        da_ref[...] = da
        dh[...] += _nt(da, w1_ref[...])

        @pl.when(jnp.logical_and(i == 0, j == 0))
        def _():
            dnw_ref[...] = jnp.zeros_like(dnw_ref)

        @pl.when(j == nj - 1)
        def _():
            xv = x_ref[...]
            r = lax.rsqrt(jnp.mean(xv * xv, axis=-1, keepdims=True) + EPS)
            dx, dw = _rms_bwd(xv, r, nw_ref[...], dh[...])
            dx_ref[...] = dxo_ref[...] + dx
            dnw_ref[...] += dw

    row = pl.BlockSpec((ts, d), lambda i, j: (i, 0))
    vec = pl.BlockSpec((1, d), lambda i, j: (0, 0))
    ffc = pl.BlockSpec((ts, FF_CHUNK), lambda i, j: (i, j))
    return pl.pallas_call(
        body,
        out_shape=(jax.ShapeDtypeStruct((s, d), F32), jax.ShapeDtypeStruct((1, d), F32), jax.ShapeDtypeStruct((s, d), BF16),
                   jax.ShapeDtypeStruct((s, D_FF), BF16), jax.ShapeDtypeStruct((s, D_FF), BF16), jax.ShapeDtypeStruct((s, d), BF16)),
        grid=(s // ts, nj),
        in_specs=[row, row, vec, _lw((d, FF_CHUNK), lambda i, j: (0, j), layer), _lw((FF_CHUNK, d), lambda i, j: (j, 0), layer), ANY],
        out_specs=(row, vec, row, ffc, ffc, row), scratch_shapes=[pltpu.VMEM((ts, d), F32)], name="ffn_bwd",
        compiler_params=_params(("arbitrary", "arbitrary")),
    )(x, dxo, nw, w1, w2, after)


def _attn_probs(q, k_ref):
    ps = []
    for hd in range(XA_HEADS):
        c = slice(hd * XA_HD, (hd + 1) * XA_HD)
        sc = _nt(q[:, c].astype(BF16), k_ref[:, c]) * (XA_HD ** -0.5)
        e = jnp.exp(sc - jnp.max(sc, axis=-1, keepdims=True))
        ps.append(e / jnp.sum(e, axis=-1, keepdims=True))
    return ps


def _attn_fwd(x, nw, wq, kv, wo, layer, ts=256):
    s, d = x.shape
    ts = min(ts, s)
    nm = kv.shape[0]

    def body(x_ref, nw_ref, wq_ref, k_ref, v_ref, wo_ref, o_ref):
        xv = x_ref[...]
        h = _rms_fwd(xv, nw_ref[...])[0].astype(BF16)
        q = _nn(h, wq_ref[...])
        ps = _attn_probs(q, k_ref)
        o = jnp.concatenate([_nn(ps[hd].astype(BF16), v_ref[:, hd * XA_HD:(hd + 1) * XA_HD]) for hd in range(XA_HEADS)], axis=1)
        o_ref[...] = xv + _nn(o.astype(BF16), wo_ref[...])

    row = pl.BlockSpec((ts, d), lambda i: (i, 0))
    full = lambda r, c: pl.BlockSpec((r, c), lambda i: (0, 0))
    wsp = _lw((d, d), lambda i: (0, 0), layer)
    return pl.pallas_call(
        body, out_shape=jax.ShapeDtypeStruct((s, d), F32), grid=(s // ts,),
        in_specs=[row, full(1, d), wsp, full(nm, d), pl.BlockSpec((nm, d), lambda i: (0, 1)), wsp], out_specs=row, name="attn_fwd",
        compiler_params=_params(("parallel",)),
    )(x, nw, wq, kv, kv, wo)


def _attn_bwd(x, dxo, nw, wq, kv, wo, layer, ts=256):
    s, d = x.shape
    ts = min(ts, s)
    nm = kv.shape[0]

    def body(x_ref, dxo_ref, nw_ref, wq_ref, k_ref, v_ref, wo_ref,
             dx_ref, dnw_ref, h_ref, o_ref, dq_ref, dxb_ref, dk_ref, dv_ref):
        xv = x_ref[...]
        hf, r = _rms_fwd(xv, nw_ref[...])
        h = hf.astype(BF16)
        h_ref[...] = h
        q = _nn(h, wq_ref[...])
        qb = q.astype(BF16)
        ps = _attn_probs(q, k_ref)
        dxb = dxo_ref[...].astype(BF16)
        dxb_ref[...] = dxb
        do = _nt(dxb, wo_ref[...])

        @pl.when(pl.program_id(0) == 0)
        def _():
            dnw_ref[...] = jnp.zeros_like(dnw_ref)
            dk_ref[...] = jnp.zeros_like(dk_ref)
            dv_ref[...] = jnp.zeros_like(dv_ref)

        dqs = []
        for hd in range(XA_HEADS):
            c = slice(hd * XA_HD, (hd + 1) * XA_HD)
            p = ps[hd]
            pb = p.astype(BF16)
            dob = do[:, c].astype(BF16)
            o_ref[:, c] = _nn(pb, v_ref[:, c]).astype(BF16)
            dp = _nt(dob, v_ref[:, c])
            ds = (p * (dp - jnp.sum(p * dp, axis=-1, keepdims=True)) * (XA_HD ** -0.5)).astype(BF16)
            dqs.append(_nn(ds, k_ref[:, c]))
            dk_ref[:, c] += _tn(ds, qb[:, c])
            dv_ref[:, c] += _tn(pb, dob)
        dq = jnp.concatenate(dqs, axis=1).astype(BF16)
        dq_ref[...] = dq
        dx, dw = _rms_bwd(xv, r, nw_ref[...], _nt(dq, wq_ref[...]))
        dx_ref[...] = dxo_ref[...] + dx
        dnw_ref[...] += dw

    row = pl.BlockSpec((ts, d), lambda i: (i, 0))
    full = lambda r, c: pl.BlockSpec((r, c), lambda i: (0, 0))
    sd = lambda dt: jax.ShapeDtypeStruct((s, d), dt)
    return pl.pallas_call(
        body,
        out_shape=(sd(F32), jax.ShapeDtypeStruct((1, d), F32), sd(BF16), sd(BF16), sd(BF16), sd(BF16),
                   jax.ShapeDtypeStruct((nm, d), F32), jax.ShapeDtypeStruct((nm, d), F32)),
        grid=(s // ts,),
        in_specs=[row, row, full(1, d), _lw((d, d), lambda i: (0, 0), layer), full(nm, d), pl.BlockSpec((nm, d), lambda i: (0, 1)),
                  _lw((d, d), lambda i: (0, 0), layer)],
        out_specs=(row, full(1, d), row, row, row, row, full(nm, d), full(nm, d)), name="attn_bwd",
        compiler_params=_params(("arbitrary",)),
    )(x, dxo, nw, wq, kv, kv, wo)


GATE_BLK0 = MIX_W // 512


def _merge_specs(ts, layer):
    row = pl.BlockSpec((ts, D_MODEL), lambda i: (i, 0))
    br = pl.BlockSpec((ts, 512), lambda i: (i, 0))
    gates = [pl.BlockSpec((ts, 512), functools.partial(lambda n, i: (i, GATE_BLK0 + n), n)) for n in range(8)]
    full = lambda *shape: pl.BlockSpec(shape, lambda i: (0,) * len(shape))
    weights = [full(4, D_MODEL), _lw((4, 512, D_MODEL), lambda i: (0, 0, 0), layer), _lw((D_MODEL, D_MODEL), lambda i: (0, 0), layer)]
    return row, br, gates, full, weights


def _merge_gates(gp_refs, gb_ref, kb):
    gp = jnp.concatenate([gp_refs[2 * kb][...], gp_refs[2 * kb + 1][...]], axis=1)
    return _sigmoid(gp + gb_ref[kb:kb + 1, :])


def _merge_fwd(x, branches, proj, gate_b, wb, wout, layer, ts=256):
    s, d = x.shape
    ts = min(ts, s)

    def body(x_ref, b0, b1, b2, b3, g0, g1, g2, g3, g4, g5, g6, g7, gb_ref, wb_ref, wo_ref, o_ref):
        brs, gps = (b0, b1, b2, b3), (g0, g1, g2, g3, g4, g5, g6, g7)
        merged = jnp.zeros((ts, d), F32)
        for kb in range(4):
            merged += _merge_gates(gps, gb_ref, kb) * _nn(brs[kb][...], wb_ref[kb])
        o_ref[...] = x_ref[...] + _nn(merged.astype(BF16), wo_ref[...])

    row, br, gates, full, weights = _merge_specs(ts, layer)
    return pl.pallas_call(
        body, out_shape=jax.ShapeDtypeStruct((s, d), F32), grid=(s // ts,),
        in_specs=[row, br, br, br, br] + gates + weights, out_specs=row, name="merge_fwd",
        compiler_params=_params(("parallel",)),
    )(x, *branches, *([proj] * 8), gate_b, wb, wout)


def _merge_bwd(dxo, branches, proj, gate_b, wb, wout, layer, after, ts=256):
    s, d = dxo.shape
    ts = min(ts, s)

    def body(dxo_ref, b0, b1, b2, b3, g0, g1, g2, g3, g4, g5, g6, g7, gb_ref, wb_ref, wo_ref, after_ref,
             db0, db1, db2, db3, dgp_ref, dup_ref, mg_ref, dxb_ref, dgb_ref):
        brs, gps, dbs = (b0, b1, b2, b3), (g0, g1, g2, g3, g4, g5, g6, g7), (db0, db1, db2, db3)
        dxb = dxo_ref[...].astype(BF16)
        dxb_ref[...] = dxb
        dm = _nt(dxb, wo_ref[...])

        @pl.when(pl.program_id(0) == 0)
        def _():
            dgb_ref[...] = jnp.zeros_like(dgb_ref)

        merged = jnp.zeros((ts, d), F32)
        for kb in range(4):
            c = slice(kb * d, (kb + 1) * d)
            g = _merge_gates(gps, gb_ref, kb)
            up = _nn(brs[kb][...], wb_ref[kb])
            merged += g * up
            dup = (dm * g).astype(BF16)
            dup_ref[:, c] = dup
            dgp = dm * up * g * (1.0 - g)
            dgp_ref[:, c] = dgp.astype(BF16)
            dgb_ref[kb:kb + 1, :] += jnp.sum(dgp, axis=0, keepdims=True)
            dbs[kb][...] = _nt(dup, wb_ref[kb])
        mg_ref[...] = merged.astype(BF16)

    row, br, gates, full, weights = _merge_specs(ts, layer)
    wide = pl.BlockSpec((ts, 4 * d), lambda i: (i, 0))
    sb = jax.ShapeDtypeStruct((s, 512), F32)
    return pl.pallas_call(
        body,
        out_shape=(sb, sb, sb, sb, jax.ShapeDtypeStruct((s, 4 * d), BF16), jax.ShapeDtypeStruct((s, 4 * d), BF16),
                   jax.ShapeDtypeStruct((s, d), BF16), jax.ShapeDtypeStruct((s, d), BF16), jax.ShapeDtypeStruct((4, d), F32)),
        grid=(s // ts,),
        in_specs=[row, br, br, br, br] + gates + weights + [ANY],
        out_specs=(br, br, br, br, wide, wide, row, row, full(4, d)), name="merge_bwd",
        compiler_params=_params(("arbitrary",)),
    )(dxo, *branches, *([proj] * 8), gate_b, wb, wout, after)


def _tri(n, upper=False):
    r = lax.broadcasted_iota(jnp.int32, (n, 3 * n), 0)
    c = lax.broadcasted_iota(jnp.int32, (n, 3 * n), 1) % n
    return jnp.where((c >= r) if upper else (c <= r), 1.0, 0.0).astype(BF16)


def _cum(tri3, x):
    hi = x.astype(BF16)
    r1 = x - hi.astype(F32)
    mid = r1.astype(BF16)
    lo = (r1 - mid.astype(F32)).astype(BF16)
    return _nn(tri3, jnp.concatenate([hi, mid, lo], axis=0))


def _hg_gates(hq, hf, lb):
    sg = _sigmoid(hf)
    fg = lb + (1.0 - lb) * sg
    sq = _sigmoid(hq)
    return sg, fg, 1.0 - fg, jnp.log(fg), hq * sq, sq


NSUB = CHUNK // SUB


def _hg_intra(qf, kk, b):
    row = lax.broadcasted_iota(jnp.int32, (CHUNK, 1), 0)
    refs = [b[i * SUB - 1:i * SUB, :] if i else jnp.zeros((1, b.shape[1]), F32) for i in range(NSUB)]
    mine = [jnp.logical_and(row >= i * SUB, row < (i + 1) * SUB) for i in range(NSUB)]
    ref_rows = refs[0]
    for i in range(1, NSUB):
        ref_rows = jnp.where(mine[i], refs[i], ref_rows)
    eq = jnp.exp(b - ref_rows)
    qt = qf * eq
    ek = jnp.concatenate([jnp.exp(jnp.minimum(r - b, EXP_CLAMP)) for r in refs], axis=1)
    kbig = jnp.concatenate([kk] * NSUB, axis=1) * ek
    qbig = jnp.concatenate([jnp.where(m, qt, 0.0) for m in mine], axis=1)
    return qt, qbig, kbig, eq, ek, mine


def _causal(n, upper=False):
    r, c = lax.broadcasted_iota(jnp.int32, (n, n), 0), lax.broadcasted_iota(jnp.int32, (n, n), 1)
    return (c >= r) if upper else (c <= r)


def _hg_chunk_fwd(qf, kk, b, v, st):
    parts = _hg_intra(qf, kk, b)
    att = jnp.where(_causal(CHUNK), _nt(parts[1].astype(BF16), parts[2].astype(BF16)), 0.0)
    qh = qf * jnp.exp(b)
    o = _nn(att.astype(BF16), v.astype(BF16)) + _nt(qh.astype(BF16), st.astype(BF16))
    bl = b[CHUNK - 1:CHUNK, :]
    kh = kk * jnp.exp(bl - b)
    return o, parts, att, qh, kh, jnp.exp(bl)


def _hgrn_fwd(proj, lb, nw, ts=256):
    s = proj.shape[0]
    ts = min(ts, s)
    nch = ts // CHUNK

    def body(q_ref, f_ref, v_ref, g_ref, lb_ref, nw_ref, o_ref, st_ref, st):
        @pl.when(pl.program_id(0) == 0)
        def _():
            st[...] = jnp.zeros_like(st)

        tri = _tri(CHUNK)

        def chunk(c, carry):
            rows = pl.ds(pl.multiple_of(c * CHUNK, CHUNK), CHUNK)
            _, _, kk, lf, qf, _ = _hg_gates(q_ref[rows, :], f_ref[rows, :], lb_ref[...])
            b = _cum(tri, lf)
            hv, hg = v_ref[rows, :], g_ref[rows, :]
            st_ref[c] = st[...]
            for h in range(HG_HEADS):
                cs = slice(h * HG_D, (h + 1) * HG_D)
                o, _, _, _, kh, ebl = _hg_chunk_fwd(qf[:, cs], kk[:, cs], b[:, cs], hv[:, cs], st[h])
                st[h] = st[h] * ebl + _tn(hv[:, cs].astype(BF16), kh.astype(BF16))
                on = _rms_fwd(o, nw_ref[...])[0]
                gh = hg[:, cs]
                o_ref[rows, cs] = (on * gh * _sigmoid(gh)).astype(BF16)
            return carry

        lax.fori_loop(0, nch, chunk, 0, unroll=4)

    col = lambda n: pl.BlockSpec((ts, 512), functools.partial(lambda n, i: (i, n), n))
    return pl.pallas_call(
        body,
        out_shape=(jax.ShapeDtypeStruct((s, 512), BF16), jax.ShapeDtypeStruct((s // CHUNK, HG_HEADS, HG_D, HG_D), F32)),
        grid=(s // ts,),
        in_specs=[col(0), col(1), col(2), col(3), pl.BlockSpec((1, 512), lambda i: (0, 0)), pl.BlockSpec((1, HG_D), lambda i: (0, 0))],
        out_specs=(pl.BlockSpec((ts, 512), lambda i: (i, 0)), pl.BlockSpec((nch, HG_HEADS, HG_D, HG_D), lambda i: (i, 0, 0, 0))),
        scratch_shapes=[pltpu.VMEM((HG_HEADS, HG_D, HG_D), F32)], name="hgrn_fwd",
        compiler_params=_params(("arbitrary",)),
    )(proj, proj, proj, proj, lb, nw)


def _hgrn_bwd(proj, dout, states, lb, nw, ts=256):
    s = proj.shape[0]
    ts = min(ts, s)
    nch = ts // CHUNK
    nt = s // ts

    def body(q_ref, f_ref, v_ref, g_ref, do_ref, st_ref, lb_ref, nw_ref, dp_ref, dlb_ref, dnw_ref, dst):
        @pl.when(pl.program_id(0) == 0)
        def _():
            dst[...] = jnp.zeros_like(dst)
            dlb_ref[...] = jnp.zeros_like(dlb_ref)
            dnw_ref[...] = jnp.zeros_like(dnw_ref)

        tri, triu = _tri(CHUNK), _tri(CHUNK, upper=True)
        last = lax.broadcasted_iota(jnp.int32, (CHUNK, HG_D), 0) == CHUNK - 1
        nwv = nw_ref[...]

        def chunk(cc, carry):
            c = nch - 1 - cc
            rows = pl.ds(pl.multiple_of(c * CHUNK, CHUNK), CHUNK)
            hq, hf, hv, hg = q_ref[rows, :], f_ref[rows, :], v_ref[rows, :], g_ref[rows, :]
            lbv = lb_ref[...]
            sg, fg, kk, lf, qf, sq = _hg_gates(hq, hf, lbv)
            b = _cum(tri, lf)
            dov = do_ref[rows, :]
            dqf_l, dkk_l, db_l, dv_l, dg_l = [], [], [], [], []
            for h in range(HG_HEADS):
                cs = slice(h * HG_D, (h + 1) * HG_D)
                stp = st_ref[c, h]
                bh, vh, gh = b[:, cs], hv[:, cs], hg[:, cs]
                o, parts, att, qh, kh, ebl = _hg_chunk_fwd(qf[:, cs], kk[:, cs], bh, vh, stp)
                sgg = _sigmoid(gh)
                on, r = _rms_fwd(o, nwv)
                d_on = dov[:, cs] * (gh * sgg)
                dg_l.append(dov[:, cs] * on * _dsilu(gh, sgg))
                do, dnw = _rms_bwd(o, r, nwv, d_on)
                dnw_ref[...] += dnw
                dob, vb = do.astype(BF16), vh.astype(BF16)
                dsth = dst[h]
                dstb = dsth.astype(BF16)
                dqh = _nn3(do, stp)
                dkh = _nn3(vh, dsth)
                dv = _nt(kh.astype(BF16), dstb)
                eb = jnp.exp(bh)
                ekl = jnp.exp(bh[CHUNK - 1:CHUNK, :] - bh)
                dqf, dkk = dqh * eb, dkh * ekl
                db = dqh * qh - dkh * kh
                dbl = jnp.sum(dkh * kh, axis=0, keepdims=True) + ebl * jnp.sum(dsth * stp, axis=0, keepdims=True)
                dst[h] = dsth * ebl + _tn(dob, qh.astype(BF16))
                qt, qbig, kbig, eq, ek, mine = parts
                da = jnp.where(_causal(CHUNK), _nt(dob, vb), 0.0)
                da_t = jnp.where(_causal(CHUNK, upper=True), _nt(vb, dob), 0.0)
                dv = dv + _tn(att.astype(BF16), dob)
                dqbig = _tn3(da_t, kbig)
                dkbig = _tn3(da, qbig)
                dkek, dkkb = dkbig * ek, dkbig * kbig
                dqt = jnp.zeros_like(qt)
                for i in range(NSUB):
                    bs = slice(i * HG_D, (i + 1) * HG_D)
                    dqt = dqt + jnp.where(mine[i], dqbig[:, bs], 0.0)
                    dkk = dkk + dkek[:, bs]
                    db = db - dkkb[:, bs]
                dqf = dqf + dqt * eq
                db = db + dqt * qt + jnp.where(last, dbl, 0.0)
                dqf_l.append(dqf); dkk_l.append(dkk); db_l.append(db); dv_l.append(dv)
            cat = lambda l: jnp.concatenate(l, axis=1)
            dlf = _cum(triu, cat(db_l))
            dfg = dlf / fg - cat(dkk_l)
            dlb_ref[...] += jnp.sum(dfg * (1.0 - sg), axis=0, keepdims=True)
            dp_ref[rows, 0:512] = (cat(dqf_l) * _dsilu(hq, sq)).astype(BF16)
            dp_ref[rows, 512:1024] = (dfg * (1.0 - lbv) * sg * (1.0 - sg)).astype(BF16)
            dp_ref[rows, 1024:1536] = cat(dv_l).astype(BF16)
            dp_ref[rows, 1536:2048] = cat(dg_l).astype(BF16)
            return carry

        lax.fori_loop(0, nch, chunk, 0, unroll=4)

    col = lambda n: pl.BlockSpec((ts, 512), functools.partial(lambda n, i: (nt - 1 - i, n), n))
    vec = lambda n: pl.BlockSpec((1, n), lambda i: (0, 0))
    return pl.pallas_call(
        body,
        out_shape=(jax.ShapeDtypeStruct((s, 2048), BF16), jax.ShapeDtypeStruct((1, 512), F32), jax.ShapeDtypeStruct((1, HG_D), F32)),
        grid=(nt,),
        in_specs=[col(0), col(1), col(2), col(3), pl.BlockSpec((ts, 512), lambda i: (nt - 1 - i, 0)),
                  pl.BlockSpec((nch, HG_HEADS, HG_D, HG_D), lambda i: (nt - 1 - i, 0, 0, 0)), vec(512), vec(HG_D)],
        out_specs=(pl.BlockSpec((ts, 2048), lambda i: (nt - 1 - i, 0)), vec(512), vec(HG_D)),
        scratch_shapes=[pltpu.VMEM((HG_HEADS, HG_D, HG_D), F32)], name="hgrn_bwd",
        compiler_params=_params(("arbitrary",)),
    )(proj, proj, proj, proj, dout, states, lb, nw)


CV_BLK = 2048 // 512


def _halo_before(ts, halo, colblk):
    return pl.BlockSpec((halo, 512), functools.partial(lambda cb, i: (jnp.maximum(i * (ts // halo) - 1, 0), cb), colblk))


def _cv_front(a_ref, g_ref, ah_ref, gh_ref, ext, first):
    a, sg = a_ref[...], _sigmoid(g_ref[...])
    zh = ah_ref[...] * _sigmoid(gh_ref[...])
    ext[0:CV_HALO, :] = jnp.where(first, 0.0, zh)
    ext[CV_HALO:, :] = a * sg
    return a, sg


CV_ROWS = 32


def _windows(ref, r0, base, ntaps, rows):
    out = []
    for phase in range(8):
        taps = [j for j in range(ntaps) if (base + j) % 8 == phase]
        if taps:
            span = max(base + j - phase for j in taps)
            big = ref[pl.ds(r0 + phase, rows + span), :]
            out += [(j, big[base + j - phase:base + j - phase + rows]) for j in taps]
    return out


def _cv_conv_ln(ext, w_ref, b_ref, r0):
    y = jnp.zeros((CV_ROWS, CV_W), F32) + b_ref[...]
    for j, win in _windows(ext, r0, CV_HALO - (CV_K - 1), CV_K, CV_ROWS):
        y = y + w_ref[j:j + 1, :] * win
    mu = jnp.mean(y, axis=-1, keepdims=True)
    yc = y - mu
    r = lax.rsqrt(jnp.mean(yc * yc, axis=-1, keepdims=True) + EPS)
    return yc * r, r


def _conv_fwd(proj, w, b, lnw, lnb, ts=256):
    s = proj.shape[0]
    ts = min(ts, s)

    def body(a_ref, g_ref, ah_ref, gh_ref, w_ref, b_ref, lnw_ref, lnb_ref, o_ref, ext):
        _cv_front(a_ref, g_ref, ah_ref, gh_ref, ext, pl.program_id(0) == 0)
        for r0 in range(0, ts, CV_ROWS):
            yh, _ = _cv_conv_ln(ext, w_ref, b_ref, r0)
            yn = yh * lnw_ref[...] + lnb_ref[...]
            o_ref[r0:r0 + CV_ROWS, :] = (yn * _sigmoid(yn)).astype(BF16)

    col = lambda n: pl.BlockSpec((ts, 512), functools.partial(lambda n, i: (i, n), n))
    vec = pl.BlockSpec((1, CV_W), lambda i: (0, 0))
    return pl.pallas_call(
        body, out_shape=jax.ShapeDtypeStruct((s, CV_W), BF16), grid=(s // ts,),
        in_specs=[col(CV_BLK), col(CV_BLK + 1), _halo_before(ts, CV_HALO, CV_BLK), _halo_before(ts, CV_HALO, CV_BLK + 1),
                  pl.BlockSpec((32, CV_W), lambda i: (0, 0)), vec, vec, vec],
        out_specs=pl.BlockSpec((ts, CV_W), lambda i: (i, 0)), scratch_shapes=[pltpu.VMEM((ts + CV_HALO, CV_W), F32)],
        name="conv_fwd", compiler_params=_params(("parallel",)),
    )(proj, proj, proj, proj, w, b, lnw, lnb)


def _conv_bwd(proj, dout, w, b, lnw, lnb, ts=256):
    s = proj.shape[0]
    ts = min(ts, s)
    nt = s // ts

    def body(a_ref, g_ref, ah_ref, gh_ref, do_ref, w_ref, b_ref, lnw_ref, lnb_ref,
             du_ref, dw_ref, db_ref, dlnw_ref, dlnb_ref, ext, dyext, carry, dwacc):
        i = pl.program_id(0)

        @pl.when(i == 0)
        def _():
            carry[...] = jnp.zeros_like(carry)
            dwacc[...] = jnp.zeros_like(dwacc)
            for ref in (db_ref, dlnw_ref, dlnb_ref):
                ref[...] = jnp.zeros_like(ref)

        _cv_front(a_ref, g_ref, ah_ref, gh_ref, ext, i == nt - 1)
        dyext[ts:, :] = carry[...]
        dlnw = dlnb = db = jnp.zeros((1, CV_W), F32)
        for r0 in range(0, ts, CV_ROWS):
            rows = slice(r0, r0 + CV_ROWS)
            yh, r = _cv_conv_ln(ext, w_ref, b_ref, r0)
            yn = yh * lnw_ref[...] + lnb_ref[...]
            dyn = do_ref[rows, :] * _dsilu(yn, _sigmoid(yn))
            dlnw += jnp.sum(dyn * yh, axis=0, keepdims=True)
            dlnb += jnp.sum(dyn, axis=0, keepdims=True)
            gl = dyn * lnw_ref[...]
            dy = r * (gl - jnp.mean(gl, axis=-1, keepdims=True) - yh * jnp.mean(gl * yh, axis=-1, keepdims=True))
            db += jnp.sum(dy, axis=0, keepdims=True)
            dyext[rows, :] = dy
            for j, win in _windows(ext, r0, CV_HALO - (CV_K - 1), CV_K, CV_ROWS):
                p = dy * win
                dwacc[8 * j:8 * j + 8, :] += (p[0:8] + p[8:16]) + (p[16:24] + p[24:32])
        dlnw_ref[...] += dlnw
        dlnb_ref[...] += dlnb
        db_ref[...] += db
        carry[...] = dyext[0:CV_HALO, :]
        for r0 in range(0, ts, CV_ROWS):
            rows = slice(r0, r0 + CV_ROWS)
            dz = jnp.zeros((CV_ROWS, CV_W), F32)
            for j, win in _windows(dyext, r0, 0, CV_K, CV_ROWS):
                dz = dz + w_ref[CV_K - 1 - j:CV_K - j, :] * win
            a, sg = a_ref[rows, :], _sigmoid(g_ref[rows, :])
            du_ref[rows, 0:CV_W] = (dz * sg).astype(BF16)
            du_ref[rows, CV_W:] = (dz * a * sg * (1.0 - sg)).astype(BF16)

        @pl.when(i == nt - 1)
        def _():
            for j in range(32):
                dw_ref[j:j + 1, :] = jnp.sum(dwacc[8 * j:8 * j + 8, :], axis=0, keepdims=True)

    rev = lambda n: pl.BlockSpec((ts, 512), functools.partial(lambda n, i: (nt - 1 - i, n), n))
    halo = lambda n: pl.BlockSpec((CV_HALO, 512), functools.partial(
        lambda n, i: (jnp.maximum((nt - 1 - i) * (ts // CV_HALO) - 1, 0), n), n))
    vec = pl.BlockSpec((1, CV_W), lambda i: (0, 0))
    wsp = pl.BlockSpec((32, CV_W), lambda i: (0, 0))
    v1 = jax.ShapeDtypeStruct((1, CV_W), F32)
    return pl.pallas_call(
        body, out_shape=(jax.ShapeDtypeStruct((s, 2 * CV_W), BF16), jax.ShapeDtypeStruct((32, CV_W), F32), v1, v1, v1),
        grid=(nt,),
        in_specs=[rev(CV_BLK), rev(CV_BLK + 1), halo(CV_BLK), halo(CV_BLK + 1), rev(0), wsp, vec, vec, vec],
        out_specs=(pl.BlockSpec((ts, 2 * CV_W), lambda i: (nt - 1 - i, 0)), wsp, vec, vec, vec),
        scratch_shapes=[pltpu.VMEM((ts + CV_HALO, CV_W), F32), pltpu.VMEM((ts + CV_HALO, CV_W), F32), pltpu.VMEM((CV_HALO, CV_W), F32),
                        pltpu.VMEM((8 * 32, CV_W), F32)],
        name="conv_bwd", compiler_params=_params(("arbitrary",)),
    )(proj, proj, proj, proj, dout, w, b, lnw, lnb)


PL_BLK = 3072 // 512


def _pool_windows(ext, t0, ts):
    n = ext.shape[0]
    t = t0 + lax.broadcasted_iota(jnp.int32, (ts, 1), 0)
    out = []
    for g, wdw in enumerate(POOL_WINDOWS):
        e = ext[:, g * 128:(g + 1) * 128]
        acc, k = e, 1
        while k < wdw:
            acc = acc + pltpu.roll(acc, k, 0)
            k *= 2
        cnt = jnp.minimum(t + 1, wdw).astype(F32)
        out.append(acc[POOL_HALO:] / cnt - e[POOL_HALO:])
    return out


def _pool_fwd(proj, w, sc, ts=256):
    s = proj.shape[0]
    ts = min(ts, s)

    def body(u_ref, uh_ref, w_ref, sc_ref, o_ref):
        i = pl.program_id(0)
        ext = jnp.concatenate([jnp.where(i == 0, 0.0, uh_ref[...]), u_ref[...]], axis=0)
        ps = _pool_windows(ext, i * ts, ts)
        y = jnp.concatenate([_nn(ps[g].astype(BF16), w_ref[g].astype(BF16)) for g in range(4)], axis=1)
        o_ref[...] = (y * sc_ref[...]).astype(BF16)

    return pl.pallas_call(
        body, out_shape=jax.ShapeDtypeStruct((s, 512), BF16), grid=(s // ts,),
        in_specs=[pl.BlockSpec((ts, 512), lambda i: (i, PL_BLK)), _halo_before(ts, POOL_HALO, PL_BLK),
                  pl.BlockSpec((4, 128, 128), lambda i: (0, 0, 0)), pl.BlockSpec((1, 512), lambda i: (0, 0))],
        out_specs=pl.BlockSpec((ts, 512), lambda i: (i, 0)), name="pool_fwd", compiler_params=_params(("parallel",)),
    )(proj, proj, w, sc)


def _pool_bwd(proj, dout, w, sc, ts=256):
    s = proj.shape[0]
    ts = min(ts, s)
    nt = s // ts
    n = ts + POOL_HALO

    def body(u_ref, uh_ref, do_ref, doh_ref, w_ref, sc_ref, du_ref, dw_ref, dsc_ref):
        i = pl.program_id(0)

        @pl.when(i == 0)
        def _():
            dw_ref[...] = jnp.zeros_like(dw_ref)
            dsc_ref[...] = jnp.zeros_like(dsc_ref)

        ext = jnp.concatenate([jnp.where(i == 0, 0.0, uh_ref[...]), u_ref[...]], axis=0)
        ps = _pool_windows(ext, i * ts, ts)
        dov = do_ref[...]
        dyext = jnp.concatenate([dov, jnp.where(i == nt - 1, 0.0, doh_ref[...])], axis=0) * sc_ref[...]
        t = i * ts + lax.broadcasted_iota(jnp.int32, (n, 1), 0)
        row = lax.broadcasted_iota(jnp.int32, (n, 1), 0)
        dus = []
        for g, wdw in enumerate(POOL_WINDOWS):
            cs = slice(g * 128, (g + 1) * 128)
            wg, pb = w_ref[g].astype(BF16), ps[g].astype(BF16)
            dsc_ref[:, cs] += jnp.sum(dov[:, cs] * _nn(pb, wg), axis=0, keepdims=True)
            dyg = dyext[:, cs].astype(BF16)
            dw_ref[g] += _tn(pb, dyg[0:ts])
            dp = _nt(dyg, wg)
            acc, k = dp / jnp.minimum(t + 1, wdw).astype(F32), 1
            while k < wdw:
                acc = acc + jnp.where(row < n - k, pltpu.roll(acc, n - k, 0), 0.0)
                k *= 2
            dus.append(acc[0:ts] - dp[0:ts])
        du_ref[...] = jnp.concatenate(dus, axis=1).astype(BF16)

    tile = lambda cb: pl.BlockSpec((ts, 512), functools.partial(lambda cb, i: (i, cb), cb))
    after = pl.BlockSpec((POOL_HALO, 512), lambda i: (jnp.minimum((i + 1) * (ts // POOL_HALO), s // POOL_HALO - 1), 0))
    wsp, vec = pl.BlockSpec((4, 128, 128), lambda i: (0, 0, 0)), pl.BlockSpec((1, 512), lambda i: (0, 0))
    return pl.pallas_call(
        body, out_shape=(jax.ShapeDtypeStruct((s, 512), BF16), jax.ShapeDtypeStruct((4, 128, 128), F32), jax.ShapeDtypeStruct((1, 512), F32)),
        grid=(nt,),
        in_specs=[tile(PL_BLK), _halo_before(ts, POOL_HALO, PL_BLK), tile(0), after, wsp, vec],
        out_specs=(tile(0), wsp, vec), name="pool_bwd", compiler_params=_params(("arbitrary",)),
    )(proj, proj, dout, dout, w, sc)


LX_BLK, LY_BLK = 3584 // 512, 4096 // 512
LRU_OFF = LRU_HALO - (LRU_CONV - 1)


def _scan_fwd(a, b):
    n = a.shape[0]
    row = lax.broadcasted_iota(jnp.int32, (n, 1), 0)
    k = 1
    while k < n:
        m = row >= k
        b = jnp.where(m, a * pltpu.roll(b, k, 0) + b, b)
        a = jnp.where(m, a * pltpu.roll(a, k, 0), a)
        k *= 2
    return a, b


def _scan_rev(a, b):
    n = a.shape[0]
    row = lax.broadcasted_iota(jnp.int32, (n, 1), 0)
    k = 1
    while k < n:
        m = row < n - k
        b = jnp.where(m, a * pltpu.roll(b, n - k, 0) + b, b)
        a = jnp.where(m, a * pltpu.roll(a, n - k, 0), a)
        k *= 2
    return b


def _lru_gates(x_ref, xh_ref, ext, first, cw_ref, cb_ref, wa_ref, ba_ref, wx_ref, bx_ref, lam_ref, ts):
    ext[0:LRU_HALO, :] = jnp.where(first, 0.0, xh_ref[...])
    ext[LRU_HALO:, :] = x_ref[...]
    xc = jnp.zeros((ts, LRU_W), F32) + cb_ref[...]
    for j in range(LRU_CONV):
        xc = xc + cw_ref[j:j + 1, :] * ext[pl.ds(LRU_OFF + j, ts), :]
    xb = xc.astype(BF16)
    r = _sigmoid(_nn(xb, wa_ref[...]) + ba_ref[...])
    ig = _sigmoid(_nn(xb, wx_ref[...]) + bx_ref[...])
    nl = -lam_ref[...]
    sp = jnp.maximum(nl, 0.0) + jnp.log(1.0 + jnp.exp(-jnp.abs(nl)))
    la = -LRU_C * r * sp
    a = jnp.exp(la)
    z = 2.0 * la
    em = jnp.where(z > -0.1, -z * (1.0 + z * 0.5 * (1.0 + z * (1.0 / 3) * (1.0 + z * 0.25 * (1.0 + z * 0.2)))), 1.0 - a * a)
    return xc, xb, r, ig, sp, a, jnp.sqrt(em)


def _lru_fwd(proj, cw, cb, wa, ba, wx, bx, lam, ts=256):
    s = proj.shape[0]
    ts = min(ts, s)

    def body(x_ref, xh_ref, y_ref, cw_ref, cb_ref, wa_ref, ba_ref, wx_ref, bx_ref, lam_ref, o_ref, h_ref, ext, hc):
        i = pl.program_id(0)

        @pl.when(i == 0)
        def _():
            hc[...] = jnp.zeros_like(hc)

        xc, _, _, ig, _, a, mult = _lru_gates(x_ref, xh_ref, ext, i == 0, cw_ref, cb_ref, wa_ref, ba_ref, wx_ref, bx_ref, lam_ref, ts)
        acum, h0 = _scan_fwd(a, mult * ig * xc)
        h = h0 + acum * hc[0:1, :]
        hc[...] = jnp.broadcast_to(h[ts - 1:ts, :], hc.shape)
        h_ref[...] = h
        o_ref[...] = (h * _gelu_parts(y_ref[...])[0]).astype(BF16)

    tile = lambda cb_: pl.BlockSpec((ts, 512), functools.partial(lambda c, i: (i, c), cb_))
    vec = pl.BlockSpec((1, LRU_W), lambda i: (0, 0))
    mat = pl.BlockSpec((LRU_W, LRU_W), lambda i: (0, 0))
    return pl.pallas_call(
        body, out_shape=(jax.ShapeDtypeStruct((s, LRU_W), BF16), jax.ShapeDtypeStruct((s, LRU_W), F32)), grid=(s // ts,),
        in_specs=[tile(LX_BLK), _halo_before(ts, LRU_HALO, LX_BLK), tile(LY_BLK), pl.BlockSpec((8, LRU_W), lambda i: (0, 0)),
                  vec, mat, vec, mat, vec, vec],
        out_specs=(tile(0), tile(0)), scratch_shapes=[pltpu.VMEM((ts + LRU_HALO, LRU_W), F32), pltpu.VMEM((8, LRU_W), F32)],
        name="lru_fwd", compiler_params=_params(("arbitrary",)),
    )(proj, proj, proj, cw, cb, wa, ba, wx, bx, lam)


def _lru_bwd(proj, hs, dout, cw, cb, wa, ba, wx, bx, lam, ts=256):
    s = proj.shape[0]
    ts = min(ts, s)
    nt = s // ts

    def body(x_ref, xh_ref, y_ref, h_ref, hh_ref, do_ref, cw_ref, cb_ref, wa_ref, ba_ref, wx_ref, bx_ref, lam_ref,
             dxy_ref, dcw_ref, dcb_ref, dwa_ref, dba_ref, dwx_ref, dbx_ref, dlam_ref, ext, dext, cg, cd):
        i = pl.program_id(0)
        first_tile = i == nt - 1

        @pl.when(i == 0)
        def _():
            cg[...] = jnp.zeros_like(cg)
            cd[...] = jnp.zeros_like(cd)
            for ref in (dcw_ref, dcb_ref, dwa_ref, dba_ref, dwx_ref, dbx_ref, dlam_ref):
                ref[...] = jnp.zeros_like(ref)

        xc, xb, r, ig, sp, a, mult = _lru_gates(x_ref, xh_ref, ext, first_tile, cw_ref, cb_ref, wa_ref, ba_ref, wx_ref, bx_ref, lam_ref, ts)
        row = lax.broadcasted_iota(jnp.int32, (ts, 1), 0)
        h, dov = h_ref[...], do_ref[...]
        gel, dgel = _gelu_parts(y_ref[...])
        dxy_ref[:, LRU_W:] = (dov * h * dgel).astype(BF16)
        alpha = jnp.where(row < ts - 1, pltpu.roll(a, ts - 1, 0), 0.0)
        g = _scan_rev(alpha, dov * gel + jnp.where(row == ts - 1, cg[0:1, :], 0.0))
        cg[...] = jnp.broadcast_to(a[0:1, :] * g[0:1, :], cg.shape)
        hprev = jnp.where(row == 0, jnp.where(first_tile, 0.0, hh_ref[LRU_HALO - 1:LRU_HALO, :]), pltpu.roll(h, 1, 0))
        dla = g * hprev * a - g * ig * xc * (a * a) / mult
        dpr = dla * (-LRU_C * sp) * r * (1.0 - r)
        dpi = g * mult * xc * ig * (1.0 - ig)
        dprb, dpib = dpr.astype(BF16), dpi.astype(BF16)
        dxc = g * mult * ig + _nt(dprb, wa_ref[...]) + _nt(dpib, wx_ref[...])
        dlam_ref[...] += jnp.sum(dla * (-LRU_C * r), axis=0, keepdims=True) * (-_sigmoid(-lam_ref[...]))
        dwa_ref[...] += _tn(xb, dprb)
        dwx_ref[...] += _tn(xb, dpib)
        dba_ref[...] += jnp.sum(dpr, axis=0, keepdims=True)
        dbx_ref[...] += jnp.sum(dpi, axis=0, keepdims=True)
        dcb_ref[...] += jnp.sum(dxc, axis=0, keepdims=True)
        dext[0:ts, :] = dxc
        dext[ts:, :] = cd[...]
        cd[...] = dxc[0:LRU_HALO, :]
        dx = jnp.zeros((ts, LRU_W), F32)
        for j in range(LRU_CONV):
            dcw_ref[j:j + 1, :] += jnp.sum(dxc * ext[pl.ds(LRU_OFF + j, ts), :], axis=0, keepdims=True)
            dx = dx + cw_ref[j:j + 1, :] * dext[pl.ds(LRU_CONV - 1 - j, ts), :]
        dxy_ref[:, 0:LRU_W] = dx.astype(BF16)

    rev = lambda c: pl.BlockSpec((ts, 512), functools.partial(lambda c, i: (nt - 1 - i, c), c))
    halo = lambda c: pl.BlockSpec((LRU_HALO, 512), functools.partial(
        lambda c, i: (jnp.maximum((nt - 1 - i) * (ts // LRU_HALO) - 1, 0), c), c))
    vec = pl.BlockSpec((1, LRU_W), lambda i: (0, 0))
    mat = pl.BlockSpec((LRU_W, LRU_W), lambda i: (0, 0))
    cws = pl.BlockSpec((8, LRU_W), lambda i: (0, 0))
    v1, m1 = jax.ShapeDtypeStruct((1, LRU_W), F32), jax.ShapeDtypeStruct((LRU_W, LRU_W), F32)
    return pl.pallas_call(
        body, out_shape=(jax.ShapeDtypeStruct((s, 2 * LRU_W), BF16), jax.ShapeDtypeStruct((8, LRU_W), F32), v1, m1, v1, m1, v1, v1),
        grid=(nt,),
        in_specs=[rev(LX_BLK), halo(LX_BLK), rev(LY_BLK), rev(0), halo(0), rev(0), cws, vec, mat, vec, mat, vec, vec],
        out_specs=(pl.BlockSpec((ts, 2 * LRU_W), lambda i: (nt - 1 - i, 0)), cws, vec, mat, vec, mat, vec, vec),
        scratch_shapes=[pltpu.VMEM((ts + LRU_HALO, LRU_W), F32), pltpu.VMEM((ts + LRU_HALO, LRU_W), F32),
                        pltpu.VMEM((8, LRU_W), F32), pltpu.VMEM((LRU_HALO, LRU_W), F32)],
        name="lru_bwd", compiler_params=_params(("arbitrary",)),
    )(proj, proj, proj, hs, hs, dout, cw, cb, wa, ba, wx, bx, lam)


def _final_loss(x, fw, tgt, ts=512):
    s, d = x.shape
    ts = min(ts, s)

    def body(x_ref, w_ref, t_ref, loss_ref, dx_ref, dw_ref):
        @pl.when(pl.program_id(0) == 0)
        def _():
            loss_ref[...] = jnp.zeros_like(loss_ref)
            dw_ref[...] = jnp.zeros_like(dw_ref)

        xv = x_ref[...]
        y, r = _rms_fwd(xv, w_ref[...])
        err = y - t_ref[...]
        loss_ref[...] += 0.5 * jnp.sum(jnp.mean(err * err, axis=-1, keepdims=True), axis=0, keepdims=True)
        dx, dw = _rms_bwd(xv, r, w_ref[...], err * (1.0 / d))
        dx_ref[...] = dx
        dw_ref[...] += dw

    row = pl.BlockSpec((ts, d), lambda i: (i, 0))
    vec = pl.BlockSpec((1, d), lambda i: (0, 0))
    return pl.pallas_call(
        body, out_shape=(jax.ShapeDtypeStruct((8, 128), F32), jax.ShapeDtypeStruct((s, d), F32), jax.ShapeDtypeStruct((1, d), F32)),
        grid=(s // ts,), in_specs=[row, vec, row], out_specs=(pl.BlockSpec((8, 128), lambda i: (0, 0)), row, vec),
        name="final_loss", compiler_params=_params(("arbitrary",)),
    )(x, fw, tgt)


def _lb_softmax(raw_ref):
    raw = raw_ref[...]
    e = jnp.exp(raw - jnp.max(raw, axis=0, keepdims=True))
    return e / jnp.sum(e, axis=0, keepdims=True)


def _lb_fwd(raw):
    def body(raw_ref, o_ref):
        sm = _lb_softmax(raw_ref)
        acc = jnp.zeros((1, sm.shape[1]), F32)
        o_ref[0:1, :] = acc
        for l in range(1, DEPTH):
            acc = acc + sm[l:l + 1, :]
            o_ref[l:l + 1, :] = acc

    return pl.pallas_call(body, out_shape=jax.ShapeDtypeStruct(raw.shape, F32), name="lb_fwd")(raw)


def _lb_bwd(raw, dlb):
    def body(raw_ref, d_ref, o_ref):
        sm = _lb_softmax(raw_ref)
        dlbv = d_ref[...]
        dsm, acc = [None] * DEPTH, jnp.zeros((1, sm.shape[1]), F32)
        for l in range(DEPTH - 1, 0, -1):
            acc = acc + dlbv[l:l + 1, :]
            dsm[l] = acc
        dsm[0] = jnp.zeros_like(acc)
        dsm = jnp.concatenate(dsm, axis=0)
        o_ref[...] = sm * (dsm - jnp.sum(sm * dsm, axis=0, keepdims=True))

    return pl.pallas_call(body, out_shape=jax.ShapeDtypeStruct(raw.shape, F32), name="lb_bwd")(raw, dlb)


def _adam_math(w, g, m, v):
    m = ADAM_B1 * m + (1.0 - ADAM_B1) * g
    v = ADAM_B2 * v + (1.0 - ADAM_B2) * (g * g)
    m_hat = m / (1.0 - ADAM_B1 ** ADAM_STEP)
    v_hat = v / (1.0 - ADAM_B2 ** ADAM_STEP)
    return -ADAM_LR * (m_hat / (jnp.sqrt(v_hat) + ADAM_EPS) + ADAM_WD * w), m, v


def _adamw(w, gs, m, v, name, tr=128):
    r, c = w.shape
    tr = min(tr, r)
    ng = len(gs)

    def body(*refs):
        w_ref, g_refs, m_ref, v_ref = refs[0], refs[1:1 + ng], refs[1 + ng], refs[2 + ng]
        outs = refs[3 + ng:]
        g = g_refs[0][...]
        if ng == 2:
            g = g + g_refs[1][...]
            outs[0][...] = g
            outs = outs[1:]
        for o, val in zip(outs, _adam_math(w_ref[...], g, m_ref[...], v_ref[...])):
            o[...] = val

    blk = pl.BlockSpec((tr, c), lambda i: (i, 0))
    sd = jax.ShapeDtypeStruct((r, c), F32)
    nout = 3 + (ng == 2)
    return pl.pallas_call(
        body, out_shape=(sd,) * nout, grid=(r // tr,), in_specs=[blk] * (3 + ng), out_specs=(blk,) * nout, name=name,
        compiler_params=_params(("parallel",)),
    )(w, *gs, m, v)


def _cast_into_full(w, kind, jj, name, tr=256):
    l, r, c = w.shape
    tr = min(tr, r)

    def body(j_ref, w_ref, o_ref):
        o_ref[...] = w_ref[...].astype(BF16)

    if kind == "col":
        full, dst = (l, r, 4 * c), pl.BlockSpec((None, tr, c), lambda a, b, j: (a, b, j[0]))
    else:
        full, dst = (l, 4 * r, c), pl.BlockSpec((None, tr, c), lambda a, b, j: (a, j[0] * (r // tr) + b, 0))
    return pl.pallas_call(
        body, out_shape=jax.ShapeDtypeStruct(full, BF16),
        grid_spec=pltpu.PrefetchScalarGridSpec(
            num_scalar_prefetch=1, grid=(l, r // tr), in_specs=[pl.BlockSpec((None, tr, c), lambda a, b, j: (a, b, 0))], out_specs=dst),
        name=name, compiler_params=_params(("parallel", "parallel")),
    )(jj, w)


def _place():
    return lax.axis_index("x"), lax.axis_index("y"), lax.axis_index("c")


def _other_chips(x, y):
    return [(1 - x, y), (x, 1 - y), (1 - x, 1 - y)]


def _slab(ref, kind, jj):
    if kind == "col":
        c = ref.shape[2] // 4
        return ref.at[:, :, pl.ds(jj * c, c)]
    r = ref.shape[1] // 4
    return ref.at[:, pl.ds(jj * r, r), :]


HBM = pl.BlockSpec(memory_space=pltpu.HBM)
SEM = pl.BlockSpec(memory_space=pltpu.SEMAPHORE)
EFFECT = pltpu.SideEffectType.DATAFLOW_SIDE_EFFECTING


def _in_hbm(a):
    return pltpu.with_memory_space_constraint(a, pltpu.HBM)


def _thru(arrs):
    return [pltpu.HBM(a.shape, a.dtype) for a in arrs]


def _half_slab(ref, kind, group, jj, half):
    per = ref.shape[0] // DEPTH
    layers = pl.ds(group * per, per)
    if kind == "col":
        r, c = ref.shape[1] // 2, ref.shape[2] // 4
        return ref.at[layers, pl.ds(half * r, r), pl.ds(jj * c, c)]
    r = ref.shape[1] // 8
    return ref.at[layers, pl.ds((2 * jj + half) * r, r), :]


def _gather_copy(fulls, kinds, send_sems, recv_sems, group, t, k, landing):
    x, y, c = _place()
    chip = _other_chips(x, y)[k]
    idx = (group * len(fulls) + t) * 3 + k
    return pltpu.make_async_remote_copy(
        src_ref=_half_slab(fulls[t], kinds[t], group, 2 * x + y, c), dst_ref=_half_slab(fulls[t], kinds[t], group, landing, c),
        send_sem=send_sems.at[idx], recv_sem=recv_sems.at[idx], device_id=(chip[0], chip[1], c), device_id_type=MESH)


def _fill_copy(fulls, kinds, send_sems, recv_sems, group, which, t, k, half):
    x, y, c = _place()
    chip = _other_chips(x, y)[k]
    idx = which.index(t) * 3 + k
    jj = 2 * chip[0] + chip[1]
    return pltpu.make_async_remote_copy(
        src_ref=_half_slab(fulls[t], kinds[t], group, jj, c), dst_ref=_half_slab(fulls[t], kinds[t], group, jj, half),
        send_sem=send_sems.at[idx], recv_sem=recv_sems.at[idx], device_id=(x, y, 1 - c), device_id_type=MESH)


def _fill_start(group, which, fulls, kinds):
    nt = len(fulls)
    ncp = 3 * len(which)

    def body(*refs):
        ins, send_sems, recv_sems = refs[:nt], refs[nt], refs[nt + 1]
        _, _, c = _place()
        for t in which:
            for k in range(3):
                _fill_copy(ins, kinds, send_sems, recv_sems, group, which, t, k, c).start()

    out = pl.pallas_call(
        body, out_shape=(pltpu.SemaphoreType.DMA((ncp,)), pltpu.SemaphoreType.DMA((ncp,)), *_thru(fulls)),
        in_specs=[HBM] * nt, out_specs=(SEM, SEM, *([HBM] * nt)), input_output_aliases={t: 2 + t for t in range(nt)},
        name="fill_start_%d_%d" % (group, which[0]), compiler_params=pltpu.CompilerParams(has_side_effects=EFFECT),
    )(*fulls)
    return out[0], out[1], list(out[2:])


def _fill_wait(group, which, send_sems, recv_sems, fulls, kinds, after):
    nt = len(fulls)

    def body(*refs):
        ins, send_ref, recv_ref = refs[:nt], refs[nt], refs[nt + 1]
        _, _, c = _place()
        for t in which:
            for k in range(3):
                cp = _fill_copy(ins, kinds, send_ref, recv_ref, group, which, t, k, 1 - c)
                cp.wait_send()
                cp.wait_recv()

    out = pl.pallas_call(
        body, out_shape=tuple(_thru(fulls)), in_specs=[HBM] * nt + [SEM, SEM, ANY], out_specs=tuple([HBM] * nt),
        input_output_aliases={t: t for t in range(nt)}, name="fill_wait_%d_%d" % (group, which[0]),
        compiler_params=pltpu.CompilerParams(has_side_effects=EFFECT),
    )(*fulls, send_sems, recv_sems, after)
    return list(out)


def _gather_start(fulls, kinds, after):
    nt = len(fulls)
    ncp = DEPTH * nt * 3

    def body(*refs):
        ins, send_sems, recv_sems = refs[:nt], refs[nt + 1], refs[nt + 2]
        x, y, _ = _place()
        for group in range(DEPTH):
            for t in range(nt):
                for k in range(3):
                    _gather_copy(ins, kinds, send_sems, recv_sems, group, t, k, 2 * x + y).start()

    out = pl.pallas_call(
        body, out_shape=(pltpu.SemaphoreType.DMA((ncp,)), pltpu.SemaphoreType.DMA((ncp,)), *_thru(fulls)),
        in_specs=[HBM] * nt + [ANY], out_specs=(SEM, SEM, *([HBM] * nt)), input_output_aliases={t: 2 + t for t in range(nt)},
        name="gather_start", compiler_params=pltpu.CompilerParams(has_side_effects=EFFECT),
    )(*[_in_hbm(a) for a in fulls], after)
    return out[0], out[1], list(out[2:])


def _gather_wait(group, which, send_sems, recv_sems, fulls, kinds, after):
    nt = len(fulls)

    def body(*refs):
        ins, send_ref, recv_ref = refs[:nt], refs[nt], refs[nt + 1]
        x, y, _ = _place()
        chips = _other_chips(x, y)
        for t in which:
            for k in range(3):
                cp = _gather_copy(ins, kinds, send_ref, recv_ref, group, t, k, 2 * chips[k][0] + chips[k][1])
                cp.wait_send()
                cp.wait_recv()

    out = pl.pallas_call(
        body, out_shape=tuple(_thru(fulls)), in_specs=[HBM] * nt + [SEM, SEM, ANY], out_specs=tuple([HBM] * nt),
        input_output_aliases={t: t for t in range(nt)}, name="gather_wait_%d_%d" % (group, which[0]),
        compiler_params=pltpu.CompilerParams(has_side_effects=EFFECT),
    )(*fulls, send_sems, recv_sems, after)
    return list(out)


def _scatter_copy(grads, lands, kinds, send_sems, recv_sems, t, k):
    x, y, c = _place()
    chip = _other_chips(x, y)[k]
    return pltpu.make_async_remote_copy(
        src_ref=_slab(grads[t], kinds[t], 2 * chip[0] + chip[1]), dst_ref=lands[t].at[k], send_sem=send_sems.at[3 * t + k],
        recv_sem=recv_sems.at[3 * t + k], device_id=(chip[0], chip[1], c), device_id_type=MESH)


def _scatter_start(grads, kinds, after, name):
    nt = len(grads)
    lands = []
    for g, kd in zip(grads, kinds):
        l, r, c = g.shape
        lands.append(lax.empty((3, l, r, c // 4) if kd == "col" else (3, l, r // 4, c), g.dtype))

    def body(*refs):
        ins, lnd, send_sems, recv_sems = refs[:nt], refs[nt:2 * nt], refs[2 * nt + 1], refs[2 * nt + 2]
        for t in range(nt):
            for k in range(3):
                _scatter_copy(ins, lnd, kinds, send_sems, recv_sems, t, k).start()
        refs[-1][...] = jnp.zeros_like(refs[-1])

    out = pl.pallas_call(
        body, out_shape=(pltpu.SemaphoreType.DMA((3 * nt,)), pltpu.SemaphoreType.DMA((3 * nt,)), *_thru(grads), *_thru(lands),
                         jax.ShapeDtypeStruct((8, 128), F32)),
        in_specs=[HBM] * (2 * nt) + [ANY], out_specs=(SEM, SEM, *([HBM] * (2 * nt)), pl.BlockSpec(memory_space=pltpu.VMEM)),
        input_output_aliases={t: 2 + t for t in range(2 * nt)}, name=name,
        compiler_params=pltpu.CompilerParams(has_side_effects=EFFECT),
    )(*[_in_hbm(a) for a in grads], *[_in_hbm(a) for a in lands], after)
    return (out[0], out[1], list(out[2:2 + nt]), list(out[2 + nt:2 + 2 * nt])), out[-1]


def _scatter_wait(send_sems, recv_sems, grads, lands, kinds, after, name):
    nt = len(grads)

    def body(*refs):
        ins, lnd, send_ref, recv_ref = refs[:nt], refs[nt:2 * nt], refs[2 * nt], refs[2 * nt + 1]
        for t in range(nt):
            for k in range(3):
                cp = _scatter_copy(ins, lnd, kinds, send_ref, recv_ref, t, k)
                cp.wait_send()
                cp.wait_recv()

    out = pl.pallas_call(
        body, out_shape=(*_thru(grads), *_thru(lands)), in_specs=[HBM] * (2 * nt) + [SEM, SEM, ANY],
        out_specs=tuple([HBM] * (2 * nt)), input_output_aliases={t: t for t in range(2 * nt)}, name=name,
        compiler_params=pltpu.CompilerParams(has_side_effects=EFFECT),
    )(*grads, *lands, send_sems, recv_sems, after)
    return list(out[:nt]), list(out[nt:])


def _sibling_swap(arrs, name="sibling_swap"):
    nt = len(arrs)

    def body(*refs):
        ins, outs = refs[:nt], refs[nt:2 * nt]
        send_sems, recv_sems = refs[2 * nt:]
        x, y, c = _place()
        sends = [pltpu.make_async_remote_copy(src_ref=ins[t], dst_ref=outs[t], send_sem=send_sems.at[t], recv_sem=recv_sems.at[t],
                                              device_id=(x, y, 1 - c), device_id_type=MESH) for t in range(nt)]
        for cp in sends:
            cp.start()
        for cp in sends:
            cp.wait_recv()
        for cp in sends:
            cp.wait_send()

    return pl.pallas_call(
        body, out_shape=[jax.ShapeDtypeStruct(a.shape, a.dtype) for a in arrs], in_specs=[ANY] * nt, out_specs=[ANY] * nt,
        scratch_shapes=[pltpu.SemaphoreType.DMA((nt,)), pltpu.SemaphoreType.DMA((nt,))], name=name,
    )(*arrs)


def _gather_small(vec, over_c):
    n = vec.shape[0]
    flips = [(dx, dy, dc) for dx in (0, 1) for dy in (0, 1) for dc in ((0, 1) if over_c else (0,))][1:]
    np_ = len(flips)

    def body(v_ref, o_ref, send_sems, recv_sems, local_sem):
        x, y, c = _place()

        def idx(px, py, pc):
            return 4 * px + 2 * py + pc if over_c else 2 * px + py

        def peer(f):
            return (1 - x if f[0] else x, 1 - y if f[1] else y, 1 - c if f[2] else c)

        def push(k, landing):
            return pltpu.make_async_remote_copy(src_ref=v_ref, dst_ref=o_ref.at[landing], send_sem=send_sems.at[k],
                                                recv_sem=recv_sems.at[k], device_id=peer(flips[k]), device_id_type=MESH)

        mine = pltpu.make_async_copy(v_ref, o_ref.at[idx(x, y, c)], local_sem)
        sends = [push(k, idx(x, y, c)) for k in range(np_)]
        for cp in [mine] + sends:
            cp.start()
        for k in range(np_):
            push(k, idx(*peer(flips[k]))).wait_recv()
        for cp in sends:
            cp.wait_send()
        mine.wait()

    return pl.pallas_call(
        body, out_shape=jax.ShapeDtypeStruct((np_ + 1, n, 128), F32), in_specs=[ANY], out_specs=ANY,
        scratch_shapes=[pltpu.SemaphoreType.DMA((np_,)), pltpu.SemaphoreType.DMA((np_,)), pltpu.SemaphoreType.DMA(())],
        name="gather_small_all" if over_c else "gather_small_xy",
    )(vec)


def _sum_rows(buf, after, tr=512):
    p, n, _ = buf.shape
    tr = min(tr, n)

    def body(b_ref, after_ref, o_ref):
        acc = b_ref[0]
        for k in range(1, p):
            acc = acc + b_ref[k]
        o_ref[...] = acc

    return pl.pallas_call(
        body, out_shape=jax.ShapeDtypeStruct((n, 128), F32), grid=(n // tr,),
        in_specs=[pl.BlockSpec((p, tr, 128), lambda i: (0, i, 0)), ANY], out_specs=pl.BlockSpec((tr, 128), lambda i: (i, 0)),
        name="sum_rows", compiler_params=_params(("parallel",)),
    )(buf, after)


def _sum_partials_into(stack, at, depth, grad, recv, kind, jj, name, tr=128):
    _, _, r, c = recv.shape
    tr = min(tr, r)

    def body(j_ref, g_ref, r0, r1, r2, *rest):
        rest[-1][...] = ((g_ref[...].astype(F32) + r0[...].astype(F32)) + r1[...].astype(F32)) + r2[...].astype(F32)

    if kind == "col":
        own = pl.BlockSpec((None, tr, c), lambda b, j: (0, b, j[0]))
    else:
        own = pl.BlockSpec((None, tr, c), lambda b, j: (0, j[0] * (r // tr) + b, 0))
    got = lambda k: pl.BlockSpec((None, None, tr, c), functools.partial(lambda k, b, j: (k, 0, b, 0), k))
    chained = stack is not None
    return pl.pallas_call(
        body, out_shape=jax.ShapeDtypeStruct((depth, r, c), F32),
        grid_spec=pltpu.PrefetchScalarGridSpec(
            num_scalar_prefetch=1, grid=(r // tr,), in_specs=[own, got(0), got(1), got(2)] + ([ANY] if chained else []),
            out_specs=pl.BlockSpec((None, tr, c), lambda b, j: (at, b, 0))),
        input_output_aliases={5: 0} if chained else {}, name=name, compiler_params=_params(("parallel",)),
    )(*([jj, grad, recv, recv, recv] + ([stack] if chained else [])))


WEIGHTS = ['norm_mix_w', 'w_in', 'hg_lb_raw', 'hg_norm_w', 'cv_dw_w', 'cv_dw_b', 'cv_ln_w', 'cv_ln_b', 'pl_w', 'pl_scale',
           'lru_conv_w', 'lru_conv_b', 'lru_wa', 'lru_ba', 'lru_wx', 'lru_bx', 'lru_lambda', 'gate_b', 'w_branch', 'w_out',
           'norm_mem_w', 'mem_norm_w', 'xa_wq', 'xa_wkv', 'xa_wo', 'norm_ffn_w', 'ffn_w1', 'ffn_w2', 'final_norm_w']
BIG = {'w_in': 'col', 'w_branch': 'col', 'w_out': 'row', 'xa_wq': 'row', 'xa_wkv': 'col', 'xa_wo': 'row', 'ffn_w1': 'col', 'ffn_w2': 'row'}
SMALL_SPLIT = ('gate_b', 'cv_dw_w', 'lru_conv_w')
SMALL = [n for n in WEIGHTS if n not in BIG]
PIECE_GROUPS = [['ffn_w1', 'ffn_w2', 'xa_wq', 'xa_wkv', 'xa_wo'],
                ['w_out', ('w_branch', 0), ('w_branch', 1), ('w_branch', 2), ('w_branch', 3), 'w_in']]


def _piece_kinds(pieces):
    return [BIG[k[0] if isinstance(k, tuple) else k] for k in pieces]
ROWS_PAD = 512


def _as3d(a):
    return a.reshape((-1,) + a.shape[-2:])


def _pack(parts):
    flat = jnp.concatenate([p.reshape(-1).astype(F32) for p in parts])
    n = -(-flat.shape[0] // (128 * ROWS_PAD)) * ROWS_PAD
    return jnp.pad(flat, (0, n * 128 - flat.shape[0])).reshape(n, 128)


def _unpack(packed, shapes):
    flat, out, o = packed.reshape(-1), [], 0
    for sh in shapes:
        sz = math.prod(sh)
        out.append(flat[o:o + sz].reshape(sh))
        o += sz
    return out


def _block_diag(w):
    h, a, b = w.shape
    eye = jnp.eye(h, dtype=w.dtype)
    return (w[:, :, None, :] * eye[:, None, :, None]).reshape(h * a, h * b)


def _diag_blocks(m, h):
    a, b = m.shape[0] // h, m.shape[1] // h
    return jnp.stack([m[i * a:(i + 1) * a, i * b:(i + 1) * b] for i in range(h)])


def kernel(x, mem, norm_mix_w, w_in, hg_lb_raw, hg_norm_w, cv_dw_w, cv_dw_b, cv_ln_w, cv_ln_b, pl_w, pl_scale, lru_conv_w, lru_conv_b, lru_wa, lru_ba, lru_wx, lru_bx, lru_lambda, gate_b, w_branch, w_out, norm_mem_w, mem_norm_w, xa_wq, xa_wkv, xa_wo, norm_ffn_w, ffn_w1, ffn_w2, final_norm_w, loss_target, m_norm_mix_w, m_w_in, m_hg_lb_raw, m_hg_norm_w, m_cv_dw_w, m_cv_dw_b, m_cv_ln_w, m_cv_ln_b, m_pl_w, m_pl_scale, m_lru_conv_w, m_lru_conv_b, m_lru_wa, m_lru_ba, m_lru_wx, m_lru_bx, m_lru_lambda, m_gate_b, m_w_branch, m_w_out, m_norm_mem_w, m_mem_norm_w, m_xa_wq, m_xa_wkv, m_xa_wo, m_norm_ffn_w, m_ffn_w1, m_ffn_w2, m_final_norm_w, v_norm_mix_w, v_w_in, v_hg_lb_raw, v_hg_norm_w, v_cv_dw_w, v_cv_dw_b, v_cv_ln_w, v_cv_ln_b, v_pl_w, v_pl_scale, v_lru_conv_w, v_lru_conv_b, v_lru_wa, v_lru_ba, v_lru_wx, v_lru_bx, v_lru_lambda, v_gate_b, v_w_branch, v_w_out, v_norm_mem_w, v_mem_norm_w, v_xa_wq, v_xa_wkv, v_xa_wo, v_norm_ffn_w, v_ffn_w1, v_ffn_w2, v_final_norm_w):
    w = dict(zip(WEIGHTS, (norm_mix_w, w_in, hg_lb_raw, hg_norm_w, cv_dw_w, cv_dw_b, cv_ln_w, cv_ln_b, pl_w, pl_scale, lru_conv_w, lru_conv_b, lru_wa, lru_ba, lru_wx, lru_bx, lru_lambda, gate_b, w_branch, w_out, norm_mem_w, mem_norm_w, xa_wq, xa_wkv, xa_wo, norm_ffn_w, ffn_w1, ffn_w2, final_norm_w)))
    m1 = dict(zip(WEIGHTS, (m_norm_mix_w, m_w_in, m_hg_lb_raw, m_hg_norm_w, m_cv_dw_w, m_cv_dw_b, m_cv_ln_w, m_cv_ln_b, m_pl_w, m_pl_scale, m_lru_conv_w, m_lru_conv_b, m_lru_wa, m_lru_ba, m_lru_wx, m_lru_bx, m_lru_lambda, m_gate_b, m_w_branch, m_w_out, m_norm_mem_w, m_mem_norm_w, m_xa_wq, m_xa_wkv, m_xa_wo, m_norm_ffn_w, m_ffn_w1, m_ffn_w2, m_final_norm_w)))
    v1 = dict(zip(WEIGHTS, (v_norm_mix_w, v_w_in, v_hg_lb_raw, v_hg_norm_w, v_cv_dw_w, v_cv_dw_b, v_cv_ln_w, v_cv_ln_b, v_pl_w, v_pl_scale, v_lru_conv_w, v_lru_conv_b, v_lru_wa, v_lru_ba, v_lru_wx, v_lru_bx, v_lru_lambda, v_gate_b, v_w_branch, v_w_out, v_norm_mem_w, v_mem_norm_w, v_xa_wq, v_xa_wkv, v_xa_wo, v_norm_ffn_w, v_ffn_w1, v_ffn_w2, v_final_norm_w)))
    seq = x.shape[1]
    xs, mems, tgt = x.reshape(seq, D_MODEL), mem.reshape(-1, D_MODEL), loss_target.reshape(seq, D_MODEL)
    jj = 2 * lax.axis_index("x") + lax.axis_index("y")
    jj1 = jnp.reshape(jj, (1,)).astype(jnp.int32)

    split_shapes = [w[n].shape for n in SMALL_SPLIT]
    got = _gather_small(_pack([w[n] for n in SMALL_SPLIT]), over_c=False)
    per_chip = [_unpack(got[k], split_shapes) for k in range(4)]
    full_small = {n: jnp.concatenate([per_chip[k][i] for k in range(4)], axis=-1) for i, n in enumerate(SMALL_SPLIT)}
    big_names = list(BIG)
    kinds = [BIG[n] for n in big_names]
    g_send, g_recv, fulls = _gather_start([_cast_into_full(_as3d(w[n]), BIG[n], jj1, "cast_" + n) for n in big_names], kinds, got)
    tix = {n: t for t, n in enumerate(big_names)}

    lb = _lb_fwd(hg_lb_raw)
    row = lambda a: a.reshape(1, -1)

    def layer_params(l):
        return dict(
            nmix=row(norm_mix_w[l]), lb=row(lb[l]), hgnw=row(hg_norm_w[l]),
            cw=jnp.pad(full_small['cv_dw_w'][l], ((0, 32 - CV_K), (0, 0))), cb=row(cv_dw_b[l]), lnw=row(cv_ln_w[l]), lnb=row(cv_ln_b[l]),
            plw=pl_w[l], plsc=row(pl_scale[l]),
            lcw=jnp.pad(full_small['lru_conv_w'][l], ((0, 8 - LRU_CONV), (0, 0))), lcb=row(lru_conv_b[l]),
            wa=_block_diag(lru_wa[l]).astype(BF16), ba=row(lru_ba[l]), wx=_block_diag(lru_wx[l]).astype(BF16), bx=row(lru_bx[l]),
            lam=row(lru_lambda[l]), gb=full_small['gate_b'][l], nmem=row(norm_mem_w[l]), memw=row(mem_norm_w[l]), nffn=row(norm_ffn_w[l]))

    saved = []
    xc = xs
    def by_name(arrays):
        wf_ = dict(zip(big_names, arrays))
        wf_['w_branch'] = wf_['w_branch'].reshape(DEPTH, 4, 512, D_MODEL)
        return wf_

    def landed(l, names, arrays, after):
        which = [tix[n] for n in names]
        arrays = _gather_wait(l, which, g_send, g_recv, arrays, kinds, after)
        f_send, f_recv, arrays = _fill_start(l, which, arrays, kinds)
        return (which, f_send, f_recv), arrays

    def complete(l, pending, arrays, after):
        which, f_send, f_recv = pending
        return _fill_wait(l, which, f_send, f_recv, arrays, kinds, after)

    rest = [n for n in big_names if n != 'w_in']
    for l in range(DEPTH):
        if l == 0:
            pending, fulls = landed(0, ['w_in'], fulls, xc)
        fulls = complete(l, pending, fulls, xc)
        wf = by_name(fulls)
        p = layer_params(l)
        h = _norm_fwd(xc, p['nmix'], "norm_mix")
        proj = _mm(h, wf['w_in'], "nn", F32, "proj", tm=seq, layer=l)
        if l == 0:
            pending, fulls = landed(0, rest, fulls, proj)
        b_hg, st = _hgrn_fwd(proj, p['lb'], p['hgnw'])
        b_cv = _conv_fwd(proj, p['cw'], p['cb'], p['lnw'], p['lnb'])
        b_pl = _pool_fwd(proj, p['plw'], p['plsc'])
        b_lru, hs = _lru_fwd(proj, p['lcw'], p['lcb'], p['wa'], p['ba'], p['wx'], p['bx'], p['lam'])
        branches = (b_hg, b_cv, b_pl, b_lru)
        if l == 0:
            fulls = complete(0, pending, fulls, b_lru)
            wf = by_name(fulls)
        x1 = _merge_fwd(xc, branches, proj, p['gb'], wf['w_branch'], wf['w_out'], l)
        if l + 1 < DEPTH:
            pending, fulls = landed(l + 1, big_names, fulls, x1)
            wf = by_name(fulls)
        memn = _norm_fwd(mems, p['memw'], "norm_memtok")
        kv = _mm(memn, wf['xa_wkv'], "nn", BF16, "kv_proj", layer=l)
        x2 = _attn_fwd(x1, p['nmem'], wf['xa_wq'], kv, wf['xa_wo'], l)
        x3 = _ffn_fwd(x2, p['nffn'], wf['ffn_w1'], wf['ffn_w2'], l, ts=1024)
        saved.append(dict(p=p, x=xc, h=h, proj=proj, st=st, hs=hs, branches=branches, x1=x1, memn=memn, kv=kv, x2=x2))
        xc = x3

    loss_blk, dx, dfinal = _final_loss(xc, row(final_norm_w), tgt)

    gs = {n: [None] * DEPTH for n in SMALL if n != 'final_norm_w'}
    dlb = [None] * DEPTH
    in_flight = [[None, None] for _ in range(DEPTH)]

    def scatter(grads, grp, after, name):
        pieces = PIECE_GROUPS[grp]
        return _scatter_start([grads[key][None] for key in pieces], _piece_kinds(pieces), after, name)

    token = loss_blk
    for l in reversed(range(DEPTH)):
        sv = saved[l]
        p = sv['p']
        gb = {}
        dx2, gs['norm_ffn_w'][l], h3, da, r, dxb = _ffn_bwd(sv['x2'], dx, p['nffn'], wf['ffn_w1'], wf['ffn_w2'], l, token, ts=512)
        gb['ffn_w1'] = _mm(h3, da, "tn", BF16, "dw_ffn1", tm=1024)
        gb['ffn_w2'] = _mm(r, dxb, "tn", BF16, "dw_ffn2", tn=1024)
        dx1, gs['norm_mem_w'][l], h2, o, dq, dxb2, dk, dv = _attn_bwd(sv['x1'], dx2, p['nmem'], wf['xa_wq'], sv['kv'], wf['xa_wo'], l)
        gb['xa_wq'] = _mm(h2, dq, "tn", BF16, "dw_q")
        gb['xa_wo'] = _mm(o, dxb2, "tn", BF16, "dw_o")
        dkv = jnp.concatenate([dk, dv], axis=1)
        gb['xa_wkv'] = _mm(sv['memn'], dkv, "tn", BF16, "dw_kv")
        dmemn = _mm(dkv, wf['xa_wkv'], "nt", F32, "dmemn", layer=l)
        _, gs['mem_norm_w'][l] = _norm_bwd(mems, p['memw'], dmemn, None, "norm_memtok_bwd")
        in_flight[l][0], token = scatter(gb, 0, dx1, "scatter_start_%d_0" % l)
        db0, db1, db2, db3, dgp, dup, mg, dxb1, gs['gate_b'][l] = _merge_bwd(
            dx1, sv['branches'], sv['proj'], p['gb'], wf['w_branch'], wf['w_out'], l, token)
        gb['w_out'] = _mm(mg, dxb1, "tn", BF16, "dw_out")
        for kb in range(4):
            gb['w_branch', kb] = _mm(sv['branches'][kb], dup, "tn", BF16, "dw_branch", b_col0=kb * D_MODEL, n=D_MODEL, tn=512)
        dhg, dlb[l], gs['hg_norm_w'][l] = _hgrn_bwd(sv['proj'], db0, sv['st'], p['lb'], p['hgnw'])
        dcv, dcw, gs['cv_dw_b'][l], gs['cv_ln_w'][l], gs['cv_ln_b'][l] = _conv_bwd(sv['proj'], db1, p['cw'], p['cb'], p['lnw'], p['lnb'])
        gs['cv_dw_w'][l] = dcw[:CV_K]
        dpl, gs['pl_w'][l], gs['pl_scale'][l] = _pool_bwd(sv['proj'], db2, p['plw'], p['plsc'])
        dlru, dlcw, gs['lru_conv_b'][l], dwa, gs['lru_ba'][l], dwx, gs['lru_bx'][l], gs['lru_lambda'][l] = _lru_bwd(
            sv['proj'], sv['hs'], db3, p['lcw'], p['lcb'], p['wa'], p['ba'], p['wx'], p['bx'], p['lam'])
        gs['lru_conv_w'][l] = dlcw[:LRU_CONV]
        gs['lru_wa'][l], gs['lru_wx'][l] = _diag_blocks(dwa, LRU_HEADS), _diag_blocks(dwx, LRU_HEADS)
        dproj = [dhg, dcv, dpl, dlru, dgp]
        gb['w_in'] = _mm_tn_pieces(sv['h'], dproj, "dw_in")
        dh = _mm_nt_pieces(dproj, wf['w_in'], l, "dh_mix")
        dx, gs['norm_mix_w'][l] = _norm_bwd(sv['x'], p['nmix'], dh, dx1, "norm_mix_bwd")
        if l:
            in_flight[l][1], token = scatter(gb, 1, dx, "scatter_start_%d_1" % l)
    grad_x = dx.reshape(x.shape)
    gs['hg_lb_raw'] = _lb_bwd(hg_lb_raw, jnp.concatenate(dlb, axis=0))

    def full_shape(n):
        return full_small[n].shape if n in SMALL_SPLIT else w[n].shape

    small_full = []
    for n in SMALL:
        g = gs[n] if n == 'hg_lb_raw' else dfinal if n == 'final_norm_w' else jnp.stack(gs[n])
        small_full.append(g.reshape(full_shape(n)))
    mine = _pack(small_full + [loss_blk[0:1, 0:1]])
    chip_sum = _sum_rows(jnp.stack([mine, _sibling_swap([mine], "sibling_swap_small")[0]]), mine)
    everyone = _gather_small(chip_sum, over_c=False)
    in_flight[0][1], token = scatter(gb, 1, everyone, "scatter_start_0_1")
    total = _sum_rows(everyone, token)
    parts = _unpack(total, [full_shape(n) for n in SMALL] + [(1,)])
    loss = parts[-1].reshape(())
    g_small = {}
    for n, g in zip(SMALL, parts[:-1]):
        if n in SMALL_SPLIT:
            width = w[n].shape[-1]
            g = lax.dynamic_slice_in_dim(g, jj * width, width, axis=g.ndim - 1)
        g_small[n] = g
    shapes = [w[n].shape for n in SMALL]
    upd = _adamw(_pack([w[n] for n in SMALL]), [_pack([g_small[n] for n in SMALL])], _pack([m1[n] for n in SMALL]),
                 _pack([v1[n] for n in SMALL]), "adamw_small")
    d_small, m_small, v_small = [dict(zip(SMALL, _unpack(u, shapes))) for u in upd]

    stacks = {n: None for n in big_names}
    done_before = upd[0]
    for l in reversed(range(DEPTH)):
        for grp, pieces in enumerate(PIECE_GROUPS):
            s_send, s_recv, g_thru, lands = in_flight[l][grp]
            g_thru, lands = _scatter_wait(s_send, s_recv, g_thru, lands, _piece_kinds(pieces), done_before,
                                          "scatter_wait_%d_%d" % (l, grp))
            for key, g, r in zip(pieces, g_thru, lands):
                n, kb = key if isinstance(key, tuple) else (key, None)
                per = 1 if kb is None else 4
                stacks[n] = _sum_partials_into(stacks[n], l * per + (kb or 0), DEPTH * per, g, r, BIG[n], jj1, "sum_" + n)
                done_before = stacks[n]
    partial = [stacks[n] for n in big_names]
    theirs = _sibling_swap(partial)
    g_big, d_big, m_big, v_big = {}, {}, {}, {}
    for n, pa, pb in zip(big_names, partial, theirs):
        c2 = lambda a: a.reshape(-1, a.shape[-1])
        out = _adamw(c2(w[n]), [c2(pa), c2(pb)], c2(m1[n]), c2(v1[n]), "adamw_" + n)
        g_big[n], d_big[n], m_big[n], v_big[n] = [o.reshape(w[n].shape) for o in out]

    pick = lambda small, big: [big[n] if n in BIG else small[n] for n in WEIGHTS]
    return (loss, grad_x, *pick(g_small, g_big), *pick(d_small, d_big), *pick(m_small, m_big), *pick(v_small, v_big))
```

```python
import functools
import math

import jax
import jax.numpy as jnp
from jax import lax
from jax.experimental import pallas as pl
from jax.experimental.pallas import tpu as pltpu

F32 = jnp.float32
BF16 = jnp.bfloat16
MESH = pl.DeviceIdType.MESH
ANY = pl.BlockSpec(memory_space=pl.ANY)

D_MODEL = 1024
DEPTH = 4
CHUNK = 64
SUB = 16
EPS = 1e-6
HG_HEADS, HG_D = 4, 128
CV_W, CV_K = 512, 31
CV_HALO = 32
POOL_WINDOWS = (2, 4, 8, 16)
POOL_HALO = 16
LRU_W, LRU_HEADS, LRU_HD, LRU_CONV = 512, 8, 64, 4
LRU_HALO = 8
LRU_C = 8.0
MIX_W = 4608
IN_W = 8704
XA_HEADS, XA_HD = 4, 256
D_FF = 4096
FF_CHUNK = 1024
ADAM_LR, ADAM_B1, ADAM_B2, ADAM_EPS, ADAM_WD, ADAM_STEP = 0.001, 0.9, 0.999, 1e-08, 0.01, 10
VMEM_LIMIT = 56 * 1024 * 1024
EXP_CLAMP = 80.0
HI = lax.Precision.HIGHEST


def _params(sem=None):
    return pltpu.CompilerParams(dimension_semantics=sem, vmem_limit_bytes=VMEM_LIMIT)


def _sigmoid(x):
    return 1.0 / (1.0 + jnp.exp(-x))


def _dsilu(x, s):
    return s * (1.0 + x * (1.0 - s))


_GELU_C = math.sqrt(2.0 / math.pi)


def _gelu_parts(x):
    t = jnp.tanh(_GELU_C * (x + 0.044715 * x * x * x))
    g = 0.5 * x * (1.0 + t)
    dg = 0.5 * (1.0 + t) + 0.5 * x * (1.0 - t * t) * _GELU_C * (1.0 + 3 * 0.044715 * x * x)
    return g, dg


def _dot(a, b, dims, precision=None):
    return lax.dot_general(a, b, (dims, ((), ())), precision=precision, preferred_element_type=F32)


def _nn(a, b, **k):
    return _dot(a, b, ((1,), (0,)), **k)


def _nt(a, b, **k):
    return _dot(a, b, ((1,), (1,)), **k)


def _tn(a, b, **k):
    return _dot(a, b, ((0,), (0,)), **k)


def _split(x):
    hi = x.astype(BF16)
    return hi, (x - hi.astype(F32)).astype(BF16)


def _nn3(a, b):
    (ah, al), (bh, bl) = _split(a), _split(b)
    return _nn(jnp.concatenate([ah, ah, al], axis=1), jnp.concatenate([bh, bl, bh], axis=0))


def _tn3(a, b):
    (ah, al), (bh, bl) = _split(a), _split(b)
    return _tn(jnp.concatenate([ah, ah, al], axis=0), jnp.concatenate([bh, bl, bh], axis=0))


def _rms_fwd(x, w):
    r = lax.rsqrt(jnp.mean(x * x, axis=-1, keepdims=True) + EPS)
    return x * r * w, r


def _rms_bwd(x, r, w, dy):
    xr = x * r
    g = dy * w
    dx = r * (g - xr * jnp.mean(g * xr, axis=-1, keepdims=True))
    return dx, jnp.sum(dy * xr, axis=0, keepdims=True)


def _lw(shape, index, layer):
    return pl.BlockSpec((None,) + tuple(shape), lambda *g: (layer,) + tuple(index(*g)))


def _mm(a, b, mode, out_dtype, name, tm=512, tn=512, b_col0=0, n=None, layer=None):
    bs = b.shape if layer is None else b.shape[1:]
    if mode == "nn":
        m, k = a.shape
        n = bs[1] if n is None else n
    elif mode == "nt":
        m, k = a.shape
        n = bs[0] if n is None else n
    else:
        k, m = a.shape
        n = bs[1] if n is None else n
    tm, tn = min(tm, m), min(tn, n)
    assert m % tm == 0 and n % tn == 0 and b_col0 % tn == 0
    off = b_col0 // tn

    def body(a_ref, b_ref, o_ref):
        av, bv = a_ref[...].astype(BF16), b_ref[...].astype(BF16)
        o_ref[...] = (_nn if mode == "nn" else _nt if mode == "nt" else _tn)(av, bv).astype(out_dtype)

    def bspec(shape, index):
        return pl.BlockSpec(shape, index) if layer is None else _lw(shape, index, layer)

    if mode == "tn":
        grid = (m // tm, n // tn)
        a_spec = pl.BlockSpec((k, tm), lambda i, j: (0, i))
        b_spec = bspec((k, tn), lambda i, j: (0, j + off))
        o_spec = pl.BlockSpec((tm, tn), lambda i, j: (i, j))
    else:
        grid = (n // tn, m // tm)
        a_spec = pl.BlockSpec((tm, k), lambda j, i: (i, 0))
        if mode == "nn":
            b_spec = bspec((k, tn), lambda j, i: (0, j + off))
        else:
            b_spec = bspec((tn, k), lambda j, i: (j + off, 0))
        o_spec = pl.BlockSpec((tm, tn), lambda j, i: (i, j))
    return pl.pallas_call(
        body, out_shape=jax.ShapeDtypeStruct((m, n), out_dtype), grid=grid,
        in_specs=[a_spec, b_spec], out_specs=o_spec, name=name,
        compiler_params=_params(("parallel", "parallel")),
    )(a, b)


def _mm_tn_pieces(a, pieces, name, tn=256):
    k, m = a.shape
    starts, o = [], 0
    for pc in pieces:
        assert pc.shape[1] % tn == 0
        starts.append(o // tn)
        o += pc.shape[1]
    counts = [pc.shape[1] // tn for pc in pieces]
    npc = len(pieces)

    def body(a_ref, *refs):
        o_ref = refs[npc]
        j = pl.program_id(0)
        for pi in range(npc):
            @pl.when(jnp.logical_and(j >= starts[pi], j < starts[pi] + counts[pi]))
            def _(pi=pi):
                o_ref[...] = _tn(a_ref[...], refs[pi][...]).astype(o_ref.dtype)

    def piece_spec(pi):
        return pl.BlockSpec((k, tn), lambda j: (0, jnp.clip(j - starts[pi], 0, counts[pi] - 1)))

    return pl.pallas_call(
        body, out_shape=jax.ShapeDtypeStruct((m, o), BF16), grid=(o // tn,),
        in_specs=[pl.BlockSpec((k, m), lambda j: (0, 0))] + [piece_spec(pi) for pi in range(npc)],
        out_specs=pl.BlockSpec((m, tn), lambda j: (0, j)), name=name, compiler_params=_params(("parallel",)),
    )(a, *pieces)


def _mm_nt_pieces(pieces, b, layer, name, tm=256, tn=512):
    m = pieces[0].shape[0]
    n, k = b.shape[1:]
    offs, o = [], 0
    for pc in pieces:
        offs.append(o)
        o += pc.shape[1]
    assert o == k
    npc = len(pieces)

    def body(*refs):
        b_ref, o_ref = refs[npc], refs[npc + 1]
        acc = jnp.zeros((tm, tn), F32)
        for pi in range(npc):
            acc += _nt(refs[pi][...], b_ref[:, offs[pi]:offs[pi] + pieces[pi].shape[1]])
        o_ref[...] = acc

    return pl.pallas_call(
        body, out_shape=jax.ShapeDtypeStruct((m, n), F32), grid=(n // tn, m // tm),
        in_specs=[pl.BlockSpec((tm, pc.shape[1]), lambda j, i: (i, 0)) for pc in pieces] + [_lw((tn, k), lambda j, i: (j, 0), layer)],
        out_specs=pl.BlockSpec((tm, tn), lambda j, i: (i, j)), name=name, compiler_params=_params(("parallel", "parallel")),
    )(*pieces, b)


def _norm_fwd(x, w, name, ts=512):
    s, d = x.shape
    ts = min(ts, s)

    def body(x_ref, w_ref, o_ref):
        o_ref[...] = _rms_fwd(x_ref[...], w_ref[...])[0].astype(BF16)

    return pl.pallas_call(
        body, out_shape=jax.ShapeDtypeStruct((s, d), BF16), grid=(s // ts,),
        in_specs=[pl.BlockSpec((ts, d), lambda i: (i, 0)), pl.BlockSpec((1, d), lambda i: (0, 0))],
        out_specs=pl.BlockSpec((ts, d), lambda i: (i, 0)), name=name, compiler_params=_params(("parallel",)),
    )(x, w)


def _norm_bwd(x, w, dy, dres, name, ts=512):
    s, d = x.shape
    ts = min(ts, s)
    with_res = dres is not None

    def body(*refs):
        if with_res:
            x_ref, w_ref, dy_ref, dres_ref, dx_ref, dw_ref = refs
        else:
            x_ref, w_ref, dy_ref, dx_ref, dw_ref = refs
        xv = x_ref[...]
        r = lax.rsqrt(jnp.mean(xv * xv, axis=-1, keepdims=True) + EPS)
        dx, dw = _rms_bwd(xv, r, w_ref[...], dy_ref[...])
        dx_ref[...] = dx + dres_ref[...] if with_res else dx

        @pl.when(pl.program_id(0) == 0)
        def _():
            dw_ref[...] = jnp.zeros_like(dw_ref)

        dw_ref[...] += dw

    row = pl.BlockSpec((ts, d), lambda i: (i, 0))
    vec = pl.BlockSpec((1, d), lambda i: (0, 0))
    return pl.pallas_call(
        body, out_shape=(jax.ShapeDtypeStruct((s, d), F32), jax.ShapeDtypeStruct((1, d), F32)), grid=(s // ts,),
        in_specs=[row, vec, row] + ([row] if with_res else []), out_specs=(row, vec), name=name,
        compiler_params=_params(("arbitrary",)),
    )(*([x, w, dy] + ([dres] if with_res else [])))


def _ffn_fwd(x, nw, w1, w2, layer, ts=256):
    s, d = x.shape
    ts = min(ts, s)
    nj = D_FF // FF_CHUNK

    def body(x_ref, nw_ref, w1_ref, w2_ref, o_ref, h_scr, acc):
        j = pl.program_id(1)

        @pl.when(j == 0)
        def _():
            h_scr[...] = _rms_fwd(x_ref[...], nw_ref[...])[0].astype(BF16)
            acc[...] = jnp.zeros_like(acc)

        a = _nn(h_scr[...], w1_ref[...])
        rl = jnp.maximum(a, 0.0)
        acc[...] += _nn((rl * rl).astype(BF16), w2_ref[...])

        @pl.when(j == nj - 1)
        def _():
            o_ref[...] = x_ref[...] + acc[...]

    row = pl.BlockSpec((ts, d), lambda i, j: (i, 0))
    return pl.pallas_call(
        body, out_shape=jax.ShapeDtypeStruct((s, d), F32), grid=(s // ts, nj),
        in_specs=[row, pl.BlockSpec((1, d), lambda i, j: (0, 0)),
                  _lw((d, FF_CHUNK), lambda i, j: (0, j), layer), _lw((FF_CHUNK, d), lambda i, j: (j, 0), layer)],
        out_specs=row, scratch_shapes=[pltpu.VMEM((ts, d), BF16), pltpu.VMEM((ts, d), F32)], name="ffn_fwd",
        compiler_params=_params(("parallel", "arbitrary")),
    )(x, nw, w1, w2)


def _ffn_bwd(x, dxo, nw, w1, w2, layer, after, ts=256):
    s, d = x.shape
    ts = min(ts, s)
    nj = D_FF // FF_CHUNK

    def body(x_ref, dxo_ref, nw_ref, w1_ref, w2_ref, after_ref, dx_ref, dnw_ref, h_ref, da_ref, r_ref, dxb_ref, dh):
        i, j = pl.program_id(0), pl.program_id(1)

        @pl.when(j == 0)
        def _():
            h_ref[...] = _rms_fwd(x_ref[...], nw_ref[...])[0].astype(BF16)
            dxb_ref[...] = dxo_ref[...].astype(BF16)
            dh[...] = jnp.zeros_like(dh)

        a = _nn(h_ref[...], w1_ref[...])
        rl = jnp.maximum(a, 0.0)
        r_ref[...] = (rl * rl).astype(BF16)
        da = (_nt(dxb_ref[...], w2_ref[...]) * (2.0 * rl)).astype(BF16)
        da_ref[...] = da
        dh[...] += _nt(da, w1_ref[...])

        @pl.when(jnp.logical_and(i == 0, j == 0))
        def _():
            dnw_ref[...] = jnp.zeros_like(dnw_ref)

        @pl.when(j == nj - 1)
        def _():
            xv = x_ref[...]
            r = lax.rsqrt(jnp.mean(xv * xv, axis=-1, keepdims=True) + EPS)
            dx, dw = _rms_bwd(xv, r, nw_ref[...], dh[...])
            dx_ref[...] = dxo_ref[...] + dx
            dnw_ref[...] += dw

    row = pl.BlockSpec((ts, d), lambda i, j: (i, 0))
    vec = pl.BlockSpec((1, d), lambda i, j: (0, 0))
    ffc = pl.BlockSpec((ts, FF_CHUNK), lambda i, j: (i, j))
    return pl.pallas_call(
        body,
        out_shape=(jax.ShapeDtypeStruct((s, d), F32), jax.ShapeDtypeStruct((1, d), F32), jax.ShapeDtypeStruct((s, d), BF16),
                   jax.ShapeDtypeStruct((s, D_FF), BF16), jax.ShapeDtypeStruct((s, D_FF), BF16), jax.ShapeDtypeStruct((s, d), BF16)),
        grid=(s // ts, nj),
        in_specs=[row, row, vec, _lw((d, FF_CHUNK), lambda i, j: (0, j), layer), _lw((FF_CHUNK, d), lambda i, j: (j, 0), layer), ANY],
        out_specs=(row, vec, row, ffc, ffc, row), scratch_shapes=[pltpu.VMEM((ts, d), F32)], name="ffn_bwd",
        compiler_params=_params(("arbitrary", "arbitrary")),
    )(x, dxo, nw, w1, w2, after)


def _attn_probs(q, k_ref):
    ps = []
    for hd in range(XA_HEADS):
        c = slice(hd * XA_HD, (hd + 1) * XA_HD)
        sc = _nt(q[:, c].astype(BF16), k_ref[:, c]) * (XA_HD ** -0.5)
        e = jnp.exp(sc - jnp.max(sc, axis=-1, keepdims=True))
        ps.append(e / jnp.sum(e, axis=-1, keepdims=True))
    return ps


def _attn_fwd(x, nw, wq, kv, wo, layer, ts=256):
    s, d = x.shape
    ts = min(ts, s)
    nm = kv.shape[0]

    def body(x_ref, nw_ref, wq_ref, k_ref, v_ref, wo_ref, o_ref):
        xv = x_ref[...]
        h = _rms_fwd(xv, nw_ref[...])[0].astype(BF16)
        q = _nn(h, wq_ref[...])
        ps = _attn_probs(q, k_ref)
        o = jnp.concatenate([_nn(ps[hd].astype(BF16), v_ref[:, hd * XA_HD:(hd + 1) * XA_HD]) for hd in range(XA_HEADS)], axis=1)
        o_ref[...] = xv + _nn(o.astype(BF16), wo_ref[...])

    row = pl.BlockSpec((ts, d), lambda i: (i, 0))
    full = lambda r, c: pl.BlockSpec((r, c), lambda i: (0, 0))
    wsp = _lw((d, d), lambda i: (0, 0), layer)
    return pl.pallas_call(
        body, out_shape=jax.ShapeDtypeStruct((s, d), F32), grid=(s // ts,),
        in_specs=[row, full(1, d), wsp, full(nm, d), pl.BlockSpec((nm, d), lambda i: (0, 1)), wsp], out_specs=row, name="attn_fwd",
        compiler_params=_params(("parallel",)),
    )(x, nw, wq, kv, kv, wo)


def _attn_bwd(x, dxo, nw, wq, kv, wo, layer, ts=256):
    s, d = x.shape
    ts = min(ts, s)
    nm = kv.shape[0]

    def body(x_ref, dxo_ref, nw_ref, wq_ref, k_ref, v_ref, wo_ref,
             dx_ref, dnw_ref, h_ref, o_ref, dq_ref, dxb_ref, dk_ref, dv_ref):
        xv = x_ref[...]
        hf, r = _rms_fwd(xv, nw_ref[...])
        h = hf.astype(BF16)
        h_ref[...] = h
        q = _nn(h, wq_ref[...])
        qb = q.astype(BF16)
        ps = _attn_probs(q, k_ref)
        dxb = dxo_ref[...].astype(BF16)
        dxb_ref[...] = dxb
        do = _nt(dxb, wo_ref[...])

        @pl.when(pl.program_id(0) == 0)
        def _():
            dnw_ref[...] = jnp.zeros_like(dnw_ref)
            dk_ref[...] = jnp.zeros_like(dk_ref)
            dv_ref[...] = jnp.zeros_like(dv_ref)

        dqs = []
        for hd in range(XA_HEADS):
            c = slice(hd * XA_HD, (hd + 1) * XA_HD)
            p = ps[hd]
            pb = p.astype(BF16)
            dob = do[:, c].astype(BF16)
            o_ref[:, c] = _nn(pb, v_ref[:, c]).astype(BF16)
            dp = _nt(dob, v_ref[:, c])
            ds = (p * (dp - jnp.sum(p * dp, axis=-1, keepdims=True)) * (XA_HD ** -0.5)).astype(BF16)
            dqs.append(_nn(ds, k_ref[:, c]))
            dk_ref[:, c] += _tn(ds, qb[:, c])
            dv_ref[:, c] += _tn(pb, dob)
        dq = jnp.concatenate(dqs, axis=1).astype(BF16)
        dq_ref[...] = dq
        dx, dw = _rms_bwd(xv, r, nw_ref[...], _nt(dq, wq_ref[...]))
        dx_ref[...] = dxo_ref[...] + dx
        dnw_ref[...] += dw

    row = pl.BlockSpec((ts, d), lambda i: (i, 0))
    full = lambda r, c: pl.BlockSpec((r, c), lambda i: (0, 0))
    sd = lambda dt: jax.ShapeDtypeStruct((s, d), dt)
    return pl.pallas_call(
        body,
        out_shape=(sd(F32), jax.ShapeDtypeStruct((1, d), F32), sd(BF16), sd(BF16), sd(BF16), sd(BF16),
                   jax.ShapeDtypeStruct((nm, d), F32), jax.ShapeDtypeStruct((nm, d), F32)),
        grid=(s // ts,),
        in_specs=[row, row, full(1, d), _lw((d, d), lambda i: (0, 0), layer), full(nm, d), pl.BlockSpec((nm, d), lambda i: (0, 1)),
                  _lw((d, d), lambda i: (0, 0), layer)],
        out_specs=(row, full(1, d), row, row, row, row, full(nm, d), full(nm, d)), name="attn_bwd",
        compiler_params=_params(("arbitrary",)),
    )(x, dxo, nw, wq, kv, kv, wo)


GATE_BLK0 = MIX_W // 512


def _merge_specs(ts, layer):
    row = pl.BlockSpec((ts, D_MODEL), lambda i: (i, 0))
    br = pl.BlockSpec((ts, 512), lambda i: (i, 0))
    gates = [pl.BlockSpec((ts, 512), functools.partial(lambda n, i: (i, GATE_BLK0 + n), n)) for n in range(8)]
    full = lambda *shape: pl.BlockSpec(shape, lambda i: (0,) * len(shape))
    weights = [full(4, D_MODEL), _lw((4, 512, D_MODEL), lambda i: (0, 0, 0), layer), _lw((D_MODEL, D_MODEL), lambda i: (0, 0), layer)]
    return row, br, gates, full, weights


def _merge_gates(gp_refs, gb_ref, kb):
    gp = jnp.concatenate([gp_refs[2 * kb][...], gp_refs[2 * kb + 1][...]], axis=1)
    return _sigmoid(gp + gb_ref[kb:kb + 1, :])


def _merge_fwd(x, branches, proj, gate_b, wb, wout, layer, ts=256):
    s, d = x.shape
    ts = min(ts, s)

    def body(x_ref, b0, b1, b2, b3, g0, g1, g2, g3, g4, g5, g6, g7, gb_ref, wb_ref, wo_ref, o_ref):
        brs, gps = (b0, b1, b2, b3), (g0, g1, g2, g3, g4, g5, g6, g7)
        merged = jnp.zeros((ts, d), F32)
        for kb in range(4):
            merged += _merge_gates(gps, gb_ref, kb) * _nn(brs[kb][...], wb_ref[kb])
        o_ref[...] = x_ref[...] + _nn(merged.astype(BF16), wo_ref[...])

    row, br, gates, full, weights = _merge_specs(ts, layer)
    return pl.pallas_call(
        body, out_shape=jax.ShapeDtypeStruct((s, d), F32), grid=(s // ts,),
        in_specs=[row, br, br, br, br] + gates + weights, out_specs=row, name="merge_fwd",
        compiler_params=_params(("parallel",)),
    )(x, *branches, *([proj] * 8), gate_b, wb, wout)


def _merge_bwd(dxo, branches, proj, gate_b, wb, wout, layer, after, ts=256):
    s, d = dxo.shape
    ts = min(ts, s)

    def body(dxo_ref, b0, b1, b2, b3, g0, g1, g2, g3, g4, g5, g6, g7, gb_ref, wb_ref, wo_ref, after_ref,
             db0, db1, db2, db3, dgp_ref, dup_ref, mg_ref, dxb_ref, dgb_ref):
        brs, gps, dbs = (b0, b1, b2, b3), (g0, g1, g2, g3, g4, g5, g6, g7), (db0, db1, db2, db3)
        dxb = dxo_ref[...].astype(BF16)
        dxb_ref[...] = dxb
        dm = _nt(dxb, wo_ref[...])

        @pl.when(pl.program_id(0) == 0)
        def _():
            dgb_ref[...] = jnp.zeros_like(dgb_ref)

        merged = jnp.zeros((ts, d), F32)
        for kb in range(4):
            c = slice(kb * d, (kb + 1) * d)
            g = _merge_gates(gps, gb_ref, kb)
            up = _nn(brs[kb][...], wb_ref[kb])
            merged += g * up
            dup = (dm * g).astype(BF16)
            dup_ref[:, c] = dup
            dgp = dm * up * g * (1.0 - g)
            dgp_ref[:, c] = dgp.astype(BF16)
            dgb_ref[kb:kb + 1, :] += jnp.sum(dgp, axis=0, keepdims=True)
            dbs[kb][...] = _nt(dup, wb_ref[kb])
        mg_ref[...] = merged.astype(BF16)

    row, br, gates, full, weights = _merge_specs(ts, layer)
    wide = pl.BlockSpec((ts, 4 * d), lambda i: (i, 0))
    sb = jax.ShapeDtypeStruct((s, 512), F32)
    return pl.pallas_call(
        body,
        out_shape=(sb, sb, sb, sb, jax.ShapeDtypeStruct((s, 4 * d), BF16), jax.ShapeDtypeStruct((s, 4 * d), BF16),
                   jax.ShapeDtypeStruct((s, d), BF16), jax.ShapeDtypeStruct((s, d), BF16), jax.ShapeDtypeStruct((4, d), F32)),
        grid=(s // ts,),
        in_specs=[row, br, br, br, br] + gates + weights + [ANY],
        out_specs=(br, br, br, br, wide, wide, row, row, full(4, d)), name="merge_bwd",
        compiler_params=_params(("arbitrary",)),
    )(dxo, *branches, *([proj] * 8), gate_b, wb, wout, after)


def _tri(n, upper=False):
    r = lax.broadcasted_iota(jnp.int32, (n, 3 * n), 0)
    c = lax.broadcasted_iota(jnp.int32, (n, 3 * n), 1) % n
    return jnp.where((c >= r) if upper else (c <= r), 1.0, 0.0).astype(BF16)


def _cum(tri3, x):
    hi = x.astype(BF16)
    r1 = x - hi.astype(F32)
    mid = r1.astype(BF16)
    lo = (r1 - mid.astype(F32)).astype(BF16)
    return _nn(tri3, jnp.concatenate([hi, mid, lo], axis=0))


def _hg_gates(hq, hf, lb):
    sg = _sigmoid(hf)
    fg = lb + (1.0 - lb) * sg
    sq = _sigmoid(hq)
    return sg, fg, 1.0 - fg, jnp.log(fg), hq * sq, sq


NSUB = CHUNK // SUB


def _hg_intra(qf, kk, b):
    row = lax.broadcasted_iota(jnp.int32, (CHUNK, 1), 0)
    refs = [b[i * SUB - 1:i * SUB, :] if i else jnp.zeros((1, b.shape[1]), F32) for i in range(NSUB)]
    mine = [jnp.logical_and(row >= i * SUB, row < (i + 1) * SUB) for i in range(NSUB)]
    ref_rows = refs[0]
    for i in range(1, NSUB):
        ref_rows = jnp.where(mine[i], refs[i], ref_rows)
    eq = jnp.exp(b - ref_rows)
    qt = qf * eq
    ek = jnp.concatenate([jnp.exp(jnp.minimum(r - b, EXP_CLAMP)) for r in refs], axis=1)
    kbig = jnp.concatenate([kk] * NSUB, axis=1) * ek
    qbig = jnp.concatenate([jnp.where(m, qt, 0.0) for m in mine], axis=1)
    return qt, qbig, kbig, eq, ek, mine


def _causal(n, upper=False):
    r, c = lax.broadcasted_iota(jnp.int32, (n, n), 0), lax.broadcasted_iota(jnp.int32, (n, n), 1)
    return (c >= r) if upper else (c <= r)


def _hg_chunk_fwd(qf, kk, b, v, st):
    parts = _hg_intra(qf, kk, b)
    att = jnp.where(_causal(CHUNK), _nt(parts[1].astype(BF16), parts[2].astype(BF16)), 0.0)
    qh = qf * jnp.exp(b)
    o = _nn(att.astype(BF16), v.astype(BF16)) + _nt(qh.astype(BF16), st.astype(BF16))
    bl = b[CHUNK - 1:CHUNK, :]
    kh = kk * jnp.exp(bl - b)
    return o, parts, att, qh, kh, jnp.exp(bl)


def _hgrn_fwd(proj, lb, nw, ts=256):
    s = proj.shape[0]
    ts = min(ts, s)
    nch = ts // CHUNK

    def body(q_ref, f_ref, v_ref, g_ref, lb_ref, nw_ref, o_ref, st_ref, st):
        @pl.when(pl.program_id(0) == 0)
        def _():
            st[...] = jnp.zeros_like(st)

        tri = _tri(CHUNK)

        def chunk(c, carry):
            rows = pl.ds(pl.multiple_of(c * CHUNK, CHUNK), CHUNK)
            _, _, kk, lf, qf, _ = _hg_gates(q_ref[rows, :], f_ref[rows, :], lb_ref[...])
            b = _cum(tri, lf)
            hv, hg = v_ref[rows, :], g_ref[rows, :]
            st_ref[c] = st[...]
            for h in range(HG_HEADS):
                cs = slice(h * HG_D, (h + 1) * HG_D)
                o, _, _, _, kh, ebl = _hg_chunk_fwd(qf[:, cs], kk[:, cs], b[:, cs], hv[:, cs], st[h])
                st[h] = st[h] * ebl + _tn(hv[:, cs].astype(BF16), kh.astype(BF16))
                on = _rms_fwd(o, nw_ref[...])[0]
                gh = hg[:, cs]
                o_ref[rows, cs] = (on * gh * _sigmoid(gh)).astype(BF16)
            return carry

        lax.fori_loop(0, nch, chunk, 0, unroll=4)

    col = lambda n: pl.BlockSpec((ts, 512), functools.partial(lambda n, i: (i, n), n))
    return pl.pallas_call(
        body,
        out_shape=(jax.ShapeDtypeStruct((s, 512), BF16), jax.ShapeDtypeStruct((s // CHUNK, HG_HEADS, HG_D, HG_D), F32)),
        grid=(s // ts,),
        in_specs=[col(0), col(1), col(2), col(3), pl.BlockSpec((1, 512), lambda i: (0, 0)), pl.BlockSpec((1, HG_D), lambda i: (0, 0))],
        out_specs=(pl.BlockSpec((ts, 512), lambda i: (i, 0)), pl.BlockSpec((nch, HG_HEADS, HG_D, HG_D), lambda i: (i, 0, 0, 0))),
        scratch_shapes=[pltpu.VMEM((HG_HEADS, HG_D, HG_D), F32)], name="hgrn_fwd",
        compiler_params=_params(("arbitrary",)),
    )(proj, proj, proj, proj, lb, nw)


def _hgrn_bwd(proj, dout, states, lb, nw, ts=256):
    s = proj.shape[0]
    ts = min(ts, s)
    nch = ts // CHUNK
    nt = s // ts

    def body(q_ref, f_ref, v_ref, g_ref, do_ref, st_ref, lb_ref, nw_ref, dp_ref, dlb_ref, dnw_ref, dst):
        @pl.when(pl.program_id(0) == 0)
        def _():
            dst[...] = jnp.zeros_like(dst)
            dlb_ref[...] = jnp.zeros_like(dlb_ref)
            dnw_ref[...] = jnp.zeros_like(dnw_ref)

        tri, triu = _tri(CHUNK), _tri(CHUNK, upper=True)
        last = lax.broadcasted_iota(jnp.int32, (CHUNK, HG_D), 0) == CHUNK - 1
        nwv = nw_ref[...]

        def chunk(cc, carry):
            c = nch - 1 - cc
            rows = pl.ds(pl.multiple_of(c * CHUNK, CHUNK), CHUNK)
            hq, hf, hv, hg = q_ref[rows, :], f_ref[rows, :], v_ref[rows, :], g_ref[rows, :]
            lbv = lb_ref[...]
            sg, fg, kk, lf, qf, sq = _hg_gates(hq, hf, lbv)
            b = _cum(tri, lf)
            dov = do_ref[rows, :]
            dqf_l, dkk_l, db_l, dv_l, dg_l = [], [], [], [], []
            for h in range(HG_HEADS):
                cs = slice(h * HG_D, (h + 1) * HG_D)
                stp = st_ref[c, h]
                bh, vh, gh = b[:, cs], hv[:, cs], hg[:, cs]
                o, parts, att, qh, kh, ebl = _hg_chunk_fwd(qf[:, cs], kk[:, cs], bh, vh, stp)
                sgg = _sigmoid(gh)
                on, r = _rms_fwd(o, nwv)
                d_on = dov[:, cs] * (gh * sgg)
                dg_l.append(dov[:, cs] * on * _dsilu(gh, sgg))
                do, dnw = _rms_bwd(o, r, nwv, d_on)
                dnw_ref[...] += dnw
                dob, vb = do.astype(BF16), vh.astype(BF16)
                dsth = dst[h]
                dstb = dsth.astype(BF16)
                dqh = _nn3(do, stp)
                dkh = _nn3(vh, dsth)
                dv = _nt(kh.astype(BF16), dstb)
                eb = jnp.exp(bh)
                ekl = jnp.exp(bh[CHUNK - 1:CHUNK, :] - bh)
                dqf, dkk = dqh * eb, dkh * ekl
                db = dqh * qh - dkh * kh
                dbl = jnp.sum(dkh * kh, axis=0, keepdims=True) + ebl * jnp.sum(dsth * stp, axis=0, keepdims=True)
                dst[h] = dsth * ebl + _tn(dob, qh.astype(BF16))
                qt, qbig, kbig, eq, ek, mine = parts
                da = jnp.where(_causal(CHUNK), _nt(dob, vb), 0.0)
                da_t = jnp.where(_causal(CHUNK, upper=True), _nt(vb, dob), 0.0)
                dv = dv + _tn(att.astype(BF16), dob)
                dqbig = _tn3(da_t, kbig)
                dkbig = _tn3(da, qbig)
                dkek, dkkb = dkbig * ek, dkbig * kbig
                dqt = jnp.zeros_like(qt)
                for i in range(NSUB):
                    bs = slice(i * HG_D, (i + 1) * HG_D)
                    dqt = dqt + jnp.where(mine[i], dqbig[:, bs], 0.0)
                    dkk = dkk + dkek[:, bs]
                    db = db - dkkb[:, bs]
                dqf = dqf + dqt * eq
                db = db + dqt * qt + jnp.where(last, dbl, 0.0)
                dqf_l.append(dqf); dkk_l.append(dkk); db_l.append(db); dv_l.append(dv)
            cat = lambda l: jnp.concatenate(l, axis=1)
            dlf = _cum(triu, cat(db_l))
            dfg = dlf / fg - cat(dkk_l)
            dlb_ref[...] += jnp.sum(dfg * (1.0 - sg), axis=0, keepdims=True)
            dp_ref[rows, 0:512] = (cat(dqf_l) * _dsilu(hq, sq)).astype(BF16)
            dp_ref[rows, 512:1024] = (dfg * (1.0 - lbv) * sg * (1.0 - sg)).astype(BF16)
            dp_ref[rows, 1024:1536] = cat(dv_l).astype(BF16)
            dp_ref[rows, 1536:2048] = cat(dg_l).astype(BF16)
            return carry

        lax.fori_loop(0, nch, chunk, 0, unroll=4)

    col = lambda n: pl.BlockSpec((ts, 512), functools.partial(lambda n, i: (nt - 1 - i, n), n))
    vec = lambda n: pl.BlockSpec((1, n), lambda i: (0, 0))
    return pl.pallas_call(
        body,
        out_shape=(jax.ShapeDtypeStruct((s, 2048), BF16), jax.ShapeDtypeStruct((1, 512), F32), jax.ShapeDtypeStruct((1, HG_D), F32)),
        grid=(nt,),
        in_specs=[col(0), col(1), col(2), col(3), pl.BlockSpec((ts, 512), lambda i: (nt - 1 - i, 0)),
                  pl.BlockSpec((nch, HG_HEADS, HG_D, HG_D), lambda i: (nt - 1 - i, 0, 0, 0)), vec(512), vec(HG_D)],
        out_specs=(pl.BlockSpec((ts, 2048), lambda i: (nt - 1 - i, 0)), vec(512), vec(HG_D)),
        scratch_shapes=[pltpu.VMEM((HG_HEADS, HG_D, HG_D), F32)], name="hgrn_bwd",
        compiler_params=_params(("arbitrary",)),
    )(proj, proj, proj, proj, dout, states, lb, nw)


CV_BLK = 2048 // 512


def _halo_before(ts, halo, colblk):
    return pl.BlockSpec((halo, 512), functools.partial(lambda cb, i: (jnp.maximum(i * (ts // halo) - 1, 0), cb), colblk))


def _cv_front(a_ref, g_ref, ah_ref, gh_ref, ext, first):
    a, sg = a_ref[...], _sigmoid(g_ref[...])
    zh = ah_ref[...] * _sigmoid(gh_ref[...])
    ext[0:CV_HALO, :] = jnp.where(first, 0.0, zh)
    ext[CV_HALO:, :] = a * sg
    return a, sg


CV_ROWS = 32


def _windows(ref, r0, base, ntaps, rows):
    out = []
    for phase in range(8):
        taps = [j for j in range(ntaps) if (base + j) % 8 == phase]
        if taps:
            span = max(base + j - phase for j in taps)
            big = ref[pl.ds(r0 + phase, rows + span), :]
            out += [(j, big[base + j - phase:base + j - phase + rows]) for j in taps]
    return out


def _cv_conv_ln(ext, w_ref, b_ref, r0):
    y = jnp.zeros((CV_ROWS, CV_W), F32) + b_ref[...]
    for j, win in _windows(ext, r0, CV_HALO - (CV_K - 1), CV_K, CV_ROWS):
        y = y + w_ref[j:j + 1, :] * win
    mu = jnp.mean(y, axis=-1, keepdims=True)
    yc = y - mu
    r = lax.rsqrt(jnp.mean(yc * yc, axis=-1, keepdims=True) + EPS)
    return yc * r, r


def _conv_fwd(proj, w, b, lnw, lnb, ts=256):
    s = proj.shape[0]
    ts = min(ts, s)

    def body(a_ref, g_ref, ah_ref, gh_ref, w_ref, b_ref, lnw_ref, lnb_ref, o_ref, ext):
        _cv_front(a_ref, g_ref, ah_ref, gh_ref, ext, pl.program_id(0) == 0)
        for r0 in range(0, ts, CV_ROWS):
            yh, _ = _cv_conv_ln(ext, w_ref, b_ref, r0)
            yn = yh * lnw_ref[...] + lnb_ref[...]
            o_ref[r0:r0 + CV_ROWS, :] = (yn * _sigmoid(yn)).astype(BF16)

    col = lambda n: pl.BlockSpec((ts, 512), functools.partial(lambda n, i: (i, n), n))
    vec = pl.BlockSpec((1, CV_W), lambda i: (0, 0))
    return pl.pallas_call(
        body, out_shape=jax.ShapeDtypeStruct((s, CV_W), BF16), grid=(s // ts,),
        in_specs=[col(CV_BLK), col(CV_BLK + 1), _halo_before(ts, CV_HALO, CV_BLK), _halo_before(ts, CV_HALO, CV_BLK + 1),
                  pl.BlockSpec((32, CV_W), lambda i: (0, 0)), vec, vec, vec],
        out_specs=pl.BlockSpec((ts, CV_W), lambda i: (i, 0)), scratch_shapes=[pltpu.VMEM((ts + CV_HALO, CV_W), F32)],
        name="conv_fwd", compiler_params=_params(("parallel",)),
    )(proj, proj, proj, proj, w, b, lnw, lnb)


def _conv_bwd(proj, dout, w, b, lnw, lnb, ts=256):
    s = proj.shape[0]
    ts = min(ts, s)
    nt = s // ts

    def body(a_ref, g_ref, ah_ref, gh_ref, do_ref, w_ref, b_ref, lnw_ref, lnb_ref,
             du_ref, dw_ref, db_ref, dlnw_ref, dlnb_ref, ext, dyext, carry, dwacc):
        i = pl.program_id(0)

        @pl.when(i == 0)
        def _():
            carry[...] = jnp.zeros_like(carry)
            dwacc[...] = jnp.zeros_like(dwacc)
            for ref in (db_ref, dlnw_ref, dlnb_ref):
                ref[...] = jnp.zeros_like(ref)

        _cv_front(a_ref, g_ref, ah_ref, gh_ref, ext, i == nt - 1)
        dyext[ts:, :] = carry[...]
        dlnw = dlnb = db = jnp.zeros((1, CV_W), F32)
        for r0 in range(0, ts, CV_ROWS):
            rows = slice(r0, r0 + CV_ROWS)
            yh, r = _cv_conv_ln(ext, w_ref, b_ref, r0)
            yn = yh * lnw_ref[...] + lnb_ref[...]
            dyn = do_ref[rows, :] * _dsilu(yn, _sigmoid(yn))
            dlnw += jnp.sum(dyn * yh, axis=0, keepdims=True)
            dlnb += jnp.sum(dyn, axis=0, keepdims=True)
            gl = dyn * lnw_ref[...]
            dy = r * (gl - jnp.mean(gl, axis=-1, keepdims=True) - yh * jnp.mean(gl * yh, axis=-1, keepdims=True))
            db += jnp.sum(dy, axis=0, keepdims=True)
            dyext[rows, :] = dy
            for j, win in _windows(ext, r0, CV_HALO - (CV_K - 1), CV_K, CV_ROWS):
                p = dy * win
                dwacc[8 * j:8 * j + 8, :] += (p[0:8] + p[8:16]) + (p[16:24] + p[24:32])
        dlnw_ref[...] += dlnw
        dlnb_ref[...] += dlnb
        db_ref[...] += db
        carry[...] = dyext[0:CV_HALO, :]
        for r0 in range(0, ts, CV_ROWS):
            rows = slice(r0, r0 + CV_ROWS)
            dz = jnp.zeros((CV_ROWS, CV_W), F32)
            for j, win in _windows(dyext, r0, 0, CV_K, CV_ROWS):
                dz = dz + w_ref[CV_K - 1 - j:CV_K - j, :] * win
            a, sg = a_ref[rows, :], _sigmoid(g_ref[rows, :])
            du_ref[rows, 0:CV_W] = (dz * sg).astype(BF16)
            du_ref[rows, CV_W:] = (dz * a * sg * (1.0 - sg)).astype(BF16)

        @pl.when(i == nt - 1)
        def _():
            for j in range(32):
                dw_ref[j:j + 1, :] = jnp.sum(dwacc[8 * j:8 * j + 8, :], axis=0, keepdims=True)

    rev = lambda n: pl.BlockSpec((ts, 512), functools.partial(lambda n, i: (nt - 1 - i, n), n))
    halo = lambda n: pl.BlockSpec((CV_HALO, 512), functools.partial(
        lambda n, i: (jnp.maximum((nt - 1 - i) * (ts // CV_HALO) - 1, 0), n), n))
    vec = pl.BlockSpec((1, CV_W), lambda i: (0, 0))
    wsp = pl.BlockSpec((32, CV_W), lambda i: (0, 0))
    v1 = jax.ShapeDtypeStruct((1, CV_W), F32)
    return pl.pallas_call(
        body, out_shape=(jax.ShapeDtypeStruct((s, 2 * CV_W), BF16), jax.ShapeDtypeStruct((32, CV_W), F32), v1, v1, v1),
        grid=(nt,),
        in_specs=[rev(CV_BLK), rev(CV_BLK + 1), halo(CV_BLK), halo(CV_BLK + 1), rev(0), wsp, vec, vec, vec],
        out_specs=(pl.BlockSpec((ts, 2 * CV_W), lambda i: (nt - 1 - i, 0)), wsp, vec, vec, vec),
        scratch_shapes=[pltpu.VMEM((ts + CV_HALO, CV_W), F32), pltpu.VMEM((ts + CV_HALO, CV_W), F32), pltpu.VMEM((CV_HALO, CV_W), F32),
                        pltpu.VMEM((8 * 32, CV_W), F32)],
        name="conv_bwd", compiler_params=_params(("arbitrary",)),
    )(proj, proj, proj, proj, dout, w, b, lnw, lnb)


PL_BLK = 3072 // 512


def _pool_windows(ext, t0, ts):
    n = ext.shape[0]
    t = t0 + lax.broadcasted_iota(jnp.int32, (ts, 1), 0)
    out = []
    for g, wdw in enumerate(POOL_WINDOWS):
        e = ext[:, g * 128:(g + 1) * 128]
        acc, k = e, 1
        while k < wdw:
            acc = acc + pltpu.roll(acc, k, 0)
            k *= 2
        cnt = jnp.minimum(t + 1, wdw).astype(F32)
        out.append(acc[POOL_HALO:] / cnt - e[POOL_HALO:])
    return out


def _pool_fwd(proj, w, sc, ts=256):
    s = proj.shape[0]
    ts = min(ts, s)

    def body(u_ref, uh_ref, w_ref, sc_ref, o_ref):
        i = pl.program_id(0)
        ext = jnp.concatenate([jnp.where(i == 0, 0.0, uh_ref[...]), u_ref[...]], axis=0)
        ps = _pool_windows(ext, i * ts, ts)
        y = jnp.concatenate([_nn(ps[g].astype(BF16), w_ref[g].astype(BF16)) for g in range(4)], axis=1)
        o_ref[...] = (y * sc_ref[...]).astype(BF16)

    return pl.pallas_call(
        body, out_shape=jax.ShapeDtypeStruct((s, 512), BF16), grid=(s // ts,),
        in_specs=[pl.BlockSpec((ts, 512), lambda i: (i, PL_BLK)), _halo_before(ts, POOL_HALO, PL_BLK),
                  pl.BlockSpec((4, 128, 128), lambda i: (0, 0, 0)), pl.BlockSpec((1, 512), lambda i: (0, 0))],
        out_specs=pl.BlockSpec((ts, 512), lambda i: (i, 0)), name="pool_fwd", compiler_params=_params(("parallel",)),
    )(proj, proj, w, sc)


def _pool_bwd(proj, dout, w, sc, ts=256):
    s = proj.shape[0]
    ts = min(ts, s)
    nt = s // ts
    n = ts + POOL_HALO

    def body(u_ref, uh_ref, do_ref, doh_ref, w_ref, sc_ref, du_ref, dw_ref, dsc_ref):
        i = pl.program_id(0)

        @pl.when(i == 0)
        def _():
            dw_ref[...] = jnp.zeros_like(dw_ref)
            dsc_ref[...] = jnp.zeros_like(dsc_ref)

        ext = jnp.concatenate([jnp.where(i == 0, 0.0, uh_ref[...]), u_ref[...]], axis=0)
        ps = _pool_windows(ext, i * ts, ts)
        dov = do_ref[...]
        dyext = jnp.concatenate([dov, jnp.where(i == nt - 1, 0.0, doh_ref[...])], axis=0) * sc_ref[...]
        t = i * ts + lax.broadcasted_iota(jnp.int32, (n, 1), 0)
        row = lax.broadcasted_iota(jnp.int32, (n, 1), 0)
        dus = []
        for g, wdw in enumerate(POOL_WINDOWS):
            cs = slice(g * 128, (g + 1) * 128)
            wg, pb = w_ref[g].astype(BF16), ps[g].astype(BF16)
            dsc_ref[:, cs] += jnp.sum(dov[:, cs] * _nn(pb, wg), axis=0, keepdims=True)
            dyg = dyext[:, cs].astype(BF16)
            dw_ref[g] += _tn(pb, dyg[0:ts])
            dp = _nt(dyg, wg)
            acc, k = dp / jnp.minimum(t + 1, wdw).astype(F32), 1
            while k < wdw:
                acc = acc + jnp.where(row < n - k, pltpu.roll(acc, n - k, 0), 0.0)
                k *= 2
            dus.append(acc[0:ts] - dp[0:ts])
        du_ref[...] = jnp.concatenate(dus, axis=1).astype(BF16)

    tile = lambda cb: pl.BlockSpec((ts, 512), functools.partial(lambda cb, i: (i, cb), cb))
    after = pl.BlockSpec((POOL_HALO, 512), lambda i: (jnp.minimum((i + 1) * (ts // POOL_HALO), s // POOL_HALO - 1), 0))
    wsp, vec = pl.BlockSpec((4, 128, 128), lambda i: (0, 0, 0)), pl.BlockSpec((1, 512), lambda i: (0, 0))
    return pl.pallas_call(
        body, out_shape=(jax.ShapeDtypeStruct((s, 512), BF16), jax.ShapeDtypeStruct((4, 128, 128), F32), jax.ShapeDtypeStruct((1, 512), F32)),
        grid=(nt,),
        in_specs=[tile(PL_BLK), _halo_before(ts, POOL_HALO, PL_BLK), tile(0), after, wsp, vec],
        out_specs=(tile(0), wsp, vec), name="pool_bwd", compiler_params=_params(("arbitrary",)),
    )(proj, proj, dout, dout, w, sc)


LX_BLK, LY_BLK = 3584 // 512, 4096 // 512
LRU_OFF = LRU_HALO - (LRU_CONV - 1)


def _scan_fwd(a, b):
    n = a.shape[0]
    row = lax.broadcasted_iota(jnp.int32, (n, 1), 0)
    k = 1
    while k < n:
        m = row >= k
        b = jnp.where(m, a * pltpu.roll(b, k, 0) + b, b)
        a = jnp.where(m, a * pltpu.roll(a, k, 0), a)
        k *= 2
    return a, b


def _scan_rev(a, b):
    n = a.shape[0]
    row = lax.broadcasted_iota(jnp.int32, (n, 1), 0)
    k = 1
    while k < n:
        m = row < n - k
        b = jnp.where(m, a * pltpu.roll(b, n - k, 0) + b, b)
        a = jnp.where(m, a * pltpu.roll(a, n - k, 0), a)
        k *= 2
    return b


def _lru_gates(x_ref, xh_ref, ext, first, cw_ref, cb_ref, wa_ref, ba_ref, wx_ref, bx_ref, lam_ref, ts):
    ext[0:LRU_HALO, :] = jnp.where(first, 0.0, xh_ref[...])
    ext[LRU_HALO:, :] = x_ref[...]
    xc = jnp.zeros((ts, LRU_W), F32) + cb_ref[...]
    for j in range(LRU_CONV):
        xc = xc + cw_ref[j:j + 1, :] * ext[pl.ds(LRU_OFF + j, ts), :]
    xb = xc.astype(BF16)
    r = _sigmoid(_nn(xb, wa_ref[...]) + ba_ref[...])
    ig = _sigmoid(_nn(xb, wx_ref[...]) + bx_ref[...])
    nl = -lam_ref[...]
    sp = jnp.maximum(nl, 0.0) + jnp.log(1.0 + jnp.exp(-jnp.abs(nl)))
    la = -LRU_C * r * sp
    a = jnp.exp(la)
    z = 2.0 * la
    em = jnp.where(z > -0.1, -z * (1.0 + z * 0.5 * (1.0 + z * (1.0 / 3) * (1.0 + z * 0.25 * (1.0 + z * 0.2)))), 1.0 - a * a)
    return xc, xb, r, ig, sp, a, jnp.sqrt(em)


def _lru_fwd(proj, cw, cb, wa, ba, wx, bx, lam, ts=256):
    s = proj.shape[0]
    ts = min(ts, s)

    def body(x_ref, xh_ref, y_ref, cw_ref, cb_ref, wa_ref, ba_ref, wx_ref, bx_ref, lam_ref, o_ref, h_ref, ext, hc):
        i = pl.program_id(0)

        @pl.when(i == 0)
        def _():
            hc[...] = jnp.zeros_like(hc)

        xc, _, _, ig, _, a, mult = _lru_gates(x_ref, xh_ref, ext, i == 0, cw_ref, cb_ref, wa_ref, ba_ref, wx_ref, bx_ref, lam_ref, ts)
        acum, h0 = _scan_fwd(a, mult * ig * xc)
        h = h0 + acum * hc[0:1, :]
        hc[...] = jnp.broadcast_to(h[ts - 1:ts, :], hc.shape)
        h_ref[...] = h
        o_ref[...] = (h * _gelu_parts(y_ref[...])[0]).astype(BF16)

    tile = lambda cb_: pl.BlockSpec((ts, 512), functools.partial(lambda c, i: (i, c), cb_))
    vec = pl.BlockSpec((1, LRU_W), lambda i: (0, 0))
    mat = pl.BlockSpec((LRU_W, LRU_W), lambda i: (0, 0))
    return pl.pallas_call(
        body, out_shape=(jax.ShapeDtypeStruct((s, LRU_W), BF16), jax.ShapeDtypeStruct((s, LRU_W), F32)), grid=(s // ts,),
        in_specs=[tile(LX_BLK), _halo_before(ts, LRU_HALO, LX_BLK), tile(LY_BLK), pl.BlockSpec((8, LRU_W), lambda i: (0, 0)),
                  vec, mat, vec, mat, vec, vec],
        out_specs=(tile(0), tile(0)), scratch_shapes=[pltpu.VMEM((ts + LRU_HALO, LRU_W), F32), pltpu.VMEM((8, LRU_W), F32)],
        name="lru_fwd", compiler_params=_params(("arbitrary",)),
    )(proj, proj, proj, cw, cb, wa, ba, wx, bx, lam)


def _lru_bwd(proj, hs, dout, cw, cb, wa, ba, wx, bx, lam, ts=256):
    s = proj.shape[0]
    ts = min(ts, s)
    nt = s // ts

    def body(x_ref, xh_ref, y_ref, h_ref, hh_ref, do_ref, cw_ref, cb_ref, wa_ref, ba_ref, wx_ref, bx_ref, lam_ref,
             dxy_ref, dcw_ref, dcb_ref, dwa_ref, dba_ref, dwx_ref, dbx_ref, dlam_ref, ext, dext, cg, cd):
        i = pl.program_id(0)
        first_tile = i == nt - 1

        @pl.when(i == 0)
        def _():
            cg[...] = jnp.zeros_like(cg)
            cd[...] = jnp.zeros_like(cd)
            for ref in (dcw_ref, dcb_ref, dwa_ref, dba_ref, dwx_ref, dbx_ref, dlam_ref):
                ref[...] = jnp.zeros_like(ref)

        xc, xb, r, ig, sp, a, mult = _lru_gates(x_ref, xh_ref, ext, first_tile, cw_ref, cb_ref, wa_ref, ba_ref, wx_ref, bx_ref, lam_ref, ts)
        row = lax.broadcasted_iota(jnp.int32, (ts, 1), 0)
        h, dov = h_ref[...], do_ref[...]
        gel, dgel = _gelu_parts(y_ref[...])
        dxy_ref[:, LRU_W:] = (dov * h * dgel).astype(BF16)
        alpha = jnp.where(row < ts - 1, pltpu.roll(a, ts - 1, 0), 0.0)
        g = _scan_rev(alpha, dov * gel + jnp.where(row == ts - 1, cg[0:1, :], 0.0))
        cg[...] = jnp.broadcast_to(a[0:1, :] * g[0:1, :], cg.shape)
        hprev = jnp.where(row == 0, jnp.where(first_tile, 0.0, hh_ref[LRU_HALO - 1:LRU_HALO, :]), pltpu.roll(h, 1, 0))
        dla = g * hprev * a - g * ig * xc * (a * a) / mult
        dpr = dla * (-LRU_C * sp) * r * (1.0 - r)
        dpi = g * mult * xc * ig * (1.0 - ig)
        dprb, dpib = dpr.astype(BF16), dpi.astype(BF16)
        dxc = g * mult * ig + _nt(dprb, wa_ref[...]) + _nt(dpib, wx_ref[...])
        dlam_ref[...] += jnp.sum(dla * (-LRU_C * r), axis=0, keepdims=True) * (-_sigmoid(-lam_ref[...]))
        dwa_ref[...] += _tn(xb, dprb)
        dwx_ref[...] += _tn(xb, dpib)
        dba_ref[...] += jnp.sum(dpr, axis=0, keepdims=True)
        dbx_ref[...] += jnp.sum(dpi, axis=0, keepdims=True)
        dcb_ref[...] += jnp.sum(dxc, axis=0, keepdims=True)
        dext[0:ts, :] = dxc
        dext[ts:, :] = cd[...]
        cd[...] = dxc[0:LRU_HALO, :]
        dx = jnp.zeros((ts, LRU_W), F32)
        for j in range(LRU_CONV):
            dcw_ref[j:j + 1, :] += jnp.sum(dxc * ext[pl.ds(LRU_OFF + j, ts), :], axis=0, keepdims=True)
            dx = dx + cw_ref[j:j + 1, :] * dext[pl.ds(LRU_CONV - 1 - j, ts), :]
        dxy_ref[:, 0:LRU_W] = dx.astype(BF16)

    rev = lambda c: pl.BlockSpec((ts, 512), functools.partial(lambda c, i: (nt - 1 - i, c), c))
    halo = lambda c: pl.BlockSpec((LRU_HALO, 512), functools.partial(
        lambda c, i: (jnp.maximum((nt - 1 - i) * (ts // LRU_HALO) - 1, 0), c), c))
    vec = pl.BlockSpec((1, LRU_W), lambda i: (0, 0))
    mat = pl.BlockSpec((LRU_W, LRU_W), lambda i: (0, 0))
    cws = pl.BlockSpec((8, LRU_W), lambda i: (0, 0))
    v1, m1 = jax.ShapeDtypeStruct((1, LRU_W), F32), jax.ShapeDtypeStruct((LRU_W, LRU_W), F32)
    return pl.pallas_call(
        body, out_shape=(jax.ShapeDtypeStruct((s, 2 * LRU_W), BF16), jax.ShapeDtypeStruct((8, LRU_W), F32), v1, m1, v1, m1, v1, v1),
        grid=(nt,),
        in_specs=[rev(LX_BLK), halo(LX_BLK), rev(LY_BLK), rev(0), halo(0), rev(0), cws, vec, mat, vec, mat, vec, vec],
        out_specs=(pl.BlockSpec((ts, 2 * LRU_W), lambda i: (nt - 1 - i, 0)), cws, vec, mat, vec, mat, vec, vec),
        scratch_shapes=[pltpu.VMEM((ts + LRU_HALO, LRU_W), F32), pltpu.VMEM((ts + LRU_HALO, LRU_W), F32),
                        pltpu.VMEM((8, LRU_W), F32), pltpu.VMEM((LRU_HALO, LRU_W), F32)],
        name="lru_bwd", compiler_params=_params(("arbitrary",)),
    )(proj, proj, proj, hs, hs, dout, cw, cb, wa, ba, wx, bx, lam)


def _final_loss(x, fw, tgt, ts=512):
    s, d = x.shape
    ts = min(ts, s)

    def body(x_ref, w_ref, t_ref, loss_ref, dx_ref, dw_ref):
        @pl.when(pl.program_id(0) == 0)
        def _():
            loss_ref[...] = jnp.zeros_like(loss_ref)
            dw_ref[...] = jnp.zeros_like(dw_ref)

        xv = x_ref[...]
        y, r = _rms_fwd(xv, w_ref[...])
        err = y - t_ref[...]
        loss_ref[...] += 0.5 * jnp.sum(jnp.mean(err * err, axis=-1, keepdims=True), axis=0, keepdims=True)
        dx, dw = _rms_bwd(xv, r, w_ref[...], err * (1.0 / d))
        dx_ref[...] = dx
        dw_ref[...] += dw

    row = pl.BlockSpec((ts, d), lambda i: (i, 0))
    vec = pl.BlockSpec((1, d), lambda i: (0, 0))
    return pl.pallas_call(
        body, out_shape=(jax.ShapeDtypeStruct((8, 128), F32), jax.ShapeDtypeStruct((s, d), F32), jax.ShapeDtypeStruct((1, d), F32)),
        grid=(s // ts,), in_specs=[row, vec, row], out_specs=(pl.BlockSpec((8, 128), lambda i: (0, 0)), row, vec),
        name="final_loss", compiler_params=_params(("arbitrary",)),
    )(x, fw, tgt)


def _lb_softmax(raw_ref):
    raw = raw_ref[...]
    e = jnp.exp(raw - jnp.max(raw, axis=0, keepdims=True))
    return e / jnp.sum(e, axis=0, keepdims=True)


def _lb_fwd(raw):
    def body(raw_ref, o_ref):
        sm = _lb_softmax(raw_ref)
        acc = jnp.zeros((1, sm.shape[1]), F32)
        o_ref[0:1, :] = acc
        for l in range(1, DEPTH):
            acc = acc + sm[l:l + 1, :]
            o_ref[l:l + 1, :] = acc

    return pl.pallas_call(body, out_shape=jax.ShapeDtypeStruct(raw.shape, F32), name="lb_fwd")(raw)


def _lb_bwd(raw, dlb):
    def body(raw_ref, d_ref, o_ref):
        sm = _lb_softmax(raw_ref)
        dlbv = d_ref[...]
        dsm, acc = [None] * DEPTH, jnp.zeros((1, sm.shape[1]), F32)
        for l in range(DEPTH - 1, 0, -1):
            acc = acc + dlbv[l:l + 1, :]
            dsm[l] = acc
        dsm[0] = jnp.zeros_like(acc)
        dsm = jnp.concatenate(dsm, axis=0)
        o_ref[...] = sm * (dsm - jnp.sum(sm * dsm, axis=0, keepdims=True))

    return pl.pallas_call(body, out_shape=jax.ShapeDtypeStruct(raw.shape, F32), name="lb_bwd")(raw, dlb)


def _adam_math(w, g, m, v):
    m = ADAM_B1 * m + (1.0 - ADAM_B1) * g
    v = ADAM_B2 * v + (1.0 - ADAM_B2) * (g * g)
    m_hat = m / (1.0 - ADAM_B1 ** ADAM_STEP)
    v_hat = v / (1.0 - ADAM_B2 ** ADAM_STEP)
    return -ADAM_LR * (m_hat / (jnp.sqrt(v_hat) + ADAM_EPS) + ADAM_WD * w), m, v


def _adamw(w, gs, m, v, name, tr=128):
    r, c = w.shape
    tr = min(tr, r)
    ng = len(gs)

    def body(*refs):
        w_ref, g_refs, m_ref, v_ref = refs[0], refs[1:1 + ng], refs[1 + ng], refs[2 + ng]
        outs = refs[3 + ng:]
        g = g_refs[0][...]
        if ng == 2:
            g = g + g_refs[1][...]
            outs[0][...] = g
            outs = outs[1:]
        for o, val in zip(outs, _adam_math(w_ref[...], g, m_ref[...], v_ref[...])):
            o[...] = val

    blk = pl.BlockSpec((tr, c), lambda i: (i, 0))
    sd = jax.ShapeDtypeStruct((r, c), F32)
    nout = 3 + (ng == 2)
    return pl.pallas_call(
        body, out_shape=(sd,) * nout, grid=(r // tr,), in_specs=[blk] * (3 + ng), out_specs=(blk,) * nout, name=name,
        compiler_params=_params(("parallel",)),
    )(w, *gs, m, v)


def _cast_into_full(w, kind, jj, name, tr=256):
    l, r, c = w.shape
    tr = min(tr, r)

    def body(j_ref, w_ref, o_ref):
        o_ref[...] = w_ref[...].astype(BF16)

    if kind == "col":
        full, dst = (l, r, 4 * c), pl.BlockSpec((None, tr, c), lambda a, b, j: (a, b, j[0]))
    else:
        full, dst = (l, 4 * r, c), pl.BlockSpec((None, tr, c), lambda a, b, j: (a, j[0] * (r // tr) + b, 0))
    return pl.pallas_call(
        body, out_shape=jax.ShapeDtypeStruct(full, BF16),
        grid_spec=pltpu.PrefetchScalarGridSpec(
            num_scalar_prefetch=1, grid=(l, r // tr), in_specs=[pl.BlockSpec((None, tr, c), lambda a, b, j: (a, b, 0))], out_specs=dst),
        name=name, compiler_params=_params(("parallel", "parallel")),
    )(jj, w)


def _place():
    return lax.axis_index("x"), lax.axis_index("y"), lax.axis_index("c")


def _other_chips(x, y):
    return [(1 - x, y), (x, 1 - y), (1 - x, 1 - y)]


def _slab(ref, kind, jj):
    if kind == "col":
        c = ref.shape[2] // 4
        return ref.at[:, :, pl.ds(jj * c, c)]
    r = ref.shape[1] // 4
    return ref.at[:, pl.ds(jj * r, r), :]


HBM = pl.BlockSpec(memory_space=pltpu.HBM)
SEM = pl.BlockSpec(memory_space=pltpu.SEMAPHORE)
EFFECT = pltpu.SideEffectType.DATAFLOW_SIDE_EFFECTING


def _in_hbm(a):
    return pltpu.with_memory_space_constraint(a, pltpu.HBM)


def _thru(arrs):
    return [pltpu.HBM(a.shape, a.dtype) for a in arrs]


def _half_slab(ref, kind, group, jj, half):
    per = ref.shape[0] // DEPTH
    layers = pl.ds(group * per, per)
    if kind == "col":
        r, c = ref.shape[1] // 2, ref.shape[2] // 4
        return ref.at[layers, pl.ds(half * r, r), pl.ds(jj * c, c)]
    r = ref.shape[1] // 8
    return ref.at[layers, pl.ds((2 * jj + half) * r, r), :]


def _gather_copy(fulls, kinds, send_sems, recv_sems, group, t, k, landing):
    x, y, c = _place()
    chip = _other_chips(x, y)[k]
    idx = (group * len(fulls) + t) * 3 + k
    return pltpu.make_async_remote_copy(
        src_ref=_half_slab(fulls[t], kinds[t], group, 2 * x + y, c), dst_ref=_half_slab(fulls[t], kinds[t], group, landing, c),
        send_sem=send_sems.at[idx], recv_sem=recv_sems.at[idx], device_id=(chip[0], chip[1], c), device_id_type=MESH)


def _fill_copy(fulls, kinds, send_sems, recv_sems, group, which, t, k, half):
    x, y, c = _place()
    chip = _other_chips(x, y)[k]
    idx = which.index(t) * 3 + k
    jj = 2 * chip[0] + chip[1]
    return pltpu.make_async_remote_copy(
        src_ref=_half_slab(fulls[t], kinds[t], group, jj, c), dst_ref=_half_slab(fulls[t], kinds[t], group, jj, half),
        send_sem=send_sems.at[idx], recv_sem=recv_sems.at[idx], device_id=(x, y, 1 - c), device_id_type=MESH)


def _fill_start(group, which, fulls, kinds):
    nt = len(fulls)
    ncp = 3 * len(which)

    def body(*refs):
        ins, send_sems, recv_sems = refs[:nt], refs[nt], refs[nt + 1]
        _, _, c = _place()
        for t in which:
            for k in range(3):
                _fill_copy(ins, kinds, send_sems, recv_sems, group, which, t, k, c).start()

    out = pl.pallas_call(
        body, out_shape=(pltpu.SemaphoreType.DMA((ncp,)), pltpu.SemaphoreType.DMA((ncp,)), *_thru(fulls)),
        in_specs=[HBM] * nt, out_specs=(SEM, SEM, *([HBM] * nt)), input_output_aliases={t: 2 + t for t in range(nt)},
        name="fill_start_%d_%d" % (group, which[0]), compiler_params=pltpu.CompilerParams(has_side_effects=EFFECT),
    )(*fulls)
    return out[0], out[1], list(out[2:])


def _fill_wait(group, which, send_sems, recv_sems, fulls, kinds, after):
    nt = len(fulls)

    def body(*refs):
        ins, send_ref, recv_ref = refs[:nt], refs[nt], refs[nt + 1]
        _, _, c = _place()
        for t in which:
            for k in range(3):
                cp = _fill_copy(ins, kinds, send_ref, recv_ref, group, which, t, k, 1 - c)
                cp.wait_send()
                cp.wait_recv()

    out = pl.pallas_call(
        body, out_shape=tuple(_thru(fulls)), in_specs=[HBM] * nt + [SEM, SEM, ANY], out_specs=tuple([HBM] * nt),
        input_output_aliases={t: t for t in range(nt)}, name="fill_wait_%d_%d" % (group, which[0]),
        compiler_params=pltpu.CompilerParams(has_side_effects=EFFECT),
    )(*fulls, send_sems, recv_sems, after)
    return list(out)


def _gather_start(fulls, kinds, after):
    nt = len(fulls)
    ncp = DEPTH * nt * 3

    def body(*refs):
        ins, send_sems, recv_sems = refs[:nt], refs[nt + 1], refs[nt + 2]
        x, y, _ = _place()
        for group in range(DEPTH):
            for t in range(nt):
                for k in range(3):
                    _gather_copy(ins, kinds, send_sems, recv_sems, group, t, k, 2 * x + y).start()

    out = pl.pallas_call(
        body, out_shape=(pltpu.SemaphoreType.DMA((ncp,)), pltpu.SemaphoreType.DMA((ncp,)), *_thru(fulls)),
        in_specs=[HBM] * nt + [ANY], out_specs=(SEM, SEM, *([HBM] * nt)), input_output_aliases={t: 2 + t for t in range(nt)},
        name="gather_start", compiler_params=pltpu.CompilerParams(has_side_effects=EFFECT),
    )(*[_in_hbm(a) for a in fulls], after)
    return out[0], out[1], list(out[2:])


def _gather_wait(group, which, send_sems, recv_sems, fulls, kinds, after):
    nt = len(fulls)

    def body(*refs):
        ins, send_ref, recv_ref = refs[:nt], refs[nt], refs[nt + 1]
        x, y, _ = _place()
        chips = _other_chips(x, y)
        for t in which:
            for k in range(3):
                cp = _gather_copy(ins, kinds, send_ref, recv_ref, group, t, k, 2 * chips[k][0] + chips[k][1])
                cp.wait_send()
                cp.wait_recv()

    out = pl.pallas_call(
        body, out_shape=tuple(_thru(fulls)), in_specs=[HBM] * nt + [SEM, SEM, ANY], out_specs=tuple([HBM] * nt),
        input_output_aliases={t: t for t in range(nt)}, name="gather_wait_%d_%d" % (group, which[0]),
        compiler_params=pltpu.CompilerParams(has_side_effects=EFFECT),
    )(*fulls, send_sems, recv_sems, after)
    return list(out)


def _scatter_copy(grads, lands, kinds, send_sems, recv_sems, t, k):
    x, y, c = _place()
    chip = _other_chips(x, y)[k]
    return pltpu.make_async_remote_copy(
        src_ref=_slab(grads[t], kinds[t], 2 * chip[0] + chip[1]), dst_ref=lands[t].at[k], send_sem=send_sems.at[3 * t + k],
        recv_sem=recv_sems.at[3 * t + k], device_id=(chip[0], chip[1], c), device_id_type=MESH)


def _scatter_start(grads, kinds, after, name):
    nt = len(grads)
    lands = []
    for g, kd in zip(grads, kinds):
        l, r, c = g.shape
        lands.append(lax.empty((3, l, r, c // 4) if kd == "col" else (3, l, r // 4, c), g.dtype))

    def body(*refs):
        ins, lnd, send_sems, recv_sems = refs[:nt], refs[nt:2 * nt], refs[2 * nt + 1], refs[2 * nt + 2]
        for t in range(nt):
            for k in range(3):
                _scatter_copy(ins, lnd, kinds, send_sems, recv_sems, t, k).start()
        refs[-1][...] = jnp.zeros_like(refs[-1])

    out = pl.pallas_call(
        body, out_shape=(pltpu.SemaphoreType.DMA((3 * nt,)), pltpu.SemaphoreType.DMA((3 * nt,)), *_thru(grads), *_thru(lands),
                         jax.ShapeDtypeStruct((8, 128), F32)),
        in_specs=[HBM] * (2 * nt) + [ANY], out_specs=(SEM, SEM, *([HBM] * (2 * nt)), pl.BlockSpec(memory_space=pltpu.VMEM)),
        input_output_aliases={t: 2 + t for t in range(2 * nt)}, name=name,
        compiler_params=pltpu.CompilerParams(has_side_effects=EFFECT),
    )(*[_in_hbm(a) for a in grads], *[_in_hbm(a) for a in lands], after)
    return (out[0], out[1], list(out[2:2 + nt]), list(out[2 + nt:2 + 2 * nt])), out[-1]


def _scatter_wait(send_sems, recv_sems, grads, lands, kinds, after, name):
    nt = len(grads)

    def body(*refs):
        ins, lnd, send_ref, recv_ref = refs[:nt], refs[nt:2 * nt], refs[2 * nt], refs[2 * nt + 1]
        for t in range(nt):
            for k in range(3):
                cp = _scatter_copy(ins, lnd, kinds, send_ref, recv_ref, t, k)
                cp.wait_send()
                cp.wait_recv()

    out = pl.pallas_call(
        body, out_shape=(*_thru(grads), *_thru(lands)), in_specs=[HBM] * (2 * nt) + [SEM, SEM, ANY],
        out_specs=tuple([HBM] * (2 * nt)), input_output_aliases={t: t for t in range(2 * nt)}, name=name,
        compiler_params=pltpu.CompilerParams(has_side_effects=EFFECT),
    )(*grads, *lands, send_sems, recv_sems, after)
    return list(out[:nt]), list(out[nt:])


def _swap_copy(arrs, lands, send_sems, recv_sems, t):
    x, y, c = _place()
    return pltpu.make_async_remote_copy(src_ref=arrs[t], dst_ref=lands[t], send_sem=send_sems.at[t], recv_sem=recv_sems.at[t],
                                        device_id=(x, y, 1 - c), device_id_type=MESH)


def _swap_start(arrs):
    nt = len(arrs)
    lands = [lax.empty(a.shape, a.dtype) for a in arrs]

    def body(*refs):
        ins, lnd, send_sems, recv_sems = refs[:nt], refs[nt:2 * nt], refs[2 * nt], refs[2 * nt + 1]
        for t in range(nt):
            _swap_copy(ins, lnd, send_sems, recv_sems, t).start()

    out = pl.pallas_call(
        body, out_shape=(pltpu.SemaphoreType.DMA((nt,)), pltpu.SemaphoreType.DMA((nt,)), *_thru(arrs), *_thru(lands)),
        in_specs=[HBM] * (2 * nt), out_specs=(SEM, SEM, *([HBM] * (2 * nt))),
        input_output_aliases={t: 2 + t for t in range(2 * nt)}, name="swap_start",
        compiler_params=pltpu.CompilerParams(has_side_effects=EFFECT),
    )(*[_in_hbm(a) for a in arrs], *[_in_hbm(a) for a in lands])
    return out[0], out[1], list(out[2:2 + nt]), list(out[2 + nt:])


def _swap_wait(t, send_sems, recv_sems, arrs, lands, after):
    nt = len(arrs)

    def body(*refs):
        ins, lnd, send_ref, recv_ref = refs[:nt], refs[nt:2 * nt], refs[2 * nt], refs[2 * nt + 1]
        cp = _swap_copy(ins, lnd, send_ref, recv_ref, t)
        cp.wait_send()
        cp.wait_recv()

    out = pl.pallas_call(
        body, out_shape=(*_thru(arrs), *_thru(lands)), in_specs=[HBM] * (2 * nt) + [SEM, SEM, ANY],
        out_specs=tuple([HBM] * (2 * nt)), input_output_aliases={t_: t_ for t_ in range(2 * nt)}, name="swap_wait_%d" % t,
        compiler_params=pltpu.CompilerParams(has_side_effects=EFFECT),
    )(*arrs, *lands, send_sems, recv_sems, after)
    return list(out[:nt]), list(out[nt:])


def _sibling_swap(arrs, name="sibling_swap"):
    nt = len(arrs)

    def body(*refs):
        ins, outs = refs[:nt], refs[nt:2 * nt]
        send_sems, recv_sems = refs[2 * nt:]
        x, y, c = _place()
        sends = [pltpu.make_async_remote_copy(src_ref=ins[t], dst_ref=outs[t], send_sem=send_sems.at[t], recv_sem=recv_sems.at[t],
                                              device_id=(x, y, 1 - c), device_id_type=MESH) for t in range(nt)]
        for cp in sends:
            cp.start()
        for cp in sends:
            cp.wait_recv()
        for cp in sends:
            cp.wait_send()

    return pl.pallas_call(
        body, out_shape=[jax.ShapeDtypeStruct(a.shape, a.dtype) for a in arrs], in_specs=[ANY] * nt, out_specs=[ANY] * nt,
        scratch_shapes=[pltpu.SemaphoreType.DMA((nt,)), pltpu.SemaphoreType.DMA((nt,))], name=name,
    )(*arrs)


def _gather_small(vec, over_c):
    n = vec.shape[0]
    flips = [(dx, dy, dc) for dx in (0, 1) for dy in (0, 1) for dc in ((0, 1) if over_c else (0,))][1:]
    np_ = len(flips)

    def body(v_ref, o_ref, send_sems, recv_sems, local_sem):
        x, y, c = _place()

        def idx(px, py, pc):
            return 4 * px + 2 * py + pc if over_c else 2 * px + py

        def peer(f):
            return (1 - x if f[0] else x, 1 - y if f[1] else y, 1 - c if f[2] else c)

        def push(k, landing):
            return pltpu.make_async_remote_copy(src_ref=v_ref, dst_ref=o_ref.at[landing], send_sem=send_sems.at[k],
                                                recv_sem=recv_sems.at[k], device_id=peer(flips[k]), device_id_type=MESH)

        mine = pltpu.make_async_copy(v_ref, o_ref.at[idx(x, y, c)], local_sem)
        sends = [push(k, idx(x, y, c)) for k in range(np_)]
        for cp in [mine] + sends:
            cp.start()
        for k in range(np_):
            push(k, idx(*peer(flips[k]))).wait_recv()
        for cp in sends:
            cp.wait_send()
        mine.wait()

    return pl.pallas_call(
        body, out_shape=jax.ShapeDtypeStruct((np_ + 1, n, 128), F32), in_specs=[ANY], out_specs=ANY,
        scratch_shapes=[pltpu.SemaphoreType.DMA((np_,)), pltpu.SemaphoreType.DMA((np_,)), pltpu.SemaphoreType.DMA(())],
        name="gather_small_all" if over_c else "gather_small_xy",
    )(vec)


def _sum_rows(buf, after, tr=512):
    p, n, _ = buf.shape
    tr = min(tr, n)

    def body(b_ref, after_ref, o_ref):
        acc = b_ref[0]
        for k in range(1, p):
            acc = acc + b_ref[k]
        o_ref[...] = acc

    return pl.pallas_call(
        body, out_shape=jax.ShapeDtypeStruct((n, 128), F32), grid=(n // tr,),
        in_specs=[pl.BlockSpec((p, tr, 128), lambda i: (0, i, 0)), ANY], out_specs=pl.BlockSpec((tr, 128), lambda i: (i, 0)),
        name="sum_rows", compiler_params=_params(("parallel",)),
    )(buf, after)


def _sum_partials_into(stack, at, depth, grad, recv, kind, jj, name, tr=128):
    _, _, r, c = recv.shape
    tr = min(tr, r)

    def body(j_ref, g_ref, r0, r1, r2, *rest):
        rest[-1][...] = ((g_ref[...].astype(F32) + r0[...].astype(F32)) + r1[...].astype(F32)) + r2[...].astype(F32)

    if kind == "col":
        own = pl.BlockSpec((None, tr, c), lambda b, j: (0, b, j[0]))
    else:
        own = pl.BlockSpec((None, tr, c), lambda b, j: (0, j[0] * (r // tr) + b, 0))
    got = lambda k: pl.BlockSpec((None, None, tr, c), functools.partial(lambda k, b, j: (k, 0, b, 0), k))
    chained = stack is not None
    return pl.pallas_call(
        body, out_shape=jax.ShapeDtypeStruct((depth, r, c), F32),
        grid_spec=pltpu.PrefetchScalarGridSpec(
            num_scalar_prefetch=1, grid=(r // tr,), in_specs=[own, got(0), got(1), got(2)] + ([ANY] if chained else []),
            out_specs=pl.BlockSpec((None, tr, c), lambda b, j: (at, b, 0))),
        input_output_aliases={5: 0} if chained else {}, name=name, compiler_params=_params(("parallel",)),
    )(*([jj, grad, recv, recv, recv] + ([stack] if chained else [])))


WEIGHTS = ['norm_mix_w', 'w_in', 'hg_lb_raw', 'hg_norm_w', 'cv_dw_w', 'cv_dw_b', 'cv_ln_w', 'cv_ln_b', 'pl_w', 'pl_scale',
           'lru_conv_w', 'lru_conv_b', 'lru_wa', 'lru_ba', 'lru_wx', 'lru_bx', 'lru_lambda', 'gate_b', 'w_branch', 'w_out',
           'norm_mem_w', 'mem_norm_w', 'xa_wq', 'xa_wkv', 'xa_wo', 'norm_ffn_w', 'ffn_w1', 'ffn_w2', 'final_norm_w']
BIG = {'w_in': 'col', 'w_branch': 'col', 'w_out': 'row', 'xa_wq': 'row', 'xa_wkv': 'col', 'xa_wo': 'row', 'ffn_w1': 'col', 'ffn_w2': 'row'}
SMALL_SPLIT = ('gate_b', 'cv_dw_w', 'lru_conv_w')
SMALL = [n for n in WEIGHTS if n not in BIG]
PIECE_GROUPS = [['ffn_w1', 'ffn_w2', 'xa_wq', 'xa_wkv', 'xa_wo'],
                ['w_out', ('w_branch', 0), ('w_branch', 1), ('w_branch', 2), ('w_branch', 3), 'w_in']]


def _piece_kinds(pieces):
    return [BIG[k[0] if isinstance(k, tuple) else k] for k in pieces]
ROWS_PAD = 512


def _as3d(a):
    return a.reshape((-1,) + a.shape[-2:])


def _pack(parts):
    flat = jnp.concatenate([p.reshape(-1).astype(F32) for p in parts])
    n = -(-flat.shape[0] // (128 * ROWS_PAD)) * ROWS_PAD
    return jnp.pad(flat, (0, n * 128 - flat.shape[0])).reshape(n, 128)


def _unpack(packed, shapes):
    flat, out, o = packed.reshape(-1), [], 0
    for sh in shapes:
        sz = math.prod(sh)
        out.append(flat[o:o + sz].reshape(sh))
        o += sz
    return out


def _block_diag(w):
    h, a, b = w.shape
    eye = jnp.eye(h, dtype=w.dtype)
    return (w[:, :, None, :] * eye[:, None, :, None]).reshape(h * a, h * b)


def _diag_blocks(m, h):
    a, b = m.shape[0] // h, m.shape[1] // h
    return jnp.stack([m[i * a:(i + 1) * a, i * b:(i + 1) * b] for i in range(h)])


def kernel(x, mem, norm_mix_w, w_in, hg_lb_raw, hg_norm_w, cv_dw_w, cv_dw_b, cv_ln_w, cv_ln_b, pl_w, pl_scale, lru_conv_w, lru_conv_b, lru_wa, lru_ba, lru_wx, lru_bx, lru_lambda, gate_b, w_branch, w_out, norm_mem_w, mem_norm_w, xa_wq, xa_wkv, xa_wo, norm_ffn_w, ffn_w1, ffn_w2, final_norm_w, loss_target, m_norm_mix_w, m_w_in, m_hg_lb_raw, m_hg_norm_w, m_cv_dw_w, m_cv_dw_b, m_cv_ln_w, m_cv_ln_b, m_pl_w, m_pl_scale, m_lru_conv_w, m_lru_conv_b, m_lru_wa, m_lru_ba, m_lru_wx, m_lru_bx, m_lru_lambda, m_gate_b, m_w_branch, m_w_out, m_norm_mem_w, m_mem_norm_w, m_xa_wq, m_xa_wkv, m_xa_wo, m_norm_ffn_w, m_ffn_w1, m_ffn_w2, m_final_norm_w, v_norm_mix_w, v_w_in, v_hg_lb_raw, v_hg_norm_w, v_cv_dw_w, v_cv_dw_b, v_cv_ln_w, v_cv_ln_b, v_pl_w, v_pl_scale, v_lru_conv_w, v_lru_conv_b, v_lru_wa, v_lru_ba, v_lru_wx, v_lru_bx, v_lru_lambda, v_gate_b, v_w_branch, v_w_out, v_norm_mem_w, v_mem_norm_w, v_xa_wq, v_xa_wkv, v_xa_wo, v_norm_ffn_w, v_ffn_w1, v_ffn_w2, v_final_norm_w):
    w = dict(zip(WEIGHTS, (norm_mix_w, w_in, hg_lb_raw, hg_norm_w, cv_dw_w, cv_dw_b, cv_ln_w, cv_ln_b, pl_w, pl_scale, lru_conv_w, lru_conv_b, lru_wa, lru_ba, lru_wx, lru_bx, lru_lambda, gate_b, w_branch, w_out, norm_mem_w, mem_norm_w, xa_wq, xa_wkv, xa_wo, norm_ffn_w, ffn_w1, ffn_w2, final_norm_w)))
    m1 = dict(zip(WEIGHTS, (m_norm_mix_w, m_w_in, m_hg_lb_raw, m_hg_norm_w, m_cv_dw_w, m_cv_dw_b, m_cv_ln_w, m_cv_ln_b, m_pl_w, m_pl_scale, m_lru_conv_w, m_lru_conv_b, m_lru_wa, m_lru_ba, m_lru_wx, m_lru_bx, m_lru_lambda, m_gate_b, m_w_branch, m_w_out, m_norm_mem_w, m_mem_norm_w, m_xa_wq, m_xa_wkv, m_xa_wo, m_norm_ffn_w, m_ffn_w1, m_ffn_w2, m_final_norm_w)))
    v1 = dict(zip(WEIGHTS, (v_norm_mix_w, v_w_in, v_hg_lb_raw, v_hg_norm_w, v_cv_dw_w, v_cv_dw_b, v_cv_ln_w, v_cv_ln_b, v_pl_w, v_pl_scale, v_lru_conv_w, v_lru_conv_b, v_lru_wa, v_lru_ba, v_lru_wx, v_lru_bx, v_lru_lambda, v_gate_b, v_w_branch, v_w_out, v_norm_mem_w, v_mem_norm_w, v_xa_wq, v_xa_wkv, v_xa_wo, v_norm_ffn_w, v_ffn_w1, v_ffn_w2, v_final_norm_w)))
    seq = x.shape[1]
    xs, mems, tgt = x.reshape(seq, D_MODEL), mem.reshape(-1, D_MODEL), loss_target.reshape(seq, D_MODEL)
    jj = 2 * lax.axis_index("x") + lax.axis_index("y")
    jj1 = jnp.reshape(jj, (1,)).astype(jnp.int32)

    split_shapes = [w[n].shape for n in SMALL_SPLIT]
    got = _gather_small(_pack([w[n] for n in SMALL_SPLIT]), over_c=False)
    per_chip = [_unpack(got[k], split_shapes) for k in range(4)]
    full_small = {n: jnp.concatenate([per_chip[k][i] for k in range(4)], axis=-1) for i, n in enumerate(SMALL_SPLIT)}
    big_names = list(BIG)
    kinds = [BIG[n] for n in big_names]
    g_send, g_recv, fulls = _gather_start([_cast_into_full(_as3d(w[n]), BIG[n], jj1, "cast_" + n) for n in big_names], kinds, got)
    tix = {n: t for t, n in enumerate(big_names)}

    lb = _lb_fwd(hg_lb_raw)
    row = lambda a: a.reshape(1, -1)

    def layer_params(l):
        return dict(
            nmix=row(norm_mix_w[l]), lb=row(lb[l]), hgnw=row(hg_norm_w[l]),
            cw=jnp.pad(full_small['cv_dw_w'][l], ((0, 32 - CV_K), (0, 0))), cb=row(cv_dw_b[l]), lnw=row(cv_ln_w[l]), lnb=row(cv_ln_b[l]),
            plw=pl_w[l], plsc=row(pl_scale[l]),
            lcw=jnp.pad(full_small['lru_conv_w'][l], ((0, 8 - LRU_CONV), (0, 0))), lcb=row(lru_conv_b[l]),
            wa=_block_diag(lru_wa[l]).astype(BF16), ba=row(lru_ba[l]), wx=_block_diag(lru_wx[l]).astype(BF16), bx=row(lru_bx[l]),
            lam=row(lru_lambda[l]), gb=full_small['gate_b'][l], nmem=row(norm_mem_w[l]), memw=row(mem_norm_w[l]), nffn=row(norm_ffn_w[l]))

    saved = []
    xc = xs
    def by_name(arrays):
        wf_ = dict(zip(big_names, arrays))
        wf_['w_branch'] = wf_['w_branch'].reshape(DEPTH, 4, 512, D_MODEL)
        return wf_

    def landed(l, names, arrays, after):
        which = [tix[n] for n in names]
        arrays = _gather_wait(l, which, g_send, g_recv, arrays, kinds, after)
        f_send, f_recv, arrays = _fill_start(l, which, arrays, kinds)
        return (which, f_send, f_recv), arrays

    def complete(l, pending, arrays, after):
        which, f_send, f_recv = pending
        return _fill_wait(l, which, f_send, f_recv, arrays, kinds, after)

    rest = [n for n in big_names if n != 'w_in']
    for l in range(DEPTH):
        if l == 0:
            pending, fulls = landed(0, ['w_in'], fulls, xc)
        fulls = complete(l, pending, fulls, xc)
        wf = by_name(fulls)
        p = layer_params(l)
        h = _norm_fwd(xc, p['nmix'], "norm_mix")
        proj = _mm(h, wf['w_in'], "nn", F32, "proj", tm=seq, layer=l)
        if l == 0:
            pending, fulls = landed(0, rest, fulls, proj)
        b_hg, st = _hgrn_fwd(proj, p['lb'], p['hgnw'])
        b_cv = _conv_fwd(proj, p['cw'], p['cb'], p['lnw'], p['lnb'])
        b_pl = _pool_fwd(proj, p['plw'], p['plsc'])
        b_lru, hs = _lru_fwd(proj, p['lcw'], p['lcb'], p['wa'], p['ba'], p['wx'], p['bx'], p['lam'])
        branches = (b_hg, b_cv, b_pl, b_lru)
        if l == 0:
            fulls = complete(0, pending, fulls, b_lru)
            wf = by_name(fulls)
        x1 = _merge_fwd(xc, branches, proj, p['gb'], wf['w_branch'], wf['w_out'], l)
        if l + 1 < DEPTH:
            pending, fulls = landed(l + 1, big_names, fulls, x1)
            wf = by_name(fulls)
        memn = _norm_fwd(mems, p['memw'], "norm_memtok")
        kv = _mm(memn, wf['xa_wkv'], "nn", BF16, "kv_proj", layer=l)
        x2 = _attn_fwd(x1, p['nmem'], wf['xa_wq'], kv, wf['xa_wo'], l)
        x3 = _ffn_fwd(x2, p['nffn'], wf['ffn_w1'], wf['ffn_w2'], l, ts=1024)
        saved.append(dict(p=p, x=xc, h=h, proj=proj, st=st, hs=hs, branches=branches, x1=x1, memn=memn, kv=kv, x2=x2))
        xc = x3

    loss_blk, dx, dfinal = _final_loss(xc, row(final_norm_w), tgt)

    gs = {n: [None] * DEPTH for n in SMALL if n != 'final_norm_w'}
    dlb = [None] * DEPTH
    in_flight = [[None, None] for _ in range(DEPTH)]

    def scatter(grads, grp, after, name):
        pieces = PIECE_GROUPS[grp]
        return _scatter_start([grads[key][None] for key in pieces], _piece_kinds(pieces), after, name)

    token = loss_blk
    for l in reversed(range(DEPTH)):
        sv = saved[l]
        p = sv['p']
        gb = {}
        dx2, gs['norm_ffn_w'][l], h3, da, r, dxb = _ffn_bwd(sv['x2'], dx, p['nffn'], wf['ffn_w1'], wf['ffn_w2'], l, token, ts=512)
        gb['ffn_w1'] = _mm(h3, da, "tn", BF16, "dw_ffn1", tm=1024)
        gb['ffn_w2'] = _mm(r, dxb, "tn", BF16, "dw_ffn2", tn=1024)
        dx1, gs['norm_mem_w'][l], h2, o, dq, dxb2, dk, dv = _attn_bwd(sv['x1'], dx2, p['nmem'], wf['xa_wq'], sv['kv'], wf['xa_wo'], l)
        gb['xa_wq'] = _mm(h2, dq, "tn", BF16, "dw_q")
        gb['xa_wo'] = _mm(o, dxb2, "tn", BF16, "dw_o")
        dkv = jnp.concatenate([dk, dv], axis=1)
        gb['xa_wkv'] = _mm(sv['memn'], dkv, "tn", BF16, "dw_kv")
        dmemn = _mm(dkv, wf['xa_wkv'], "nt", F32, "dmemn", layer=l)
        _, gs['mem_norm_w'][l] = _norm_bwd(mems, p['memw'], dmemn, None, "norm_memtok_bwd")
        in_flight[l][0], token = scatter(gb, 0, dx1, "scatter_start_%d_0" % l)
        db0, db1, db2, db3, dgp, dup, mg, dxb1, gs['gate_b'][l] = _merge_bwd(
            dx1, sv['branches'], sv['proj'], p['gb'], wf['w_branch'], wf['w_out'], l, token)
        gb['w_out'] = _mm(mg, dxb1, "tn", BF16, "dw_out")
        for kb in range(4):
            gb['w_branch', kb] = _mm(sv['branches'][kb], dup, "tn", BF16, "dw_branch", b_col0=kb * D_MODEL, n=D_MODEL, tn=512)
        dhg, dlb[l], gs['hg_norm_w'][l] = _hgrn_bwd(sv['proj'], db0, sv['st'], p['lb'], p['hgnw'])
        dcv, dcw, gs['cv_dw_b'][l], gs['cv_ln_w'][l], gs['cv_ln_b'][l] = _conv_bwd(sv['proj'], db1, p['cw'], p['cb'], p['lnw'], p['lnb'])
        gs['cv_dw_w'][l] = dcw[:CV_K]
        dpl, gs['pl_w'][l], gs['pl_scale'][l] = _pool_bwd(sv['proj'], db2, p['plw'], p['plsc'])
        dlru, dlcw, gs['lru_conv_b'][l], dwa, gs['lru_ba'][l], dwx, gs['lru_bx'][l], gs['lru_lambda'][l] = _lru_bwd(
            sv['proj'], sv['hs'], db3, p['lcw'], p['lcb'], p['wa'], p['ba'], p['wx'], p['bx'], p['lam'])
        gs['lru_conv_w'][l] = dlcw[:LRU_CONV]
        gs['lru_wa'][l], gs['lru_wx'][l] = _diag_blocks(dwa, LRU_HEADS), _diag_blocks(dwx, LRU_HEADS)
        dproj = [dhg, dcv, dpl, dlru, dgp]
        gb['w_in'] = _mm_tn_pieces(sv['h'], dproj, "dw_in")
        dh = _mm_nt_pieces(dproj, wf['w_in'], l, "dh_mix")
        dx, gs['norm_mix_w'][l] = _norm_bwd(sv['x'], p['nmix'], dh, dx1, "norm_mix_bwd")
        if l:
            in_flight[l][1], token = scatter(gb, 1, dx, "scatter_start_%d_1" % l)
    grad_x = dx.reshape(x.shape)
    gs['hg_lb_raw'] = _lb_bwd(hg_lb_raw, jnp.concatenate(dlb, axis=0))

    def full_shape(n):
        return full_small[n].shape if n in SMALL_SPLIT else w[n].shape

    small_full = []
    for n in SMALL:
        g = gs[n] if n == 'hg_lb_raw' else dfinal if n == 'final_norm_w' else jnp.stack(gs[n])
        small_full.append(g.reshape(full_shape(n)))
    mine = _pack(small_full + [loss_blk[0:1, 0:1]])
    chip_sum = _sum_rows(jnp.stack([mine, _sibling_swap([mine], "sibling_swap_small")[0]]), mine)
    everyone = _gather_small(chip_sum, over_c=False)
    in_flight[0][1], token = scatter(gb, 1, everyone, "scatter_start_0_1")
    total = _sum_rows(everyone, token)
    parts = _unpack(total, [full_shape(n) for n in SMALL] + [(1,)])
    loss = parts[-1].reshape(())
    g_small = {}
    for n, g in zip(SMALL, parts[:-1]):
        if n in SMALL_SPLIT:
            width = w[n].shape[-1]
            g = lax.dynamic_slice_in_dim(g, jj * width, width, axis=g.ndim - 1)
        g_small[n] = g
    shapes = [w[n].shape for n in SMALL]
    upd = _adamw(_pack([w[n] for n in SMALL]), [_pack([g_small[n] for n in SMALL])], _pack([m1[n] for n in SMALL]),
                 _pack([v1[n] for n in SMALL]), "adamw_small")
    d_small, m_small, v_small = [dict(zip(SMALL, _unpack(u, shapes))) for u in upd]

    stacks = {n: None for n in big_names}
    done_before = upd[0]
    for l in reversed(range(DEPTH)):
        for grp, pieces in enumerate(PIECE_GROUPS):
            s_send, s_recv, g_thru, lands = in_flight[l][grp]
            g_thru, lands = _scatter_wait(s_send, s_recv, g_thru, lands, _piece_kinds(pieces), done_before,
                                          "scatter_wait_%d_%d" % (l, grp))
            for key, g, r in zip(pieces, g_thru, lands):
                n, kb = key if isinstance(key, tuple) else (key, None)
                per = 1 if kb is None else 4
                stacks[n] = _sum_partials_into(stacks[n], l * per + (kb or 0), DEPTH * per, g, r, BIG[n], jj1, "sum_" + n)
                done_before = stacks[n]
    order = sorted(big_names, key=lambda n: math.prod(w[n].shape))
    w_send, w_recv, partial, theirs = _swap_start([stacks[n] for n in order])
    g_big, d_big, m_big, v_big = {}, {}, {}, {}
    for t, n in enumerate(order):
        partial, theirs = _swap_wait(t, w_send, w_recv, partial, theirs, done_before)
        c2 = lambda a: a.reshape(-1, a.shape[-1])
        out = _adamw(c2(w[n]), [c2(partial[t]), c2(theirs[t])], c2(m1[n]), c2(v1[n]), "adamw_" + n)
        g_big[n], d_big[n], m_big[n], v_big[n] = [o.reshape(w[n].shape) for o in out]
        done_before = out[0]

    pick = lambda small, big: [big[n] if n in BIG else small[n] for n in WEIGHTS]
    return (loss, grad_x, *pick(g_small, g_big), *pick(d_small, d_big), *pick(m_small, m_big), *pick(v_small, v_big))
```

```python
import functools
import math

import jax
import jax.numpy as jnp
from jax import lax
from jax.experimental import pallas as pl
from jax.experimental.pallas import tpu as pltpu

F32 = jnp.float32
BF16 = jnp.bfloat16
MESH = pl.DeviceIdType.MESH
ANY = pl.BlockSpec(memory_space=pl.ANY)

D_MODEL = 1024
DEPTH = 4
CHUNK = 64
SUB = 16
EPS = 1e-6
HG_HEADS, HG_D = 4, 128
CV_W, CV_K = 512, 31
CV_HALO = 32
POOL_WINDOWS = (2, 4, 8, 16)
POOL_HALO = 16
LRU_W, LRU_HEADS, LRU_HD, LRU_CONV = 512, 8, 64, 4
LRU_HALO = 8
LRU_C = 8.0
MIX_W = 4608
IN_W = 8704
XA_HEADS, XA_HD = 4, 256
D_FF = 4096
FF_CHUNK = 1024
ADAM_LR, ADAM_B1, ADAM_B2, ADAM_EPS, ADAM_WD, ADAM_STEP = 0.001, 0.9, 0.999, 1e-08, 0.01, 10
VMEM_LIMIT = 56 * 1024 * 1024
EXP_CLAMP = 80.0
HI = lax.Precision.HIGHEST


def _params(sem=None):
    return pltpu.CompilerParams(dimension_semantics=sem, vmem_limit_bytes=VMEM_LIMIT)


def _sigmoid(x):
    return 1.0 / (1.0 + jnp.exp(-x))


def _dsilu(x, s):
    return s * (1.0 + x * (1.0 - s))


_GELU_C = math.sqrt(2.0 / math.pi)


def _gelu_parts(x):
    t = jnp.tanh(_GELU_C * (x + 0.044715 * x * x * x))
    g = 0.5 * x * (1.0 + t)
    dg = 0.5 * (1.0 + t) + 0.5 * x * (1.0 - t * t) * _GELU_C * (1.0 + 3 * 0.044715 * x * x)
    return g, dg


def _dot(a, b, dims, precision=None):
    return lax.dot_general(a, b, (dims, ((), ())), precision=precision, preferred_element_type=F32)


def _nn(a, b, **k):
    return _dot(a, b, ((1,), (0,)), **k)


def _nt(a, b, **k):
    return _dot(a, b, ((1,), (1,)), **k)


def _tn(a, b, **k):
    return _dot(a, b, ((0,), (0,)), **k)


def _split(x):
    hi = x.astype(BF16)
    return hi, (x - hi.astype(F32)).astype(BF16)


def _nn3(a, b):
    (ah, al), (bh, bl) = _split(a), _split(b)
    return _nn(jnp.concatenate([ah, ah, al], axis=1), jnp.concatenate([bh, bl, bh], axis=0))


def _tn3(a, b):
    (ah, al), (bh, bl) = _split(a), _split(b)
    return _tn(jnp.concatenate([ah, ah, al], axis=0), jnp.concatenate([bh, bl, bh], axis=0))


def _rms_fwd(x, w):
    r = lax.rsqrt(jnp.mean(x * x, axis=-1, keepdims=True) + EPS)
    return x * r * w, r


def _rms_bwd(x, r, w, dy):
    xr = x * r
    g = dy * w
    dx = r * (g - xr * jnp.mean(g * xr, axis=-1, keepdims=True))
    return dx, jnp.sum(dy * xr, axis=0, keepdims=True)


def _lw(shape, index, layer):
    return pl.BlockSpec((None,) + tuple(shape), lambda *g: (layer,) + tuple(index(*g)))


def _mm(a, b, mode, out_dtype, name, tm=512, tn=512, b_col0=0, n=None, layer=None):
    bs = b.shape if layer is None else b.shape[1:]
    if mode == "nn":
        m, k = a.shape
        n = bs[1] if n is None else n
    elif mode == "nt":
        m, k = a.shape
        n = bs[0] if n is None else n
    else:
        k, m = a.shape
        n = bs[1] if n is None else n
    tm, tn = min(tm, m), min(tn, n)
    assert m % tm == 0 and n % tn == 0 and b_col0 % tn == 0
    off = b_col0 // tn

    def body(a_ref, b_ref, o_ref):
        av, bv = a_ref[...].astype(BF16), b_ref[...].astype(BF16)
        o_ref[...] = (_nn if mode == "nn" else _nt if mode == "nt" else _tn)(av, bv).astype(out_dtype)

    def bspec(shape, index):
        return pl.BlockSpec(shape, index) if layer is None else _lw(shape, index, layer)

    if mode == "tn":
        grid = (m // tm, n // tn)
        a_spec = pl.BlockSpec((k, tm), lambda i, j: (0, i))
        b_spec = bspec((k, tn), lambda i, j: (0, j + off))
        o_spec = pl.BlockSpec((tm, tn), lambda i, j: (i, j))
    else:
        grid = (n // tn, m // tm)
        a_spec = pl.BlockSpec((tm, k), lambda j, i: (i, 0))
        if mode == "nn":
            b_spec = bspec((k, tn), lambda j, i: (0, j + off))
        else:
            b_spec = bspec((tn, k), lambda j, i: (j + off, 0))
        o_spec = pl.BlockSpec((tm, tn), lambda j, i: (i, j))
    return pl.pallas_call(
        body, out_shape=jax.ShapeDtypeStruct((m, n), out_dtype), grid=grid,
        in_specs=[a_spec, b_spec], out_specs=o_spec, name=name,
        compiler_params=_params(("parallel", "parallel")),
    )(a, b)


def _mm_tn_pieces(a, pieces, name, tn=256):
    k, m = a.shape
    starts, o = [], 0
    for pc in pieces:
        assert pc.shape[1] % tn == 0
        starts.append(o // tn)
        o += pc.shape[1]
    counts = [pc.shape[1] // tn for pc in pieces]
    npc = len(pieces)

    def body(a_ref, *refs):
        o_ref = refs[npc]
        j = pl.program_id(0)
        for pi in range(npc):
            @pl.when(jnp.logical_and(j >= starts[pi], j < starts[pi] + counts[pi]))
            def _(pi=pi):
                o_ref[...] = _tn(a_ref[...], refs[pi][...]).astype(o_ref.dtype)

    def piece_spec(pi):
        return pl.BlockSpec((k, tn), lambda j: (0, jnp.clip(j - starts[pi], 0, counts[pi] - 1)))

    return pl.pallas_call(
        body, out_shape=jax.ShapeDtypeStruct((m, o), BF16), grid=(o // tn,),
        in_specs=[pl.BlockSpec((k, m), lambda j: (0, 0))] + [piece_spec(pi) for pi in range(npc)],
        out_specs=pl.BlockSpec((m, tn), lambda j: (0, j)), name=name, compiler_params=_params(("parallel",)),
    )(a, *pieces)


def _mm_nt_pieces(pieces, b, layer, name, tm=256, tn=512):
    m = pieces[0].shape[0]
    n, k = b.shape[1:]
    offs, o = [], 0
    for pc in pieces:
        offs.append(o)
        o += pc.shape[1]
    assert o == k
    npc = len(pieces)

    def body(*refs):
        b_ref, o_ref = refs[npc], refs[npc + 1]
        acc = jnp.zeros((tm, tn), F32)
        for pi in range(npc):
            acc += _nt(refs[pi][...], b_ref[:, offs[pi]:offs[pi] + pieces[pi].shape[1]])
        o_ref[...] = acc

    return pl.pallas_call(
        body, out_shape=jax.ShapeDtypeStruct((m, n), F32), grid=(n // tn, m // tm),
        in_specs=[pl.BlockSpec((tm, pc.shape[1]), lambda j, i: (i, 0)) for pc in pieces] + [_lw((tn, k), lambda j, i: (j, 0), layer)],
        out_specs=pl.BlockSpec((tm, tn), lambda j, i: (i, j)), name=name, compiler_params=_params(("parallel", "parallel")),
    )(*pieces, b)


def _norm_fwd(x, w, name, ts=512):
    s, d = x.shape
    ts = min(ts, s)

    def body(x_ref, w_ref, o_ref):
        o_ref[...] = _rms_fwd(x_ref[...], w_ref[...])[0].astype(BF16)

    return pl.pallas_call(
        body, out_shape=jax.ShapeDtypeStruct((s, d), BF16), grid=(s // ts,),
        in_specs=[pl.BlockSpec((ts, d), lambda i: (i, 0)), pl.BlockSpec((1, d), lambda i: (0, 0))],
        out_specs=pl.BlockSpec((ts, d), lambda i: (i, 0)), name=name, compiler_params=_params(("parallel",)),
    )(x, w)


def _norm_bwd(x, w, dy, dres, name, ts=512):
    s, d = x.shape
    ts = min(ts, s)
    with_res = dres is not None

    def body(*refs):
        if with_res:
            x_ref, w_ref, dy_ref, dres_ref, dx_ref, dw_ref = refs
        else:
            x_ref, w_ref, dy_ref, dx_ref, dw_ref = refs
        xv = x_ref[...]
        r = lax.rsqrt(jnp.mean(xv * xv, axis=-1, keepdims=True) + EPS)
        dx, dw = _rms_bwd(xv, r, w_ref[...], dy_ref[...])
        dx_ref[...] = dx + dres_ref[...] if with_res else dx

        @pl.when(pl.program_id(0) == 0)
        def _():
            dw_ref[...] = jnp.zeros_like(dw_ref)

        dw_ref[...] += dw

    row = pl.BlockSpec((ts, d), lambda i: (i, 0))
    vec = pl.BlockSpec((1, d), lambda i: (0, 0))
    return pl.pallas_call(
        body, out_shape=(jax.ShapeDtypeStruct((s, d), F32), jax.ShapeDtypeStruct((1, d), F32)), grid=(s // ts,),
        in_specs=[row, vec, row] + ([row] if with_res else []), out_specs=(row, vec), name=name,
        compiler_params=_params(("arbitrary",)),
    )(*([x, w, dy] + ([dres] if with_res else [])))


def _ffn_fwd(x, nw, w1, w2, layer, ts=256):
    s, d = x.shape
    ts = min(ts, s)
    nj = D_FF // FF_CHUNK

    def body(x_ref, nw_ref, w1_ref, w2_ref, o_ref, h_scr, acc):
        j = pl.program_id(1)

        @pl.when(j == 0)
        def _():
            h_scr[...] = _rms_fwd(x_ref[...], nw_ref[...])[0].astype(BF16)
            acc[...] = jnp.zeros_like(acc)

        a = _nn(h_scr[...], w1_ref[...])
        rl = jnp.maximum(a, 0.0)
        acc[...] += _nn((rl * rl).astype(BF16), w2_ref[...])

        @pl.when(j == nj - 1)
        def _():
            o_ref[...] = x_ref[...] + acc[...]

    row = pl.BlockSpec((ts, d), lambda i, j: (i, 0))
    return pl.pallas_call(
        body, out_shape=jax.ShapeDtypeStruct((s, d), F32), grid=(s // ts, nj),
        in_specs=[row, pl.BlockSpec((1, d), lambda i, j: (0, 0)),
                  _lw((d, FF_CHUNK), lambda i, j: (0, j), layer), _lw((FF_CHUNK, d), lambda i, j: (j, 0), layer)],
        out_specs=row, scratch_shapes=[pltpu.VMEM((ts, d), BF16), pltpu.VMEM((ts, d), F32)], name="ffn_fwd",
        compiler_params=_params(("parallel", "arbitrary")),
    )(x, nw, w1, w2)


def _ffn_bwd(x, dxo, nw, w1, w2, layer, after, ts=256):
    s, d = x.shape
    ts = min(ts, s)
    nj = D_FF // FF_CHUNK

    def body(x_ref, dxo_ref, nw_ref, w1_ref, w2_ref, after_ref, dx_ref, dnw_ref, h_ref, da_ref, r_ref, dxb_ref, dh):
        i, j = pl.program_id(0), pl.program_id(1)

        @pl.when(j == 0)
        def _():
            h_ref[...] = _rms_fwd(x_ref[...], nw_ref[...])[0].astype(BF16)
            dxb_ref[...] = dxo_ref[...].astype(BF16)
            dh[...] = jnp.zeros_like(dh)

        a = _nn(h_ref[...], w1_ref[...])
        rl = jnp.maximum(a, 0.0)
        r_ref[...] = (rl * rl).astype(BF16)
        da = (_nt(dxb_ref[...], w2_ref[...]) * (2.0 * rl)).astype(BF16)
        da_ref[...] = da
        dh[...] += _nt(da, w1_ref[...])

        @pl.when(jnp.logical_and(i == 0, j == 0))
        def _():
            dnw_ref[...] = jnp.zeros_like(dnw_ref)

        @pl.when(j == nj - 1)
        def _():
            xv = x_ref[...]
            r = lax.rsqrt(jnp.mean(xv * xv, axis=-1, keepdims=True) + EPS)
            dx, dw = _rms_bwd(xv, r, nw_ref[...], dh[...])
            dx_ref[...] = dxo_ref[...] + dx
            dnw_ref[...] += dw

    row = pl.BlockSpec((ts, d), lambda i, j: (i, 0))
    vec = pl.BlockSpec((1, d), lambda i, j: (0, 0))
    ffc = pl.BlockSpec((ts, FF_CHUNK), lambda i, j: (i, j))
    return pl.pallas_call(
        body,
        out_shape=(jax.ShapeDtypeStruct((s, d), F32), jax.ShapeDtypeStruct((1, d), F32), jax.ShapeDtypeStruct((s, d), BF16),
                   jax.ShapeDtypeStruct((s, D_FF), BF16), jax.ShapeDtypeStruct((s, D_FF), BF16), jax.ShapeDtypeStruct((s, d), BF16)),
        grid=(s // ts, nj),
        in_specs=[row, row, vec, _lw((d, FF_CHUNK), lambda i, j: (0, j), layer), _lw((FF_CHUNK, d), lambda i, j: (j, 0), layer), ANY],
        out_specs=(row, vec, row, ffc, ffc, row), scratch_shapes=[pltpu.VMEM((ts, d), F32)], name="ffn_bwd",
        compiler_params=_params(("arbitrary", "arbitrary")),
    )(x, dxo, nw, w1, w2, after)


def _attn_probs(q, k_ref):
    ps = []
    for hd in range(XA_HEADS):
        c = slice(hd * XA_HD, (hd + 1) * XA_HD)
        sc = _nt(q[:, c].astype(BF16), k_ref[:, c]) * (XA_HD ** -0.5)
        e = jnp.exp(sc - jnp.max(sc, axis=-1, keepdims=True))
        ps.append(e / jnp.sum(e, axis=-1, keepdims=True))
    return ps


def _attn_fwd(x, nw, wq, kv, wo, layer, ts=256):
    s, d = x.shape
    ts = min(ts, s)
    nm = kv.shape[0]

    def body(x_ref, nw_ref, wq_ref, k_ref, v_ref, wo_ref, o_ref):
        xv = x_ref[...]
        h = _rms_fwd(xv, nw_ref[...])[0].astype(BF16)
        q = _nn(h, wq_ref[...])
        ps = _attn_probs(q, k_ref)
        o = jnp.concatenate([_nn(ps[hd].astype(BF16), v_ref[:, hd * XA_HD:(hd + 1) * XA_HD]) for hd in range(XA_HEADS)], axis=1)
        o_ref[...] = xv + _nn(o.astype(BF16), wo_ref[...])

    row = pl.BlockSpec((ts, d), lambda i: (i, 0))
    full = lambda r, c: pl.BlockSpec((r, c), lambda i: (0, 0))
    wsp = _lw((d, d), lambda i: (0, 0), layer)
    return pl.pallas_call(
        body, out_shape=jax.ShapeDtypeStruct((s, d), F32), grid=(s // ts,),
        in_specs=[row, full(1, d), wsp, full(nm, d), pl.BlockSpec((nm, d), lambda i: (0, 1)), wsp], out_specs=row, name="attn_fwd",
        compiler_params=_params(("parallel",)),
    )(x, nw, wq, kv, kv, wo)


def _attn_bwd(x, dxo, nw, wq, kv, wo, layer, ts=256):
    s, d = x.shape
    ts = min(ts, s)
    nm = kv.shape[0]

    def body(x_ref, dxo_ref, nw_ref, wq_ref, k_ref, v_ref, wo_ref,
             dx_ref, dnw_ref, h_ref, o_ref, dq_ref, dxb_ref, dk_ref, dv_ref):
        xv = x_ref[...]
        hf, r = _rms_fwd(xv, nw_ref[...])
        h = hf.astype(BF16)
        h_ref[...] = h
        q = _nn(h, wq_ref[...])
        qb = q.astype(BF16)
        ps = _attn_probs(q, k_ref)
        dxb = dxo_ref[...].astype(BF16)
        dxb_ref[...] = dxb
        do = _nt(dxb, wo_ref[...])

        @pl.when(pl.program_id(0) == 0)
        def _():
            dnw_ref[...] = jnp.zeros_like(dnw_ref)
            dk_ref[...] = jnp.zeros_like(dk_ref)
            dv_ref[...] = jnp.zeros_like(dv_ref)

        dqs = []
        for hd in range(XA_HEADS):
            c = slice(hd * XA_HD, (hd + 1) * XA_HD)
            p = ps[hd]
            pb = p.astype(BF16)
            dob = do[:, c].astype(BF16)
            o_ref[:, c] = _nn(pb, v_ref[:, c]).astype(BF16)
            dp = _nt(dob, v_ref[:, c])
            ds = (p * (dp - jnp.sum(p * dp, axis=-1, keepdims=True)) * (XA_HD ** -0.5)).astype(BF16)
            dqs.append(_nn(ds, k_ref[:, c]))
            dk_ref[:, c] += _tn(ds, qb[:, c])
            dv_ref[:, c] += _tn(pb, dob)
        dq = jnp.concatenate(dqs, axis=1).astype(BF16)
        dq_ref[...] = dq
        dx, dw = _rms_bwd(xv, r, nw_ref[...], _nt(dq, wq_ref[...]))
        dx_ref[...] = dxo_ref[...] + dx
        dnw_ref[...] += dw

    row = pl.BlockSpec((ts, d), lambda i: (i, 0))
    full = lambda r, c: pl.BlockSpec((r, c), lambda i: (0, 0))
    sd = lambda dt: jax.ShapeDtypeStruct((s, d), dt)
    return pl.pallas_call(
        body,
        out_shape=(sd(F32), jax.ShapeDtypeStruct((1, d), F32), sd(BF16), sd(BF16), sd(BF16), sd(BF16),
                   jax.ShapeDtypeStruct((nm, d), F32), jax.ShapeDtypeStruct((nm, d), F32)),
        grid=(s // ts,),
        in_specs=[row, row, full(1, d), _lw((d, d), lambda i: (0, 0), layer), full(nm, d), pl.BlockSpec((nm, d), lambda i: (0, 1)),
                  _lw((d, d), lambda i: (0, 0), layer)],
        out_specs=(row, full(1, d), row, row, row, row, full(nm, d), full(nm, d)), name="attn_bwd",
        compiler_params=_params(("arbitrary",)),
    )(x, dxo, nw, wq, kv, kv, wo)


GATE_BLK0 = MIX_W // 512


def _merge_specs(ts, layer):
    row = pl.BlockSpec((ts, D_MODEL), lambda i: (i, 0))
    br = pl.BlockSpec((ts, 512), lambda i: (i, 0))
    gates = [pl.BlockSpec((ts, 512), functools.partial(lambda n, i: (i, GATE_BLK0 + n), n)) for n in range(8)]
    full = lambda *shape: pl.BlockSpec(shape, lambda i: (0,) * len(shape))
    weights = [full(4, D_MODEL), _lw((4, 512, D_MODEL), lambda i: (0, 0, 0), layer), _lw((D_MODEL, D_MODEL), lambda i: (0, 0), layer)]
    return row, br, gates, full, weights


def _merge_gates(gp_refs, gb_ref, kb):
    gp = jnp.concatenate([gp_refs[2 * kb][...], gp_refs[2 * kb + 1][...]], axis=1)
    return _sigmoid(gp + gb_ref[kb:kb + 1, :])


def _merge_fwd(x, branches, proj, gate_b, wb, wout, layer, ts=256):
    s, d = x.shape
    ts = min(ts, s)

    def body(x_ref, b0, b1, b2, b3, g0, g1, g2, g3, g4, g5, g6, g7, gb_ref, wb_ref, wo_ref, o_ref):
        brs, gps = (b0, b1, b2, b3), (g0, g1, g2, g3, g4, g5, g6, g7)
        merged = jnp.zeros((ts, d), F32)
        for kb in range(4):
            merged += _merge_gates(gps, gb_ref, kb) * _nn(brs[kb][...], wb_ref[kb])
        o_ref[...] = x_ref[...] + _nn(merged.astype(BF16), wo_ref[...])

    row, br, gates, full, weights = _merge_specs(ts, layer)
    return pl.pallas_call(
        body, out_shape=jax.ShapeDtypeStruct((s, d), F32), grid=(s // ts,),
        in_specs=[row, br, br, br, br] + gates + weights, out_specs=row, name="merge_fwd",
        compiler_params=_params(("parallel",)),
    )(x, *branches, *([proj] * 8), gate_b, wb, wout)


def _merge_bwd(dxo, branches, proj, gate_b, wb, wout, layer, after, ts=256):
    s, d = dxo.shape
    ts = min(ts, s)

    def body(dxo_ref, b0, b1, b2, b3, g0, g1, g2, g3, g4, g5, g6, g7, gb_ref, wb_ref, wo_ref, after_ref,
             db0, db1, db2, db3, dgp_ref, dup_ref, mg_ref, dxb_ref, dgb_ref):
        brs, gps, dbs = (b0, b1, b2, b3), (g0, g1, g2, g3, g4, g5, g6, g7), (db0, db1, db2, db3)
        dxb = dxo_ref[...].astype(BF16)
        dxb_ref[...] = dxb
        dm = _nt(dxb, wo_ref[...])

        @pl.when(pl.program_id(0) == 0)
        def _():
            dgb_ref[...] = jnp.zeros_like(dgb_ref)

        merged = jnp.zeros((ts, d), F32)
        for kb in range(4):
            c = slice(kb * d, (kb + 1) * d)
            g = _merge_gates(gps, gb_ref, kb)
            up = _nn(brs[kb][...], wb_ref[kb])
            merged += g * up
            dup = (dm * g).astype(BF16)
            dup_ref[:, c] = dup
            dgp = dm * up * g * (1.0 - g)
            dgp_ref[:, c] = dgp.astype(BF16)
            dgb_ref[kb:kb + 1, :] += jnp.sum(dgp, axis=0, keepdims=True)
            dbs[kb][...] = _nt(dup, wb_ref[kb])
        mg_ref[...] = merged.astype(BF16)

    row, br, gates, full, weights = _merge_specs(ts, layer)
    wide = pl.BlockSpec((ts, 4 * d), lambda i: (i, 0))
    sb = jax.ShapeDtypeStruct((s, 512), F32)
    return pl.pallas_call(
        body,
        out_shape=(sb, sb, sb, sb, jax.ShapeDtypeStruct((s, 4 * d), BF16), jax.ShapeDtypeStruct((s, 4 * d), BF16),
                   jax.ShapeDtypeStruct((s, d), BF16), jax.ShapeDtypeStruct((s, d), BF16), jax.ShapeDtypeStruct((4, d), F32)),
        grid=(s // ts,),
        in_specs=[row, br, br, br, br] + gates + weights + [ANY],
        out_specs=(br, br, br, br, wide, wide, row, row, full(4, d)), name="merge_bwd",
        compiler_params=_params(("arbitrary",)),
    )(dxo, *branches, *([proj] * 8), gate_b, wb, wout, after)


def _tri(n, upper=False):
    r = lax.broadcasted_iota(jnp.int32, (n, 3 * n), 0)
    c = lax.broadcasted_iota(jnp.int32, (n, 3 * n), 1) % n
    return jnp.where((c >= r) if upper else (c <= r), 1.0, 0.0).astype(BF16)


def _cum(tri3, x):
    hi = x.astype(BF16)
    r1 = x - hi.astype(F32)
    mid = r1.astype(BF16)
    lo = (r1 - mid.astype(F32)).astype(BF16)
    return _nn(tri3, jnp.concatenate([hi, mid, lo], axis=0))


def _hg_gates(hq, hf, lb):
    sg = _sigmoid(hf)
    fg = lb + (1.0 - lb) * sg
    sq = _sigmoid(hq)
    return sg, fg, 1.0 - fg, jnp.log(fg), hq * sq, sq


NSUB = CHUNK // SUB


def _hg_intra(qf, kk, b):
    row = lax.broadcasted_iota(jnp.int32, (CHUNK, 1), 0)
    refs = [b[i * SUB - 1:i * SUB, :] if i else jnp.zeros((1, b.shape[1]), F32) for i in range(NSUB)]
    mine = [jnp.logical_and(row >= i * SUB, row < (i + 1) * SUB) for i in range(NSUB)]
    ref_rows = refs[0]
    for i in range(1, NSUB):
        ref_rows = jnp.where(mine[i], refs[i], ref_rows)
    eq = jnp.exp(b - ref_rows)
    qt = qf * eq
    ek = jnp.concatenate([jnp.exp(jnp.minimum(r - b, EXP_CLAMP)) for r in refs], axis=1)
    kbig = jnp.concatenate([kk] * NSUB, axis=1) * ek
    qbig = jnp.concatenate([jnp.where(m, qt, 0.0) for m in mine], axis=1)
    return qt, qbig, kbig, eq, ek, mine


def _causal(n, upper=False):
    r, c = lax.broadcasted_iota(jnp.int32, (n, n), 0), lax.broadcasted_iota(jnp.int32, (n, n), 1)
    return (c >= r) if upper else (c <= r)


def _hg_chunk_fwd(qf, kk, b, v, st):
    parts = _hg_intra(qf, kk, b)
    att = jnp.where(_causal(CHUNK), _nt(parts[1].astype(BF16), parts[2].astype(BF16)), 0.0)
    qh = qf * jnp.exp(b)
    o = _nn(att.astype(BF16), v.astype(BF16)) + _nt(qh.astype(BF16), st.astype(BF16))
    bl = b[CHUNK - 1:CHUNK, :]
    kh = kk * jnp.exp(bl - b)
    return o, parts, att, qh, kh, jnp.exp(bl)


def _hgrn_fwd(proj, lb, nw, ts=256):
    s = proj.shape[0]
    ts = min(ts, s)
    nch = ts // CHUNK

    def body(q_ref, f_ref, v_ref, g_ref, lb_ref, nw_ref, o_ref, st_ref, st):
        @pl.when(pl.program_id(0) == 0)
        def _():
            st[...] = jnp.zeros_like(st)

        tri = _tri(CHUNK)

        def chunk(c, carry):
            rows = pl.ds(pl.multiple_of(c * CHUNK, CHUNK), CHUNK)
            _, _, kk, lf, qf, _ = _hg_gates(q_ref[rows, :], f_ref[rows, :], lb_ref[...])
            b = _cum(tri, lf)
            hv, hg = v_ref[rows, :], g_ref[rows, :]
            st_ref[c] = st[...]
            for h in range(HG_HEADS):
                cs = slice(h * HG_D, (h + 1) * HG_D)
                o, _, _, _, kh, ebl = _hg_chunk_fwd(qf[:, cs], kk[:, cs], b[:, cs], hv[:, cs], st[h])
                st[h] = st[h] * ebl + _tn(hv[:, cs].astype(BF16), kh.astype(BF16))
                on = _rms_fwd(o, nw_ref[...])[0]
                gh = hg[:, cs]
                o_ref[rows, cs] = (on * gh * _sigmoid(gh)).astype(BF16)
            return carry

        lax.fori_loop(0, nch, chunk, 0, unroll=4)

    col = lambda n: pl.BlockSpec((ts, 512), functools.partial(lambda n, i: (i, n), n))
    return pl.pallas_call(
        body,
        out_shape=(jax.ShapeDtypeStruct((s, 512), BF16), jax.ShapeDtypeStruct((s // CHUNK, HG_HEADS, HG_D, HG_D), F32)),
        grid=(s // ts,),
        in_specs=[col(0), col(1), col(2), col(3), pl.BlockSpec((1, 512), lambda i: (0, 0)), pl.BlockSpec((1, HG_D), lambda i: (0, 0))],
        out_specs=(pl.BlockSpec((ts, 512), lambda i: (i, 0)), pl.BlockSpec((nch, HG_HEADS, HG_D, HG_D), lambda i: (i, 0, 0, 0))),
        scratch_shapes=[pltpu.VMEM((HG_HEADS, HG_D, HG_D), F32)], name="hgrn_fwd",
        compiler_params=_params(("arbitrary",)),
    )(proj, proj, proj, proj, lb, nw)


def _hgrn_bwd(proj, dout, states, lb, nw, ts=256):
    s = proj.shape[0]
    ts = min(ts, s)
    nch = ts // CHUNK
    nt = s // ts

    def body(q_ref, f_ref, v_ref, g_ref, do_ref, st_ref, lb_ref, nw_ref, dp_ref, dlb_ref, dnw_ref, dst):
        @pl.when(pl.program_id(0) == 0)
        def _():
            dst[...] = jnp.zeros_like(dst)
            dlb_ref[...] = jnp.zeros_like(dlb_ref)
            dnw_ref[...] = jnp.zeros_like(dnw_ref)

        tri, triu = _tri(CHUNK), _tri(CHUNK, upper=True)
        last = lax.broadcasted_iota(jnp.int32, (CHUNK, HG_D), 0) == CHUNK - 1
        nwv = nw_ref[...]

        def chunk(cc, carry):
            c = nch - 1 - cc
            rows = pl.ds(pl.multiple_of(c * CHUNK, CHUNK), CHUNK)
            hq, hf, hv, hg = q_ref[rows, :], f_ref[rows, :], v_ref[rows, :], g_ref[rows, :]
            lbv = lb_ref[...]
            sg, fg, kk, lf, qf, sq = _hg_gates(hq, hf, lbv)
            b = _cum(tri, lf)
            dov = do_ref[rows, :]
            dqf_l, dkk_l, db_l, dv_l, dg_l = [], [], [], [], []
            for h in range(HG_HEADS):
                cs = slice(h * HG_D, (h + 1) * HG_D)
                stp = st_ref[c, h]
                bh, vh, gh = b[:, cs], hv[:, cs], hg[:, cs]
                o, parts, att, qh, kh, ebl = _hg_chunk_fwd(qf[:, cs], kk[:, cs], bh, vh, stp)
                sgg = _sigmoid(gh)
                on, r = _rms_fwd(o, nwv)
                d_on = dov[:, cs] * (gh * sgg)
                dg_l.append(dov[:, cs] * on * _dsilu(gh, sgg))
                do, dnw = _rms_bwd(o, r, nwv, d_on)
                dnw_ref[...] += dnw
                dob, vb = do.astype(BF16), vh.astype(BF16)
                dsth = dst[h]
                dstb = dsth.astype(BF16)
                dqh = _nn3(do, stp)
                dkh = _nn3(vh, dsth)
                dv = _nt(kh.astype(BF16), dstb)
                eb = jnp.exp(bh)
                ekl = jnp.exp(bh[CHUNK - 1:CHUNK, :] - bh)
                dqf, dkk = dqh * eb, dkh * ekl
                db = dqh * qh - dkh * kh
                dbl = jnp.sum(dkh * kh, axis=0, keepdims=True) + ebl * jnp.sum(dsth * stp, axis=0, keepdims=True)
                dst[h] = dsth * ebl + _tn(dob, qh.astype(BF16))
                qt, qbig, kbig, eq, ek, mine = parts
                da = jnp.where(_causal(CHUNK), _nt(dob, vb), 0.0)
                da_t = jnp.where(_causal(CHUNK, upper=True), _nt(vb, dob), 0.0)
                dv = dv + _tn(att.astype(BF16), dob)
                dqbig = _tn3(da_t, kbig)
                dkbig = _tn3(da, qbig)
                dkek, dkkb = dkbig * ek, dkbig * kbig
                dqt = jnp.zeros_like(qt)
                for i in range(NSUB):
                    bs = slice(i * HG_D, (i + 1) * HG_D)
                    dqt = dqt + jnp.where(mine[i], dqbig[:, bs], 0.0)
                    dkk = dkk + dkek[:, bs]
                    db = db - dkkb[:, bs]
                dqf = dqf + dqt * eq
                db = db + dqt * qt + jnp.where(last, dbl, 0.0)
                dqf_l.append(dqf); dkk_l.append(dkk); db_l.append(db); dv_l.append(dv)
            cat = lambda l: jnp.concatenate(l, axis=1)
            dlf = _cum(triu, cat(db_l))
            dfg = dlf / fg - cat(dkk_l)
            dlb_ref[...] += jnp.sum(dfg * (1.0 - sg), axis=0, keepdims=True)
            dp_ref[rows, 0:512] = (cat(dqf_l) * _dsilu(hq, sq)).astype(BF16)
            dp_ref[rows, 512:1024] = (dfg * (1.0 - lbv) * sg * (1.0 - sg)).astype(BF16)
            dp_ref[rows, 1024:1536] = cat(dv_l).astype(BF16)
            dp_ref[rows, 1536:2048] = cat(dg_l).astype(BF16)
            return carry

        lax.fori_loop(0, nch, chunk, 0, unroll=4)

    col = lambda n: pl.BlockSpec((ts, 512), functools.partial(lambda n, i: (nt - 1 - i, n), n))
    vec = lambda n: pl.BlockSpec((1, n), lambda i: (0, 0))
    return pl.pallas_call(
        body,
        out_shape=(jax.ShapeDtypeStruct((s, 2048), BF16), jax.ShapeDtypeStruct((1, 512), F32), jax.ShapeDtypeStruct((1, HG_D), F32)),
        grid=(nt,),
        in_specs=[col(0), col(1), col(2), col(3), pl.BlockSpec((ts, 512), lambda i: (nt - 1 - i, 0)),
                  pl.BlockSpec((nch, HG_HEADS, HG_D, HG_D), lambda i: (nt - 1 - i, 0, 0, 0)), vec(512), vec(HG_D)],
        out_specs=(pl.BlockSpec((ts, 2048), lambda i: (nt - 1 - i, 0)), vec(512), vec(HG_D)),
        scratch_shapes=[pltpu.VMEM((HG_HEADS, HG_D, HG_D), F32)], name="hgrn_bwd",
        compiler_params=_params(("arbitrary",)),
    )(proj, proj, proj, proj, dout, states, lb, nw)


CV_BLK = 2048 // 512


def _halo_before(ts, halo, colblk):
    return pl.BlockSpec((halo, 512), functools.partial(lambda cb, i: (jnp.maximum(i * (ts // halo) - 1, 0), cb), colblk))


def _cv_front(a_ref, g_ref, ah_ref, gh_ref, ext, first):
    a, sg = a_ref[...], _sigmoid(g_ref[...])
    zh = ah_ref[...] * _sigmoid(gh_ref[...])
    ext[0:CV_HALO, :] = jnp.where(first, 0.0, zh)
    ext[CV_HALO:, :] = a * sg
    return a, sg


CV_ROWS = 32


def _windows(ref, r0, base, ntaps, rows):
    out = []
    for phase in range(8):
        taps = [j for j in range(ntaps) if (base + j) % 8 == phase]
        if taps:
            span = max(base + j - phase for j in taps)
            big = ref[pl.ds(r0 + phase, rows + span), :]
            out += [(j, big[base + j - phase:base + j - phase + rows]) for j in taps]
    return out


def _cv_conv_ln(ext, w_ref, b_ref, r0):
    y = jnp.zeros((CV_ROWS, CV_W), F32) + b_ref[...]
    for j, win in _windows(ext, r0, CV_HALO - (CV_K - 1), CV_K, CV_ROWS):
        y = y + w_ref[j:j + 1, :] * win
    mu = jnp.mean(y, axis=-1, keepdims=True)
    yc = y - mu
    r = lax.rsqrt(jnp.mean(yc * yc, axis=-1, keepdims=True) + EPS)
    return yc * r, r


def _conv_fwd(proj, w, b, lnw, lnb, ts=256):
    s = proj.shape[0]
    ts = min(ts, s)

    def body(a_ref, g_ref, ah_ref, gh_ref, w_ref, b_ref, lnw_ref, lnb_ref, o_ref, ext):
        _cv_front(a_ref, g_ref, ah_ref, gh_ref, ext, pl.program_id(0) == 0)
        for r0 in range(0, ts, CV_ROWS):
            yh, _ = _cv_conv_ln(ext, w_ref, b_ref, r0)
            yn = yh * lnw_ref[...] + lnb_ref[...]
            o_ref[r0:r0 + CV_ROWS, :] = (yn * _sigmoid(yn)).astype(BF16)

    col = lambda n: pl.BlockSpec((ts, 512), functools.partial(lambda n, i: (i, n), n))
    vec = pl.BlockSpec((1, CV_W), lambda i: (0, 0))
    return pl.pallas_call(
        body, out_shape=jax.ShapeDtypeStruct((s, CV_W), BF16), grid=(s // ts,),
        in_specs=[col(CV_BLK), col(CV_BLK + 1), _halo_before(ts, CV_HALO, CV_BLK), _halo_before(ts, CV_HALO, CV_BLK + 1),
                  pl.BlockSpec((32, CV_W), lambda i: (0, 0)), vec, vec, vec],
        out_specs=pl.BlockSpec((ts, CV_W), lambda i: (i, 0)), scratch_shapes=[pltpu.VMEM((ts + CV_HALO, CV_W), F32)],
        name="conv_fwd", compiler_params=_params(("parallel",)),
    )(proj, proj, proj, proj, w, b, lnw, lnb)


def _conv_bwd(proj, dout, w, b, lnw, lnb, ts=256):
    s = proj.shape[0]
    ts = min(ts, s)
    nt = s // ts

    def body(a_ref, g_ref, ah_ref, gh_ref, do_ref, w_ref, b_ref, lnw_ref, lnb_ref,
             du_ref, dw_ref, db_ref, dlnw_ref, dlnb_ref, ext, dyext, carry, dwacc):
        i = pl.program_id(0)

        @pl.when(i == 0)
        def _():
            carry[...] = jnp.zeros_like(carry)
            dwacc[...] = jnp.zeros_like(dwacc)
            for ref in (db_ref, dlnw_ref, dlnb_ref):
                ref[...] = jnp.zeros_like(ref)

        _cv_front(a_ref, g_ref, ah_ref, gh_ref, ext, i == nt - 1)
        dyext[ts:, :] = carry[...]
        dlnw = dlnb = db = jnp.zeros((1, CV_W), F32)
        for r0 in range(0, ts, CV_ROWS):
            rows = slice(r0, r0 + CV_ROWS)
            yh, r = _cv_conv_ln(ext, w_ref, b_ref, r0)
            yn = yh * lnw_ref[...] + lnb_ref[...]
            dyn = do_ref[rows, :] * _dsilu(yn, _sigmoid(yn))
            dlnw += jnp.sum(dyn * yh, axis=0, keepdims=True)
            dlnb += jnp.sum(dyn, axis=0, keepdims=True)
            gl = dyn * lnw_ref[...]
            dy = r * (gl - jnp.mean(gl, axis=-1, keepdims=True) - yh * jnp.mean(gl * yh, axis=-1, keepdims=True))
            db += jnp.sum(dy, axis=0, keepdims=True)
            dyext[rows, :] = dy
            for j, win in _windows(ext, r0, CV_HALO - (CV_K - 1), CV_K, CV_ROWS):
                p = dy * win
                dwacc[8 * j:8 * j + 8, :] += (p[0:8] + p[8:16]) + (p[16:24] + p[24:32])
        dlnw_ref[...] += dlnw
        dlnb_ref[...] += dlnb
        db_ref[...] += db
        carry[...] = dyext[0:CV_HALO, :]
        for r0 in range(0, ts, CV_ROWS):
            rows = slice(r0, r0 + CV_ROWS)
            dz = jnp.zeros((CV_ROWS, CV_W), F32)
            for j, win in _windows(dyext, r0, 0, CV_K, CV_ROWS):
                dz = dz + w_ref[CV_K - 1 - j:CV_K - j, :] * win
            a, sg = a_ref[rows, :], _sigmoid(g_ref[rows, :])
            du_ref[rows, 0:CV_W] = (dz * sg).astype(BF16)
            du_ref[rows, CV_W:] = (dz * a * sg * (1.0 - sg)).astype(BF16)

        @pl.when(i == nt - 1)
        def _():
            for j in range(32):
                dw_ref[j:j + 1, :] = jnp.sum(dwacc[8 * j:8 * j + 8, :], axis=0, keepdims=True)

    rev = lambda n: pl.BlockSpec((ts, 512), functools.partial(lambda n, i: (nt - 1 - i, n), n))
    halo = lambda n: pl.BlockSpec((CV_HALO, 512), functools.partial(
        lambda n, i: (jnp.maximum((nt - 1 - i) * (ts // CV_HALO) - 1, 0), n), n))
    vec = pl.BlockSpec((1, CV_W), lambda i: (0, 0))
    wsp = pl.BlockSpec((32, CV_W), lambda i: (0, 0))
    v1 = jax.ShapeDtypeStruct((1, CV_W), F32)
    return pl.pallas_call(
        body, out_shape=(jax.ShapeDtypeStruct((s, 2 * CV_W), BF16), jax.ShapeDtypeStruct((32, CV_W), F32), v1, v1, v1),
        grid=(nt,),
        in_specs=[rev(CV_BLK), rev(CV_BLK + 1), halo(CV_BLK), halo(CV_BLK + 1), rev(0), wsp, vec, vec, vec],
        out_specs=(pl.BlockSpec((ts, 2 * CV_W), lambda i: (nt - 1 - i, 0)), wsp, vec, vec, vec),
        scratch_shapes=[pltpu.VMEM((ts + CV_HALO, CV_W), F32), pltpu.VMEM((ts + CV_HALO, CV_W), F32), pltpu.VMEM((CV_HALO, CV_W), F32),
                        pltpu.VMEM((8 * 32, CV_W), F32)],
        name="conv_bwd", compiler_params=_params(("arbitrary",)),
    )(proj, proj, proj, proj, dout, w, b, lnw, lnb)


PL_BLK = 3072 // 512


def _pool_windows(ext, t0, ts):
    n = ext.shape[0]
    t = t0 + lax.broadcasted_iota(jnp.int32, (ts, 1), 0)
    out = []
    for g, wdw in enumerate(POOL_WINDOWS):
        e = ext[:, g * 128:(g + 1) * 128]
        acc, k = e, 1
        while k < wdw:
            acc = acc + pltpu.roll(acc, k, 0)
            k *= 2
        cnt = jnp.minimum(t + 1, wdw).astype(F32)
        out.append(acc[POOL_HALO:] / cnt - e[POOL_HALO:])
    return out


def _pool_fwd(proj, w, sc, ts=256):
    s = proj.shape[0]
    ts = min(ts, s)

    def body(u_ref, uh_ref, w_ref, sc_ref, o_ref):
        i = pl.program_id(0)
        ext = jnp.concatenate([jnp.where(i == 0, 0.0, uh_ref[...]), u_ref[...]], axis=0)
        ps = _pool_windows(ext, i * ts, ts)
        y = jnp.concatenate([_nn(ps[g].astype(BF16), w_ref[g].astype(BF16)) for g in range(4)], axis=1)
        o_ref[...] = (y * sc_ref[...]).astype(BF16)

    return pl.pallas_call(
        body, out_shape=jax.ShapeDtypeStruct((s, 512), BF16), grid=(s // ts,),
        in_specs=[pl.BlockSpec((ts, 512), lambda i: (i, PL_BLK)), _halo_before(ts, POOL_HALO, PL_BLK),
                  pl.BlockSpec((4, 128, 128), lambda i: (0, 0, 0)), pl.BlockSpec((1, 512), lambda i: (0, 0))],
        out_specs=pl.BlockSpec((ts, 512), lambda i: (i, 0)), name="pool_fwd", compiler_params=_params(("parallel",)),
    )(proj, proj, w, sc)


def _pool_bwd(proj, dout, w, sc, ts=256):
    s = proj.shape[0]
    ts = min(ts, s)
    nt = s // ts
    n = ts + POOL_HALO

    def body(u_ref, uh_ref, do_ref, doh_ref, w_ref, sc_ref, du_ref, dw_ref, dsc_ref):
        i = pl.program_id(0)

        @pl.when(i == 0)
        def _():
            dw_ref[...] = jnp.zeros_like(dw_ref)
            dsc_ref[...] = jnp.zeros_like(dsc_ref)

        ext = jnp.concatenate([jnp.where(i == 0, 0.0, uh_ref[...]), u_ref[...]], axis=0)
        ps = _pool_windows(ext, i * ts, ts)
        dov = do_ref[...]
        dyext = jnp.concatenate([dov, jnp.where(i == nt - 1, 0.0, doh_ref[...])], axis=0) * sc_ref[...]
        t = i * ts + lax.broadcasted_iota(jnp.int32, (n, 1), 0)
        row = lax.broadcasted_iota(jnp.int32, (n, 1), 0)
        dus = []
        for g, wdw in enumerate(POOL_WINDOWS):
            cs = slice(g * 128, (g + 1) * 128)
            wg, pb = w_ref[g].astype(BF16), ps[g].astype(BF16)
            dsc_ref[:, cs] += jnp.sum(dov[:, cs] * _nn(pb, wg), axis=0, keepdims=True)
            dyg = dyext[:, cs].astype(BF16)
            dw_ref[g] += _tn(pb, dyg[0:ts])
            dp = _nt(dyg, wg)
            acc, k = dp / jnp.minimum(t + 1, wdw).astype(F32), 1
            while k < wdw:
                acc = acc + jnp.where(row < n - k, pltpu.roll(acc, n - k, 0), 0.0)
                k *= 2
            dus.append(acc[0:ts] - dp[0:ts])
        du_ref[...] = jnp.concatenate(dus, axis=1).astype(BF16)

    tile = lambda cb: pl.BlockSpec((ts, 512), functools.partial(lambda cb, i: (i, cb), cb))
    after = pl.BlockSpec((POOL_HALO, 512), lambda i: (jnp.minimum((i + 1) * (ts // POOL_HALO), s // POOL_HALO - 1), 0))
    wsp, vec = pl.BlockSpec((4, 128, 128), lambda i: (0, 0, 0)), pl.BlockSpec((1, 512), lambda i: (0, 0))
    return pl.pallas_call(
        body, out_shape=(jax.ShapeDtypeStruct((s, 512), BF16), jax.ShapeDtypeStruct((4, 128, 128), F32), jax.ShapeDtypeStruct((1, 512), F32)),
        grid=(nt,),
        in_specs=[tile(PL_BLK), _halo_before(ts, POOL_HALO, PL_BLK), tile(0), after, wsp, vec],
        out_specs=(tile(0), wsp, vec), name="pool_bwd", compiler_params=_params(("arbitrary",)),
    )(proj, proj, dout, dout, w, sc)


LX_BLK, LY_BLK = 3584 // 512, 4096 // 512
LRU_OFF = LRU_HALO - (LRU_CONV - 1)


def _scan_fwd(a, b):
    n = a.shape[0]
    row = lax.broadcasted_iota(jnp.int32, (n, 1), 0)
    k = 1
    while k < n:
        m = row >= k
        b = jnp.where(m, a * pltpu.roll(b, k, 0) + b, b)
        a = jnp.where(m, a * pltpu.roll(a, k, 0), a)
        k *= 2
    return a, b


def _scan_rev(a, b):
    n = a.shape[0]
    row = lax.broadcasted_iota(jnp.int32, (n, 1), 0)
    k = 1
    while k < n:
        m = row < n - k
        b = jnp.where(m, a * pltpu.roll(b, n - k, 0) + b, b)
        a = jnp.where(m, a * pltpu.roll(a, n - k, 0), a)
        k *= 2
    return b


def _lru_gates(x_ref, xh_ref, ext, first, cw_ref, cb_ref, wa_ref, ba_ref, wx_ref, bx_ref, lam_ref, ts):
    ext[0:LRU_HALO, :] = jnp.where(first, 0.0, xh_ref[...])
    ext[LRU_HALO:, :] = x_ref[...]
    xc = jnp.zeros((ts, LRU_W), F32) + cb_ref[...]
    for j in range(LRU_CONV):
        xc = xc + cw_ref[j:j + 1, :] * ext[pl.ds(LRU_OFF + j, ts), :]
    xb = xc.astype(BF16)
    r = _sigmoid(_nn(xb, wa_ref[...]) + ba_ref[...])
    ig = _sigmoid(_nn(xb, wx_ref[...]) + bx_ref[...])
    nl = -lam_ref[...]
    sp = jnp.maximum(nl, 0.0) + jnp.log(1.0 + jnp.exp(-jnp.abs(nl)))
    la = -LRU_C * r * sp
    a = jnp.exp(la)
    z = 2.0 * la
    em = jnp.where(z > -0.1, -z * (1.0 + z * 0.5 * (1.0 + z * (1.0 / 3) * (1.0 + z * 0.25 * (1.0 + z * 0.2)))), 1.0 - a * a)
    return xc, xb, r, ig, sp, a, jnp.sqrt(em)


def _lru_fwd(proj, cw, cb, wa, ba, wx, bx, lam, ts=256):
    s = proj.shape[0]
    ts = min(ts, s)

    def body(x_ref, xh_ref, y_ref, cw_ref, cb_ref, wa_ref, ba_ref, wx_ref, bx_ref, lam_ref, o_ref, h_ref, ext, hc):
        i = pl.program_id(0)

        @pl.when(i == 0)
        def _():
            hc[...] = jnp.zeros_like(hc)

        xc, _, _, ig, _, a, mult = _lru_gates(x_ref, xh_ref, ext, i == 0, cw_ref, cb_ref, wa_ref, ba_ref, wx_ref, bx_ref, lam_ref, ts)
        acum, h0 = _scan_fwd(a, mult * ig * xc)
        h = h0 + acum * hc[0:1, :]
        hc[...] = jnp.broadcast_to(h[ts - 1:ts, :], hc.shape)
        h_ref[...] = h
        o_ref[...] = (h * _gelu_parts(y_ref[...])[0]).astype(BF16)

    tile = lambda cb_: pl.BlockSpec((ts, 512), functools.partial(lambda c, i: (i, c), cb_))
    vec = pl.BlockSpec((1, LRU_W), lambda i: (0, 0))
    mat = pl.BlockSpec((LRU_W, LRU_W), lambda i: (0, 0))
    return pl.pallas_call(
        body, out_shape=(jax.ShapeDtypeStruct((s, LRU_W), BF16), jax.ShapeDtypeStruct((s, LRU_W), F32)), grid=(s // ts,),
        in_specs=[tile(LX_BLK), _halo_before(ts, LRU_HALO, LX_BLK), tile(LY_BLK), pl.BlockSpec((8, LRU_W), lambda i: (0, 0)),
                  vec, mat, vec, mat, vec, vec],
        out_specs=(tile(0), tile(0)), scratch_shapes=[pltpu.VMEM((ts + LRU_HALO, LRU_W), F32), pltpu.VMEM((8, LRU_W), F32)],
        name="lru_fwd", compiler_params=_params(("arbitrary",)),
    )(proj, proj, proj, cw, cb, wa, ba, wx, bx, lam)


def _lru_bwd(proj, hs, dout, cw, cb, wa, ba, wx, bx, lam, ts=256):
    s = proj.shape[0]
    ts = min(ts, s)
    nt = s // ts

    def body(x_ref, xh_ref, y_ref, h_ref, hh_ref, do_ref, cw_ref, cb_ref, wa_ref, ba_ref, wx_ref, bx_ref, lam_ref,
             dxy_ref, dcw_ref, dcb_ref, dwa_ref, dba_ref, dwx_ref, dbx_ref, dlam_ref, ext, dext, cg, cd):
        i = pl.program_id(0)
        first_tile = i == nt - 1

        @pl.when(i == 0)
        def _():
            cg[...] = jnp.zeros_like(cg)
            cd[...] = jnp.zeros_like(cd)
            for ref in (dcw_ref, dcb_ref, dwa_ref, dba_ref, dwx_ref, dbx_ref, dlam_ref):
                ref[...] = jnp.zeros_like(ref)

        xc, xb, r, ig, sp, a, mult = _lru_gates(x_ref, xh_ref, ext, first_tile, cw_ref, cb_ref, wa_ref, ba_ref, wx_ref, bx_ref, lam_ref, ts)
        row = lax.broadcasted_iota(jnp.int32, (ts, 1), 0)
        h, dov = h_ref[...], do_ref[...]
        gel, dgel = _gelu_parts(y_ref[...])
        dxy_ref[:, LRU_W:] = (dov * h * dgel).astype(BF16)
        alpha = jnp.where(row < ts - 1, pltpu.roll(a, ts - 1, 0), 0.0)
        g = _scan_rev(alpha, dov * gel + jnp.where(row == ts - 1, cg[0:1, :], 0.0))
        cg[...] = jnp.broadcast_to(a[0:1, :] * g[0:1, :], cg.shape)
        hprev = jnp.where(row == 0, jnp.where(first_tile, 0.0, hh_ref[LRU_HALO - 1:LRU_HALO, :]), pltpu.roll(h, 1, 0))
        dla = g * hprev * a - g * ig * xc * (a * a) / mult
        dpr = dla * (-LRU_C * sp) * r * (1.0 - r)
        dpi = g * mult * xc * ig * (1.0 - ig)
        dprb, dpib = dpr.astype(BF16), dpi.astype(BF16)
        dxc = g * mult * ig + _nt(dprb, wa_ref[...]) + _nt(dpib, wx_ref[...])
        dlam_ref[...] += jnp.sum(dla * (-LRU_C * r), axis=0, keepdims=True) * (-_sigmoid(-lam_ref[...]))
        dwa_ref[...] += _tn(xb, dprb)
        dwx_ref[...] += _tn(xb, dpib)
        dba_ref[...] += jnp.sum(dpr, axis=0, keepdims=True)
        dbx_ref[...] += jnp.sum(dpi, axis=0, keepdims=True)
        dcb_ref[...] += jnp.sum(dxc, axis=0, keepdims=True)
        dext[0:ts, :] = dxc
        dext[ts:, :] = cd[...]
        cd[...] = dxc[0:LRU_HALO, :]
        dx = jnp.zeros((ts, LRU_W), F32)
        for j in range(LRU_CONV):
            dcw_ref[j:j + 1, :] += jnp.sum(dxc * ext[pl.ds(LRU_OFF + j, ts), :], axis=0, keepdims=True)
            dx = dx + cw_ref[j:j + 1, :] * dext[pl.ds(LRU_CONV - 1 - j, ts), :]
        dxy_ref[:, 0:LRU_W] = dx.astype(BF16)

    rev = lambda c: pl.BlockSpec((ts, 512), functools.partial(lambda c, i: (nt - 1 - i, c), c))
    halo = lambda c: pl.BlockSpec((LRU_HALO, 512), functools.partial(
        lambda c, i: (jnp.maximum((nt - 1 - i) * (ts // LRU_HALO) - 1, 0), c), c))
    vec = pl.BlockSpec((1, LRU_W), lambda i: (0, 0))
    mat = pl.BlockSpec((LRU_W, LRU_W), lambda i: (0, 0))
    cws = pl.BlockSpec((8, LRU_W), lambda i: (0, 0))
    v1, m1 = jax.ShapeDtypeStruct((1, LRU_W), F32), jax.ShapeDtypeStruct((LRU_W, LRU_W), F32)
    return pl.pallas_call(
        body, out_shape=(jax.ShapeDtypeStruct((s, 2 * LRU_W), BF16), jax.ShapeDtypeStruct((8, LRU_W), F32), v1, m1, v1, m1, v1, v1),
        grid=(nt,),
        in_specs=[rev(LX_BLK), halo(LX_BLK), rev(LY_BLK), rev(0), halo(0), rev(0), cws, vec, mat, vec, mat, vec, vec],
        out_specs=(pl.BlockSpec((ts, 2 * LRU_W), lambda i: (nt - 1 - i, 0)), cws, vec, mat, vec, mat, vec, vec),
        scratch_shapes=[pltpu.VMEM((ts + LRU_HALO, LRU_W), F32), pltpu.VMEM((ts + LRU_HALO, LRU_W), F32),
                        pltpu.VMEM((8, LRU_W), F32), pltpu.VMEM((LRU_HALO, LRU_W), F32)],
        name="lru_bwd", compiler_params=_params(("arbitrary",)),
    )(proj, proj, proj, hs, hs, dout, cw, cb, wa, ba, wx, bx, lam)


def _final_loss(x, fw, tgt, ts=512):
    s, d = x.shape
    ts = min(ts, s)

    def body(x_ref, w_ref, t_ref, loss_ref, dx_ref, dw_ref):
        @pl.when(pl.program_id(0) == 0)
        def _():
            loss_ref[...] = jnp.zeros_like(loss_ref)
            dw_ref[...] = jnp.zeros_like(dw_ref)

        xv = x_ref[...]
        y, r = _rms_fwd(xv, w_ref[...])
        err = y - t_ref[...]
        loss_ref[...] += 0.5 * jnp.sum(jnp.mean(err * err, axis=-1, keepdims=True), axis=0, keepdims=True)
        dx, dw = _rms_bwd(xv, r, w_ref[...], err * (1.0 / d))
        dx_ref[...] = dx
        dw_ref[...] += dw

    row = pl.BlockSpec((ts, d), lambda i: (i, 0))
    vec = pl.BlockSpec((1, d), lambda i: (0, 0))
    return pl.pallas_call(
        body, out_shape=(jax.ShapeDtypeStruct((8, 128), F32), jax.ShapeDtypeStruct((s, d), F32), jax.ShapeDtypeStruct((1, d), F32)),
        grid=(s // ts,), in_specs=[row, vec, row], out_specs=(pl.BlockSpec((8, 128), lambda i: (0, 0)), row, vec),
        name="final_loss", compiler_params=_params(("arbitrary",)),
    )(x, fw, tgt)


def _lb_softmax(raw_ref):
    raw = raw_ref[...]
    e = jnp.exp(raw - jnp.max(raw, axis=0, keepdims=True))
    return e / jnp.sum(e, axis=0, keepdims=True)


def _lb_fwd(raw):
    def body(raw_ref, o_ref):
        sm = _lb_softmax(raw_ref)
        acc = jnp.zeros((1, sm.shape[1]), F32)
        o_ref[0:1, :] = acc
        for l in range(1, DEPTH):
            acc = acc + sm[l:l + 1, :]
            o_ref[l:l + 1, :] = acc

    return pl.pallas_call(body, out_shape=jax.ShapeDtypeStruct(raw.shape, F32), name="lb_fwd")(raw)


def _lb_bwd(raw, dlb):
    def body(raw_ref, d_ref, o_ref):
        sm = _lb_softmax(raw_ref)
        dlbv = d_ref[...]
        dsm, acc = [None] * DEPTH, jnp.zeros((1, sm.shape[1]), F32)
        for l in range(DEPTH - 1, 0, -1):
            acc = acc + dlbv[l:l + 1, :]
            dsm[l] = acc
        dsm[0] = jnp.zeros_like(acc)
        dsm = jnp.concatenate(dsm, axis=0)
        o_ref[...] = sm * (dsm - jnp.sum(sm * dsm, axis=0, keepdims=True))

    return pl.pallas_call(body, out_shape=jax.ShapeDtypeStruct(raw.shape, F32), name="lb_bwd")(raw, dlb)


def _adam_math(w, g, m, v):
    m = ADAM_B1 * m + (1.0 - ADAM_B1) * g
    v = ADAM_B2 * v + (1.0 - ADAM_B2) * (g * g)
    m_hat = m / (1.0 - ADAM_B1 ** ADAM_STEP)
    v_hat = v / (1.0 - ADAM_B2 ** ADAM_STEP)
    return -ADAM_LR * (m_hat / (jnp.sqrt(v_hat) + ADAM_EPS) + ADAM_WD * w), m, v


def _adamw(w, gs, m, v, name, tr=128):
    r, c = w.shape
    tr = min(tr, r)
    ng = len(gs)

    def body(*refs):
        w_ref, g_refs, m_ref, v_ref = refs[0], refs[1:1 + ng], refs[1 + ng], refs[2 + ng]
        outs = refs[3 + ng:]
        g = g_refs[0][...]
        if ng == 2:
            g = g + g_refs[1][...]
            outs[0][...] = g
            outs = outs[1:]
        for o, val in zip(outs, _adam_math(w_ref[...], g, m_ref[...], v_ref[...])):
            o[...] = val

    blk = pl.BlockSpec((tr, c), lambda i: (i, 0))
    sd = jax.ShapeDtypeStruct((r, c), F32)
    nout = 3 + (ng == 2)
    return pl.pallas_call(
        body, out_shape=(sd,) * nout, grid=(r // tr,), in_specs=[blk] * (3 + ng), out_specs=(blk,) * nout, name=name,
        compiler_params=_params(("parallel",)),
    )(w, *gs, m, v)


def _cast_into_full(w, kind, jj, name, tr=256):
    l, r, c = w.shape
    tr = min(tr, r)

    def body(j_ref, w_ref, o_ref):
        o_ref[...] = w_ref[...].astype(BF16)

    if kind == "col":
        full, dst = (l, r, 4 * c), pl.BlockSpec((None, tr, c), lambda a, b, j: (a, b, j[0]))
    else:
        full, dst = (l, 4 * r, c), pl.BlockSpec((None, tr, c), lambda a, b, j: (a, j[0] * (r // tr) + b, 0))
    return pl.pallas_call(
        body, out_shape=jax.ShapeDtypeStruct(full, BF16),
        grid_spec=pltpu.PrefetchScalarGridSpec(
            num_scalar_prefetch=1, grid=(l, r // tr), in_specs=[pl.BlockSpec((None, tr, c), lambda a, b, j: (a, b, 0))], out_specs=dst),
        name=name, compiler_params=_params(("parallel", "parallel")),
    )(jj, w)


def _place():
    return lax.axis_index("x"), lax.axis_index("y"), lax.axis_index("c")


def _other_chips(x, y):
    return [(1 - x, y), (x, 1 - y), (1 - x, 1 - y)]


def _slab(ref, kind, jj):
    if kind == "col":
        c = ref.shape[2] // 4
        return ref.at[:, :, pl.ds(jj * c, c)]
    r = ref.shape[1] // 4
    return ref.at[:, pl.ds(jj * r, r), :]


HBM = pl.BlockSpec(memory_space=pltpu.HBM)
SEM = pl.BlockSpec(memory_space=pltpu.SEMAPHORE)
EFFECT = pltpu.SideEffectType.DATAFLOW_SIDE_EFFECTING


def _in_hbm(a):
    return pltpu.with_memory_space_constraint(a, pltpu.HBM)


def _thru(arrs):
    return [pltpu.HBM(a.shape, a.dtype) for a in arrs]


def _half_slab(ref, kind, group, jj, half):
    per = ref.shape[0] // DEPTH
    layers = pl.ds(group * per, per)
    if kind == "col":
        r, c = ref.shape[1] // 2, ref.shape[2] // 4
        return ref.at[layers, pl.ds(half * r, r), pl.ds(jj * c, c)]
    r = ref.shape[1] // 8
    return ref.at[layers, pl.ds((2 * jj + half) * r, r), :]


def _gather_copy(fulls, kinds, send_sems, recv_sems, group, t, k, landing):
    x, y, c = _place()
    chip = _other_chips(x, y)[k]
    idx = (group * len(fulls) + t) * 3 + k
    return pltpu.make_async_remote_copy(
        src_ref=_half_slab(fulls[t], kinds[t], group, 2 * x + y, c), dst_ref=_half_slab(fulls[t], kinds[t], group, landing, c),
        send_sem=send_sems.at[idx], recv_sem=recv_sems.at[idx], device_id=(chip[0], chip[1], c), device_id_type=MESH)


def _fill_copy(fulls, kinds, send_sems, recv_sems, group, which, t, k, half):
    x, y, c = _place()
    chip = _other_chips(x, y)[k]
    idx = which.index(t) * 3 + k
    jj = 2 * chip[0] + chip[1]
    return pltpu.make_async_remote_copy(
        src_ref=_half_slab(fulls[t], kinds[t], group, jj, c), dst_ref=_half_slab(fulls[t], kinds[t], group, jj, half),
        send_sem=send_sems.at[idx], recv_sem=recv_sems.at[idx], device_id=(x, y, 1 - c), device_id_type=MESH)


def _fill_start(group, which, fulls, kinds):
    nt = len(fulls)
    ncp = 3 * len(which)

    def body(*refs):
        ins, send_sems, recv_sems = refs[:nt], refs[nt], refs[nt + 1]
        _, _, c = _place()
        for t in which:
            for k in range(3):
                _fill_copy(ins, kinds, send_sems, recv_sems, group, which, t, k, c).start()

    out = pl.pallas_call(
        body, out_shape=(pltpu.SemaphoreType.DMA((ncp,)), pltpu.SemaphoreType.DMA((ncp,)), *_thru(fulls)),
        in_specs=[HBM] * nt, out_specs=(SEM, SEM, *([HBM] * nt)), input_output_aliases={t: 2 + t for t in range(nt)},
        name="fill_start_%d_%d" % (group, which[0]), compiler_params=pltpu.CompilerParams(has_side_effects=EFFECT),
    )(*fulls)
    return out[0], out[1], list(out[2:])


def _fill_wait(group, which, send_sems, recv_sems, fulls, kinds, after):
    nt = len(fulls)

    def body(*refs):
        ins, send_ref, recv_ref = refs[:nt], refs[nt], refs[nt + 1]
        _, _, c = _place()
        for t in which:
            for k in range(3):
                cp = _fill_copy(ins, kinds, send_ref, recv_ref, group, which, t, k, 1 - c)
                cp.wait_send()
                cp.wait_recv()

    out = pl.pallas_call(
        body, out_shape=tuple(_thru(fulls)), in_specs=[HBM] * nt + [SEM, SEM, ANY], out_specs=tuple([HBM] * nt),
        input_output_aliases={t: t for t in range(nt)}, name="fill_wait_%d_%d" % (group, which[0]),
        compiler_params=pltpu.CompilerParams(has_side_effects=EFFECT),
    )(*fulls, send_sems, recv_sems, after)
    return list(out)


def _gather_start(fulls, kinds, after, name, groups=range(DEPTH), skip=()):
    nt = len(fulls)
    ncp = DEPTH * nt * 3

    def body(*refs):
        ins, send_sems, recv_sems = refs[:nt], refs[nt + 1], refs[nt + 2]
        x, y, _ = _place()
        for group in groups:
            for t in range(nt):
                for k in range(3):
                    if (group, t) not in skip:
                        _gather_copy(ins, kinds, send_sems, recv_sems, group, t, k, 2 * x + y).start()

    out = pl.pallas_call(
        body, out_shape=(pltpu.SemaphoreType.DMA((ncp,)), pltpu.SemaphoreType.DMA((ncp,)), *_thru(fulls)),
        in_specs=[HBM] * nt + [ANY], out_specs=(SEM, SEM, *([HBM] * nt)), input_output_aliases={t: 2 + t for t in range(nt)},
        name=name, compiler_params=pltpu.CompilerParams(has_side_effects=EFFECT),
    )(*[_in_hbm(a) for a in fulls], after)
    return out[0], out[1], list(out[2:])


def _gather_wait(group, which, send_sems, recv_sems, fulls, kinds, after):
    nt = len(fulls)

    def body(*refs):
        ins, send_ref, recv_ref = refs[:nt], refs[nt], refs[nt + 1]
        x, y, _ = _place()
        chips = _other_chips(x, y)
        for t in which:
            for k in range(3):
                cp = _gather_copy(ins, kinds, send_ref, recv_ref, group, t, k, 2 * chips[k][0] + chips[k][1])
                cp.wait_send()
                cp.wait_recv()

    out = pl.pallas_call(
        body, out_shape=tuple(_thru(fulls)), in_specs=[HBM] * nt + [SEM, SEM, ANY], out_specs=tuple([HBM] * nt),
        input_output_aliases={t: t for t in range(nt)}, name="gather_wait_%d_%d" % (group, which[0]),
        compiler_params=pltpu.CompilerParams(has_side_effects=EFFECT),
    )(*fulls, send_sems, recv_sems, after)
    return list(out)


def _scatter_copy(grads, lands, kinds, send_sems, recv_sems, t, k):
    x, y, c = _place()
    chip = _other_chips(x, y)[k]
    return pltpu.make_async_remote_copy(
        src_ref=_slab(grads[t], kinds[t], 2 * chip[0] + chip[1]), dst_ref=lands[t].at[k], send_sem=send_sems.at[3 * t + k],
        recv_sem=recv_sems.at[3 * t + k], device_id=(chip[0], chip[1], c), device_id_type=MESH)


def _scatter_start(grads, kinds, after, name):
    nt = len(grads)
    lands = []
    for g, kd in zip(grads, kinds):
        l, r, c = g.shape
        lands.append(lax.empty((3, l, r, c // 4) if kd == "col" else (3, l, r // 4, c), g.dtype))

    def body(*refs):
        ins, lnd, send_sems, recv_sems = refs[:nt], refs[nt:2 * nt], refs[2 * nt + 1], refs[2 * nt + 2]
        for t in range(nt):
            for k in range(3):
                _scatter_copy(ins, lnd, kinds, send_sems, recv_sems, t, k).start()
        refs[-1][...] = jnp.zeros_like(refs[-1])

    out = pl.pallas_call(
        body, out_shape=(pltpu.SemaphoreType.DMA((3 * nt,)), pltpu.SemaphoreType.DMA((3 * nt,)), *_thru(grads), *_thru(lands),
                         jax.ShapeDtypeStruct((8, 128), F32)),
        in_specs=[HBM] * (2 * nt) + [ANY], out_specs=(SEM, SEM, *([HBM] * (2 * nt)), pl.BlockSpec(memory_space=pltpu.VMEM)),
        input_output_aliases={t: 2 + t for t in range(2 * nt)}, name=name,
        compiler_params=pltpu.CompilerParams(has_side_effects=EFFECT),
    )(*[_in_hbm(a) for a in grads], *[_in_hbm(a) for a in lands], after)
    return (out[0], out[1], list(out[2:2 + nt]), list(out[2 + nt:2 + 2 * nt])), out[-1]


def _scatter_wait(send_sems, recv_sems, grads, lands, kinds, after, name):
    nt = len(grads)

    def body(*refs):
        ins, lnd, send_ref, recv_ref = refs[:nt], refs[nt:2 * nt], refs[2 * nt], refs[2 * nt + 1]
        for t in range(nt):
            for k in range(3):
                cp = _scatter_copy(ins, lnd, kinds, send_ref, recv_ref, t, k)
                cp.wait_send()
                cp.wait_recv()

    out = pl.pallas_call(
        body, out_shape=(*_thru(grads), *_thru(lands)), in_specs=[HBM] * (2 * nt) + [SEM, SEM, ANY],
        out_specs=tuple([HBM] * (2 * nt)), input_output_aliases={t: t for t in range(2 * nt)}, name=name,
        compiler_params=pltpu.CompilerParams(has_side_effects=EFFECT),
    )(*grads, *lands, send_sems, recv_sems, after)
    return list(out[:nt]), list(out[nt:])


def _sibling_swap(arrs, name="sibling_swap"):
    nt = len(arrs)

    def body(*refs):
        ins, outs = refs[:nt], refs[nt:2 * nt]
        send_sems, recv_sems = refs[2 * nt:]
        x, y, c = _place()
        sends = [pltpu.make_async_remote_copy(src_ref=ins[t], dst_ref=outs[t], send_sem=send_sems.at[t], recv_sem=recv_sems.at[t],
                                              device_id=(x, y, 1 - c), device_id_type=MESH) for t in range(nt)]
        for cp in sends:
            cp.start()
        for cp in sends:
            cp.wait_recv()
        for cp in sends:
            cp.wait_send()

    return pl.pallas_call(
        body, out_shape=[jax.ShapeDtypeStruct(a.shape, a.dtype) for a in arrs], in_specs=[ANY] * nt, out_specs=[ANY] * nt,
        scratch_shapes=[pltpu.SemaphoreType.DMA((nt,)), pltpu.SemaphoreType.DMA((nt,))], name=name,
    )(*arrs)


def _gather_small(vec, over_c):
    n = vec.shape[0]
    flips = [(dx, dy, dc) for dx in (0, 1) for dy in (0, 1) for dc in ((0, 1) if over_c else (0,))][1:]
    np_ = len(flips)

    def body(v_ref, o_ref, send_sems, recv_sems, local_sem):
        x, y, c = _place()

        def idx(px, py, pc):
            return 4 * px + 2 * py + pc if over_c else 2 * px + py

        def peer(f):
            return (1 - x if f[0] else x, 1 - y if f[1] else y, 1 - c if f[2] else c)

        def push(k, landing):
            return pltpu.make_async_remote_copy(src_ref=v_ref, dst_ref=o_ref.at[landing], send_sem=send_sems.at[k],
                                                recv_sem=recv_sems.at[k], device_id=peer(flips[k]), device_id_type=MESH)

        mine = pltpu.make_async_copy(v_ref, o_ref.at[idx(x, y, c)], local_sem)
        sends = [push(k, idx(x, y, c)) for k in range(np_)]
        for cp in [mine] + sends:
            cp.start()
        for k in range(np_):
            push(k, idx(*peer(flips[k]))).wait_recv()
        for cp in sends:
            cp.wait_send()
        mine.wait()

    return pl.pallas_call(
        body, out_shape=jax.ShapeDtypeStruct((np_ + 1, n, 128), F32), in_specs=[ANY], out_specs=ANY,
        scratch_shapes=[pltpu.SemaphoreType.DMA((np_,)), pltpu.SemaphoreType.DMA((np_,)), pltpu.SemaphoreType.DMA(())],
        name="gather_small_all" if over_c else "gather_small_xy",
    )(vec)


def _sum_rows(buf, after, tr=512):
    p, n, _ = buf.shape
    tr = min(tr, n)

    def body(b_ref, after_ref, o_ref):
        acc = b_ref[0]
        for k in range(1, p):
            acc = acc + b_ref[k]
        o_ref[...] = acc

    return pl.pallas_call(
        body, out_shape=jax.ShapeDtypeStruct((n, 128), F32), grid=(n // tr,),
        in_specs=[pl.BlockSpec((p, tr, 128), lambda i: (0, i, 0)), ANY], out_specs=pl.BlockSpec((tr, 128), lambda i: (i, 0)),
        name="sum_rows", compiler_params=_params(("parallel",)),
    )(buf, after)


def _sum_partials_into(stack, at, depth, grad, recv, kind, jj, name, tr=128):
    _, _, r, c = recv.shape
    tr = min(tr, r)

    def body(j_ref, g_ref, r0, r1, r2, *rest):
        rest[-1][...] = ((g_ref[...].astype(F32) + r0[...].astype(F32)) + r1[...].astype(F32)) + r2[...].astype(F32)

    if kind == "col":
        own = pl.BlockSpec((None, tr, c), lambda b, j: (0, b, j[0]))
    else:
        own = pl.BlockSpec((None, tr, c), lambda b, j: (0, j[0] * (r // tr) + b, 0))
    got = lambda k: pl.BlockSpec((None, None, tr, c), functools.partial(lambda k, b, j: (k, 0, b, 0), k))
    chained = stack is not None
    return pl.pallas_call(
        body, out_shape=jax.ShapeDtypeStruct((depth, r, c), F32),
        grid_spec=pltpu.PrefetchScalarGridSpec(
            num_scalar_prefetch=1, grid=(r // tr,), in_specs=[own, got(0), got(1), got(2)] + ([ANY] if chained else []),
            out_specs=pl.BlockSpec((None, tr, c), lambda b, j: (at, b, 0))),
        input_output_aliases={5: 0} if chained else {}, name=name, compiler_params=_params(("parallel",)),
    )(*([jj, grad, recv, recv, recv] + ([stack] if chained else [])))


WEIGHTS = ['norm_mix_w', 'w_in', 'hg_lb_raw', 'hg_norm_w', 'cv_dw_w', 'cv_dw_b', 'cv_ln_w', 'cv_ln_b', 'pl_w', 'pl_scale',
           'lru_conv_w', 'lru_conv_b', 'lru_wa', 'lru_ba', 'lru_wx', 'lru_bx', 'lru_lambda', 'gate_b', 'w_branch', 'w_out',
           'norm_mem_w', 'mem_norm_w', 'xa_wq', 'xa_wkv', 'xa_wo', 'norm_ffn_w', 'ffn_w1', 'ffn_w2', 'final_norm_w']
BIG = {'w_in': 'col', 'w_branch': 'col', 'w_out': 'row', 'xa_wq': 'row', 'xa_wkv': 'col', 'xa_wo': 'row', 'ffn_w1': 'col', 'ffn_w2': 'row'}
SMALL_SPLIT = ('gate_b', 'cv_dw_w', 'lru_conv_w')
SMALL = [n for n in WEIGHTS if n not in BIG]
PIECE_GROUPS = [['ffn_w1', 'ffn_w2', 'xa_wq', 'xa_wkv', 'xa_wo'],
                ['w_out', ('w_branch', 0), ('w_branch', 1), ('w_branch', 2), ('w_branch', 3), 'w_in']]


def _piece_kinds(pieces):
    return [BIG[k[0] if isinstance(k, tuple) else k] for k in pieces]
ROWS_PAD = 512


def _as3d(a):
    return a.reshape((-1,) + a.shape[-2:])


def _pack(parts):
    flat = jnp.concatenate([p.reshape(-1).astype(F32) for p in parts])
    n = -(-flat.shape[0] // (128 * ROWS_PAD)) * ROWS_PAD
    return jnp.pad(flat, (0, n * 128 - flat.shape[0])).reshape(n, 128)


def _unpack(packed, shapes):
    flat, out, o = packed.reshape(-1), [], 0
    for sh in shapes:
        sz = math.prod(sh)
        out.append(flat[o:o + sz].reshape(sh))
        o += sz
    return out


def _block_diag(w):
    h, a, b = w.shape
    eye = jnp.eye(h, dtype=w.dtype)
    return (w[:, :, None, :] * eye[:, None, :, None]).reshape(h * a, h * b)


def _diag_blocks(m, h):
    a, b = m.shape[0] // h, m.shape[1] // h
    return jnp.stack([m[i * a:(i + 1) * a, i * b:(i + 1) * b] for i in range(h)])


def kernel(x, mem, norm_mix_w, w_in, hg_lb_raw, hg_norm_w, cv_dw_w, cv_dw_b, cv_ln_w, cv_ln_b, pl_w, pl_scale, lru_conv_w, lru_conv_b, lru_wa, lru_ba, lru_wx, lru_bx, lru_lambda, gate_b, w_branch, w_out, norm_mem_w, mem_norm_w, xa_wq, xa_wkv, xa_wo, norm_ffn_w, ffn_w1, ffn_w2, final_norm_w, loss_target, m_norm_mix_w, m_w_in, m_hg_lb_raw, m_hg_norm_w, m_cv_dw_w, m_cv_dw_b, m_cv_ln_w, m_cv_ln_b, m_pl_w, m_pl_scale, m_lru_conv_w, m_lru_conv_b, m_lru_wa, m_lru_ba, m_lru_wx, m_lru_bx, m_lru_lambda, m_gate_b, m_w_branch, m_w_out, m_norm_mem_w, m_mem_norm_w, m_xa_wq, m_xa_wkv, m_xa_wo, m_norm_ffn_w, m_ffn_w1, m_ffn_w2, m_final_norm_w, v_norm_mix_w, v_w_in, v_hg_lb_raw, v_hg_norm_w, v_cv_dw_w, v_cv_dw_b, v_cv_ln_w, v_cv_ln_b, v_pl_w, v_pl_scale, v_lru_conv_w, v_lru_conv_b, v_lru_wa, v_lru_ba, v_lru_wx, v_lru_bx, v_lru_lambda, v_gate_b, v_w_branch, v_w_out, v_norm_mem_w, v_mem_norm_w, v_xa_wq, v_xa_wkv, v_xa_wo, v_norm_ffn_w, v_ffn_w1, v_ffn_w2, v_final_norm_w):
    w = dict(zip(WEIGHTS, (norm_mix_w, w_in, hg_lb_raw, hg_norm_w, cv_dw_w, cv_dw_b, cv_ln_w, cv_ln_b, pl_w, pl_scale, lru_conv_w, lru_conv_b, lru_wa, lru_ba, lru_wx, lru_bx, lru_lambda, gate_b, w_branch, w_out, norm_mem_w, mem_norm_w, xa_wq, xa_wkv, xa_wo, norm_ffn_w, ffn_w1, ffn_w2, final_norm_w)))
    m1 = dict(zip(WEIGHTS, (m_norm_mix_w, m_w_in, m_hg_lb_raw, m_hg_norm_w, m_cv_dw_w, m_cv_dw_b, m_cv_ln_w, m_cv_ln_b, m_pl_w, m_pl_scale, m_lru_conv_w, m_lru_conv_b, m_lru_wa, m_lru_ba, m_lru_wx, m_lru_bx, m_lru_lambda, m_gate_b, m_w_branch, m_w_out, m_norm_mem_w, m_mem_norm_w, m_xa_wq, m_xa_wkv, m_xa_wo, m_norm_ffn_w, m_ffn_w1, m_ffn_w2, m_final_norm_w)))
    v1 = dict(zip(WEIGHTS, (v_norm_mix_w, v_w_in, v_hg_lb_raw, v_hg_norm_w, v_cv_dw_w, v_cv_dw_b, v_cv_ln_w, v_cv_ln_b, v_pl_w, v_pl_scale, v_lru_conv_w, v_lru_conv_b, v_lru_wa, v_lru_ba, v_lru_wx, v_lru_bx, v_lru_lambda, v_gate_b, v_w_branch, v_w_out, v_norm_mem_w, v_mem_norm_w, v_xa_wq, v_xa_wkv, v_xa_wo, v_norm_ffn_w, v_ffn_w1, v_ffn_w2, v_final_norm_w)))
    seq = x.shape[1]
    xs, mems, tgt = x.reshape(seq, D_MODEL), mem.reshape(-1, D_MODEL), loss_target.reshape(seq, D_MODEL)
    jj = 2 * lax.axis_index("x") + lax.axis_index("y")
    jj1 = jnp.reshape(jj, (1,)).astype(jnp.int32)

    split_shapes = [w[n].shape for n in SMALL_SPLIT]
    got = _gather_small(_pack([w[n] for n in SMALL_SPLIT]), over_c=False)
    per_chip = [_unpack(got[k], split_shapes) for k in range(4)]
    full_small = {n: jnp.concatenate([per_chip[k][i] for k in range(4)], axis=-1) for i, n in enumerate(SMALL_SPLIT)}
    big_names = list(BIG)
    kinds = [BIG[n] for n in big_names]
    first_send, first_recv, first = _gather_start([_cast_into_full(_as3d(w['w_in']), BIG['w_in'], jj1, "cast_w_in")], kinds[:1], got,
                                                  "gather_start_first", groups=(0,))
    casts = first + [_cast_into_full(_as3d(w[n]), BIG[n], jj1, "cast_" + n) for n in big_names[1:]]
    g_send, g_recv, fulls = _gather_start(casts, kinds, got, "gather_start", skip=((0, 0),))
    tix = {n: t for t, n in enumerate(big_names)}
    assert tix['w_in'] == 0

    lb = _lb_fwd(hg_lb_raw)
    row = lambda a: a.reshape(1, -1)

    def layer_params(l):
        return dict(
            nmix=row(norm_mix_w[l]), lb=row(lb[l]), hgnw=row(hg_norm_w[l]),
            cw=jnp.pad(full_small['cv_dw_w'][l], ((0, 32 - CV_K), (0, 0))), cb=row(cv_dw_b[l]), lnw=row(cv_ln_w[l]), lnb=row(cv_ln_b[l]),
            plw=pl_w[l], plsc=row(pl_scale[l]),
            lcw=jnp.pad(full_small['lru_conv_w'][l], ((0, 8 - LRU_CONV), (0, 0))), lcb=row(lru_conv_b[l]),
            wa=_block_diag(lru_wa[l]).astype(BF16), ba=row(lru_ba[l]), wx=_block_diag(lru_wx[l]).astype(BF16), bx=row(lru_bx[l]),
            lam=row(lru_lambda[l]), gb=full_small['gate_b'][l], nmem=row(norm_mem_w[l]), memw=row(mem_norm_w[l]), nffn=row(norm_ffn_w[l]))

    saved = []
    xc = xs
    def by_name(arrays):
        wf_ = dict(zip(big_names, arrays))
        wf_['w_branch'] = wf_['w_branch'].reshape(DEPTH, 4, 512, D_MODEL)
        return wf_

    def landed(l, names, arrays, after):
        which = [tix[n] for n in names]
        if l == 0 and which == [0]:
            arrays = _gather_wait(0, [0], first_send, first_recv, arrays[:1], kinds[:1], after) + arrays[1:]
        else:
            arrays = _gather_wait(l, which, g_send, g_recv, arrays, kinds, after)
        f_send, f_recv, arrays = _fill_start(l, which, arrays, kinds)
        return (which, f_send, f_recv), arrays

    def complete(l, pending, arrays, after):
        which, f_send, f_recv = pending
        return _fill_wait(l, which, f_send, f_recv, arrays, kinds, after)

    rest = [n for n in big_names if n != 'w_in']
    for l in range(DEPTH):
        if l == 0:
            pending, fulls = landed(0, ['w_in'], fulls, xc)
        fulls = complete(l, pending, fulls, xc)
        wf = by_name(fulls)
        p = layer_params(l)
        h = _norm_fwd(xc, p['nmix'], "norm_mix")
        proj = _mm(h, wf['w_in'], "nn", F32, "proj", tm=seq, layer=l)
        if l == 0:
            pending, fulls = landed(0, rest, fulls, proj)
        b_hg, st = _hgrn_fwd(proj, p['lb'], p['hgnw'])
        b_cv = _conv_fwd(proj, p['cw'], p['cb'], p['lnw'], p['lnb'])
        b_pl = _pool_fwd(proj, p['plw'], p['plsc'])
        b_lru, hs = _lru_fwd(proj, p['lcw'], p['lcb'], p['wa'], p['ba'], p['wx'], p['bx'], p['lam'])
        branches = (b_hg, b_cv, b_pl, b_lru)
        if l == 0:
            fulls = complete(0, pending, fulls, b_lru)
            wf = by_name(fulls)
        x1 = _merge_fwd(xc, branches, proj, p['gb'], wf['w_branch'], wf['w_out'], l)
        if l + 1 < DEPTH:
            pending, fulls = landed(l + 1, big_names, fulls, x1)
            wf = by_name(fulls)
        memn = _norm_fwd(mems, p['memw'], "norm_memtok")
        kv = _mm(memn, wf['xa_wkv'], "nn", BF16, "kv_proj", layer=l)
        x2 = _attn_fwd(x1, p['nmem'], wf['xa_wq'], kv, wf['xa_wo'], l)
        x3 = _ffn_fwd(x2, p['nffn'], wf['ffn_w1'], wf['ffn_w2'], l, ts=1024)
        saved.append(dict(p=p, x=xc, h=h, proj=proj, st=st, hs=hs, branches=branches, x1=x1, memn=memn, kv=kv, x2=x2))
        xc = x3

    loss_blk, dx, dfinal = _final_loss(xc, row(final_norm_w), tgt)

    gs = {n: [None] * DEPTH for n in SMALL if n != 'final_norm_w'}
    dlb = [None] * DEPTH
    in_flight = [[None, None] for _ in range(DEPTH)]

    def scatter(grads, grp, after, name):
        pieces = PIECE_GROUPS[grp]
        return _scatter_start([grads[key][None] for key in pieces], _piece_kinds(pieces), after, name)

    token = loss_blk
    for l in reversed(range(DEPTH)):
        sv = saved[l]
        p = sv['p']
        gb = {}
        dx2, gs['norm_ffn_w'][l], h3, da, r, dxb = _ffn_bwd(sv['x2'], dx, p['nffn'], wf['ffn_w1'], wf['ffn_w2'], l, token, ts=512)
        gb['ffn_w1'] = _mm(h3, da, "tn", BF16, "dw_ffn1", tm=1024)
        gb['ffn_w2'] = _mm(r, dxb, "tn", BF16, "dw_ffn2", tn=1024)
        dx1, gs['norm_mem_w'][l], h2, o, dq, dxb2, dk, dv = _attn_bwd(sv['x1'], dx2, p['nmem'], wf['xa_wq'], sv['kv'], wf['xa_wo'], l)
        gb['xa_wq'] = _mm(h2, dq, "tn", BF16, "dw_q")
        gb['xa_wo'] = _mm(o, dxb2, "tn", BF16, "dw_o")
        dkv = jnp.concatenate([dk, dv], axis=1)
        gb['xa_wkv'] = _mm(sv['memn'], dkv, "tn", BF16, "dw_kv")
        dmemn = _mm(dkv, wf['xa_wkv'], "nt", F32, "dmemn", layer=l)
        _, gs['mem_norm_w'][l] = _norm_bwd(mems, p['memw'], dmemn, None, "norm_memtok_bwd")
        in_flight[l][0], token = scatter(gb, 0, dx1, "scatter_start_%d_0" % l)
        db0, db1, db2, db3, dgp, dup, mg, dxb1, gs['gate_b'][l] = _merge_bwd(
            dx1, sv['branches'], sv['proj'], p['gb'], wf['w_branch'], wf['w_out'], l, token)
        gb['w_out'] = _mm(mg, dxb1, "tn", BF16, "dw_out")
        for kb in range(4):
            gb['w_branch', kb] = _mm(sv['branches'][kb], dup, "tn", BF16, "dw_branch", b_col0=kb * D_MODEL, n=D_MODEL, tn=512)
        dhg, dlb[l], gs['hg_norm_w'][l] = _hgrn_bwd(sv['proj'], db0, sv['st'], p['lb'], p['hgnw'])
        dcv, dcw, gs['cv_dw_b'][l], gs['cv_ln_w'][l], gs['cv_ln_b'][l] = _conv_bwd(sv['proj'], db1, p['cw'], p['cb'], p['lnw'], p['lnb'])
        gs['cv_dw_w'][l] = dcw[:CV_K]
        dpl, gs['pl_w'][l], gs['pl_scale'][l] = _pool_bwd(sv['proj'], db2, p['plw'], p['plsc'])
        dlru, dlcw, gs['lru_conv_b'][l], dwa, gs['lru_ba'][l], dwx, gs['lru_bx'][l], gs['lru_lambda'][l] = _lru_bwd(
            sv['proj'], sv['hs'], db3, p['lcw'], p['lcb'], p['wa'], p['ba'], p['wx'], p['bx'], p['lam'])
        gs['lru_conv_w'][l] = dlcw[:LRU_CONV]
        gs['lru_wa'][l], gs['lru_wx'][l] = _diag_blocks(dwa, LRU_HEADS), _diag_blocks(dwx, LRU_HEADS)
        dproj = [dhg, dcv, dpl, dlru, dgp]
        gb['w_in'] = _mm_tn_pieces(sv['h'], dproj, "dw_in")
        dh = _mm_nt_pieces(dproj, wf['w_in'], l, "dh_mix")
        dx, gs['norm_mix_w'][l] = _norm_bwd(sv['x'], p['nmix'], dh, dx1, "norm_mix_bwd")
        if l:
            in_flight[l][1], token = scatter(gb, 1, dx, "scatter_start_%d_1" % l)
    grad_x = dx.reshape(x.shape)
    gs['hg_lb_raw'] = _lb_bwd(hg_lb_raw, jnp.concatenate(dlb, axis=0))

    def full_shape(n):
        return full_small[n].shape if n in SMALL_SPLIT else w[n].shape

    small_full = []
    for n in SMALL:
        g = gs[n] if n == 'hg_lb_raw' else dfinal if n == 'final_norm_w' else jnp.stack(gs[n])
        small_full.append(g.reshape(full_shape(n)))
    mine = _pack(small_full + [loss_blk[0:1, 0:1]])
    chip_sum = _sum_rows(jnp.stack([mine, _sibling_swap([mine], "sibling_swap_small")[0]]), mine)
    everyone = _gather_small(chip_sum, over_c=False)
    in_flight[0][1], token = scatter(gb, 1, everyone, "scatter_start_0_1")
    total = _sum_rows(everyone, token)
    parts = _unpack(total, [full_shape(n) for n in SMALL] + [(1,)])
    loss = parts[-1].reshape(())
    g_small = {}
    for n, g in zip(SMALL, parts[:-1]):
        if n in SMALL_SPLIT:
            width = w[n].shape[-1]
            g = lax.dynamic_slice_in_dim(g, jj * width, width, axis=g.ndim - 1)
        g_small[n] = g
    shapes = [w[n].shape for n in SMALL]
    upd = _adamw(_pack([w[n] for n in SMALL]), [_pack([g_small[n] for n in SMALL])], _pack([m1[n] for n in SMALL]),
                 _pack([v1[n] for n in SMALL]), "adamw_small")
    d_small, m_small, v_small = [dict(zip(SMALL, _unpack(u, shapes))) for u in upd]

    stacks = {n: None for n in big_names}
    done_before = upd[0]
    for l in reversed(range(DEPTH)):
        for grp, pieces in enumerate(PIECE_GROUPS):
            s_send, s_recv, g_thru, lands = in_flight[l][grp]
            g_thru, lands = _scatter_wait(s_send, s_recv, g_thru, lands, _piece_kinds(pieces), done_before,
                                          "scatter_wait_%d_%d" % (l, grp))
            for key, g, r in zip(pieces, g_thru, lands):
                n, kb = key if isinstance(key, tuple) else (key, None)
                per = 1 if kb is None else 4
                stacks[n] = _sum_partials_into(stacks[n], l * per + (kb or 0), DEPTH * per, g, r, BIG[n], jj1, "sum_" + n)
                done_before = stacks[n]
    partial = [stacks[n] for n in big_names]
    theirs = _sibling_swap(partial)
    g_big, d_big, m_big, v_big = {}, {}, {}, {}
    for n, pa, pb in zip(big_names, partial, theirs):
        c2 = lambda a: a.reshape(-1, a.shape[-1])
        out = _adamw(c2(w[n]), [c2(pa), c2(pb)], c2(m1[n]), c2(v1[n]), "adamw_" + n)
        g_big[n], d_big[n], m_big[n], v_big[n] = [o.reshape(w[n].shape) for o in out]

    pick = lambda small, big: [big[n] if n in BIG else small[n] for n in WEIGHTS]
    return (loss, grad_x, *pick(g_small, g_big), *pick(d_small, d_big), *pick(m_small, m_big), *pick(v_small, v_big))
```

```python
import functools
import math

import jax
import jax.numpy as jnp
from jax import lax
from jax.experimental import pallas as pl
from jax.experimental.pallas import tpu as pltpu

F32 = jnp.float32
BF16 = jnp.bfloat16
MESH = pl.DeviceIdType.MESH
ANY = pl.BlockSpec(memory_space=pl.ANY)

D_MODEL = 1024
DEPTH = 4
CHUNK = 64
SUB = 16
EPS = 1e-6
HG_HEADS, HG_D = 4, 128
CV_W, CV_K = 512, 31
CV_HALO = 32
POOL_WINDOWS = (2, 4, 8, 16)
POOL_HALO = 16
LRU_W, LRU_HEADS, LRU_HD, LRU_CONV = 512, 8, 64, 4
LRU_HALO = 8
LRU_C = 8.0
MIX_W = 4608
IN_W = 8704
XA_HEADS, XA_HD = 4, 256
D_FF = 4096
FF_CHUNK = 1024
ADAM_LR, ADAM_B1, ADAM_B2, ADAM_EPS, ADAM_WD, ADAM_STEP = 0.001, 0.9, 0.999, 1e-08, 0.01, 10
VMEM_LIMIT = 56 * 1024 * 1024
EXP_CLAMP = 80.0
HI = lax.Precision.HIGHEST


def _params(sem=None):
    return pltpu.CompilerParams(dimension_semantics=sem, vmem_limit_bytes=VMEM_LIMIT)


def _sigmoid(x):
    return 1.0 / (1.0 + jnp.exp(-x))


def _dsilu(x, s):
    return s * (1.0 + x * (1.0 - s))


_GELU_C = math.sqrt(2.0 / math.pi)


def _gelu_parts(x):
    t = jnp.tanh(_GELU_C * (x + 0.044715 * x * x * x))
    g = 0.5 * x * (1.0 + t)
    dg = 0.5 * (1.0 + t) + 0.5 * x * (1.0 - t * t) * _GELU_C * (1.0 + 3 * 0.044715 * x * x)
    return g, dg


def _dot(a, b, dims, precision=None):
    return lax.dot_general(a, b, (dims, ((), ())), precision=precision, preferred_element_type=F32)


def _nn(a, b, **k):
    return _dot(a, b, ((1,), (0,)), **k)


def _nt(a, b, **k):
    return _dot(a, b, ((1,), (1,)), **k)


def _tn(a, b, **k):
    return _dot(a, b, ((0,), (0,)), **k)


def _split(x):
    hi = x.astype(BF16)
    return hi, (x - hi.astype(F32)).astype(BF16)


def _nn3(a, b):
    (ah, al), (bh, bl) = _split(a), _split(b)
    return _nn(jnp.concatenate([ah, ah, al], axis=1), jnp.concatenate([bh, bl, bh], axis=0))


def _tn3(a, b):
    (ah, al), (bh, bl) = _split(a), _split(b)
    return _tn(jnp.concatenate([ah, ah, al], axis=0), jnp.concatenate([bh, bl, bh], axis=0))


def _rms_fwd(x, w):
    r = lax.rsqrt(jnp.mean(x * x, axis=-1, keepdims=True) + EPS)
    return x * r * w, r


def _rms_bwd(x, r, w, dy):
    xr = x * r
    g = dy * w
    dx = r * (g - xr * jnp.mean(g * xr, axis=-1, keepdims=True))
    return dx, jnp.sum(dy * xr, axis=0, keepdims=True)


def _lw(shape, index, layer):
    return pl.BlockSpec((None,) + tuple(shape), lambda *g: (layer,) + tuple(index(*g)))


def _mm(a, b, mode, out_dtype, name, tm=512, tn=512, b_col0=0, n=None, layer=None):
    bs = b.shape if layer is None else b.shape[1:]
    if mode == "nn":
        m, k = a.shape
        n = bs[1] if n is None else n
    elif mode == "nt":
        m, k = a.shape
        n = bs[0] if n is None else n
    else:
        k, m = a.shape
        n = bs[1] if n is None else n
    tm, tn = min(tm, m), min(tn, n)
    assert m % tm == 0 and n % tn == 0 and b_col0 % tn == 0
    off = b_col0 // tn

    def body(a_ref, b_ref, o_ref):
        av, bv = a_ref[...].astype(BF16), b_ref[...].astype(BF16)
        o_ref[...] = (_nn if mode == "nn" else _nt if mode == "nt" else _tn)(av, bv).astype(out_dtype)

    def bspec(shape, index):
        return pl.BlockSpec(shape, index) if layer is None else _lw(shape, index, layer)

    if mode == "tn":
        grid = (m // tm, n // tn)
        a_spec = pl.BlockSpec((k, tm), lambda i, j: (0, i))
        b_spec = bspec((k, tn), lambda i, j: (0, j + off))
        o_spec = pl.BlockSpec((tm, tn), lambda i, j: (i, j))
    else:
        grid = (n // tn, m // tm)
        a_spec = pl.BlockSpec((tm, k), lambda j, i: (i, 0))
        if mode == "nn":
            b_spec = bspec((k, tn), lambda j, i: (0, j + off))
        else:
            b_spec = bspec((tn, k), lambda j, i: (j + off, 0))
        o_spec = pl.BlockSpec((tm, tn), lambda j, i: (i, j))
    return pl.pallas_call(
        body, out_shape=jax.ShapeDtypeStruct((m, n), out_dtype), grid=grid,
        in_specs=[a_spec, b_spec], out_specs=o_spec, name=name,
        compiler_params=_params(("parallel", "parallel")),
    )(a, b)


def _mm_tn_pieces(a, pieces, name, tn=512):
    k, m = a.shape
    starts, o = [], 0
    for pc in pieces:
        assert pc.shape[1] % tn == 0
        starts.append(o // tn)
        o += pc.shape[1]
    counts = [pc.shape[1] // tn for pc in pieces]
    npc = len(pieces)

    def body(a_ref, *refs):
        o_ref = refs[npc]
        j = pl.program_id(0)
        for pi in range(npc):
            @pl.when(jnp.logical_and(j >= starts[pi], j < starts[pi] + counts[pi]))
            def _(pi=pi):
                o_ref[...] = _tn(a_ref[...], refs[pi][...]).astype(o_ref.dtype)

    def piece_spec(pi):
        return pl.BlockSpec((k, tn), lambda j: (0, jnp.clip(j - starts[pi], 0, counts[pi] - 1)))

    return pl.pallas_call(
        body, out_shape=jax.ShapeDtypeStruct((m, o), BF16), grid=(o // tn,),
        in_specs=[pl.BlockSpec((k, m), lambda j: (0, 0), pipeline_mode=pl.Buffered(1))] + [piece_spec(pi) for pi in range(npc)],
        out_specs=pl.BlockSpec((m, tn), lambda j: (0, j)), name=name, compiler_params=_params(("parallel",)),
    )(a, *pieces)


def _mm_nt_pieces(pieces, b, layer, name, tm=256, tn=512):
    m = pieces[0].shape[0]
    n, k = b.shape[1:]
    offs, o = [], 0
    for pc in pieces:
        offs.append(o)
        o += pc.shape[1]
    assert o == k
    npc = len(pieces)

    def body(*refs):
        b_ref, o_ref = refs[npc], refs[npc + 1]
        acc = jnp.zeros((tm, tn), F32)
        for pi in range(npc):
            acc += _nt(refs[pi][...], b_ref[:, offs[pi]:offs[pi] + pieces[pi].shape[1]])
        o_ref[...] = acc

    return pl.pallas_call(
        body, out_shape=jax.ShapeDtypeStruct((m, n), F32), grid=(n // tn, m // tm),
        in_specs=[pl.BlockSpec((tm, pc.shape[1]), lambda j, i: (i, 0)) for pc in pieces] + [_lw((tn, k), lambda j, i: (j, 0), layer)],
        out_specs=pl.BlockSpec((tm, tn), lambda j, i: (i, j)), name=name, compiler_params=_params(("parallel", "parallel")),
    )(*pieces, b)


def _norm_fwd(x, w, name, ts=512):
    s, d = x.shape
    ts = min(ts, s)

    def body(x_ref, w_ref, o_ref):
        o_ref[...] = _rms_fwd(x_ref[...], w_ref[...])[0].astype(BF16)

    return pl.pallas_call(
        body, out_shape=jax.ShapeDtypeStruct((s, d), BF16), grid=(s // ts,),
        in_specs=[pl.BlockSpec((ts, d), lambda i: (i, 0)), pl.BlockSpec((1, d), lambda i: (0, 0))],
        out_specs=pl.BlockSpec((ts, d), lambda i: (i, 0)), name=name, compiler_params=_params(("parallel",)),
    )(x, w)


def _norm_bwd(x, w, dy, dres, name, ts=512):
    s, d = x.shape
    ts = min(ts, s)
    with_res = dres is not None

    def body(*refs):
        if with_res:
            x_ref, w_ref, dy_ref, dres_ref, dx_ref, dw_ref = refs
        else:
            x_ref, w_ref, dy_ref, dx_ref, dw_ref = refs
        xv = x_ref[...]
        r = lax.rsqrt(jnp.mean(xv * xv, axis=-1, keepdims=True) + EPS)
        dx, dw = _rms_bwd(xv, r, w_ref[...], dy_ref[...])
        dx_ref[...] = dx + dres_ref[...] if with_res else dx

        @pl.when(pl.program_id(0) == 0)
        def _():
            dw_ref[...] = jnp.zeros_like(dw_ref)

        dw_ref[...] += dw

    row = pl.BlockSpec((ts, d), lambda i: (i, 0))
    vec = pl.BlockSpec((1, d), lambda i: (0, 0))
    return pl.pallas_call(
        body, out_shape=(jax.ShapeDtypeStruct((s, d), F32), jax.ShapeDtypeStruct((1, d), F32)), grid=(s // ts,),
        in_specs=[row, vec, row] + ([row] if with_res else []), out_specs=(row, vec), name=name,
        compiler_params=_params(("arbitrary",)),
    )(*([x, w, dy] + ([dres] if with_res else [])))


def _ffn_fwd(x, nw, w1, w2, layer, ts=256):
    s, d = x.shape
    ts = min(ts, s)
    nj = D_FF // FF_CHUNK

    def body(x_ref, nw_ref, w1_ref, w2_ref, o_ref, h_scr, acc):
        j = pl.program_id(1)

        @pl.when(j == 0)
        def _():
            h_scr[...] = _rms_fwd(x_ref[...], nw_ref[...])[0].astype(BF16)
            acc[...] = jnp.zeros_like(acc)

        a = _nn(h_scr[...], w1_ref[...])
        rl = jnp.maximum(a, 0.0)
        acc[...] += _nn((rl * rl).astype(BF16), w2_ref[...])

        @pl.when(j == nj - 1)
        def _():
            o_ref[...] = x_ref[...] + acc[...]

    row = pl.BlockSpec((ts, d), lambda i, j: (i, 0))
    return pl.pallas_call(
        body, out_shape=jax.ShapeDtypeStruct((s, d), F32), grid=(s // ts, nj),
        in_specs=[row, pl.BlockSpec((1, d), lambda i, j: (0, 0)),
                  _lw((d, FF_CHUNK), lambda i, j: (0, j), layer), _lw((FF_CHUNK, d), lambda i, j: (j, 0), layer)],
        out_specs=row, scratch_shapes=[pltpu.VMEM((ts, d), BF16), pltpu.VMEM((ts, d), F32)], name="ffn_fwd",
        compiler_params=_params(("parallel", "arbitrary")),
    )(x, nw, w1, w2)


def _ffn_bwd(x, dxo, nw, w1, w2, layer, after, ts=256):
    s, d = x.shape
    ts = min(ts, s)
    nj = D_FF // FF_CHUNK

    def body(x_ref, dxo_ref, nw_ref, w1_ref, w2_ref, after_ref, dx_ref, dnw_ref, h_ref, da_ref, r_ref, dxb_ref, dh):
        i, j = pl.program_id(0), pl.program_id(1)

        @pl.when(j == 0)
        def _():
            h_ref[...] = _rms_fwd(x_ref[...], nw_ref[...])[0].astype(BF16)
            dxb_ref[...] = dxo_ref[...].astype(BF16)
            dh[...] = jnp.zeros_like(dh)

        a = _nn(h_ref[...], w1_ref[...])
        rl = jnp.maximum(a, 0.0)
        r_ref[...] = (rl * rl).astype(BF16)
        da = (_nt(dxb_ref[...], w2_ref[...]) * (2.0 * rl)).astype(BF16)
        da_ref[...] = da
        dh[...] += _nt(da, w1_ref[...])

        @pl.when(jnp.logical_and(i == 0, j == 0))
        def _():
            dnw_ref[...] = jnp.zeros_like(dnw_ref)

        @pl.when(j == nj - 1)
        def _():
            xv = x_ref[...]
            r = lax.rsqrt(jnp.mean(xv * xv, axis=-1, keepdims=True) + EPS)
            dx, dw = _rms_bwd(xv, r, nw_ref[...], dh[...])
            dx_ref[...] = dxo_ref[...] + dx
            dnw_ref[...] += dw

    row = pl.BlockSpec((ts, d), lambda i, j: (i, 0))
    vec = pl.BlockSpec((1, d), lambda i, j: (0, 0))
    ffc = pl.BlockSpec((ts, FF_CHUNK), lambda i, j: (i, j))
    return pl.pallas_call(
        body,
        out_shape=(jax.ShapeDtypeStruct((s, d), F32), jax.ShapeDtypeStruct((1, d), F32), jax.ShapeDtypeStruct((s, d), BF16),
                   jax.ShapeDtypeStruct((s, D_FF), BF16), jax.ShapeDtypeStruct((s, D_FF), BF16), jax.ShapeDtypeStruct((s, d), BF16)),
        grid=(s // ts, nj),
        in_specs=[row, row, vec, _lw((d, FF_CHUNK), lambda i, j: (0, j), layer), _lw((FF_CHUNK, d), lambda i, j: (j, 0), layer), ANY],
        out_specs=(row, vec, row, ffc, ffc, row), scratch_shapes=[pltpu.VMEM((ts, d), F32)], name="ffn_bwd",
        compiler_params=_params(("arbitrary", "arbitrary")),
    )(x, dxo, nw, w1, w2, after)


def _attn_probs(q, k_ref):
    ps = []
    for hd in range(XA_HEADS):
        c = slice(hd * XA_HD, (hd + 1) * XA_HD)
        sc = _nt(q[:, c].astype(BF16), k_ref[:, c]) * (XA_HD ** -0.5)
        e = jnp.exp(sc - jnp.max(sc, axis=-1, keepdims=True))
        ps.append(e / jnp.sum(e, axis=-1, keepdims=True))
    return ps


def _attn_fwd(x, nw, wq, kv, wo, layer, ts=256):
    s, d = x.shape
    ts = min(ts, s)
    nm = kv.shape[0]

    def body(x_ref, nw_ref, wq_ref, k_ref, v_ref, wo_ref, o_ref):
        xv = x_ref[...]
        h = _rms_fwd(xv, nw_ref[...])[0].astype(BF16)
        q = _nn(h, wq_ref[...])
        ps = _attn_probs(q, k_ref)
        o = jnp.concatenate([_nn(ps[hd].astype(BF16), v_ref[:, hd * XA_HD:(hd + 1) * XA_HD]) for hd in range(XA_HEADS)], axis=1)
        o_ref[...] = xv + _nn(o.astype(BF16), wo_ref[...])

    row = pl.BlockSpec((ts, d), lambda i: (i, 0))
    full = lambda r, c: pl.BlockSpec((r, c), lambda i: (0, 0))
    wsp = _lw((d, d), lambda i: (0, 0), layer)
    return pl.pallas_call(
        body, out_shape=jax.ShapeDtypeStruct((s, d), F32), grid=(s // ts,),
        in_specs=[row, full(1, d), wsp, full(nm, d), pl.BlockSpec((nm, d), lambda i: (0, 1)), wsp], out_specs=row, name="attn_fwd",
        compiler_params=_params(("parallel",)),
    )(x, nw, wq, kv, kv, wo)


def _attn_bwd(x, dxo, nw, wq, kv, wo, layer, ts=256):
    s, d = x.shape
    ts = min(ts, s)
    nm = kv.shape[0]

    def body(x_ref, dxo_ref, nw_ref, wq_ref, k_ref, v_ref, wo_ref,
             dx_ref, dnw_ref, h_ref, o_ref, dq_ref, dxb_ref, dk_ref, dv_ref):
        xv = x_ref[...]
        hf, r = _rms_fwd(xv, nw_ref[...])
        h = hf.astype(BF16)
        h_ref[...] = h
        q = _nn(h, wq_ref[...])
        qb = q.astype(BF16)
        ps = _attn_probs(q, k_ref)
        dxb = dxo_ref[...].astype(BF16)
        dxb_ref[...] = dxb
        do = _nt(dxb, wo_ref[...])

        @pl.when(pl.program_id(0) == 0)
        def _():
            dnw_ref[...] = jnp.zeros_like(dnw_ref)
            dk_ref[...] = jnp.zeros_like(dk_ref)
            dv_ref[...] = jnp.zeros_like(dv_ref)

        dqs = []
        for hd in range(XA_HEADS):
            c = slice(hd * XA_HD, (hd + 1) * XA_HD)
            p = ps[hd]
            pb = p.astype(BF16)
            dob = do[:, c].astype(BF16)
            o_ref[:, c] = _nn(pb, v_ref[:, c]).astype(BF16)
            dp = _nt(dob, v_ref[:, c])
            ds = (p * (dp - jnp.sum(p * dp, axis=-1, keepdims=True)) * (XA_HD ** -0.5)).astype(BF16)
            dqs.append(_nn(ds, k_ref[:, c]))
            dk_ref[:, c] += _tn(ds, qb[:, c])
            dv_ref[:, c] += _tn(pb, dob)
        dq = jnp.concatenate(dqs, axis=1).astype(BF16)
        dq_ref[...] = dq
        dx, dw = _rms_bwd(xv, r, nw_ref[...], _nt(dq, wq_ref[...]))
        dx_ref[...] = dxo_ref[...] + dx
        dnw_ref[...] += dw

    row = pl.BlockSpec((ts, d), lambda i: (i, 0))
    full = lambda r, c: pl.BlockSpec((r, c), lambda i: (0, 0))
    sd = lambda dt: jax.ShapeDtypeStruct((s, d), dt)
    return pl.pallas_call(
        body,
        out_shape=(sd(F32), jax.ShapeDtypeStruct((1, d), F32), sd(BF16), sd(BF16), sd(BF16), sd(BF16),
                   jax.ShapeDtypeStruct((nm, d), F32), jax.ShapeDtypeStruct((nm, d), F32)),
        grid=(s // ts,),
        in_specs=[row, row, full(1, d), _lw((d, d), lambda i: (0, 0), layer), full(nm, d), pl.BlockSpec((nm, d), lambda i: (0, 1)),
                  _lw((d, d), lambda i: (0, 0), layer)],
        out_specs=(row, full(1, d), row, row, row, row, full(nm, d), full(nm, d)), name="attn_bwd",
        compiler_params=_params(("arbitrary",)),
    )(x, dxo, nw, wq, kv, kv, wo)


GATE_BLK0 = MIX_W // 512


def _merge_specs(ts, layer):
    row = pl.BlockSpec((ts, D_MODEL), lambda i: (i, 0))
    br = pl.BlockSpec((ts, 512), lambda i: (i, 0))
    gates = [pl.BlockSpec((ts, 512), functools.partial(lambda n, i: (i, GATE_BLK0 + n), n)) for n in range(8)]
    full = lambda *shape: pl.BlockSpec(shape, lambda i: (0,) * len(shape))
    weights = [full(4, D_MODEL), _lw((4, 512, D_MODEL), lambda i: (0, 0, 0), layer), _lw((D_MODEL, D_MODEL), lambda i: (0, 0), layer)]
    return row, br, gates, full, weights


def _merge_gates(gp_refs, gb_ref, kb):
    gp = jnp.concatenate([gp_refs[2 * kb][...], gp_refs[2 * kb + 1][...]], axis=1)
    return _sigmoid(gp + gb_ref[kb:kb + 1, :])


def _merge_fwd(x, branches, proj, gate_b, wb, wout, layer, ts=256):
    s, d = x.shape
    ts = min(ts, s)

    def body(x_ref, b0, b1, b2, b3, g0, g1, g2, g3, g4, g5, g6, g7, gb_ref, wb_ref, wo_ref, o_ref):
        brs, gps = (b0, b1, b2, b3), (g0, g1, g2, g3, g4, g5, g6, g7)
        merged = jnp.zeros((ts, d), F32)
        for kb in range(4):
            merged += _merge_gates(gps, gb_ref, kb) * _nn(brs[kb][...], wb_ref[kb])
        o_ref[...] = x_ref[...] + _nn(merged.astype(BF16), wo_ref[...])

    row, br, gates, full, weights = _merge_specs(ts, layer)
    return pl.pallas_call(
        body, out_shape=jax.ShapeDtypeStruct((s, d), F32), grid=(s // ts,),
        in_specs=[row, br, br, br, br] + gates + weights, out_specs=row, name="merge_fwd",
        compiler_params=_params(("parallel",)),
    )(x, *branches, *([proj] * 8), gate_b, wb, wout)


def _merge_bwd(dxo, branches, proj, gate_b, wb, wout, layer, after, ts=256):
    s, d = dxo.shape
    ts = min(ts, s)

    def body(dxo_ref, b0, b1, b2, b3, g0, g1, g2, g3, g4, g5, g6, g7, gb_ref, wb_ref, wo_ref, after_ref,
             db0, db1, db2, db3, dgp_ref, dup_ref, mg_ref, dxb_ref, dgb_ref):
        brs, gps, dbs = (b0, b1, b2, b3), (g0, g1, g2, g3, g4, g5, g6, g7), (db0, db1, db2, db3)
        dxb = dxo_ref[...].astype(BF16)
        dxb_ref[...] = dxb
        dm = _nt(dxb, wo_ref[...])

        @pl.when(pl.program_id(0) == 0)
        def _():
            dgb_ref[...] = jnp.zeros_like(dgb_ref)

        merged = jnp.zeros((ts, d), F32)
        for kb in range(4):
            c = slice(kb * d, (kb + 1) * d)
            g = _merge_gates(gps, gb_ref, kb)
            up = _nn(brs[kb][...], wb_ref[kb])
            merged += g * up
            dup = (dm * g).astype(BF16)
            dup_ref[:, c] = dup
            dgp = dm * up * g * (1.0 - g)
            dgp_ref[:, c] = dgp.astype(BF16)
            dgb_ref[kb:kb + 1, :] += jnp.sum(dgp, axis=0, keepdims=True)
            dbs[kb][...] = _nt(dup, wb_ref[kb])
        mg_ref[...] = merged.astype(BF16)

    row, br, gates, full, weights = _merge_specs(ts, layer)
    wide = pl.BlockSpec((ts, 4 * d), lambda i: (i, 0))
    sb = jax.ShapeDtypeStruct((s, 512), F32)
    return pl.pallas_call(
        body,
        out_shape=(sb, sb, sb, sb, jax.ShapeDtypeStruct((s, 4 * d), BF16), jax.ShapeDtypeStruct((s, 4 * d), BF16),
                   jax.ShapeDtypeStruct((s, d), BF16), jax.ShapeDtypeStruct((s, d), BF16), jax.ShapeDtypeStruct((4, d), F32)),
        grid=(s // ts,),
        in_specs=[row, br, br, br, br] + gates + weights + [ANY],
        out_specs=(br, br, br, br, wide, wide, row, row, full(4, d)), name="merge_bwd",
        compiler_params=_params(("arbitrary",)),
    )(dxo, *branches, *([proj] * 8), gate_b, wb, wout, after)


def _tri(n, upper=False):
    r = lax.broadcasted_iota(jnp.int32, (n, 3 * n), 0)
    c = lax.broadcasted_iota(jnp.int32, (n, 3 * n), 1) % n
    return jnp.where((c >= r) if upper else (c <= r), 1.0, 0.0).astype(BF16)


def _cum(tri3, x):
    hi = x.astype(BF16)
    r1 = x - hi.astype(F32)
    mid = r1.astype(BF16)
    lo = (r1 - mid.astype(F32)).astype(BF16)
    return _nn(tri3, jnp.concatenate([hi, mid, lo], axis=0))


def _hg_gates(hq, hf, lb):
    sg = _sigmoid(hf)
    fg = lb + (1.0 - lb) * sg
    sq = _sigmoid(hq)
    return sg, fg, 1.0 - fg, jnp.log(fg), hq * sq, sq


NSUB = CHUNK // SUB


def _hg_intra(qf, kk, b):
    row = lax.broadcasted_iota(jnp.int32, (CHUNK, 1), 0)
    refs = [b[i * SUB - 1:i * SUB, :] if i else jnp.zeros((1, b.shape[1]), F32) for i in range(NSUB)]
    mine = [jnp.logical_and(row >= i * SUB, row < (i + 1) * SUB) for i in range(NSUB)]
    ref_rows = refs[0]
    for i in range(1, NSUB):
        ref_rows = jnp.where(mine[i], refs[i], ref_rows)
    eq = jnp.exp(b - ref_rows)
    qt = qf * eq
    ek = jnp.concatenate([jnp.exp(jnp.minimum(r - b, EXP_CLAMP)) for r in refs], axis=1)
    kbig = jnp.concatenate([kk] * NSUB, axis=1) * ek
    qbig = jnp.concatenate([jnp.where(m, qt, 0.0) for m in mine], axis=1)
    return qt, qbig, kbig, eq, ek, mine


def _causal(n, upper=False):
    r, c = lax.broadcasted_iota(jnp.int32, (n, n), 0), lax.broadcasted_iota(jnp.int32, (n, n), 1)
    return (c >= r) if upper else (c <= r)


def _hg_chunk_fwd(qf, kk, b, v, st):
    parts = _hg_intra(qf, kk, b)
    att = jnp.where(_causal(CHUNK), _nt(parts[1].astype(BF16), parts[2].astype(BF16)), 0.0)
    qh = qf * jnp.exp(b)
    o = _nn(att.astype(BF16), v.astype(BF16)) + _nt(qh.astype(BF16), st.astype(BF16))
    bl = b[CHUNK - 1:CHUNK, :]
    kh = kk * jnp.exp(bl - b)
    return o, parts, att, qh, kh, jnp.exp(bl)


def _hgrn_fwd(proj, lb, nw, ts=256):
    s = proj.shape[0]
    ts = min(ts, s)
    nch = ts // CHUNK

    def body(q_ref, f_ref, v_ref, g_ref, lb_ref, nw_ref, o_ref, st_ref, st):
        @pl.when(pl.program_id(0) == 0)
        def _():
            st[...] = jnp.zeros_like(st)

        tri = _tri(CHUNK)

        def chunk(c, carry):
            rows = pl.ds(pl.multiple_of(c * CHUNK, CHUNK), CHUNK)
            _, _, kk, lf, qf, _ = _hg_gates(q_ref[rows, :], f_ref[rows, :], lb_ref[...])
            b = _cum(tri, lf)
            hv, hg = v_ref[rows, :], g_ref[rows, :]
            st_ref[c] = st[...]
            for h in range(HG_HEADS):
                cs = slice(h * HG_D, (h + 1) * HG_D)
                o, _, _, _, kh, ebl = _hg_chunk_fwd(qf[:, cs], kk[:, cs], b[:, cs], hv[:, cs], st[h])
                st[h] = st[h] * ebl + _tn(hv[:, cs].astype(BF16), kh.astype(BF16))
                on = _rms_fwd(o, nw_ref[...])[0]
                gh = hg[:, cs]
                o_ref[rows, cs] = (on * gh * _sigmoid(gh)).astype(BF16)
            return carry

        lax.fori_loop(0, nch, chunk, 0, unroll=4)

    col = lambda n: pl.BlockSpec((ts, 512), functools.partial(lambda n, i: (i, n), n))
    return pl.pallas_call(
        body,
        out_shape=(jax.ShapeDtypeStruct((s, 512), BF16), jax.ShapeDtypeStruct((s // CHUNK, HG_HEADS, HG_D, HG_D), F32)),
        grid=(s // ts,),
        in_specs=[col(0), col(1), col(2), col(3), pl.BlockSpec((1, 512), lambda i: (0, 0)), pl.BlockSpec((1, HG_D), lambda i: (0, 0))],
        out_specs=(pl.BlockSpec((ts, 512), lambda i: (i, 0)), pl.BlockSpec((nch, HG_HEADS, HG_D, HG_D), lambda i: (i, 0, 0, 0))),
        scratch_shapes=[pltpu.VMEM((HG_HEADS, HG_D, HG_D), F32)], name="hgrn_fwd",
        compiler_params=_params(("arbitrary",)),
    )(proj, proj, proj, proj, lb, nw)


def _hgrn_bwd(proj, dout, states, lb, nw, ts=256):
    s = proj.shape[0]
    ts = min(ts, s)
    nch = ts // CHUNK
    nt = s // ts

    def body(q_ref, f_ref, v_ref, g_ref, do_ref, st_ref, lb_ref, nw_ref, dp_ref, dlb_ref, dnw_ref, dst):
        @pl.when(pl.program_id(0) == 0)
        def _():
            dst[...] = jnp.zeros_like(dst)
            dlb_ref[...] = jnp.zeros_like(dlb_ref)
            dnw_ref[...] = jnp.zeros_like(dnw_ref)

        tri, triu = _tri(CHUNK), _tri(CHUNK, upper=True)
        last = lax.broadcasted_iota(jnp.int32, (CHUNK, HG_D), 0) == CHUNK - 1
        nwv = nw_ref[...]

        def chunk(cc, carry):
            c = nch - 1 - cc
            rows = pl.ds(pl.multiple_of(c * CHUNK, CHUNK), CHUNK)
            hq, hf, hv, hg = q_ref[rows, :], f_ref[rows, :], v_ref[rows, :], g_ref[rows, :]
            lbv = lb_ref[...]
            sg, fg, kk, lf, qf, sq = _hg_gates(hq, hf, lbv)
            b = _cum(tri, lf)
            dov = do_ref[rows, :]
            dqf_l, dkk_l, db_l, dv_l, dg_l = [], [], [], [], []
            for h in range(HG_HEADS):
                cs = slice(h * HG_D, (h + 1) * HG_D)
                stp = st_ref[c, h]
                bh, vh, gh = b[:, cs], hv[:, cs], hg[:, cs]
                o, parts, att, qh, kh, ebl = _hg_chunk_fwd(qf[:, cs], kk[:, cs], bh, vh, stp)
                sgg = _sigmoid(gh)
                on, r = _rms_fwd(o, nwv)
                d_on = dov[:, cs] * (gh * sgg)
                dg_l.append(dov[:, cs] * on * _dsilu(gh, sgg))
                do, dnw = _rms_bwd(o, r, nwv, d_on)
                dnw_ref[...] += dnw
                dob, vb = do.astype(BF16), vh.astype(BF16)
                dsth = dst[h]
                dstb = dsth.astype(BF16)
                dqh = _nn3(do, stp)
                dkh = _nn3(vh, dsth)
                dv = _nt(kh.astype(BF16), dstb)
                eb = jnp.exp(bh)
                ekl = jnp.exp(bh[CHUNK - 1:CHUNK, :] - bh)
                dqf, dkk = dqh * eb, dkh * ekl
                db = dqh * qh - dkh * kh
                dbl = jnp.sum(dkh * kh, axis=0, keepdims=True) + ebl * jnp.sum(dsth * stp, axis=0, keepdims=True)
                dst[h] = dsth * ebl + _tn(dob, qh.astype(BF16))
                qt, qbig, kbig, eq, ek, mine = parts
                da = jnp.where(_causal(CHUNK), _nt(dob, vb), 0.0)
                da_t = jnp.where(_causal(CHUNK, upper=True), _nt(vb, dob), 0.0)
                dv = dv + _tn(att.astype(BF16), dob)
                dqbig = _tn3(da_t, kbig)
                dkbig = _tn3(da, qbig)
                dkek, dkkb = dkbig * ek, dkbig * kbig
                dqt = jnp.zeros_like(qt)
                for i in range(NSUB):
                    bs = slice(i * HG_D, (i + 1) * HG_D)
                    dqt = dqt + jnp.where(mine[i], dqbig[:, bs], 0.0)
                    dkk = dkk + dkek[:, bs]
                    db = db - dkkb[:, bs]
                dqf = dqf + dqt * eq
                db = db + dqt * qt + jnp.where(last, dbl, 0.0)
                dqf_l.append(dqf); dkk_l.append(dkk); db_l.append(db); dv_l.append(dv)
            cat = lambda l: jnp.concatenate(l, axis=1)
            dlf = _cum(triu, cat(db_l))
            dfg = dlf / fg - cat(dkk_l)
            dlb_ref[...] += jnp.sum(dfg * (1.0 - sg), axis=0, keepdims=True)
            dp_ref[rows, 0:512] = (cat(dqf_l) * _dsilu(hq, sq)).astype(BF16)
            dp_ref[rows, 512:1024] = (dfg * (1.0 - lbv) * sg * (1.0 - sg)).astype(BF16)
            dp_ref[rows, 1024:1536] = cat(dv_l).astype(BF16)
            dp_ref[rows, 1536:2048] = cat(dg_l).astype(BF16)
            return carry

        lax.fori_loop(0, nch, chunk, 0, unroll=4)

    col = lambda n: pl.BlockSpec((ts, 512), functools.partial(lambda n, i: (nt - 1 - i, n), n))
    vec = lambda n: pl.BlockSpec((1, n), lambda i: (0, 0))
    return pl.pallas_call(
        body,
        out_shape=(jax.ShapeDtypeStruct((s, 2048), BF16), jax.ShapeDtypeStruct((1, 512), F32), jax.ShapeDtypeStruct((1, HG_D), F32)),
        grid=(nt,),
        in_specs=[col(0), col(1), col(2), col(3), pl.BlockSpec((ts, 512), lambda i: (nt - 1 - i, 0)),
                  pl.BlockSpec((nch, HG_HEADS, HG_D, HG_D), lambda i: (nt - 1 - i, 0, 0, 0)), vec(512), vec(HG_D)],
        out_specs=(pl.BlockSpec((ts, 2048), lambda i: (nt - 1 - i, 0)), vec(512), vec(HG_D)),
        scratch_shapes=[pltpu.VMEM((HG_HEADS, HG_D, HG_D), F32)], name="hgrn_bwd",
        compiler_params=_params(("arbitrary",)),
    )(proj, proj, proj, proj, dout, states, lb, nw)


CV_BLK = 2048 // 512


def _halo_before(ts, halo, colblk):
    return pl.BlockSpec((halo, 512), functools.partial(lambda cb, i: (jnp.maximum(i * (ts // halo) - 1, 0), cb), colblk))


def _cv_front(a_ref, g_ref, ah_ref, gh_ref, ext, first):
    a, sg = a_ref[...], _sigmoid(g_ref[...])
    zh = ah_ref[...] * _sigmoid(gh_ref[...])
    ext[0:CV_HALO, :] = jnp.where(first, 0.0, zh)
    ext[CV_HALO:, :] = a * sg
    return a, sg


CV_ROWS = 32


def _windows(ref, r0, base, ntaps, rows):
    out = []
    for phase in range(8):
        taps = [j for j in range(ntaps) if (base + j) % 8 == phase]
        if taps:
            span = max(base + j - phase for j in taps)
            big = ref[pl.ds(r0 + phase, rows + span), :]
            out += [(j, big[base + j - phase:base + j - phase + rows]) for j in taps]
    return out


def _cv_conv_ln(ext, w_ref, b_ref, r0):
    y = jnp.zeros((CV_ROWS, CV_W), F32) + b_ref[...]
    for j, win in _windows(ext, r0, CV_HALO - (CV_K - 1), CV_K, CV_ROWS):
        y = y + w_ref[j:j + 1, :] * win
    mu = jnp.mean(y, axis=-1, keepdims=True)
    yc = y - mu
    r = lax.rsqrt(jnp.mean(yc * yc, axis=-1, keepdims=True) + EPS)
    return yc * r, r


def _conv_fwd(proj, w, b, lnw, lnb, ts=256):
    s = proj.shape[0]
    ts = min(ts, s)

    def body(a_ref, g_ref, ah_ref, gh_ref, w_ref, b_ref, lnw_ref, lnb_ref, o_ref, ext):
        _cv_front(a_ref, g_ref, ah_ref, gh_ref, ext, pl.program_id(0) == 0)
        for r0 in range(0, ts, CV_ROWS):
            yh, _ = _cv_conv_ln(ext, w_ref, b_ref, r0)
            yn = yh * lnw_ref[...] + lnb_ref[...]
            o_ref[r0:r0 + CV_ROWS, :] = (yn * _sigmoid(yn)).astype(BF16)

    col = lambda n: pl.BlockSpec((ts, 512), functools.partial(lambda n, i: (i, n), n))
    vec = pl.BlockSpec((1, CV_W), lambda i: (0, 0))
    return pl.pallas_call(
        body, out_shape=jax.ShapeDtypeStruct((s, CV_W), BF16), grid=(s // ts,),
        in_specs=[col(CV_BLK), col(CV_BLK + 1), _halo_before(ts, CV_HALO, CV_BLK), _halo_before(ts, CV_HALO, CV_BLK + 1),
                  pl.BlockSpec((32, CV_W), lambda i: (0, 0)), vec, vec, vec],
        out_specs=pl.BlockSpec((ts, CV_W), lambda i: (i, 0)), scratch_shapes=[pltpu.VMEM((ts + CV_HALO, CV_W), F32)],
        name="conv_fwd", compiler_params=_params(("parallel",)),
    )(proj, proj, proj, proj, w, b, lnw, lnb)


def _conv_bwd(proj, dout, w, b, lnw, lnb, ts=256):
    s = proj.shape[0]
    ts = min(ts, s)
    nt = s // ts

    def body(a_ref, g_ref, ah_ref, gh_ref, do_ref, w_ref, b_ref, lnw_ref, lnb_ref,
             du_ref, dw_ref, db_ref, dlnw_ref, dlnb_ref, ext, dyext, carry, dwacc):
        i = pl.program_id(0)

        @pl.when(i == 0)
        def _():
            carry[...] = jnp.zeros_like(carry)
            dwacc[...] = jnp.zeros_like(dwacc)
            for ref in (db_ref, dlnw_ref, dlnb_ref):
                ref[...] = jnp.zeros_like(ref)

        _cv_front(a_ref, g_ref, ah_ref, gh_ref, ext, i == nt - 1)
        dyext[ts:, :] = carry[...]
        dlnw = dlnb = db = jnp.zeros((1, CV_W), F32)
        for r0 in range(0, ts, CV_ROWS):
            rows = slice(r0, r0 + CV_ROWS)
            yh, r = _cv_conv_ln(ext, w_ref, b_ref, r0)
            yn = yh * lnw_ref[...] + lnb_ref[...]
            dyn = do_ref[rows, :] * _dsilu(yn, _sigmoid(yn))
            dlnw += jnp.sum(dyn * yh, axis=0, keepdims=True)
            dlnb += jnp.sum(dyn, axis=0, keepdims=True)
            gl = dyn * lnw_ref[...]
            dy = r * (gl - jnp.mean(gl, axis=-1, keepdims=True) - yh * jnp.mean(gl * yh, axis=-1, keepdims=True))
            db += jnp.sum(dy, axis=0, keepdims=True)
            dyext[rows, :] = dy
            for j, win in _windows(ext, r0, CV_HALO - (CV_K - 1), CV_K, CV_ROWS):
                p = dy * win
                dwacc[8 * j:8 * j + 8, :] += (p[0:8] + p[8:16]) + (p[16:24] + p[24:32])
        dlnw_ref[...] += dlnw
        dlnb_ref[...] += dlnb
        db_ref[...] += db
        carry[...] = dyext[0:CV_HALO, :]
        for r0 in range(0, ts, CV_ROWS):
            rows = slice(r0, r0 + CV_ROWS)
            dz = jnp.zeros((CV_ROWS, CV_W), F32)
            for j, win in _windows(dyext, r0, 0, CV_K, CV_ROWS):
                dz = dz + w_ref[CV_K - 1 - j:CV_K - j, :] * win
            a, sg = a_ref[rows, :], _sigmoid(g_ref[rows, :])
            du_ref[rows, 0:CV_W] = (dz * sg).astype(BF16)
            du_ref[rows, CV_W:] = (dz * a * sg * (1.0 - sg)).astype(BF16)

        @pl.when(i == nt - 1)
        def _():
            for j in range(32):
                dw_ref[j:j + 1, :] = jnp.sum(dwacc[8 * j:8 * j + 8, :], axis=0, keepdims=True)

    rev = lambda n: pl.BlockSpec((ts, 512), functools.partial(lambda n, i: (nt - 1 - i, n), n))
    halo = lambda n: pl.BlockSpec((CV_HALO, 512), functools.partial(
        lambda n, i: (jnp.maximum((nt - 1 - i) * (ts // CV_HALO) - 1, 0), n), n))
    vec = pl.BlockSpec((1, CV_W), lambda i: (0, 0))
    wsp = pl.BlockSpec((32, CV_W), lambda i: (0, 0))
    v1 = jax.ShapeDtypeStruct((1, CV_W), F32)
    return pl.pallas_call(
        body, out_shape=(jax.ShapeDtypeStruct((s, 2 * CV_W), BF16), jax.ShapeDtypeStruct((32, CV_W), F32), v1, v1, v1),
        grid=(nt,),
        in_specs=[rev(CV_BLK), rev(CV_BLK + 1), halo(CV_BLK), halo(CV_BLK + 1), rev(0), wsp, vec, vec, vec],
        out_specs=(pl.BlockSpec((ts, 2 * CV_W), lambda i: (nt - 1 - i, 0)), wsp, vec, vec, vec),
        scratch_shapes=[pltpu.VMEM((ts + CV_HALO, CV_W), F32), pltpu.VMEM((ts + CV_HALO, CV_W), F32), pltpu.VMEM((CV_HALO, CV_W), F32),
                        pltpu.VMEM((8 * 32, CV_W), F32)],
        name="conv_bwd", compiler_params=_params(("arbitrary",)),
    )(proj, proj, proj, proj, dout, w, b, lnw, lnb)


PL_BLK = 3072 // 512


def _pool_windows(ext, t0, ts):
    n = ext.shape[0]
    t = t0 + lax.broadcasted_iota(jnp.int32, (ts, 1), 0)
    out = []
    for g, wdw in enumerate(POOL_WINDOWS):
        e = ext[:, g * 128:(g + 1) * 128]
        acc, k = e, 1
        while k < wdw:
            acc = acc + pltpu.roll(acc, k, 0)
            k *= 2
        cnt = jnp.minimum(t + 1, wdw).astype(F32)
        out.append(acc[POOL_HALO:] / cnt - e[POOL_HALO:])
    return out


def _pool_fwd(proj, w, sc, ts=256):
    s = proj.shape[0]
    ts = min(ts, s)

    def body(u_ref, uh_ref, w_ref, sc_ref, o_ref):
        i = pl.program_id(0)
        ext = jnp.concatenate([jnp.where(i == 0, 0.0, uh_ref[...]), u_ref[...]], axis=0)
        ps = _pool_windows(ext, i * ts, ts)
        y = jnp.concatenate([_nn(ps[g].astype(BF16), w_ref[g].astype(BF16)) for g in range(4)], axis=1)
        o_ref[...] = (y * sc_ref[...]).astype(BF16)

    return pl.pallas_call(
        body, out_shape=jax.ShapeDtypeStruct((s, 512), BF16), grid=(s // ts,),
        in_specs=[pl.BlockSpec((ts, 512), lambda i: (i, PL_BLK)), _halo_before(ts, POOL_HALO, PL_BLK),
                  pl.BlockSpec((4, 128, 128), lambda i: (0, 0, 0)), pl.BlockSpec((1, 512), lambda i: (0, 0))],
        out_specs=pl.BlockSpec((ts, 512), lambda i: (i, 0)), name="pool_fwd", compiler_params=_params(("parallel",)),
    )(proj, proj, w, sc)


def _pool_bwd(proj, dout, w, sc, ts=256):
    s = proj.shape[0]
    ts = min(ts, s)
    nt = s // ts
    n = ts + POOL_HALO

    def body(u_ref, uh_ref, do_ref, doh_ref, w_ref, sc_ref, du_ref, dw_ref, dsc_ref):
        i = pl.program_id(0)

        @pl.when(i == 0)
        def _():
            dw_ref[...] = jnp.zeros_like(dw_ref)
            dsc_ref[...] = jnp.zeros_like(dsc_ref)

        ext = jnp.concatenate([jnp.where(i == 0, 0.0, uh_ref[...]), u_ref[...]], axis=0)
        ps = _pool_windows(ext, i * ts, ts)
        dov = do_ref[...]
        dyext = jnp.concatenate([dov, jnp.where(i == nt - 1, 0.0, doh_ref[...])], axis=0) * sc_ref[...]
        t = i * ts + lax.broadcasted_iota(jnp.int32, (n, 1), 0)
        row = lax.broadcasted_iota(jnp.int32, (n, 1), 0)
        dus = []
        for g, wdw in enumerate(POOL_WINDOWS):
            cs = slice(g * 128, (g + 1) * 128)
            wg, pb = w_ref[g].astype(BF16), ps[g].astype(BF16)
            dsc_ref[:, cs] += jnp.sum(dov[:, cs] * _nn(pb, wg), axis=0, keepdims=True)
            dyg = dyext[:, cs].astype(BF16)
            dw_ref[g] += _tn(pb, dyg[0:ts])
            dp = _nt(dyg, wg)
            acc, k = dp / jnp.minimum(t + 1, wdw).astype(F32), 1
            while k < wdw:
                acc = acc + jnp.where(row < n - k, pltpu.roll(acc, n - k, 0), 0.0)
                k *= 2
            dus.append(acc[0:ts] - dp[0:ts])
        du_ref[...] = jnp.concatenate(dus, axis=1).astype(BF16)

    tile = lambda cb: pl.BlockSpec((ts, 512), functools.partial(lambda cb, i: (i, cb), cb))
    after = pl.BlockSpec((POOL_HALO, 512), lambda i: (jnp.minimum((i + 1) * (ts // POOL_HALO), s // POOL_HALO - 1), 0))
    wsp, vec = pl.BlockSpec((4, 128, 128), lambda i: (0, 0, 0)), pl.BlockSpec((1, 512), lambda i: (0, 0))
    return pl.pallas_call(
        body, out_shape=(jax.ShapeDtypeStruct((s, 512), BF16), jax.ShapeDtypeStruct((4, 128, 128), F32), jax.ShapeDtypeStruct((1, 512), F32)),
        grid=(nt,),
        in_specs=[tile(PL_BLK), _halo_before(ts, POOL_HALO, PL_BLK), tile(0), after, wsp, vec],
        out_specs=(tile(0), wsp, vec), name="pool_bwd", compiler_params=_params(("arbitrary",)),
    )(proj, proj, dout, dout, w, sc)


LX_BLK, LY_BLK = 3584 // 512, 4096 // 512
LRU_OFF = LRU_HALO - (LRU_CONV - 1)


def _scan_fwd(a, b):
    n = a.shape[0]
    row = lax.broadcasted_iota(jnp.int32, (n, 1), 0)
    k = 1
    while k < n:
        m = row >= k
        b = jnp.where(m, a * pltpu.roll(b, k, 0) + b, b)
        a = jnp.where(m, a * pltpu.roll(a, k, 0), a)
        k *= 2
    return a, b


def _scan_rev(a, b):
    n = a.shape[0]
    row = lax.broadcasted_iota(jnp.int32, (n, 1), 0)
    k = 1
    while k < n:
        m = row < n - k
        b = jnp.where(m, a * pltpu.roll(b, n - k, 0) + b, b)
        a = jnp.where(m, a * pltpu.roll(a, n - k, 0), a)
        k *= 2
    return b


def _lru_gates(x_ref, xh_ref, ext, first, cw_ref, cb_ref, wa_ref, ba_ref, wx_ref, bx_ref, lam_ref, ts):
    ext[0:LRU_HALO, :] = jnp.where(first, 0.0, xh_ref[...])
    ext[LRU_HALO:, :] = x_ref[...]
    xc = jnp.zeros((ts, LRU_W), F32) + cb_ref[...]
    for j in range(LRU_CONV):
        xc = xc + cw_ref[j:j + 1, :] * ext[pl.ds(LRU_OFF + j, ts), :]
    xb = xc.astype(BF16)
    r = _sigmoid(_nn(xb, wa_ref[...]) + ba_ref[...])
    ig = _sigmoid(_nn(xb, wx_ref[...]) + bx_ref[...])
    nl = -lam_ref[...]
    sp = jnp.maximum(nl, 0.0) + jnp.log(1.0 + jnp.exp(-jnp.abs(nl)))
    la = -LRU_C * r * sp
    a = jnp.exp(la)
    z = 2.0 * la
    em = jnp.where(z > -0.1, -z * (1.0 + z * 0.5 * (1.0 + z * (1.0 / 3) * (1.0 + z * 0.25 * (1.0 + z * 0.2)))), 1.0 - a * a)
    return xc, xb, r, ig, sp, a, jnp.sqrt(em)


def _lru_fwd(proj, cw, cb, wa, ba, wx, bx, lam, ts=256):
    s = proj.shape[0]
    ts = min(ts, s)

    def body(x_ref, xh_ref, y_ref, cw_ref, cb_ref, wa_ref, ba_ref, wx_ref, bx_ref, lam_ref, o_ref, h_ref, ext, hc):
        i = pl.program_id(0)

        @pl.when(i == 0)
        def _():
            hc[...] = jnp.zeros_like(hc)

        xc, _, _, ig, _, a, mult = _lru_gates(x_ref, xh_ref, ext, i == 0, cw_ref, cb_ref, wa_ref, ba_ref, wx_ref, bx_ref, lam_ref, ts)
        acum, h0 = _scan_fwd(a, mult * ig * xc)
        h = h0 + acum * hc[0:1, :]
        hc[...] = jnp.broadcast_to(h[ts - 1:ts, :], hc.shape)
        h_ref[...] = h
        o_ref[...] = (h * _gelu_parts(y_ref[...])[0]).astype(BF16)

    tile = lambda cb_: pl.BlockSpec((ts, 512), functools.partial(lambda c, i: (i, c), cb_))
    vec = pl.BlockSpec((1, LRU_W), lambda i: (0, 0))
    mat = pl.BlockSpec((LRU_W, LRU_W), lambda i: (0, 0))
    return pl.pallas_call(
        body, out_shape=(jax.ShapeDtypeStruct((s, LRU_W), BF16), jax.ShapeDtypeStruct((s, LRU_W), F32)), grid=(s // ts,),
        in_specs=[tile(LX_BLK), _halo_before(ts, LRU_HALO, LX_BLK), tile(LY_BLK), pl.BlockSpec((8, LRU_W), lambda i: (0, 0)),
                  vec, mat, vec, mat, vec, vec],
        out_specs=(tile(0), tile(0)), scratch_shapes=[pltpu.VMEM((ts + LRU_HALO, LRU_W), F32), pltpu.VMEM((8, LRU_W), F32)],
        name="lru_fwd", compiler_params=_params(("arbitrary",)),
    )(proj, proj, proj, cw, cb, wa, ba, wx, bx, lam)


def _lru_bwd(proj, hs, dout, cw, cb, wa, ba, wx, bx, lam, ts=256):
    s = proj.shape[0]
    ts = min(ts, s)
    nt = s // ts

    def body(x_ref, xh_ref, y_ref, h_ref, hh_ref, do_ref, cw_ref, cb_ref, wa_ref, ba_ref, wx_ref, bx_ref, lam_ref,
             dxy_ref, dcw_ref, dcb_ref, dwa_ref, dba_ref, dwx_ref, dbx_ref, dlam_ref, ext, dext, cg, cd):
        i = pl.program_id(0)
        first_tile = i == nt - 1

        @pl.when(i == 0)
        def _():
            cg[...] = jnp.zeros_like(cg)
            cd[...] = jnp.zeros_like(cd)
            for ref in (dcw_ref, dcb_ref, dwa_ref, dba_ref, dwx_ref, dbx_ref, dlam_ref):
                ref[...] = jnp.zeros_like(ref)

        xc, xb, r, ig, sp, a, mult = _lru_gates(x_ref, xh_ref, ext, first_tile, cw_ref, cb_ref, wa_ref, ba_ref, wx_ref, bx_ref, lam_ref, ts)
        row = lax.broadcasted_iota(jnp.int32, (ts, 1), 0)
        h, dov = h_ref[...], do_ref[...]
        gel, dgel = _gelu_parts(y_ref[...])
        dxy_ref[:, LRU_W:] = (dov * h * dgel).astype(BF16)
        alpha = jnp.where(row < ts - 1, pltpu.roll(a, ts - 1, 0), 0.0)
        g = _scan_rev(alpha, dov * gel + jnp.where(row == ts - 1, cg[0:1, :], 0.0))
        cg[...] = jnp.broadcast_to(a[0:1, :] * g[0:1, :], cg.shape)
        hprev = jnp.where(row == 0, jnp.where(first_tile, 0.0, hh_ref[LRU_HALO - 1:LRU_HALO, :]), pltpu.roll(h, 1, 0))
        dla = g * hprev * a - g * ig * xc * (a * a) / mult
        dpr = dla * (-LRU_C * sp) * r * (1.0 - r)
        dpi = g * mult * xc * ig * (1.0 - ig)
        dprb, dpib = dpr.astype(BF16), dpi.astype(BF16)
        dxc = g * mult * ig + _nt(dprb, wa_ref[...]) + _nt(dpib, wx_ref[...])
        dlam_ref[...] += jnp.sum(dla * (-LRU_C * r), axis=0, keepdims=True) * (-_sigmoid(-lam_ref[...]))
        dwa_ref[...] += _tn(xb, dprb)
        dwx_ref[...] += _tn(xb, dpib)
        dba_ref[...] += jnp.sum(dpr, axis=0, keepdims=True)
        dbx_ref[...] += jnp.sum(dpi, axis=0, keepdims=True)
        dcb_ref[...] += jnp.sum(dxc, axis=0, keepdims=True)
        dext[0:ts, :] = dxc
        dext[ts:, :] = cd[...]
        cd[...] = dxc[0:LRU_HALO, :]
        dx = jnp.zeros((ts, LRU_W), F32)
        for j in range(LRU_CONV):
            dcw_ref[j:j + 1, :] += jnp.sum(dxc * ext[pl.ds(LRU_OFF + j, ts), :], axis=0, keepdims=True)
            dx = dx + cw_ref[j:j + 1, :] * dext[pl.ds(LRU_CONV - 1 - j, ts), :]
        dxy_ref[:, 0:LRU_W] = dx.astype(BF16)

    rev = lambda c: pl.BlockSpec((ts, 512), functools.partial(lambda c, i: (nt - 1 - i, c), c))
    halo = lambda c: pl.BlockSpec((LRU_HALO, 512), functools.partial(
        lambda c, i: (jnp.maximum((nt - 1 - i) * (ts // LRU_HALO) - 1, 0), c), c))
    vec = pl.BlockSpec((1, LRU_W), lambda i: (0, 0))
    mat = pl.BlockSpec((LRU_W, LRU_W), lambda i: (0, 0))
    cws = pl.BlockSpec((8, LRU_W), lambda i: (0, 0))
    v1, m1 = jax.ShapeDtypeStruct((1, LRU_W), F32), jax.ShapeDtypeStruct((LRU_W, LRU_W), F32)
    return pl.pallas_call(
        body, out_shape=(jax.ShapeDtypeStruct((s, 2 * LRU_W), BF16), jax.ShapeDtypeStruct((8, LRU_W), F32), v1, m1, v1, m1, v1, v1),
        grid=(nt,),
        in_specs=[rev(LX_BLK), halo(LX_BLK), rev(LY_BLK), rev(0), halo(0), rev(0), cws, vec, mat, vec, mat, vec, vec],
        out_specs=(pl.BlockSpec((ts, 2 * LRU_W), lambda i: (nt - 1 - i, 0)), cws, vec, mat, vec, mat, vec, vec),
        scratch_shapes=[pltpu.VMEM((ts + LRU_HALO, LRU_W), F32), pltpu.VMEM((ts + LRU_HALO, LRU_W), F32),
                        pltpu.VMEM((8, LRU_W), F32), pltpu.VMEM((LRU_HALO, LRU_W), F32)],
        name="lru_bwd", compiler_params=_params(("arbitrary",)),
    )(proj, proj, proj, hs, hs, dout, cw, cb, wa, ba, wx, bx, lam)


def _final_loss(x, fw, tgt, ts=512):
    s, d = x.shape
    ts = min(ts, s)

    def body(x_ref, w_ref, t_ref, loss_ref, dx_ref, dw_ref):
        @pl.when(pl.program_id(0) == 0)
        def _():
            loss_ref[...] = jnp.zeros_like(loss_ref)
            dw_ref[...] = jnp.zeros_like(dw_ref)

        xv = x_ref[...]
        y, r = _rms_fwd(xv, w_ref[...])
        err = y - t_ref[...]
        loss_ref[...] += 0.5 * jnp.sum(jnp.mean(err * err, axis=-1, keepdims=True), axis=0, keepdims=True)
        dx, dw = _rms_bwd(xv, r, w_ref[...], err * (1.0 / d))
        dx_ref[...] = dx
        dw_ref[...] += dw

    row = pl.BlockSpec((ts, d), lambda i: (i, 0))
    vec = pl.BlockSpec((1, d), lambda i: (0, 0))
    return pl.pallas_call(
        body, out_shape=(jax.ShapeDtypeStruct((8, 128), F32), jax.ShapeDtypeStruct((s, d), F32), jax.ShapeDtypeStruct((1, d), F32)),
        grid=(s // ts,), in_specs=[row, vec, row], out_specs=(pl.BlockSpec((8, 128), lambda i: (0, 0)), row, vec),
        name="final_loss", compiler_params=_params(("arbitrary",)),
    )(x, fw, tgt)


def _lb_softmax(raw_ref):
    raw = raw_ref[...]
    e = jnp.exp(raw - jnp.max(raw, axis=0, keepdims=True))
    return e / jnp.sum(e, axis=0, keepdims=True)


def _lb_fwd(raw):
    def body(raw_ref, o_ref):
        sm = _lb_softmax(raw_ref)
        acc = jnp.zeros((1, sm.shape[1]), F32)
        o_ref[0:1, :] = acc
        for l in range(1, DEPTH):
            acc = acc + sm[l:l + 1, :]
            o_ref[l:l + 1, :] = acc

    return pl.pallas_call(body, out_shape=jax.ShapeDtypeStruct(raw.shape, F32), name="lb_fwd")(raw)


def _lb_bwd(raw, dlb):
    def body(raw_ref, d_ref, o_ref):
        sm = _lb_softmax(raw_ref)
        dlbv = d_ref[...]
        dsm, acc = [None] * DEPTH, jnp.zeros((1, sm.shape[1]), F32)
        for l in range(DEPTH - 1, 0, -1):
            acc = acc + dlbv[l:l + 1, :]
            dsm[l] = acc
        dsm[0] = jnp.zeros_like(acc)
        dsm = jnp.concatenate(dsm, axis=0)
        o_ref[...] = sm * (dsm - jnp.sum(sm * dsm, axis=0, keepdims=True))

    return pl.pallas_call(body, out_shape=jax.ShapeDtypeStruct(raw.shape, F32), name="lb_bwd")(raw, dlb)


def _adam_math(w, g, m, v):
    m = ADAM_B1 * m + (1.0 - ADAM_B1) * g
    v = ADAM_B2 * v + (1.0 - ADAM_B2) * (g * g)
    m_hat = m / (1.0 - ADAM_B1 ** ADAM_STEP)
    v_hat = v / (1.0 - ADAM_B2 ** ADAM_STEP)
    return -ADAM_LR * (m_hat / (jnp.sqrt(v_hat) + ADAM_EPS) + ADAM_WD * w), m, v


def _adamw(w, gs, m, v, name, tr=128):
    r, c = w.shape
    tr = min(tr, r)
    ng = len(gs)

    def body(*refs):
        w_ref, g_refs, m_ref, v_ref = refs[0], refs[1:1 + ng], refs[1 + ng], refs[2 + ng]
        outs = refs[3 + ng:]
        g = g_refs[0][...]
        if ng == 2:
            g = g + g_refs[1][...]
            outs[0][...] = g
            outs = outs[1:]
        for o, val in zip(outs, _adam_math(w_ref[...], g, m_ref[...], v_ref[...])):
            o[...] = val

    blk = pl.BlockSpec((tr, c), lambda i: (i, 0))
    sd = jax.ShapeDtypeStruct((r, c), F32)
    nout = 3 + (ng == 2)
    return pl.pallas_call(
        body, out_shape=(sd,) * nout, grid=(r // tr,), in_specs=[blk] * (3 + ng), out_specs=(blk,) * nout, name=name,
        compiler_params=_params(("parallel",)),
    )(w, *gs, m, v)


def _cast_into_full(w, kind, jj, name, tr=256):
    l, r, c = w.shape
    tr = min(tr, r)

    def body(j_ref, w_ref, o_ref):
        o_ref[...] = w_ref[...].astype(BF16)

    if kind == "col":
        full, dst = (l, r, 4 * c), pl.BlockSpec((None, tr, c), lambda a, b, j: (a, b, j[0]))
    else:
        full, dst = (l, 4 * r, c), pl.BlockSpec((None, tr, c), lambda a, b, j: (a, j[0] * (r // tr) + b, 0))
    return pl.pallas_call(
        body, out_shape=jax.ShapeDtypeStruct(full, BF16),
        grid_spec=pltpu.PrefetchScalarGridSpec(
            num_scalar_prefetch=1, grid=(l, r // tr), in_specs=[pl.BlockSpec((None, tr, c), lambda a, b, j: (a, b, 0))], out_specs=dst),
        name=name, compiler_params=_params(("parallel", "parallel")),
    )(jj, w)


def _place():
    return lax.axis_index("x"), lax.axis_index("y"), lax.axis_index("c")


def _other_chips(x, y):
    return [(1 - x, y), (x, 1 - y), (1 - x, 1 - y)]


def _slab(ref, kind, jj):
    if kind == "col":
        c = ref.shape[2] // 4
        return ref.at[:, :, pl.ds(jj * c, c)]
    r = ref.shape[1] // 4
    return ref.at[:, pl.ds(jj * r, r), :]


HBM = pl.BlockSpec(memory_space=pltpu.HBM)
SEM = pl.BlockSpec(memory_space=pltpu.SEMAPHORE)
EFFECT = pltpu.SideEffectType.DATAFLOW_SIDE_EFFECTING


def _in_hbm(a):
    return pltpu.with_memory_space_constraint(a, pltpu.HBM)


def _thru(arrs):
    return [pltpu.HBM(a.shape, a.dtype) for a in arrs]


def _half_slab(ref, kind, group, jj, half):
    per = ref.shape[0] // DEPTH
    layers = pl.ds(group * per, per)
    if kind == "col":
        r, c = ref.shape[1] // 2, ref.shape[2] // 4
        return ref.at[layers, pl.ds(half * r, r), pl.ds(jj * c, c)]
    r = ref.shape[1] // 8
    return ref.at[layers, pl.ds((2 * jj + half) * r, r), :]


def _gather_copy(fulls, kinds, send_sems, recv_sems, group, t, k, landing):
    x, y, c = _place()
    chip = _other_chips(x, y)[k]
    idx = (group * len(fulls) + t) * 3 + k
    return pltpu.make_async_remote_copy(
        src_ref=_half_slab(fulls[t], kinds[t], group, 2 * x + y, c), dst_ref=_half_slab(fulls[t], kinds[t], group, landing, c),
        send_sem=send_sems.at[idx], recv_sem=recv_sems.at[idx], device_id=(chip[0], chip[1], c), device_id_type=MESH)


def _fill_copy(fulls, kinds, send_sems, recv_sems, group, which, t, k, half):
    x, y, c = _place()
    chip = _other_chips(x, y)[k]
    idx = which.index(t) * 3 + k
    jj = 2 * chip[0] + chip[1]
    return pltpu.make_async_remote_copy(
        src_ref=_half_slab(fulls[t], kinds[t], group, jj, c), dst_ref=_half_slab(fulls[t], kinds[t], group, jj, half),
        send_sem=send_sems.at[idx], recv_sem=recv_sems.at[idx], device_id=(x, y, 1 - c), device_id_type=MESH)


def _fill_start(group, which, fulls, kinds):
    nt = len(fulls)
    ncp = 3 * len(which)

    def body(*refs):
        ins, send_sems, recv_sems = refs[:nt], refs[nt], refs[nt + 1]
        _, _, c = _place()
        for t in which:
            for k in range(3):
                _fill_copy(ins, kinds, send_sems, recv_sems, group, which, t, k, c).start()

    out = pl.pallas_call(
        body, out_shape=(pltpu.SemaphoreType.DMA((ncp,)), pltpu.SemaphoreType.DMA((ncp,)), *_thru(fulls)),
        in_specs=[HBM] * nt, out_specs=(SEM, SEM, *([HBM] * nt)), input_output_aliases={t: 2 + t for t in range(nt)},
        name="fill_start_%d_%d" % (group, which[0]), compiler_params=pltpu.CompilerParams(has_side_effects=EFFECT),
    )(*fulls)
    return out[0], out[1], list(out[2:])


def _fill_wait(group, which, send_sems, recv_sems, fulls, kinds, after):
    nt = len(fulls)

    def body(*refs):
        ins, send_ref, recv_ref = refs[:nt], refs[nt], refs[nt + 1]
        _, _, c = _place()
        for t in which:
            for k in range(3):
                cp = _fill_copy(ins, kinds, send_ref, recv_ref, group, which, t, k, 1 - c)
                cp.wait_send()
                cp.wait_recv()

    out = pl.pallas_call(
        body, out_shape=tuple(_thru(fulls)), in_specs=[HBM] * nt + [SEM, SEM, ANY], out_specs=tuple([HBM] * nt),
        input_output_aliases={t: t for t in range(nt)}, name="fill_wait_%d_%d" % (group, which[0]),
        compiler_params=pltpu.CompilerParams(has_side_effects=EFFECT),
    )(*fulls, send_sems, recv_sems, after)
    return list(out)


def _gather_start(fulls, kinds, after, name, groups=range(DEPTH), skip=()):
    nt = len(fulls)
    ncp = DEPTH * nt * 3

    def body(*refs):
        ins, send_sems, recv_sems = refs[:nt], refs[nt + 1], refs[nt + 2]
        x, y, _ = _place()
        for group in groups:
            for t in range(nt):
                for k in range(3):
                    if (group, t) not in skip:
                        _gather_copy(ins, kinds, send_sems, recv_sems, group, t, k, 2 * x + y).start()

    out = pl.pallas_call(
        body, out_shape=(pltpu.SemaphoreType.DMA((ncp,)), pltpu.SemaphoreType.DMA((ncp,)), *_thru(fulls)),
        in_specs=[HBM] * nt + [ANY], out_specs=(SEM, SEM, *([HBM] * nt)), input_output_aliases={t: 2 + t for t in range(nt)},
        name=name, compiler_params=pltpu.CompilerParams(has_side_effects=EFFECT),
    )(*[_in_hbm(a) for a in fulls], after)
    return out[0], out[1], list(out[2:])


def _gather_wait(group, which, send_sems, recv_sems, fulls, kinds, after):
    nt = len(fulls)

    def body(*refs):
        ins, send_ref, recv_ref = refs[:nt], refs[nt], refs[nt + 1]
        x, y, _ = _place()
        chips = _other_chips(x, y)
        for t in which:
            for k in range(3):
                cp = _gather_copy(ins, kinds, send_ref, recv_ref, group, t, k, 2 * chips[k][0] + chips[k][1])
                cp.wait_send()
                cp.wait_recv()

    out = pl.pallas_call(
        body, out_shape=tuple(_thru(fulls)), in_specs=[HBM] * nt + [SEM, SEM, ANY], out_specs=tuple([HBM] * nt),
        input_output_aliases={t: t for t in range(nt)}, name="gather_wait_%d_%d" % (group, which[0]),
        compiler_params=pltpu.CompilerParams(has_side_effects=EFFECT),
    )(*fulls, send_sems, recv_sems, after)
    return list(out)


def _scatter_copy(grads, lands, kinds, send_sems, recv_sems, t, k):
    x, y, c = _place()
    chip = _other_chips(x, y)[k]
    return pltpu.make_async_remote_copy(
        src_ref=_slab(grads[t], kinds[t], 2 * chip[0] + chip[1]), dst_ref=lands[t].at[k], send_sem=send_sems.at[3 * t + k],
        recv_sem=recv_sems.at[3 * t + k], device_id=(chip[0], chip[1], c), device_id_type=MESH)


def _scatter_start(grads, kinds, after, name):
    nt = len(grads)
    lands = []
    for g, kd in zip(grads, kinds):
        l, r, c = g.shape
        lands.append(lax.empty((3, l, r, c // 4) if kd == "col" else (3, l, r // 4, c), g.dtype))

    def body(*refs):
        ins, lnd, send_sems, recv_sems = refs[:nt], refs[nt:2 * nt], refs[2 * nt + 1], refs[2 * nt + 2]
        for t in range(nt):
            for k in range(3):
                _scatter_copy(ins, lnd, kinds, send_sems, recv_sems, t, k).start()
        refs[-1][...] = jnp.zeros_like(refs[-1])

    out = pl.pallas_call(
        body, out_shape=(pltpu.SemaphoreType.DMA((3 * nt,)), pltpu.SemaphoreType.DMA((3 * nt,)), *_thru(grads), *_thru(lands),
                         jax.ShapeDtypeStruct((8, 128), F32)),
        in_specs=[HBM] * (2 * nt) + [ANY], out_specs=(SEM, SEM, *([HBM] * (2 * nt)), pl.BlockSpec(memory_space=pltpu.VMEM)),
        input_output_aliases={t: 2 + t for t in range(2 * nt)}, name=name,
        compiler_params=pltpu.CompilerParams(has_side_effects=EFFECT),
    )(*[_in_hbm(a) for a in grads], *[_in_hbm(a) for a in lands], after)
    return (out[0], out[1], list(out[2:2 + nt]), list(out[2 + nt:2 + 2 * nt])), out[-1]


def _scatter_wait(send_sems, recv_sems, grads, lands, kinds, after, name):
    nt = len(grads)

    def body(*refs):
        ins, lnd, send_ref, recv_ref = refs[:nt], refs[nt:2 * nt], refs[2 * nt], refs[2 * nt + 1]
        for t in range(nt):
            for k in range(3):
                cp = _scatter_copy(ins, lnd, kinds, send_ref, recv_ref, t, k)
                cp.wait_send()
                cp.wait_recv()

    out = pl.pallas_call(
        body, out_shape=(*_thru(grads), *_thru(lands)), in_specs=[HBM] * (2 * nt) + [SEM, SEM, ANY],
        out_specs=tuple([HBM] * (2 * nt)), input_output_aliases={t: t for t in range(2 * nt)}, name=name,
        compiler_params=pltpu.CompilerParams(has_side_effects=EFFECT),
    )(*grads, *lands, send_sems, recv_sems, after)
    return list(out[:nt]), list(out[nt:])


def _sibling_swap(arrs, name="sibling_swap"):
    nt = len(arrs)

    def body(*refs):
        ins, outs = refs[:nt], refs[nt:2 * nt]
        send_sems, recv_sems = refs[2 * nt:]
        x, y, c = _place()
        sends = [pltpu.make_async_remote_copy(src_ref=ins[t], dst_ref=outs[t], send_sem=send_sems.at[t], recv_sem=recv_sems.at[t],
                                              device_id=(x, y, 1 - c), device_id_type=MESH) for t in range(nt)]
        for cp in sends:
            cp.start()
        for cp in sends:
            cp.wait_recv()
        for cp in sends:
            cp.wait_send()

    return pl.pallas_call(
        body, out_shape=[jax.ShapeDtypeStruct(a.shape, a.dtype) for a in arrs], in_specs=[ANY] * nt, out_specs=[ANY] * nt,
        scratch_shapes=[pltpu.SemaphoreType.DMA((nt,)), pltpu.SemaphoreType.DMA((nt,))], name=name,
    )(*arrs)


def _gather_small(vec, over_c):
    n = vec.shape[0]
    flips = [(dx, dy, dc) for dx in (0, 1) for dy in (0, 1) for dc in ((0, 1) if over_c else (0,))][1:]
    np_ = len(flips)

    def body(v_ref, o_ref, send_sems, recv_sems, local_sem):
        x, y, c = _place()

        def idx(px, py, pc):
            return 4 * px + 2 * py + pc if over_c else 2 * px + py

        def peer(f):
            return (1 - x if f[0] else x, 1 - y if f[1] else y, 1 - c if f[2] else c)

        def push(k, landing):
            return pltpu.make_async_remote_copy(src_ref=v_ref, dst_ref=o_ref.at[landing], send_sem=send_sems.at[k],
                                                recv_sem=recv_sems.at[k], device_id=peer(flips[k]), device_id_type=MESH)

        mine = pltpu.make_async_copy(v_ref, o_ref.at[idx(x, y, c)], local_sem)
        sends = [push(k, idx(x, y, c)) for k in range(np_)]
        for cp in [mine] + sends:
            cp.start()
        for k in range(np_):
            push(k, idx(*peer(flips[k]))).wait_recv()
        for cp in sends:
            cp.wait_send()
        mine.wait()

    return pl.pallas_call(
        body, out_shape=jax.ShapeDtypeStruct((np_ + 1, n, 128), F32), in_specs=[ANY], out_specs=ANY,
        scratch_shapes=[pltpu.SemaphoreType.DMA((np_,)), pltpu.SemaphoreType.DMA((np_,)), pltpu.SemaphoreType.DMA(())],
        name="gather_small_all" if over_c else "gather_small_xy",
    )(vec)


def _sum_rows(buf, after, tr=512):
    p, n, _ = buf.shape
    tr = min(tr, n)

    def body(b_ref, after_ref, o_ref):
        acc = b_ref[0]
        for k in range(1, p):
            acc = acc + b_ref[k]
        o_ref[...] = acc

    return pl.pallas_call(
        body, out_shape=jax.ShapeDtypeStruct((n, 128), F32), grid=(n // tr,),
        in_specs=[pl.BlockSpec((p, tr, 128), lambda i: (0, i, 0)), ANY], out_specs=pl.BlockSpec((tr, 128), lambda i: (i, 0)),
        name="sum_rows", compiler_params=_params(("parallel",)),
    )(buf, after)


def _sum_partials_into(stack, at, depth, grad, recv, kind, jj, name, tr=128):
    _, _, r, c = recv.shape
    tr = min(tr, r)

    def body(j_ref, g_ref, r0, r1, r2, *rest):
        rest[-1][...] = ((g_ref[...].astype(F32) + r0[...].astype(F32)) + r1[...].astype(F32)) + r2[...].astype(F32)

    if kind == "col":
        own = pl.BlockSpec((None, tr, c), lambda b, j: (0, b, j[0]))
    else:
        own = pl.BlockSpec((None, tr, c), lambda b, j: (0, j[0] * (r // tr) + b, 0))
    got = lambda k: pl.BlockSpec((None, None, tr, c), functools.partial(lambda k, b, j: (k, 0, b, 0), k))
    chained = stack is not None
    return pl.pallas_call(
        body, out_shape=jax.ShapeDtypeStruct((depth, r, c), F32),
        grid_spec=pltpu.PrefetchScalarGridSpec(
            num_scalar_prefetch=1, grid=(r // tr,), in_specs=[own, got(0), got(1), got(2)] + ([ANY] if chained else []),
            out_specs=pl.BlockSpec((None, tr, c), lambda b, j: (at, b, 0))),
        input_output_aliases={5: 0} if chained else {}, name=name, compiler_params=_params(("parallel",)),
    )(*([jj, grad, recv, recv, recv] + ([stack] if chained else [])))


WEIGHTS = ['norm_mix_w', 'w_in', 'hg_lb_raw', 'hg_norm_w', 'cv_dw_w', 'cv_dw_b', 'cv_ln_w', 'cv_ln_b', 'pl_w', 'pl_scale',
           'lru_conv_w', 'lru_conv_b', 'lru_wa', 'lru_ba', 'lru_wx', 'lru_bx', 'lru_lambda', 'gate_b', 'w_branch', 'w_out',
           'norm_mem_w', 'mem_norm_w', 'xa_wq', 'xa_wkv', 'xa_wo', 'norm_ffn_w', 'ffn_w1', 'ffn_w2', 'final_norm_w']
BIG = {'w_in': 'col', 'w_branch': 'col', 'w_out': 'row', 'xa_wq': 'row', 'xa_wkv': 'col', 'xa_wo': 'row', 'ffn_w1': 'col', 'ffn_w2': 'row'}
SMALL_SPLIT = ('gate_b', 'cv_dw_w', 'lru_conv_w')
SMALL = [n for n in WEIGHTS if n not in BIG]
PIECE_GROUPS = [['ffn_w1', 'ffn_w2', 'xa_wq', 'xa_wkv', 'xa_wo'],
                ['w_out', ('w_branch', 0), ('w_branch', 1), ('w_branch', 2), ('w_branch', 3), 'w_in']]


def _piece_kinds(pieces):
    return [BIG[k[0] if isinstance(k, tuple) else k] for k in pieces]
ROWS_PAD = 512


def _as3d(a):
    return a.reshape((-1,) + a.shape[-2:])


def _pack(parts):
    flat = jnp.concatenate([p.reshape(-1).astype(F32) for p in parts])
    n = -(-flat.shape[0] // (128 * ROWS_PAD)) * ROWS_PAD
    return jnp.pad(flat, (0, n * 128 - flat.shape[0])).reshape(n, 128)


def _unpack(packed, shapes):
    flat, out, o = packed.reshape(-1), [], 0
    for sh in shapes:
        sz = math.prod(sh)
        out.append(flat[o:o + sz].reshape(sh))
        o += sz
    return out


def _block_diag(w):
    h, a, b = w.shape
    eye = jnp.eye(h, dtype=w.dtype)
    return (w[:, :, None, :] * eye[:, None, :, None]).reshape(h * a, h * b)


def _diag_blocks(m, h):
    a, b = m.shape[0] // h, m.shape[1] // h
    return jnp.stack([m[i * a:(i + 1) * a, i * b:(i + 1) * b] for i in range(h)])


def kernel(x, mem, norm_mix_w, w_in, hg_lb_raw, hg_norm_w, cv_dw_w, cv_dw_b, cv_ln_w, cv_ln_b, pl_w, pl_scale, lru_conv_w, lru_conv_b, lru_wa, lru_ba, lru_wx, lru_bx, lru_lambda, gate_b, w_branch, w_out, norm_mem_w, mem_norm_w, xa_wq, xa_wkv, xa_wo, norm_ffn_w, ffn_w1, ffn_w2, final_norm_w, loss_target, m_norm_mix_w, m_w_in, m_hg_lb_raw, m_hg_norm_w, m_cv_dw_w, m_cv_dw_b, m_cv_ln_w, m_cv_ln_b, m_pl_w, m_pl_scale, m_lru_conv_w, m_lru_conv_b, m_lru_wa, m_lru_ba, m_lru_wx, m_lru_bx, m_lru_lambda, m_gate_b, m_w_branch, m_w_out, m_norm_mem_w, m_mem_norm_w, m_xa_wq, m_xa_wkv, m_xa_wo, m_norm_ffn_w, m_ffn_w1, m_ffn_w2, m_final_norm_w, v_norm_mix_w, v_w_in, v_hg_lb_raw, v_hg_norm_w, v_cv_dw_w, v_cv_dw_b, v_cv_ln_w, v_cv_ln_b, v_pl_w, v_pl_scale, v_lru_conv_w, v_lru_conv_b, v_lru_wa, v_lru_ba, v_lru_wx, v_lru_bx, v_lru_lambda, v_gate_b, v_w_branch, v_w_out, v_norm_mem_w, v_mem_norm_w, v_xa_wq, v_xa_wkv, v_xa_wo, v_norm_ffn_w, v_ffn_w1, v_ffn_w2, v_final_norm_w):
    w = dict(zip(WEIGHTS, (norm_mix_w, w_in, hg_lb_raw, hg_norm_w, cv_dw_w, cv_dw_b, cv_ln_w, cv_ln_b, pl_w, pl_scale, lru_conv_w, lru_conv_b, lru_wa, lru_ba, lru_wx, lru_bx, lru_lambda, gate_b, w_branch, w_out, norm_mem_w, mem_norm_w, xa_wq, xa_wkv, xa_wo, norm_ffn_w, ffn_w1, ffn_w2, final_norm_w)))
    m1 = dict(zip(WEIGHTS, (m_norm_mix_w, m_w_in, m_hg_lb_raw, m_hg_norm_w, m_cv_dw_w, m_cv_dw_b, m_cv_ln_w, m_cv_ln_b, m_pl_w, m_pl_scale, m_lru_conv_w, m_lru_conv_b, m_lru_wa, m_lru_ba, m_lru_wx, m_lru_bx, m_lru_lambda, m_gate_b, m_w_branch, m_w_out, m_norm_mem_w, m_mem_norm_w, m_xa_wq, m_xa_wkv, m_xa_wo, m_norm_ffn_w, m_ffn_w1, m_ffn_w2, m_final_norm_w)))
    v1 = dict(zip(WEIGHTS, (v_norm_mix_w, v_w_in, v_hg_lb_raw, v_hg_norm_w, v_cv_dw_w, v_cv_dw_b, v_cv_ln_w, v_cv_ln_b, v_pl_w, v_pl_scale, v_lru_conv_w, v_lru_conv_b, v_lru_wa, v_lru_ba, v_lru_wx, v_lru_bx, v_lru_lambda, v_gate_b, v_w_branch, v_w_out, v_norm_mem_w, v_mem_norm_w, v_xa_wq, v_xa_wkv, v_xa_wo, v_norm_ffn_w, v_ffn_w1, v_ffn_w2, v_final_norm_w)))
    seq = x.shape[1]
    xs, mems, tgt = x.reshape(seq, D_MODEL), mem.reshape(-1, D_MODEL), loss_target.reshape(seq, D_MODEL)
    jj = 2 * lax.axis_index("x") + lax.axis_index("y")
    jj1 = jnp.reshape(jj, (1,)).astype(jnp.int32)

    split_shapes = [w[n].shape for n in SMALL_SPLIT]
    got = _gather_small(_pack([w[n] for n in SMALL_SPLIT]), over_c=False)
    per_chip = [_unpack(got[k], split_shapes) for k in range(4)]
    full_small = {n: jnp.concatenate([per_chip[k][i] for k in range(4)], axis=-1) for i, n in enumerate(SMALL_SPLIT)}
    big_names = list(BIG)
    kinds = [BIG[n] for n in big_names]
    first_send, first_recv, first = _gather_start([_cast_into_full(_as3d(w['w_in']), BIG['w_in'], jj1, "cast_w_in")], kinds[:1], got,
                                                  "gather_start_first", groups=(0,))
    casts = first + [_cast_into_full(_as3d(w[n]), BIG[n], jj1, "cast_" + n) for n in big_names[1:]]
    g_send, g_recv, fulls = _gather_start(casts, kinds, got, "gather_start", skip=((0, 0),))
    tix = {n: t for t, n in enumerate(big_names)}
    assert tix['w_in'] == 0

    lb = _lb_fwd(hg_lb_raw)
    row = lambda a: a.reshape(1, -1)

    def layer_params(l):
        return dict(
            nmix=row(norm_mix_w[l]), lb=row(lb[l]), hgnw=row(hg_norm_w[l]),
            cw=jnp.pad(full_small['cv_dw_w'][l], ((0, 32 - CV_K), (0, 0))), cb=row(cv_dw_b[l]), lnw=row(cv_ln_w[l]), lnb=row(cv_ln_b[l]),
            plw=pl_w[l], plsc=row(pl_scale[l]),
            lcw=jnp.pad(full_small['lru_conv_w'][l], ((0, 8 - LRU_CONV), (0, 0))), lcb=row(lru_conv_b[l]),
            wa=_block_diag(lru_wa[l]).astype(BF16), ba=row(lru_ba[l]), wx=_block_diag(lru_wx[l]).astype(BF16), bx=row(lru_bx[l]),
            lam=row(lru_lambda[l]), gb=full_small['gate_b'][l], nmem=row(norm_mem_w[l]), memw=row(mem_norm_w[l]), nffn=row(norm_ffn_w[l]))

    saved = []
    xc = xs
    def by_name(arrays):
        wf_ = dict(zip(big_names, arrays))
        wf_['w_branch'] = wf_['w_branch'].reshape(DEPTH, 4, 512, D_MODEL)
        return wf_

    def landed(l, names, arrays, after):
        which = [tix[n] for n in names]
        if l == 0 and which == [0]:
            arrays = _gather_wait(0, [0], first_send, first_recv, arrays[:1], kinds[:1], after) + arrays[1:]
        else:
            arrays = _gather_wait(l, which, g_send, g_recv, arrays, kinds, after)
        f_send, f_recv, arrays = _fill_start(l, which, arrays, kinds)
        return (which, f_send, f_recv), arrays

    def complete(l, pending, arrays, after):
        which, f_send, f_recv = pending
        return _fill_wait(l, which, f_send, f_recv, arrays, kinds, after)

    rest = [n for n in big_names if n != 'w_in']
    for l in range(DEPTH):
        if l == 0:
            pending, fulls = landed(0, ['w_in'], fulls, xc)
        fulls = complete(l, pending, fulls, xc)
        wf = by_name(fulls)
        p = layer_params(l)
        h = _norm_fwd(xc, p['nmix'], "norm_mix")
        proj = _mm(h, wf['w_in'], "nn", F32, "proj", tm=seq, layer=l)
        if l == 0:
            pending, fulls = landed(0, rest, fulls, proj)
        b_hg, st = _hgrn_fwd(proj, p['lb'], p['hgnw'])
        b_cv = _conv_fwd(proj, p['cw'], p['cb'], p['lnw'], p['lnb'])
        b_pl = _pool_fwd(proj, p['plw'], p['plsc'])
        b_lru, hs = _lru_fwd(proj, p['lcw'], p['lcb'], p['wa'], p['ba'], p['wx'], p['bx'], p['lam'])
        branches = (b_hg, b_cv, b_pl, b_lru)
        if l == 0:
            fulls = complete(0, pending, fulls, b_lru)
            wf = by_name(fulls)
        x1 = _merge_fwd(xc, branches, proj, p['gb'], wf['w_branch'], wf['w_out'], l)
        if l + 1 < DEPTH:
            pending, fulls = landed(l + 1, big_names, fulls, x1)
            wf = by_name(fulls)
        memn = _norm_fwd(mems, p['memw'], "norm_memtok")
        kv = _mm(memn, wf['xa_wkv'], "nn", BF16, "kv_proj", layer=l)
        x2 = _attn_fwd(x1, p['nmem'], wf['xa_wq'], kv, wf['xa_wo'], l)
        x3 = _ffn_fwd(x2, p['nffn'], wf['ffn_w1'], wf['ffn_w2'], l, ts=1024)
        saved.append(dict(p=p, x=xc, h=h, proj=proj, st=st, hs=hs, branches=branches, x1=x1, memn=memn, kv=kv, x2=x2))
        xc = x3

    loss_blk, dx, dfinal = _final_loss(xc, row(final_norm_w), tgt)

    gs = {n: [None] * DEPTH for n in SMALL if n != 'final_norm_w'}
    dlb = [None] * DEPTH
    in_flight = [[None, None] for _ in range(DEPTH)]

    def scatter(grads, grp, after, name):
        pieces = PIECE_GROUPS[grp]
        return _scatter_start([grads[key][None] for key in pieces], _piece_kinds(pieces), after, name)

    token = loss_blk
    for l in reversed(range(DEPTH)):
        sv = saved[l]
        p = sv['p']
        gb = {}
        dx2, gs['norm_ffn_w'][l], h3, da, r, dxb = _ffn_bwd(sv['x2'], dx, p['nffn'], wf['ffn_w1'], wf['ffn_w2'], l, token, ts=512)
        gb['ffn_w1'] = _mm(h3, da, "tn", BF16, "dw_ffn1", tm=1024)
        gb['ffn_w2'] = _mm(r, dxb, "tn", BF16, "dw_ffn2", tn=1024)
        dx1, gs['norm_mem_w'][l], h2, o, dq, dxb2, dk, dv = _attn_bwd(sv['x1'], dx2, p['nmem'], wf['xa_wq'], sv['kv'], wf['xa_wo'], l)
        gb['xa_wq'] = _mm(h2, dq, "tn", BF16, "dw_q")
        gb['xa_wo'] = _mm(o, dxb2, "tn", BF16, "dw_o")
        dkv = jnp.concatenate([dk, dv], axis=1)
        gb['xa_wkv'] = _mm(sv['memn'], dkv, "tn", BF16, "dw_kv")
        dmemn = _mm(dkv, wf['xa_wkv'], "nt", F32, "dmemn", layer=l)
        _, gs['mem_norm_w'][l] = _norm_bwd(mems, p['memw'], dmemn, None, "norm_memtok_bwd")
        in_flight[l][0], token = scatter(gb, 0, dx1, "scatter_start_%d_0" % l)
        db0, db1, db2, db3, dgp, dup, mg, dxb1, gs['gate_b'][l] = _merge_bwd(
            dx1, sv['branches'], sv['proj'], p['gb'], wf['w_branch'], wf['w_out'], l, token)
        gb['w_out'] = _mm(mg, dxb1, "tn", BF16, "dw_out")
        for kb in range(4):
            gb['w_branch', kb] = _mm(sv['branches'][kb], dup, "tn", BF16, "dw_branch", b_col0=kb * D_MODEL, n=D_MODEL, tn=512)
        dhg, dlb[l], gs['hg_norm_w'][l] = _hgrn_bwd(sv['proj'], db0, sv['st'], p['lb'], p['hgnw'])
        dcv, dcw, gs['cv_dw_b'][l], gs['cv_ln_w'][l], gs['cv_ln_b'][l] = _conv_bwd(sv['proj'], db1, p['cw'], p['cb'], p['lnw'], p['lnb'])
        gs['cv_dw_w'][l] = dcw[:CV_K]
        dpl, gs['pl_w'][l], gs['pl_scale'][l] = _pool_bwd(sv['proj'], db2, p['plw'], p['plsc'])
        dlru, dlcw, gs['lru_conv_b'][l], dwa, gs['lru_ba'][l], dwx, gs['lru_bx'][l], gs['lru_lambda'][l] = _lru_bwd(
            sv['proj'], sv['hs'], db3, p['lcw'], p['lcb'], p['wa'], p['ba'], p['wx'], p['bx'], p['lam'])
        gs['lru_conv_w'][l] = dlcw[:LRU_CONV]
        gs['lru_wa'][l], gs['lru_wx'][l] = _diag_blocks(dwa, LRU_HEADS), _diag_blocks(dwx, LRU_HEADS)
        dproj = [dhg, dcv, dpl, dlru, dgp]
        gb['w_in'] = _mm_tn_pieces(sv['h'], dproj, "dw_in")
        dh = _mm_nt_pieces(dproj, wf['w_in'], l, "dh_mix")
        dx, gs['norm_mix_w'][l] = _norm_bwd(sv['x'], p['nmix'], dh, dx1, "norm_mix_bwd")
        if l:
            in_flight[l][1], token = scatter(gb, 1, dx, "scatter_start_%d_1" % l)
    grad_x = dx.reshape(x.shape)
    gs['hg_lb_raw'] = _lb_bwd(hg_lb_raw, jnp.concatenate(dlb, axis=0))

    def full_shape(n):
        return full_small[n].shape if n in SMALL_SPLIT else w[n].shape

    small_full = []
    for n in SMALL:
        g = gs[n] if n == 'hg_lb_raw' else dfinal if n == 'final_norm_w' else jnp.stack(gs[n])
        small_full.append(g.reshape(full_shape(n)))
    mine = _pack(small_full + [loss_blk[0:1, 0:1]])
    chip_sum = _sum_rows(jnp.stack([mine, _sibling_swap([mine], "sibling_swap_small")[0]]), mine)
    everyone = _gather_small(chip_sum, over_c=False)
    in_flight[0][1], token = scatter(gb, 1, everyone, "scatter_start_0_1")
    total = _sum_rows(everyone, token)
    parts = _unpack(total, [full_shape(n) for n in SMALL] + [(1,)])
    loss = parts[-1].reshape(())
    g_small = {}
    for n, g in zip(SMALL, parts[:-1]):
        if n in SMALL_SPLIT:
            width = w[n].shape[-1]
            g = lax.dynamic_slice_in_dim(g, jj * width, width, axis=g.ndim - 1)
        g_small[n] = g
    shapes = [w[n].shape for n in SMALL]
    upd = _adamw(_pack([w[n] for n in SMALL]), [_pack([g_small[n] for n in SMALL])], _pack([m1[n] for n in SMALL]),
                 _pack([v1[n] for n in SMALL]), "adamw_small")
    d_small, m_small, v_small = [dict(zip(SMALL, _unpack(u, shapes))) for u in upd]

    stacks = {n: None for n in big_names}
    done_before = upd[0]
    for l in reversed(range(DEPTH)):
        for grp, pieces in enumerate(PIECE_GROUPS):
            s_send, s_recv, g_thru, lands = in_flight[l][grp]
            g_thru, lands = _scatter_wait(s_send, s_recv, g_thru, lands, _piece_kinds(pieces), done_before,
                                          "scatter_wait_%d_%d" % (l, grp))
            for key, g, r in zip(pieces, g_thru, lands):
                n, kb = key if isinstance(key, tuple) else (key, None)
                per = 1 if kb is None else 4
                stacks[n] = _sum_partials_into(stacks[n], l * per + (kb or 0), DEPTH * per, g, r, BIG[n], jj1, "sum_" + n)
                done_before = stacks[n]
    partial = [stacks[n] for n in big_names]
    theirs = _sibling_swap(partial)
    g_big, d_big, m_big, v_big = {}, {}, {}, {}
    for n, pa, pb in zip(big_names, partial, theirs):
        c2 = lambda a: a.reshape(-1, a.shape[-1])
        out = _adamw(c2(w[n]), [c2(pa), c2(pb)], c2(m1[n]), c2(v1[n]), "adamw_" + n)
        g_big[n], d_big[n], m_big[n], v_big[n] = [o.reshape(w[n].shape) for o in out]

    pick = lambda small, big: [big[n] if n in BIG else small[n] for n in WEIGHTS]
    return (loss, grad_x, *pick(g_small, g_big), *pick(d_small, d_big), *pick(m_small, m_big), *pick(v_small, v_big))
```
